```python
import math
import jax, jax.numpy as jnp
from jax import lax
import numpy as np

D_MODEL = 1024
BATCH = 8
SEQ = 2048
DEPTH = 4

CONV_WIDTH = 1024
CONV_K = 3
MLSTM_HEADS = 8
MLSTM_DQK = 128
MLSTM_DV = 128
MLSTM_CHUNK = 64
MLSTM_QK_W = MLSTM_HEADS * MLSTM_DQK
MLSTM_V_W = MLSTM_HEADS * MLSTM_DV
RMS_EPS = 1e-6
N_BRANCHES = 2
N_EXPERTS = 64
N_GROUPS = 8
TOPK_GROUPS = 4
GROUP_SCORE_TOPK = 2
TOP_K = 8
D_EXPERT = 256
D_SHARED = 256
ROUTED_SCALE = 2.5
PLE_DIM = 256
ALPHA = (2 * DEPTH) ** 0.25
BETA = (8 * DEPTH) ** -0.25
LN_EPS = 1e-5
IN_WIDTH = 3 * CONV_WIDTH + 2 * MLSTM_QK_W + 2 * MLSTM_V_W + 2 * MLSTM_HEADS + N_BRANCHES * D_MODEL

kernel_name = 'hybrid_conv_mlstm_moe_deepnorm'


def layer_norm(x, g, b):
    xf = x.astype(jnp.float32)
    mu = xf.mean(-1, keepdims=True)
    var = jnp.square(xf - mu).mean(-1, keepdims=True)
    return ((xf - mu) * lax.rsqrt(var + LN_EPS) * g + b).astype(x.dtype)


def split_columns(z):
    sizes = (CONV_WIDTH, CONV_WIDTH, CONV_WIDTH, MLSTM_QK_W, MLSTM_QK_W, MLSTM_V_W,
             MLSTM_V_W, MLSTM_HEADS, MLSTM_HEADS, N_BRANCHES * D_MODEL)
    offsets = []
    acc = 0
    for s in sizes[:-1]:
        acc += s
        offsets.append(acc)
    return jnp.split(z, offsets, axis=-1)


def short_conv_branch(gate_in, gate_out, val, conv_w):
    u = gate_in * val
    seq = u.shape[1]
    u_pad = jnp.pad(u, ((0, 0), (CONV_K - 1, 0), (0, 0)))
    conv = sum(conv_w[j] * u_pad[:, j:j + seq] for j in range(CONV_K))
    return gate_out * conv


def mlstm_branch(q, k, v, o, i_pre, f_pre, norm_g):
    f32 = jnp.float32
    bsz, seq, _ = q.shape
    nc = seq // MLSTM_CHUNK
    L = MLSTM_CHUNK

    def heads(t, d):
        return t.astype(f32).reshape(bsz, nc, L, MLSTM_HEADS, d).transpose(0, 3, 1, 2, 4)

    def gates(t):
        return t.astype(f32).reshape(bsz, nc, L, MLSTM_HEADS).transpose(0, 3, 1, 2)

    qh = heads(q, MLSTM_DQK) * (MLSTM_DQK ** -0.5)
    kh = heads(k, MLSTM_DQK)
    vh = heads(v, MLSTM_DV)
    ig = gates(i_pre)
    log_f = jax.nn.log_sigmoid(gates(f_pre))
    b = jnp.cumsum(log_f, axis=-1)
    b_last = b[..., -1]

    a = b_last[..., None] - b + ig
    m_loc = jnp.max(a, axis=-1)
    w_loc = jnp.exp(a - m_loc[..., None])
    c_loc = jnp.einsum('bhcsk,bhcsv->bhckv', kh * w_loc[..., None], vh)
    n_loc = jnp.einsum('bhcs,bhcsk->bhck', w_loc, kh)

    def step(carry, inp):
        c, n, m = carry
        bl, ml, cl, nl = inp
        m_new = jnp.maximum(bl + m, ml)
        sp = jnp.exp(bl + m - m_new)
        sl = jnp.exp(ml - m_new)
        c_new = sp[..., None, None] * c + sl[..., None, None] * cl
        n_new = sp[..., None] * n + sl[..., None] * nl
        return (c_new, n_new, m_new), (c, n, m)

    init = (jnp.zeros((bsz, MLSTM_HEADS, MLSTM_DQK, MLSTM_DV), f32),
            jnp.zeros((bsz, MLSTM_HEADS, MLSTM_DQK), f32),
            jnp.zeros((bsz, MLSTM_HEADS), f32))
    xs = (jnp.moveaxis(b_last, 2, 0), jnp.moveaxis(m_loc, 2, 0),
          jnp.moveaxis(c_loc, 2, 0), jnp.moveaxis(n_loc, 2, 0))
    _, (c_prev, n_prev, m_prev) = lax.scan(step, init, xs)
    c_prev = jnp.moveaxis(c_prev, 0, 2)
    n_prev = jnp.moveaxis(n_prev, 0, 2)
    m_prev = jnp.moveaxis(m_prev, 0, 2)

    log_inter = b + m_prev[..., None]
    causal = jnp.tril(jnp.ones((L, L), dtype=bool))
    log_d = jnp.where(causal, b[..., :, None] - b[..., None, :] + ig[..., None, :], -jnp.inf)
    m_out = jnp.maximum(log_inter, jnp.max(log_d, axis=-1))
    d = jnp.exp(log_d - m_out[..., None])
    s = jnp.einsum('bhcjk,bhcsk->bhcjs', qh, kh) * d
    w_inter = jnp.exp(log_inter - m_out)
    num = (jnp.einsum('bhcjs,bhcsv->bhcjv', s, vh)
           + w_inter[..., None] * jnp.einsum('bhcjk,bhckv->bhcjv', qh, c_prev))
    den = s.sum(-1) + w_inter * jnp.einsum('bhcjk,bhck->bhcj', qh, n_prev)
    h = num / jnp.maximum(jnp.abs(den), jnp.exp(-m_out))[..., None]

    h = h.transpose(0, 2, 3, 1, 4).reshape(bsz, seq, MLSTM_HEADS, MLSTM_DV)
    h = h * lax.rsqrt(jnp.mean(jnp.square(h), axis=-1, keepdims=True) + RMS_EPS)
    h = h.reshape(bsz, seq, MLSTM_V_W) * norm_g
    return (h * jax.nn.sigmoid(o.astype(f32))).astype(q.dtype)


def hybrid_mixer(x, w_in, conv_w, b_igate, b_fgate, norm_g, w_branch_a, w_branch_b, w_out):
    z = x @ w_in
    c_in, c_out, c_val, q, k, v, o, ig, fg, merge = split_columns(z)
    y_a = short_conv_branch(c_in, c_out, c_val, conv_w)
    y_b = mlstm_branch(q, k, v, o, ig + b_igate, fg + b_fgate, norm_g)
    g = jax.nn.sigmoid(merge).reshape(merge.shape[:-1] + (N_BRANCHES, D_MODEL))
    mixed = g[..., 0, :] * (y_a @ w_branch_a) + g[..., 1, :] * (y_b @ w_branch_b)
    return mixed @ w_out


def route(xr, w_router, router_bias):
    f32 = jnp.float32
    scores = jax.nn.sigmoid(xr.astype(f32) @ w_router.astype(f32))
    sel = scores + router_bias.astype(f32)
    grp = sel.reshape(sel.shape[0], N_GROUPS, N_EXPERTS // N_GROUPS)
    grp_score = lax.top_k(grp, GROUP_SCORE_TOPK)[0].sum(-1)
    _, gidx = lax.top_k(grp_score, TOPK_GROUPS)
    gmask = jax.nn.one_hot(gidx, N_GROUPS, dtype=f32).sum(-2) > 0
    emask = jnp.repeat(gmask, N_EXPERTS // N_GROUPS, axis=-1)
    _, eidx = lax.top_k(jnp.where(emask, sel, -jnp.inf), TOP_K)
    w = jnp.take_along_axis(scores, eidx, axis=-1)
    w = w / w.sum(-1, keepdims=True) * ROUTED_SCALE
    return jnp.sum(jax.nn.one_hot(eidx, N_EXPERTS, dtype=f32) * w[..., None], axis=-2)


def moe_ffn(x, w_router, router_bias, w_eg, w_eu, w_ed, w_sg, w_su, w_sd):
    def per_row(xr):
        gate = route(xr, w_router, router_bias).astype(xr.dtype)
        hg = jnp.einsum('sd,edf->sef', xr, w_eg)
        hu = jnp.einsum('sd,edf->sef', xr, w_eu)
        h = jax.nn.silu(hg) * hu * gate[..., None]
        routed = jnp.einsum('sef,efd->sd', h, w_ed)
        shared = (jax.nn.silu(xr @ w_sg) * (xr @ w_su)) @ w_sd
        return routed + shared
    return lax.map(per_row, x)


def setup_inputs(seed: int = 0) -> dict:
    key = jax.random.key(seed)
    ks = jax.random.split(key, 32)

    def nrm(k, shape, scale):
        return jax.random.normal(k, shape, jnp.float32) * scale

    H = MLSTM_HEADS
    return {
        'x': nrm(ks[0], (BATCH, SEQ, D_MODEL), 1.0),
        'p': nrm(ks[1], (DEPTH, BATCH, SEQ, PLE_DIM), 1.0),
        'w_in': nrm(ks[2], (DEPTH, D_MODEL, IN_WIDTH), D_MODEL ** -0.5),
        'conv_w': nrm(ks[3], (DEPTH, CONV_K, CONV_WIDTH), CONV_K ** -0.5),
        'b_igate': nrm(ks[4], (DEPTH, H), 0.1),
        'b_fgate': jnp.linspace(3.0, 6.0, H, dtype=jnp.float32)[None, :] + nrm(ks[5], (DEPTH, H), 0.1),
        'mlstm_norm_g': 1.0 + nrm(ks[6], (DEPTH, MLSTM_V_W), 0.02),
        'w_branch_a': nrm(ks[7], (DEPTH, CONV_WIDTH, D_MODEL), BETA * CONV_WIDTH ** -0.5),
        'w_branch_b': nrm(ks[8], (DEPTH, MLSTM_V_W, D_MODEL), BETA * MLSTM_V_W ** -0.5),
        'w_out': nrm(ks[9], (DEPTH, D_MODEL, D_MODEL), BETA * D_MODEL ** -0.5),
        'ln1_g': 1.0 + nrm(ks[10], (DEPTH, D_MODEL), 0.02),
        'ln1_b': nrm(ks[11], (DEPTH, D_MODEL), 0.02),
        'w_router': nrm(ks[12], (DEPTH, D_MODEL, N_EXPERTS), D_MODEL ** -0.5),
        'router_bias': nrm(ks[13], (DEPTH, N_EXPERTS), 0.01),
        'w_exp_gate': nrm(ks[14], (DEPTH, N_EXPERTS, D_MODEL, D_EXPERT), D_MODEL ** -0.5),
        'w_exp_up': nrm(ks[15], (DEPTH, N_EXPERTS, D_MODEL, D_EXPERT), D_MODEL ** -0.5),
        'w_exp_down': nrm(ks[16], (DEPTH, N_EXPERTS, D_EXPERT, D_MODEL), BETA * D_EXPERT ** -0.5),
        'w_sh_gate': nrm(ks[17], (DEPTH, D_MODEL, D_SHARED), D_MODEL ** -0.5),
        'w_sh_up': nrm(ks[18], (DEPTH, D_MODEL, D_SHARED), D_MODEL ** -0.5),
        'w_sh_down': nrm(ks[19], (DEPTH, D_SHARED, D_MODEL), BETA * D_SHARED ** -0.5),
        'w_ple_gate': nrm(ks[20], (DEPTH, D_MODEL, D_MODEL), D_MODEL ** -0.5),
        'w_ple_proj': nrm(ks[21], (DEPTH, PLE_DIM, D_MODEL), BETA * PLE_DIM ** -0.5),
        'ln2_g': 1.0 + nrm(ks[22], (DEPTH, D_MODEL), 0.02),
        'ln2_b': nrm(ks[23], (DEPTH, D_MODEL), 0.02),
    }


def reference(x, p, w_in, conv_w, b_igate, b_fgate, mlstm_norm_g, w_branch_a, w_branch_b,
              w_out, ln1_g, ln1_b, w_router, router_bias, w_exp_gate, w_exp_up, w_exp_down,
              w_sh_gate, w_sh_up, w_sh_down, w_ple_gate, w_ple_proj, ln2_g, ln2_b):
    for i in range(DEPTH):
        h = hybrid_mixer(x, w_in[i], conv_w[i], b_igate[i], b_fgate[i], mlstm_norm_g[i],
                         w_branch_a[i], w_branch_b[i], w_out[i])
        x = layer_norm(ALPHA * x + h, ln1_g[i], ln1_b[i])
        y = moe_ffn(x, w_router[i], router_bias[i], w_exp_gate[i], w_exp_up[i], w_exp_down[i],
                    w_sh_gate[i], w_sh_up[i], w_sh_down[i])
        ple = jax.nn.sigmoid(x @ w_ple_gate[i]) * (p[i] @ w_ple_proj[i])
        x = layer_norm(ALPHA * x + y + ple, ln2_g[i], ln2_b[i])
    return x
```

```python
import functools

import jax
import jax.numpy as jnp
from jax import lax
from jax.experimental import pallas as pl
from jax.experimental.pallas import tpu as pltpu

F32 = jnp.float32
BF16 = jnp.bfloat16

HEADS = 8
HEAD_DIM = 128
N_EXPERTS = 64
N_GROUPS = 8
GROUP_SIZE = N_EXPERTS // N_GROUPS
TOPK_GROUPS = 4
TOP_K = 8
ROUTED_SCALE = 2.5
DEPTH = 4
ALPHA = (2 * DEPTH) ** 0.25
LN_EPS = 1e-5
RMS_EPS = 1e-6
QK_SCALE = HEAD_DIM ** -0.5

LANES = 128
MLSTM_CHUNK = 256
VMEM_LIMIT = 56 * 1024 * 1024
NEG_INF = float("-inf")


def _params(*sem):
    return pltpu.CompilerParams(dimension_semantics=sem, vmem_limit_bytes=VMEM_LIMIT)


def _sigmoid(x):
    return 1.0 / (1.0 + jnp.exp(-x))


def _layer_norm(r, g, b):
    mu = jnp.mean(r, axis=-1, keepdims=True)
    d = r - mu
    var = jnp.mean(d * d, axis=-1, keepdims=True)
    return d * lax.rsqrt(var + LN_EPS) * g + b


def _mm_kernel(x_ref, w_ref, o_ref):
    o_ref[...] = jnp.dot(x_ref[...], w_ref[...], preferred_element_type=F32).astype(o_ref.dtype)


def _matmul(x, w, out_dtype, tm, tn, name):
    m, k = x.shape
    n = w.shape[1]
    tm = min(tm, m)
    return pl.pallas_call(
        _mm_kernel,
        grid=(m // tm, n // tn),
        in_specs=[pl.BlockSpec((tm, k), lambda i, j: (i, 0)),
                  pl.BlockSpec((k, tn), lambda i, j: (0, j))],
        out_specs=pl.BlockSpec((tm, tn), lambda i, j: (i, j)),
        out_shape=jax.ShapeDtypeStruct((m, n), out_dtype),
        compiler_params=_params("parallel", "parallel"),
        name=name,
    )(x, w)


def _mlstm_kernel(q_ref, k_ref, v_ref, o_ref, g_ref, gb_ref, ng_ref, y_ref, c_scr, m_scr):
    L = q_ref.shape[1]

    @pl.when(pl.program_id(1) == 0)
    def _():
        c_scr[...] = jnp.zeros_like(c_scr)
        m_scr[...] = jnp.zeros_like(m_scr)

    g = g_ref[0] + gb_ref[...]
    ig = g
    fg = pltpu.roll(g, LANES - HEADS, axis=1)
    log_f = jnp.minimum(fg, 0.0) - jnp.log1p(jnp.exp(-jnp.abs(fg)))
    row = lax.broadcasted_iota(jnp.int32, (L, L), 0)
    col = lax.broadcasted_iota(jnp.int32, (L, L), 1)
    causal = col <= row
    tri = jnp.where(causal, 1.0, 0.0).astype(F32)
    b = jnp.dot(tri, log_f, precision=lax.Precision.HIGHEST, preferred_element_type=F32)
    b_last = b[L - 1:L, :]
    m_prev = m_scr[...]
    a = b_last - b + ig
    m_loc = jnp.max(a, axis=0, keepdims=True)
    w_loc = jnp.exp(a - m_loc)
    m_new = jnp.maximum(b_last + m_prev, m_loc)
    sp = jnp.exp(b_last + m_prev - m_new)
    sl = jnp.exp(m_loc - m_new)
    log_inter = b + m_prev
    r_t = jnp.transpose(ig - b)

    ones_blk = jnp.ones((L, HEAD_DIM), BF16)
    for h in range(HEADS):
        sl_h = slice(h * HEAD_DIM, (h + 1) * HEAD_DIM)
        qh = q_ref[0, :, sl_h]
        kh = k_ref[0, :, sl_h]
        vh = v_ref[0, :, sl_h]
        v_aug = jnp.concatenate([vh, ones_blk], axis=1)
        s1 = lax.dot_general(qh, kh, (((1,), (1,)), ((), ())), preferred_element_type=F32)
        log_d = jnp.where(causal, b[:, h:h + 1] + r_t[h:h + 1, :], NEG_INF)
        li = log_inter[:, h:h + 1]
        m_out = jnp.maximum(li, jnp.max(log_d, axis=1, keepdims=True))
        d = jnp.exp(log_d - m_out)
        s = (s1 * QK_SCALE * d).astype(BF16)
        intra = jnp.dot(s, v_aug, preferred_element_type=F32)
        c_prev = c_scr[h]
        inter = jnp.dot(qh, c_prev.astype(BF16), preferred_element_type=F32) * QK_SCALE
        tot = intra + jnp.exp(li - m_out) * inter
        num = tot[:, :HEAD_DIM]
        den = tot[:, HEAD_DIM:]
        hh = num / jnp.maximum(jnp.abs(den), jnp.exp(-m_out))
        hh = hh * lax.rsqrt(jnp.mean(hh * hh, axis=1, keepdims=True) + RMS_EPS)
        og = _sigmoid(o_ref[0, :, sl_h].astype(F32))
        y_ref[0, :, sl_h] = (hh * ng_ref[:, sl_h] * og).astype(y_ref.dtype)
        kw = (kh.astype(F32) * w_loc[:, h:h + 1]).astype(BF16)
        c_loc = lax.dot_general(kw, v_aug, (((0,), (0,)), ((), ())), preferred_element_type=F32)
        c_scr[h] = sp[:, h:h + 1] * c_prev + sl[:, h:h + 1] * c_loc
    m_scr[...] = m_new


def _mlstm(z, gates, gate_bias, norm_g):
    bsz, seq, _ = z.shape
    L = MLSTM_CHUNK
    hw = HEADS * HEAD_DIM

    def zspec(cb):
        return pl.BlockSpec((1, L, hw), lambda b, c, cb=cb: (b, c, cb))

    return pl.pallas_call(
        _mlstm_kernel,
        grid=(bsz, seq // L),
        in_specs=[zspec(3), zspec(4), zspec(5), zspec(6),
                  pl.BlockSpec((1, L, LANES), lambda b, c: (b, c, 0)),
                  pl.BlockSpec((1, LANES), lambda b, c: (0, 0)),
                  pl.BlockSpec((1, hw), lambda b, c: (0, 0))],
        out_specs=pl.BlockSpec((1, L, hw), lambda b, c: (b, c, 0)),
        out_shape=jax.ShapeDtypeStruct((bsz, seq, hw), BF16),
        scratch_shapes=[pltpu.VMEM((HEADS, HEAD_DIM, 2 * HEAD_DIM), F32),
                        pltpu.VMEM((1, LANES), F32)],
        compiler_params=_params("parallel", "arbitrary"),
        name="mlstm",
    )(z, z, z, z, gates, gate_bias, norm_g)


def _mix_kernel(cin_ref, cout_ref, cval_ref, mg0_ref, mg1_ref, yb_ref, x_ref, cw_ref,
                wa_ref, wb_ref, wo_ref, g_ref, b_ref, x1_ref, x1b_ref, carry):
    ts = x_ref.shape[1]

    @pl.when(pl.program_id(1) == 0)
    def _():
        carry[...] = jnp.zeros_like(carry)

    u = cin_ref[0].astype(F32) * cval_ref[0].astype(F32)
    prev = carry[...]
    carry[...] = u[ts - 8:, :]
    r8 = lax.broadcasted_iota(jnp.int32, (8, u.shape[1]), 0)

    def shifted(k):
        body = pltpu.roll(u, k, axis=0)
        head = jnp.where(r8 < k, pltpu.roll(prev, k, axis=0), body[:8, :])
        return jnp.concatenate([head, body[8:, :]], axis=0)

    cw = cw_ref[...]
    conv = cw[0:1, :] * shifted(2) + cw[1:2, :] * shifted(1) + cw[2:3, :] * u
    y_a = (cout_ref[0].astype(F32) * conv).astype(BF16)
    pa = jnp.dot(y_a, wa_ref[...], preferred_element_type=F32)
    pb = jnp.dot(yb_ref[0], wb_ref[...], preferred_element_type=F32)
    mixed = _sigmoid(mg0_ref[0].astype(F32)) * pa + _sigmoid(mg1_ref[0].astype(F32)) * pb
    hmix = jnp.dot(mixed.astype(BF16), wo_ref[...], preferred_element_type=F32)
    x1 = _layer_norm(ALPHA * x_ref[0] + hmix, g_ref[...], b_ref[...])
    x1_ref[0] = x1
    x1b_ref[0] = x1.astype(BF16)


def _mix(z, y_b, x, conv_w, wa, wb, wo, g, b, ts=512):
    bsz, seq, d = x.shape

    def zspec(cb):
        return pl.BlockSpec((1, ts, d), lambda i, j, cb=cb: (i, j, cb))

    def full(shape):
        return pl.BlockSpec(shape, lambda i, j: (0,) * len(shape))

    tile = pl.BlockSpec((1, ts, d), lambda i, j: (i, j, 0))
    return pl.pallas_call(
        _mix_kernel,
        grid=(bsz, seq // ts),
        in_specs=[zspec(0), zspec(1), zspec(2), zspec(7), zspec(8), tile, tile,
                  full(conv_w.shape), full(wa.shape), full(wb.shape), full(wo.shape),
                  full(g.shape), full(b.shape)],
        out_specs=[tile, tile],
        out_shape=[jax.ShapeDtypeStruct((bsz, seq, d), F32),
                   jax.ShapeDtypeStruct((bsz, seq, d), BF16)],
        scratch_shapes=[pltpu.VMEM((8, d), F32)],
        compiler_params=_params("parallel", "arbitrary"),
        name="mix",
    )(z, z, z, z, z, y_b, x, conv_w, wa, wb, wo, g, b)


def _router_kernel(x_ref, wrt_ref, rb_ref, gate_ref):
    t = x_ref.shape[0]
    logits = lax.dot_general(wrt_ref[...], x_ref[...], (((1,), (1,)), ((), ())),
                             precision=lax.Precision.HIGHEST, preferred_element_type=F32)
    scores = _sigmoid(logits)
    sel = (scores + rb_ref[...]).reshape(N_GROUPS, GROUP_SIZE, t)
    scores = scores.reshape(N_GROUPS, GROUP_SIZE, t)
    shape3 = (N_GROUPS, GROUP_SIZE, t)
    sub = lax.broadcasted_iota(jnp.int32, shape3, 1)
    grp = lax.broadcasted_iota(jnp.int32, shape3, 0)
    eidx = grp * GROUP_SIZE + sub
    m1 = jnp.max(sel, axis=1, keepdims=True)
    first = jnp.min(jnp.where(sel == m1, sub, GROUP_SIZE), axis=1, keepdims=True)
    m2 = jnp.max(jnp.where(sub == first, NEG_INF, sel), axis=1, keepdims=True)
    rem = m1 + m2
    gidx = lax.broadcasted_iota(jnp.int32, rem.shape, 0)
    gmask = jnp.zeros(rem.shape, F32)
    for _ in range(TOPK_GROUPS):
        mx = jnp.max(rem, axis=0, keepdims=True)
        pick = gidx == jnp.min(jnp.where(rem == mx, gidx, N_GROUPS), axis=0, keepdims=True)
        gmask = jnp.where(pick, 1.0, gmask)
        rem = jnp.where(pick, NEG_INF, rem)
    masked = jnp.where(jnp.broadcast_to(gmask, shape3) > 0.5, sel, NEG_INF)
    chosen = jnp.zeros(shape3, F32)
    for _ in range(TOP_K):
        mx = jnp.max(jnp.max(masked, axis=1, keepdims=True), axis=0, keepdims=True)
        cand = jnp.where(masked == mx, eidx, N_EXPERTS)
        fi = jnp.min(jnp.min(cand, axis=1, keepdims=True), axis=0, keepdims=True)
        pick = eidx == fi
        chosen = jnp.where(pick, 1.0, chosen)
        masked = jnp.where(pick, NEG_INF, masked)
    w = chosen * scores
    denom = jnp.sum(jnp.sum(w, axis=1, keepdims=True), axis=0, keepdims=True)
    gate_t = (w / denom * ROUTED_SCALE).reshape(N_EXPERTS, t)
    pad = jnp.zeros((LANES - N_EXPERTS, t), F32)
    gate_ref[...] = jnp.transpose(jnp.concatenate([gate_t, pad], axis=0))


def _router(x1, w_router_t, router_bias, tt=512):
    m, d = x1.shape
    return pl.pallas_call(
        _router_kernel,
        grid=(m // tt,),
        in_specs=[pl.BlockSpec((tt, d), lambda i: (i, 0)),
                  pl.BlockSpec((N_EXPERTS, d), lambda i: (0, 0)),
                  pl.BlockSpec((N_EXPERTS, 1), lambda i: (0, 0))],
        out_specs=pl.BlockSpec((tt, LANES), lambda i: (i, 0)),
        out_shape=jax.ShapeDtypeStruct((m, LANES), F32),
        compiler_params=_params("parallel"),
        name="router",
    )(x1, w_router_t, router_bias)


def _experts_kernel(x_ref, gate_ref, wg_ref, wu_ref, wd_ref, y_ref, acc):
    e = pl.program_id(1)

    @pl.when(e == 0)
    def _():
        acc[...] = jnp.zeros_like(acc)

    x = x_ref[...]
    hg = jnp.dot(x, wg_ref[0], preferred_element_type=F32)
    hu = jnp.dot(x, wu_ref[0], preferred_element_type=F32)
    f = hg.shape[1]
    onehot = jnp.where(lax.broadcasted_iota(jnp.int32, (LANES, f), 0) == e, 1.0, 0.0).astype(F32)
    gcol = jnp.dot(gate_ref[...], onehot, precision=lax.Precision.HIGHEST,
                   preferred_element_type=F32)
    hmid = (hg * _sigmoid(hg) * hu * gcol).astype(BF16)
    acc[...] += jnp.dot(hmid, wd_ref[0], preferred_element_type=F32)

    @pl.when(e == pl.num_programs(1) - 1)
    def _():
        y_ref[...] = acc[...]


def _experts(x1b, gate, wg, wu, wd, tm=1024):
    m, d = x1b.shape
    n_e, _, f = wg.shape
    return pl.pallas_call(
        _experts_kernel,
        grid=(m // tm, n_e),
        in_specs=[pl.BlockSpec((tm, d), lambda i, e: (i, 0)),
                  pl.BlockSpec((tm, LANES), lambda i, e: (i, 0)),
                  pl.BlockSpec((1, d, f), lambda i, e: (e, 0, 0)),
                  pl.BlockSpec((1, d, f), lambda i, e: (e, 0, 0)),
                  pl.BlockSpec((1, f, d), lambda i, e: (e, 0, 0))],
        out_specs=pl.BlockSpec((tm, d), lambda i, e: (i, 0)),
        out_shape=jax.ShapeDtypeStruct((m, d), F32),
        scratch_shapes=[pltpu.VMEM((tm, d), F32)],
        compiler_params=_params("parallel", "arbitrary"),
        name="experts",
    )(x1b, gate, wg, wu, wd)


def _final_kernel(x1_ref, x1b_ref, yr_ref, p_ref, wsg_ref, wsu_ref, wsd_ref, wpg_ref, wpp_ref,
                  g_ref, b_ref, x2_ref, x2b_ref):
    xb = x1b_ref[...]
    hg = jnp.dot(xb, wsg_ref[...], preferred_element_type=F32)
    hu = jnp.dot(xb, wsu_ref[...], preferred_element_type=F32)
    shared = jnp.dot((hg * _sigmoid(hg) * hu).astype(BF16), wsd_ref[...], preferred_element_type=F32)
    pgate = _sigmoid(jnp.dot(xb, wpg_ref[...], preferred_element_type=F32))
    pproj = jnp.dot(p_ref[...].astype(BF16), wpp_ref[...], preferred_element_type=F32)
    r = ALPHA * x1_ref[...] + (yr_ref[...] + shared) + pgate * pproj
    x2 = _layer_norm(r, g_ref[...], b_ref[...])
    x2_ref[...] = x2
    x2b_ref[...] = x2.astype(BF16)


def _final(x1, x1b, y_routed, p, wsg, wsu, wsd, wpg, wpp, g, b, tm=512):
    m, d = x1.shape

    def full(a):
        return pl.BlockSpec(a.shape, lambda i: (0,) * a.ndim)

    def rows(a):
        return pl.BlockSpec((tm, a.shape[1]), lambda i: (i, 0))

    return pl.pallas_call(
        _final_kernel,
        grid=(m // tm,),
        in_specs=[rows(x1), rows(x1b), rows(y_routed), rows(p),
                  full(wsg), full(wsu), full(wsd), full(wpg), full(wpp), full(g), full(b)],
        out_specs=[rows(x1), rows(x1b)],
        out_shape=[jax.ShapeDtypeStruct((m, d), F32), jax.ShapeDtypeStruct((m, d), BF16)],
        compiler_params=_params("parallel"),
        name="final",
    )(x1, x1b, y_routed, p, wsg, wsu, wsd, wpg, wpp, g, b)


def kernel(x, p, w_in, conv_w, b_igate, b_fgate, mlstm_norm_g, w_branch_a, w_branch_b, w_out, ln1_g, ln1_b, w_router, router_bias, w_exp_gate, w_exp_up, w_exp_down, w_sh_gate, w_sh_up, w_sh_down, w_ple_gate, w_ple_proj, ln2_g, ln2_b):
    bsz, seq, d = x.shape
    m = bsz * seq
    depth = w_in.shape[0]
    hw = HEADS * HEAD_DIM
    if_lo = 3 * d + 4 * hw
    if_hi = if_lo + 2 * HEADS

    xf = x.reshape(m, d)
    xb = xf.astype(BF16)
    for i in range(depth):
        w_main = jnp.concatenate([w_in[i][:, :if_lo], w_in[i][:, if_hi:]], axis=1).astype(BF16)
        w_if = jnp.pad(w_in[i][:, if_lo:if_hi], ((0, 0), (0, LANES - 2 * HEADS))).astype(BF16)
        gate_bias = jnp.pad(jnp.concatenate([b_igate[i], b_fgate[i]]), (0, LANES - 2 * HEADS)).reshape(1, LANES)

        z = _matmul(xb, w_main, BF16, tm=2048, tn=1024, name="inproj").reshape(bsz, seq, -1)
        gates = _matmul(xb, w_if, F32, tm=2048, tn=LANES, name="gateproj").reshape(bsz, seq, LANES)
        y_b = _mlstm(z, gates, gate_bias, mlstm_norm_g[i].reshape(1, hw))
        x1, x1b = _mix(z, y_b, xf.reshape(bsz, seq, d), conv_w[i],
                       w_branch_a[i].astype(BF16), w_branch_b[i].astype(BF16), w_out[i].astype(BF16),
                       ln1_g[i].reshape(1, d), ln1_b[i].reshape(1, d))
        x1 = x1.reshape(m, d)
        x1b = x1b.reshape(m, d)

        gate = _router(x1, w_router[i].T, router_bias[i].reshape(N_EXPERTS, 1))
        y_routed = _experts(x1b, gate, w_exp_gate[i].astype(BF16), w_exp_up[i].astype(BF16),
                            w_exp_down[i].astype(BF16))
        xf, xb = _final(x1, x1b, y_routed, p[i].reshape(m, -1),
                        w_sh_gate[i].astype(BF16), w_sh_up[i].astype(BF16), w_sh_down[i].astype(BF16),
                        w_ple_gate[i].astype(BF16), w_ple_proj[i].astype(BF16),
                        ln2_g[i].reshape(1, d), ln2_b[i].reshape(1, d))
    return xf.reshape(bsz, seq, d)
```

```python
import jax
import jax.numpy as jnp
from jax import lax
from jax.experimental import pallas as pl
from jax.experimental.pallas import tpu as pltpu

F32 = jnp.float32
BF16 = jnp.bfloat16
U32 = jnp.uint32
I32 = jnp.int32

HEADS = 8
HEAD_DIM = 128
N_EXPERTS = 64
N_GROUPS = 8
GROUP_SIZE = N_EXPERTS // N_GROUPS
TOPK_GROUPS = 4
TOP_K = 8
ROUTED_SCALE = 2.5
DEPTH = 4
ALPHA = (2 * DEPTH) ** 0.25
LN_EPS = 1e-5
RMS_EPS = 1e-6
QK_SCALE = HEAD_DIM ** -0.5

LANES = 128
VMEM_LIMIT = 56 * 1024 * 1024
NEG_INF = float("-inf")

MLSTM_CHUNK = 256
INPROJ_TM, INPROJ_TN = 2048, 1024
MIX_TS = 512
ROUTER_TT = 512
DISPATCH_TT = 1024
FFN_TM = 512
COMBINE_TT = 256


def _params(*sem):
    return pltpu.CompilerParams(dimension_semantics=sem, vmem_limit_bytes=VMEM_LIMIT)


def _sigmoid(x):
    return 1.0 / (1.0 + jnp.exp(-x))


def _layer_norm(r, g, b):
    mu = jnp.mean(r, axis=-1, keepdims=True)
    d = r - mu
    var = jnp.mean(d * d, axis=-1, keepdims=True)
    return d * lax.rsqrt(var + LN_EPS) * g + b


def _pack_pair(lo, hi):
    lo_w = lax.bitcast_convert_type(lo.astype(BF16).astype(F32), U32) >> 16
    hi_w = lax.bitcast_convert_type(hi.astype(BF16).astype(F32), U32) & jnp.uint32(0xFFFF0000)
    return hi_w | lo_w


def _unpack_pair(w):
    lo = lax.bitcast_convert_type(w << 16, F32)
    hi = lax.bitcast_convert_type(w & jnp.uint32(0xFFFF0000), F32)
    return lo, hi


def _unpack_rows(w):
    lo, hi = _unpack_pair(w)
    return jnp.concatenate([lo.astype(BF16), hi.astype(BF16)], axis=1)


def _mm_kernel(x_ref, w_ref, o_ref):
    o_ref[...] = jnp.dot(x_ref[...], w_ref[...], preferred_element_type=F32).astype(o_ref.dtype)


def _matmul(x, w, out_dtype, tm, tn, name):
    m, k = x.shape
    n = w.shape[1]
    tm = min(tm, m)
    return pl.pallas_call(
        _mm_kernel,
        grid=(m // tm, n // tn),
        in_specs=[pl.BlockSpec((tm, k), lambda i, j: (i, 0)),
                  pl.BlockSpec((k, tn), lambda i, j: (0, j))],
        out_specs=pl.BlockSpec((tm, tn), lambda i, j: (i, j)),
        out_shape=jax.ShapeDtypeStruct((m, n), out_dtype),
        compiler_params=_params("parallel", "parallel"),
        name=name,
    )(x, w)


def _mlstm_kernel(q_ref, k_ref, v_ref, o_ref, g_ref, gb_ref, ng_ref, y_ref, c_scr, m_scr):
    L = q_ref.shape[1]

    @pl.when(pl.program_id(1) == 0)
    def _():
        c_scr[...] = jnp.zeros_like(c_scr)
        m_scr[...] = jnp.zeros_like(m_scr)

    g = g_ref[0] + gb_ref[...]
    ig = g
    fg = pltpu.roll(g, LANES - HEADS, axis=1)
    log_f = jnp.minimum(fg, 0.0) - jnp.log1p(jnp.exp(-jnp.abs(fg)))
    row = lax.broadcasted_iota(I32, (L, L), 0)
    col = lax.broadcasted_iota(I32, (L, L), 1)
    causal = col <= row
    tri = jnp.where(causal, 1.0, 0.0).astype(F32)
    b = jnp.dot(tri, log_f, precision=lax.Precision.HIGHEST, preferred_element_type=F32)
    b_last = b[L - 1:L, :]
    m_prev = m_scr[...]
    a = b_last - b + ig
    m_loc = jnp.max(a, axis=0, keepdims=True)
    w_loc = jnp.exp(a - m_loc)
    m_new = jnp.maximum(b_last + m_prev, m_loc)
    sp = jnp.exp(b_last + m_prev - m_new)
    sl = jnp.exp(m_loc - m_new)
    log_inter = b + m_prev
    r_t = jnp.transpose(ig - b)

    ones_blk = jnp.ones((L, HEAD_DIM), BF16)
    for h in range(HEADS):
        sl_h = slice(h * HEAD_DIM, (h + 1) * HEAD_DIM)
        qh = q_ref[0, :, sl_h]
        kh = k_ref[0, :, sl_h]
        vh = v_ref[0, :, sl_h]
        v_aug = jnp.concatenate([vh, ones_blk], axis=1)
        s1 = lax.dot_general(qh, kh, (((1,), (1,)), ((), ())), preferred_element_type=F32)
        log_d = jnp.where(causal, b[:, h:h + 1] + r_t[h:h + 1, :], NEG_INF)
        li = log_inter[:, h:h + 1]
        m_out = jnp.maximum(li, jnp.max(log_d, axis=1, keepdims=True))
        d = jnp.exp(log_d - m_out)
        s = (s1 * QK_SCALE * d).astype(BF16)
        intra = jnp.dot(s, v_aug, preferred_element_type=F32)
        c_prev = c_scr[h]
        inter = jnp.dot(qh, c_prev.astype(BF16), preferred_element_type=F32) * QK_SCALE
        tot = intra + jnp.exp(li - m_out) * inter
        num = tot[:, :HEAD_DIM]
        den = tot[:, HEAD_DIM:]
        hh = num / jnp.maximum(jnp.abs(den), jnp.exp(-m_out))
        hh = hh * lax.rsqrt(jnp.mean(hh * hh, axis=1, keepdims=True) + RMS_EPS)
        og = _sigmoid(o_ref[0, :, sl_h].astype(F32))
        y_ref[0, :, sl_h] = (hh * ng_ref[:, sl_h] * og).astype(y_ref.dtype)
        kw = (kh.astype(F32) * w_loc[:, h:h + 1]).astype(BF16)
        c_loc = lax.dot_general(kw, v_aug, (((0,), (0,)), ((), ())), preferred_element_type=F32)
        c_scr[h] = sp[:, h:h + 1] * c_prev + sl[:, h:h + 1] * c_loc
    m_scr[...] = m_new


def _mlstm(z, gates, gate_bias, norm_g):
    bsz, seq, _ = z.shape
    L = MLSTM_CHUNK
    hw = HEADS * HEAD_DIM

    def zspec(cb):
        return pl.BlockSpec((1, L, hw), lambda b, c, cb=cb: (b, c, cb))

    return pl.pallas_call(
        _mlstm_kernel,
        grid=(bsz, seq // L),
        in_specs=[zspec(3), zspec(4), zspec(5), zspec(6),
                  pl.BlockSpec((1, L, LANES), lambda b, c: (b, c, 0)),
                  pl.BlockSpec((1, LANES), lambda b, c: (0, 0)),
                  pl.BlockSpec((1, hw), lambda b, c: (0, 0))],
        out_specs=pl.BlockSpec((1, L, hw), lambda b, c: (b, c, 0)),
        out_shape=jax.ShapeDtypeStruct((bsz, seq, hw), BF16),
        scratch_shapes=[pltpu.VMEM((HEADS, HEAD_DIM, 2 * HEAD_DIM), F32),
                        pltpu.VMEM((1, LANES), F32)],
        compiler_params=_params("parallel", "arbitrary"),
        name="mlstm",
    )(z, z, z, z, gates, gate_bias, norm_g)


def _mix_kernel(cin_ref, cout_ref, cval_ref, mg0_ref, mg1_ref, yb_ref, x_ref, cw_ref,
                wa_ref, wb_ref, wo_ref, g_ref, b_ref, x1_ref, x1p_ref, carry):
    ts = x_ref.shape[1]

    @pl.when(pl.program_id(1) == 0)
    def _():
        carry[...] = jnp.zeros_like(carry)

    u = cin_ref[0].astype(F32) * cval_ref[0].astype(F32)
    prev = carry[...]
    carry[...] = u[ts - 8:, :]
    r8 = lax.broadcasted_iota(I32, (8, u.shape[1]), 0)

    def shifted(k):
        body = pltpu.roll(u, k, axis=0)
        head = jnp.where(r8 < k, pltpu.roll(prev, k, axis=0), body[:8, :])
        return jnp.concatenate([head, body[8:, :]], axis=0)

    cw = cw_ref[...]
    conv = cw[0:1, :] * shifted(2) + cw[1:2, :] * shifted(1) + cw[2:3, :] * u
    y_a = (cout_ref[0].astype(F32) * conv).astype(BF16)
    pa = jnp.dot(y_a, wa_ref[...], preferred_element_type=F32)
    pb = jnp.dot(yb_ref[0], wb_ref[...], preferred_element_type=F32)
    mixed = _sigmoid(mg0_ref[0].astype(F32)) * pa + _sigmoid(mg1_ref[0].astype(F32)) * pb
    hmix = jnp.dot(mixed.astype(BF16), wo_ref[...], preferred_element_type=F32)
    x1 = _layer_norm(ALPHA * x_ref[0] + hmix, g_ref[...], b_ref[...])
    x1_ref[0] = x1
    half = x1.shape[1] // 2
    x1p_ref[0] = _pack_pair(x1[:, :half], x1[:, half:])


def _mix(z, y_b, x, conv_w, wa, wb, wo, g, b):
    bsz, seq, d = x.shape
    ts = min(MIX_TS, seq)

    def zspec(cb):
        return pl.BlockSpec((1, ts, d), lambda i, j, cb=cb: (i, j, cb))

    def full(shape):
        return pl.BlockSpec(shape, lambda i, j: (0,) * len(shape))

    tile = pl.BlockSpec((1, ts, d), lambda i, j: (i, j, 0))
    ptile = pl.BlockSpec((1, ts, d // 2), lambda i, j: (i, j, 0))
    return pl.pallas_call(
        _mix_kernel,
        grid=(bsz, seq // ts),
        in_specs=[zspec(0), zspec(1), zspec(2), zspec(7), zspec(8), tile, tile,
                  full(conv_w.shape), full(wa.shape), full(wb.shape), full(wo.shape),
                  full(g.shape), full(b.shape)],
        out_specs=[tile, ptile],
        out_shape=[jax.ShapeDtypeStruct((bsz, seq, d), F32),
                   jax.ShapeDtypeStruct((bsz, seq, d // 2), U32)],
        scratch_shapes=[pltpu.VMEM((8, d), F32)],
        compiler_params=_params("parallel", "arbitrary"),
        name="mix",
    )(z, z, z, z, z, y_b, x, conv_w, wa, wb, wo, g, b)


def _router_kernel(x_ref, wrt_ref, rb_ref, idx_ref, rank_ref, wtm_ref, cnt_ref, carry):
    t = x_ref.shape[0]

    @pl.when(pl.program_id(0) == 0)
    def _():
        carry[...] = jnp.zeros_like(carry)

    logits = lax.dot_general(wrt_ref[...], x_ref[...], (((1,), (1,)), ((), ())),
                             precision=lax.Precision.HIGHEST, preferred_element_type=F32)
    scores = _sigmoid(logits)
    sel = (scores + rb_ref[...]).reshape(N_GROUPS, GROUP_SIZE, t)
    scores = scores.reshape(N_GROUPS, GROUP_SIZE, t)
    shape3 = (N_GROUPS, GROUP_SIZE, t)
    sub = lax.broadcasted_iota(I32, shape3, 1)
    grp = lax.broadcasted_iota(I32, shape3, 0)
    eidx = grp * GROUP_SIZE + sub
    m1 = jnp.max(sel, axis=1, keepdims=True)
    first = jnp.min(jnp.where(sel == m1, sub, GROUP_SIZE), axis=1, keepdims=True)
    m2 = jnp.max(jnp.where(sub == first, NEG_INF, sel), axis=1, keepdims=True)
    rem = m1 + m2
    gidx = lax.broadcasted_iota(I32, rem.shape, 0)
    gmask = jnp.zeros(rem.shape, F32)
    for _ in range(TOPK_GROUPS):
        mx = jnp.max(rem, axis=0, keepdims=True)
        pick = gidx == jnp.min(jnp.where(rem == mx, gidx, N_GROUPS), axis=0, keepdims=True)
        gmask = jnp.where(pick, 1.0, gmask)
        rem = jnp.where(pick, NEG_INF, rem)
    masked = jnp.where(jnp.broadcast_to(gmask, shape3) > 0.5, sel, NEG_INF)
    chosen = jnp.zeros(shape3, F32)
    picks = []
    for _ in range(TOP_K):
        mx = jnp.max(jnp.max(masked, axis=1, keepdims=True), axis=0, keepdims=True)
        cand = jnp.where(masked == mx, eidx, N_EXPERTS)
        fi = jnp.min(jnp.min(cand, axis=1, keepdims=True), axis=0, keepdims=True)
        pick = eidx == fi
        picks.append((fi, pick))
        chosen = jnp.where(pick, 1.0, chosen)
        masked = jnp.where(pick, NEG_INF, masked)
    w = chosen * scores
    denom = jnp.sum(jnp.sum(w, axis=1, keepdims=True), axis=0, keepdims=True)
    gate3 = w / denom * ROUTED_SCALE

    chosen2 = chosen.reshape(N_EXPERTS, t).astype(BF16)
    tok_r = lax.broadcasted_iota(I32, (t, t), 0)
    tok_c = lax.broadcasted_iota(I32, (t, t), 1)
    before = jnp.where(tok_r < tok_c, 1.0, 0.0).astype(BF16)
    prefix = (jnp.dot(chosen2, before, preferred_element_type=F32) + carry[...]).reshape(shape3)
    carry[...] += jnp.dot(chosen2, jnp.ones((t, t), BF16), preferred_element_type=F32)
    cnt_ref[...] = carry[:, :LANES]

    def pick_sum(pick, val):
        return jnp.sum(jnp.sum(jnp.where(pick, val, 0.0), axis=1, keepdims=True), axis=0, keepdims=True)

    sub8 = lax.broadcasted_iota(I32, (TOP_K, t), 0)
    idx8 = jnp.zeros((TOP_K, t), I32)
    rank8 = jnp.zeros((TOP_K, t), F32)
    w8 = jnp.zeros((TOP_K, t), F32)
    for k, (fi, pick) in enumerate(picks):
        idx8 = jnp.where(sub8 == k, jnp.broadcast_to(fi.reshape(1, t), (TOP_K, t)), idx8)
        rank8 = jnp.where(sub8 == k, jnp.broadcast_to(pick_sum(pick, prefix).reshape(1, t), (TOP_K, t)), rank8)
        w8 = jnp.where(sub8 == k, jnp.broadcast_to(pick_sum(pick, gate3).reshape(1, t), (TOP_K, t)), w8)
    idx_ref[...] = idx8
    rank_ref[...] = rank8.astype(I32)
    pad = jnp.zeros((LANES - TOP_K, t), F32)
    wtm_ref[...] = jnp.transpose(jnp.concatenate([w8, pad], axis=0))


def _router(x1, w_router_t, router_bias):
    m, d = x1.shape
    tt = min(ROUTER_TT, m)
    kt = pl.BlockSpec((TOP_K, tt), lambda i: (0, i))
    return pl.pallas_call(
        _router_kernel,
        grid=(m // tt,),
        in_specs=[pl.BlockSpec((tt, d), lambda i: (i, 0)),
                  pl.BlockSpec((N_EXPERTS, d), lambda i: (0, 0)),
                  pl.BlockSpec((N_EXPERTS, 1), lambda i: (0, 0))],
        out_specs=[kt, kt,
                   pl.BlockSpec((tt, LANES), lambda i: (i, 0)),
                   pl.BlockSpec((N_EXPERTS, LANES), lambda i: (0, 0))],
        out_shape=[jax.ShapeDtypeStruct((TOP_K, m), I32),
                   jax.ShapeDtypeStruct((TOP_K, m), I32),
                   jax.ShapeDtypeStruct((m, LANES), F32),
                   jax.ShapeDtypeStruct((N_EXPERTS, LANES), F32)],
        scratch_shapes=[pltpu.VMEM((N_EXPERTS, tt), F32)],
        compiler_params=_params("arbitrary"),
        name="router",
    )(x1, w_router_t, router_bias)


def _visit_metadata(counts, tm, n_rows):
    nt = n_rows // tm
    nv = nt + N_EXPERTS - 1
    ends = jnp.cumsum(counts)
    starts = ends - counts
    first_tile = starts // tm
    ntiles = jnp.where(counts > 0, (ends - 1) // tm - first_tile + 1, 0)
    vend = jnp.cumsum(ntiles)
    vstart = vend - ntiles
    v = jnp.arange(nv, dtype=I32)
    valid = v < vend[-1]
    ve = jnp.minimum(jnp.searchsorted(vend, v, side="right").astype(I32), N_EXPERTS - 1)
    ve = jnp.where(valid, ve, ve[jnp.maximum(vend[-1] - 1, 0)])
    vt = jnp.where(valid, first_tile[ve] + v - vstart[ve], nt - 1)
    lo = jnp.where(valid, jnp.clip(starts[ve] - vt * tm, 0, tm), 0)
    hi = jnp.where(valid, jnp.clip(ends[ve] - vt * tm, 0, tm), 0)
    return ve.astype(I32), vt.astype(I32), lo.astype(I32), hi.astype(I32)


def _dispatch_kernel(dest_ref, x_ref, xs_hbm, sem):
    tt = x_ref.shape[0]

    def body(t, c):
        for k in range(TOP_K):
            row = dest_ref[k, t]
            pltpu.make_async_copy(x_ref.at[pl.ds(t, 1), :], xs_hbm.at[pl.ds(row, 1), :], sem).start(priority=k % 2)
        return c

    lax.fori_loop(0, tt, body, 0)
    done = xs_hbm.at[pl.ds(0, TOP_K * tt), :]
    pltpu.make_async_copy(done, done, sem).wait()


def _dispatch(x1p, dest):
    m, dp = x1p.shape
    tt = min(DISPATCH_TT, m)
    return pl.pallas_call(
        _dispatch_kernel,
        grid=(m // tt,),
        in_specs=[pl.BlockSpec((TOP_K, tt), lambda i: (0, i), memory_space=pltpu.SMEM),
                  pl.BlockSpec((tt, dp), lambda i: (i, 0))],
        out_specs=pl.BlockSpec(memory_space=pl.ANY),
        out_shape=jax.ShapeDtypeStruct((m * TOP_K, dp), U32),
        scratch_shapes=[pltpu.SemaphoreType.DMA],
        compiler_params=_params("arbitrary"),
        name="dispatch",
    )(dest, x1p)


def _ffn_kernel(ve_ref, vt_ref, lo_ref, hi_ref, xs_ref, wg_ref, wu_ref, wd_ref, ys_ref,
                wg_b, wu_b, wd_b, acc):
    v = pl.program_id(0)
    lo = lo_ref[v]
    hi = hi_ref[v]
    tm = xs_ref.shape[0]

    @pl.when((v == 0) | (ve_ref[v] != ve_ref[jnp.maximum(v - 1, 0)]))
    def _():
        wg_b[...] = wg_ref[0].astype(BF16)
        wu_b[...] = wu_ref[0].astype(BF16)
        wd_b[...] = wd_ref[0].astype(BF16)

    @pl.when(hi > lo)
    def _():
        x = _unpack_rows(xs_ref[...])
        hg = jnp.dot(x, wg_b[...], preferred_element_type=F32)
        hu = jnp.dot(x, wu_b[...], preferred_element_type=F32)
        r = lax.broadcasted_iota(I32, hg.shape, 0)
        mine = (r >= lo) & (r < hi)
        hmid = jnp.where(mine, hg * _sigmoid(hg) * hu, 0.0).astype(BF16)
        y = jnp.dot(hmid, wd_b[...], preferred_element_type=F32)

        @pl.when(lo == 0)
        def _():
            acc[...] = y

        @pl.when(lo > 0)
        def _():
            acc[...] += y

        @pl.when(hi == tm)
        def _():
            a = acc[...]
            half = a.shape[1] // 2
            ys_ref[...] = _pack_pair(a[:, :half], a[:, half:])


def _ffn(xs, meta, wg, wu, wd):
    n_rows, dp = xs.shape
    n_e, d, f = wg.shape
    tm = FFN_TM
    nv = n_rows // tm + N_EXPERTS - 1
    grid_spec = pltpu.PrefetchScalarGridSpec(
        num_scalar_prefetch=4,
        grid=(nv,),
        in_specs=[pl.BlockSpec((tm, dp), lambda v, ve, vt, lo, hi: (vt[v], 0)),
                  pl.BlockSpec((1, d, f), lambda v, ve, vt, lo, hi: (ve[v], 0, 0)),
                  pl.BlockSpec((1, d, f), lambda v, ve, vt, lo, hi: (ve[v], 0, 0)),
                  pl.BlockSpec((1, f, d), lambda v, ve, vt, lo, hi: (ve[v], 0, 0))],
        out_specs=pl.BlockSpec((tm, dp), lambda v, ve, vt, lo, hi: (vt[v], 0)),
        scratch_shapes=[pltpu.VMEM((d, f), BF16), pltpu.VMEM((d, f), BF16), pltpu.VMEM((f, d), BF16),
                        pltpu.VMEM((tm, d), F32)],
    )
    return pl.pallas_call(
        _ffn_kernel,
        grid_spec=grid_spec,
        out_shape=jax.ShapeDtypeStruct((n_rows, dp), U32),
        compiler_params=_params("arbitrary"),
        name="ffn",
    )(*meta, xs, wg, wu, wd)


def _combine_kernel(dest_ref, ys_hbm, x1_ref, x1p_ref, wtm_ref, p_ref, wsg_ref, wsu_ref, wsd_ref,
                    wpg_ref, wpp_ref, g_ref, b_ref, x2_ref, x2b_ref, buf, sem):
    tt = x1_ref.shape[0]

    def body(t, c):
        for k in range(TOP_K):
            row = dest_ref[k, t]
            pltpu.make_async_copy(ys_hbm.at[pl.ds(row, 1), :], buf.at[k, pl.ds(t, 1), :], sem).start(priority=k % 2)
        return c

    lax.fori_loop(0, tt, body, 0)

    xb = _unpack_rows(x1p_ref[...])
    hg = jnp.dot(xb, wsg_ref[...], preferred_element_type=F32)
    hu = jnp.dot(xb, wsu_ref[...], preferred_element_type=F32)
    shared = jnp.dot((hg * _sigmoid(hg) * hu).astype(BF16), wsd_ref[...], preferred_element_type=F32)
    pgate = _sigmoid(jnp.dot(xb, wpg_ref[...], preferred_element_type=F32))
    pproj = jnp.dot(p_ref[...].astype(BF16), wpp_ref[...], preferred_element_type=F32)
    rest = ALPHA * x1_ref[...] + shared + pgate * pproj

    pltpu.make_async_copy(buf, buf, sem).wait()
    wt = wtm_ref[...]
    r_lo = jnp.zeros((tt, buf.shape[2]), F32)
    r_hi = jnp.zeros((tt, buf.shape[2]), F32)
    for k in range(TOP_K):
        lo, hi = _unpack_pair(buf[k])
        wk = wt[:, k:k + 1]
        r_lo = r_lo + wk * lo
        r_hi = r_hi + wk * hi
    routed = jnp.concatenate([r_lo, r_hi], axis=1)
    x2 = _layer_norm(rest + routed, g_ref[...], b_ref[...])
    x2_ref[...] = x2
    x2b_ref[...] = x2.astype(BF16)


def _combine(dest, ys, x1, x1p, wtm, p, wsg, wsu, wsd, wpg, wpp, g, b):
    m, d = x1.shape
    dp = x1p.shape[1]
    tt = min(COMBINE_TT, m)

    def full(a):
        return pl.BlockSpec(a.shape, lambda i: (0,) * a.ndim)

    def rows(a):
        return pl.BlockSpec((tt, a.shape[1]), lambda i: (i, 0))

    return pl.pallas_call(
        _combine_kernel,
        grid=(m // tt,),
        in_specs=[pl.BlockSpec((TOP_K, tt), lambda i: (0, i), memory_space=pltpu.SMEM),
                  pl.BlockSpec(memory_space=pl.ANY),
                  rows(x1), rows(x1p), rows(wtm), rows(p),
                  full(wsg), full(wsu), full(wsd), full(wpg), full(wpp), full(g), full(b)],
        out_specs=[rows(x1), rows(x1)],
        out_shape=[jax.ShapeDtypeStruct((m, d), F32), jax.ShapeDtypeStruct((m, d), BF16)],
        scratch_shapes=[pltpu.VMEM((TOP_K, tt, dp), U32), pltpu.SemaphoreType.DMA],
        compiler_params=_params("arbitrary"),
        name="combine",
    )(dest, ys, x1, x1p, wtm, p, wsg, wsu, wsd, wpg, wpp, g, b)


def kernel(x, p, w_in, conv_w, b_igate, b_fgate, mlstm_norm_g, w_branch_a, w_branch_b, w_out, ln1_g, ln1_b, w_router, router_bias, w_exp_gate, w_exp_up, w_exp_down, w_sh_gate, w_sh_up, w_sh_down, w_ple_gate, w_ple_proj, ln2_g, ln2_b):
    bsz, seq, d = x.shape
    m = bsz * seq
    depth = w_in.shape[0]
    hw = HEADS * HEAD_DIM
    if_lo = 3 * d + 4 * hw
    if_hi = if_lo + 2 * HEADS

    xf = x.reshape(m, d)
    xb = xf.astype(BF16)
    for i in range(depth):
        w_main = jnp.concatenate([w_in[i][:, :if_lo], w_in[i][:, if_hi:]], axis=1).astype(BF16)
        w_if = jnp.pad(w_in[i][:, if_lo:if_hi], ((0, 0), (0, LANES - 2 * HEADS))).astype(BF16)
        gate_bias = jnp.pad(jnp.concatenate([b_igate[i], b_fgate[i]]), (0, LANES - 2 * HEADS)).reshape(1, LANES)

        z = _matmul(xb, w_main, BF16, INPROJ_TM, INPROJ_TN, "inproj").reshape(bsz, seq, -1)
        gates = _matmul(xb, w_if, F32, INPROJ_TM, LANES, "gateproj").reshape(bsz, seq, LANES)
        y_b = _mlstm(z, gates, gate_bias, mlstm_norm_g[i].reshape(1, hw))
        x1, x1p = _mix(z, y_b, xf.reshape(bsz, seq, d), conv_w[i],
                       w_branch_a[i].astype(BF16), w_branch_b[i].astype(BF16), w_out[i].astype(BF16),
                       ln1_g[i].reshape(1, d), ln1_b[i].reshape(1, d))
        x1 = x1.reshape(m, d)
        x1p = x1p.reshape(m, d // 2)

        idx, rank, wtm, cnt = _router(x1, w_router[i].T, router_bias[i].reshape(N_EXPERTS, 1))
        counts = cnt[:, 0].astype(I32)
        row_start = jnp.cumsum(counts) - counts
        dest = row_start[idx] + rank
        meta = _visit_metadata(counts, FFN_TM, m * TOP_K)
        xs = _dispatch(x1p, dest)
        ys = _ffn(xs, meta, w_exp_gate[i], w_exp_up[i], w_exp_down[i])
        xf, xb = _combine(dest, ys, x1, x1p, wtm, p[i].reshape(m, -1),
                          w_sh_gate[i].astype(BF16), w_sh_up[i].astype(BF16), w_sh_down[i].astype(BF16),
                          w_ple_gate[i].astype(BF16), w_ple_proj[i].astype(BF16),
                          ln2_g[i].reshape(1, d), ln2_b[i].reshape(1, d))
    return xf.reshape(bsz, seq, d)
```

```python
import jax
import jax.numpy as jnp
from jax import lax
from jax.experimental import pallas as pl
from jax.experimental.pallas import tpu as pltpu

F32 = jnp.float32
BF16 = jnp.bfloat16
U32 = jnp.uint32
I32 = jnp.int32

HEADS = 8
HEAD_DIM = 128
N_EXPERTS = 64
N_GROUPS = 8
GROUP_SIZE = N_EXPERTS // N_GROUPS
TOPK_GROUPS = 4
TOP_K = 8
ROUTED_SCALE = 2.5
DEPTH = 4
ALPHA = (2 * DEPTH) ** 0.25
LN_EPS = 1e-5
RMS_EPS = 1e-6
QK_SCALE = HEAD_DIM ** -0.5

LANES = 128
VMEM_LIMIT = 56 * 1024 * 1024
NEG_INF = float("-inf")

MLSTM_CHUNK = 256
INPROJ_TM, INPROJ_TN = 2048, 1024
MIX_TS = 512
ROUTER_TT = 512
DISPATCH_TT = 1024
FFN_TM = 512
COMBINE_TT = 256


def _params(*sem):
    return pltpu.CompilerParams(dimension_semantics=sem, vmem_limit_bytes=VMEM_LIMIT)


def _sigmoid(x):
    return 1.0 / (1.0 + jnp.exp(-x))


def _layer_norm(r, g, b):
    mu = jnp.mean(r, axis=-1, keepdims=True)
    d = r - mu
    var = jnp.mean(d * d, axis=-1, keepdims=True)
    return d * lax.rsqrt(var + LN_EPS) * g + b


def _pack_pair(lo, hi):
    lo_w = lax.bitcast_convert_type(lo.astype(BF16).astype(F32), U32) >> 16
    hi_w = lax.bitcast_convert_type(hi.astype(BF16).astype(F32), U32) & jnp.uint32(0xFFFF0000)
    return hi_w | lo_w


def _unpack_pair(w):
    lo = lax.bitcast_convert_type(w << 16, F32)
    hi = lax.bitcast_convert_type(w & jnp.uint32(0xFFFF0000), F32)
    return lo, hi


def _unpack_rows(w):
    lo, hi = _unpack_pair(w)
    return jnp.concatenate([lo.astype(BF16), hi.astype(BF16)], axis=1)


def _mm_kernel(x_ref, w_ref, o_ref):
    o_ref[...] = jnp.dot(x_ref[...], w_ref[...], preferred_element_type=F32).astype(o_ref.dtype)


def _matmul(x, w, out_dtype, tm, tn, name, layer=None, n=None):
    m, k = x.shape
    n = w.shape[-1] if n is None else n
    tm = min(tm, m)
    if layer is None:
        w_spec = pl.BlockSpec((k, tn), lambda i, j: (0, j))
    else:
        w_spec = pl.BlockSpec((None, k, tn), lambda i, j: (layer, 0, j))
    return pl.pallas_call(
        _mm_kernel,
        grid=(m // tm, n // tn),
        in_specs=[pl.BlockSpec((tm, k), lambda i, j: (i, 0)), w_spec],
        out_specs=pl.BlockSpec((tm, tn), lambda i, j: (i, j)),
        out_shape=jax.ShapeDtypeStruct((m, n), out_dtype),
        compiler_params=_params("parallel", "parallel"),
        name=name,
    )(x, w)


def _mlstm_kernel(q_ref, k_ref, v_ref, o_ref, g_ref, gb_ref, ng_ref, y_ref, c_scr, m_scr):
    L = q_ref.shape[1]

    @pl.when(pl.program_id(1) == 0)
    def _():
        c_scr[...] = jnp.zeros_like(c_scr)
        m_scr[...] = jnp.zeros_like(m_scr)

    g = g_ref[0] + gb_ref[...]
    ig = g
    fg = pltpu.roll(g, LANES - HEADS, axis=1)
    log_f = jnp.minimum(fg, 0.0) - jnp.log1p(jnp.exp(-jnp.abs(fg)))
    row = lax.broadcasted_iota(I32, (L, L), 0)
    col = lax.broadcasted_iota(I32, (L, L), 1)
    causal = col <= row
    tri = jnp.where(causal, 1.0, 0.0).astype(F32)
    b = jnp.dot(tri, log_f, precision=lax.Precision.HIGHEST, preferred_element_type=F32)
    b_last = b[L - 1:L, :]
    m_prev = m_scr[...]
    a = b_last - b + ig
    m_loc = jnp.max(a, axis=0, keepdims=True)
    w_loc = jnp.exp(a - m_loc)
    m_new = jnp.maximum(b_last + m_prev, m_loc)
    sp = jnp.exp(b_last + m_prev - m_new)
    sl = jnp.exp(m_loc - m_new)
    log_inter = b + m_prev
    r_t = jnp.transpose(ig - b)

    ones_blk = jnp.ones((L, HEAD_DIM), BF16)
    for h in range(HEADS):
        sl_h = slice(h * HEAD_DIM, (h + 1) * HEAD_DIM)
        qh = q_ref[0, :, sl_h]
        kh = k_ref[0, :, sl_h]
        vh = v_ref[0, :, sl_h]
        v_aug = jnp.concatenate([vh, ones_blk], axis=1)
        s1 = lax.dot_general(qh, kh, (((1,), (1,)), ((), ())), preferred_element_type=F32)
        log_d = jnp.where(causal, b[:, h:h + 1] + r_t[h:h + 1, :], NEG_INF)
        li = log_inter[:, h:h + 1]
        m_out = jnp.maximum(li, jnp.max(log_d, axis=1, keepdims=True))
        d = jnp.exp(log_d - m_out)
        s = (s1 * QK_SCALE * d).astype(BF16)
        intra = jnp.dot(s, v_aug, preferred_element_type=F32)
        c_prev = c_scr[h]
        inter = jnp.dot(qh, c_prev.astype(BF16), preferred_element_type=F32) * QK_SCALE
        tot = intra + jnp.exp(li - m_out) * inter
        num = tot[:, :HEAD_DIM]
        den = tot[:, HEAD_DIM:]
        hh = num / jnp.maximum(jnp.abs(den), jnp.exp(-m_out))
        hh = hh * lax.rsqrt(jnp.mean(hh * hh, axis=1, keepdims=True) + RMS_EPS)
        og = _sigmoid(o_ref[0, :, sl_h].astype(F32))
        y_ref[0, :, sl_h] = (hh * ng_ref[:, sl_h] * og).astype(y_ref.dtype)
        kw = (kh.astype(F32) * w_loc[:, h:h + 1]).astype(BF16)
        c_loc = lax.dot_general(kw, v_aug, (((0,), (0,)), ((), ())), preferred_element_type=F32)
        c_scr[h] = sp[:, h:h + 1] * c_prev + sl[:, h:h + 1] * c_loc
    m_scr[...] = m_new


def _mlstm(z, gates, gate_bias, norm_g):
    bsz, seq, _ = z.shape
    L = MLSTM_CHUNK
    hw = HEADS * HEAD_DIM

    def zspec(cb):
        return pl.BlockSpec((1, L, hw), lambda b, c, cb=cb: (b, c, cb))

    return pl.pallas_call(
        _mlstm_kernel,
        grid=(bsz, seq // L),
        in_specs=[zspec(3), zspec(4), zspec(5), zspec(6),
                  pl.BlockSpec((1, L, LANES), lambda b, c: (b, c, 0)),
                  pl.BlockSpec((1, LANES), lambda b, c: (0, 0)),
                  pl.BlockSpec((1, hw), lambda b, c: (0, 0))],
        out_specs=pl.BlockSpec((1, L, hw), lambda b, c: (b, c, 0)),
        out_shape=jax.ShapeDtypeStruct((bsz, seq, hw), BF16),
        scratch_shapes=[pltpu.VMEM((HEADS, HEAD_DIM, 2 * HEAD_DIM), F32),
                        pltpu.VMEM((1, LANES), F32)],
        compiler_params=_params("parallel", "arbitrary"),
        name="mlstm",
    )(z, z, z, z, gates, gate_bias, norm_g)


def _mix_kernel(cin_ref, cout_ref, cval_ref, mg0_ref, mg1_ref, yb_ref, x_ref, cw_ref,
                wa_ref, wb_ref, wo_ref, g_ref, b_ref, x1_ref, x1p_ref, carry):
    ts = x_ref.shape[1]

    @pl.when(pl.program_id(1) == 0)
    def _():
        carry[...] = jnp.zeros_like(carry)

    u = cin_ref[0].astype(F32) * cval_ref[0].astype(F32)
    prev = carry[...]
    carry[...] = u[ts - 8:, :]
    r8 = lax.broadcasted_iota(I32, (8, u.shape[1]), 0)

    def shifted(k):
        body = pltpu.roll(u, k, axis=0)
        head = jnp.where(r8 < k, pltpu.roll(prev, k, axis=0), body[:8, :])
        return jnp.concatenate([head, body[8:, :]], axis=0)

    cw = cw_ref[...]
    conv = cw[0:1, :] * shifted(2) + cw[1:2, :] * shifted(1) + cw[2:3, :] * u
    y_a = (cout_ref[0].astype(F32) * conv).astype(BF16)
    pa = jnp.dot(y_a, wa_ref[...], preferred_element_type=F32)
    pb = jnp.dot(yb_ref[0], wb_ref[...], preferred_element_type=F32)
    mixed = _sigmoid(mg0_ref[0].astype(F32)) * pa + _sigmoid(mg1_ref[0].astype(F32)) * pb
    hmix = jnp.dot(mixed.astype(BF16), wo_ref[...], preferred_element_type=F32)
    x1 = _layer_norm(ALPHA * x_ref[0] + hmix, g_ref[...], b_ref[...])
    x1_ref[0] = x1
    half = x1.shape[1] // 2
    x1p_ref[0] = _pack_pair(x1[:, :half], x1[:, half:])


def _mix(z, zm, y_b, x, conv_w, wa, wb, wo, g, b):
    bsz, seq, d = x.shape
    ts = min(MIX_TS, seq)

    def zspec(cb):
        return pl.BlockSpec((1, ts, d), lambda i, j, cb=cb: (i, j, cb))

    def full(shape):
        return pl.BlockSpec(shape, lambda i, j: (0,) * len(shape))

    tile = pl.BlockSpec((1, ts, d), lambda i, j: (i, j, 0))
    ptile = pl.BlockSpec((1, ts, d // 2), lambda i, j: (i, j, 0))
    return pl.pallas_call(
        _mix_kernel,
        grid=(bsz, seq // ts),
        in_specs=[zspec(0), zspec(1), zspec(2), zspec(0), zspec(1), tile, tile,
                  full(conv_w.shape), full(wa.shape), full(wb.shape), full(wo.shape),
                  full(g.shape), full(b.shape)],
        out_specs=[tile, ptile],
        out_shape=[jax.ShapeDtypeStruct((bsz, seq, d), F32),
                   jax.ShapeDtypeStruct((bsz, seq, d // 2), U32)],
        scratch_shapes=[pltpu.VMEM((8, d), F32)],
        compiler_params=_params("parallel", "arbitrary"),
        name="mix",
    )(z, z, z, zm, zm, y_b, x, conv_w, wa, wb, wo, g, b)


def _router_kernel(x_ref, wrt_ref, rb_ref, idx_ref, rank_ref, wtm_ref, cnt_ref, carry):
    t = x_ref.shape[0]

    @pl.when(pl.program_id(0) == 0)
    def _():
        carry[...] = jnp.zeros_like(carry)

    logits = lax.dot_general(wrt_ref[...], x_ref[...], (((1,), (1,)), ((), ())),
                             precision=lax.Precision.HIGHEST, preferred_element_type=F32)
    scores = _sigmoid(logits)
    sel = (scores + rb_ref[...]).reshape(N_GROUPS, GROUP_SIZE, t)
    scores = scores.reshape(N_GROUPS, GROUP_SIZE, t)
    shape3 = (N_GROUPS, GROUP_SIZE, t)
    sub = lax.broadcasted_iota(I32, shape3, 1)
    grp = lax.broadcasted_iota(I32, shape3, 0)
    eidx = grp * GROUP_SIZE + sub
    m1 = jnp.max(sel, axis=1, keepdims=True)
    first = jnp.min(jnp.where(sel == m1, sub, GROUP_SIZE), axis=1, keepdims=True)
    m2 = jnp.max(jnp.where(sub == first, NEG_INF, sel), axis=1, keepdims=True)
    rem = m1 + m2
    gidx = lax.broadcasted_iota(I32, rem.shape, 0)
    gmask = jnp.zeros(rem.shape, F32)
    for _ in range(TOPK_GROUPS):
        mx = jnp.max(rem, axis=0, keepdims=True)
        pick = gidx == jnp.min(jnp.where(rem == mx, gidx, N_GROUPS), axis=0, keepdims=True)
        gmask = jnp.where(pick, 1.0, gmask)
        rem = jnp.where(pick, NEG_INF, rem)
    masked = jnp.where(jnp.broadcast_to(gmask, shape3) > 0.5, sel, NEG_INF)
    chosen = jnp.zeros(shape3, F32)
    picks = []
    for _ in range(TOP_K):
        mx = jnp.max(jnp.max(masked, axis=1, keepdims=True), axis=0, keepdims=True)
        cand = jnp.where(masked == mx, eidx, N_EXPERTS)
        fi = jnp.min(jnp.min(cand, axis=1, keepdims=True), axis=0, keepdims=True)
        pick = eidx == fi
        picks.append((fi, pick))
        chosen = jnp.where(pick, 1.0, chosen)
        masked = jnp.where(pick, NEG_INF, masked)
    w = chosen * scores
    denom = jnp.sum(jnp.sum(w, axis=1, keepdims=True), axis=0, keepdims=True)
    gate3 = w / denom * ROUTED_SCALE

    chosen2 = chosen.reshape(N_EXPERTS, t).astype(BF16)
    tok_r = lax.broadcasted_iota(I32, (t, t), 0)
    tok_c = lax.broadcasted_iota(I32, (t, t), 1)
    before = jnp.where(tok_r < tok_c, 1.0, 0.0).astype(BF16)
    prefix = (jnp.dot(chosen2, before, preferred_element_type=F32) + carry[...]).reshape(shape3)
    carry[...] += jnp.dot(chosen2, jnp.ones((t, t), BF16), preferred_element_type=F32)
    cnt_ref[...] = carry[:, :LANES]

    def pick_sum(pick, val):
        return jnp.sum(jnp.sum(jnp.where(pick, val, 0.0), axis=1, keepdims=True), axis=0, keepdims=True)

    sub8 = lax.broadcasted_iota(I32, (TOP_K, t), 0)
    idx8 = jnp.zeros((TOP_K, t), I32)
    rank8 = jnp.zeros((TOP_K, t), F32)
    w8 = jnp.zeros((TOP_K, t), F32)
    for k, (fi, pick) in enumerate(picks):
        idx8 = jnp.where(sub8 == k, jnp.broadcast_to(fi.reshape(1, t), (TOP_K, t)), idx8)
        rank8 = jnp.where(sub8 == k, jnp.broadcast_to(pick_sum(pick, prefix).reshape(1, t), (TOP_K, t)), rank8)
        w8 = jnp.where(sub8 == k, jnp.broadcast_to(pick_sum(pick, gate3).reshape(1, t), (TOP_K, t)), w8)
    idx_ref[...] = idx8
    rank_ref[...] = rank8.astype(I32)
    pad = jnp.zeros((LANES - TOP_K, t), F32)
    wtm_ref[...] = jnp.transpose(jnp.concatenate([w8, pad], axis=0))


def _router(x1, w_router_t, router_bias):
    m, d = x1.shape
    tt = min(ROUTER_TT, m)
    kt = pl.BlockSpec((TOP_K, tt), lambda i: (0, i))
    return pl.pallas_call(
        _router_kernel,
        grid=(m // tt,),
        in_specs=[pl.BlockSpec((tt, d), lambda i: (i, 0)),
                  pl.BlockSpec((N_EXPERTS, d), lambda i: (0, 0)),
                  pl.BlockSpec((N_EXPERTS, 1), lambda i: (0, 0))],
        out_specs=[kt, kt,
                   pl.BlockSpec((tt, LANES), lambda i: (i, 0)),
                   pl.BlockSpec((N_EXPERTS, LANES), lambda i: (0, 0))],
        out_shape=[jax.ShapeDtypeStruct((TOP_K, m), I32),
                   jax.ShapeDtypeStruct((TOP_K, m), I32),
                   jax.ShapeDtypeStruct((m, LANES), F32),
                   jax.ShapeDtypeStruct((N_EXPERTS, LANES), F32)],
        scratch_shapes=[pltpu.VMEM((N_EXPERTS, tt), F32)],
        compiler_params=_params("arbitrary"),
        name="router",
    )(x1, w_router_t, router_bias)


def _visit_metadata(counts, tm, n_rows):
    nt = n_rows // tm
    nv = nt + N_EXPERTS - 1
    ends = jnp.cumsum(counts)
    starts = ends - counts
    first_tile = starts // tm
    ntiles = jnp.where(counts > 0, (ends - 1) // tm - first_tile + 1, 0)
    vend = jnp.cumsum(ntiles)
    vstart = vend - ntiles
    v = jnp.arange(nv, dtype=I32)
    valid = v < vend[-1]
    ve = jnp.minimum(jnp.sum((v[:, None] >= vend[None, :]).astype(I32), axis=1), N_EXPERTS - 1)
    ve = jnp.where(valid, ve, ve[jnp.maximum(vend[-1] - 1, 0)])
    vt = jnp.where(valid, first_tile[ve] + v - vstart[ve], nt - 1)
    lo = jnp.where(valid, jnp.clip(starts[ve] - vt * tm, 0, tm), 0)
    hi = jnp.where(valid, jnp.clip(ends[ve] - vt * tm, 0, tm), 0)
    return ve.astype(I32), vt.astype(I32), lo.astype(I32), hi.astype(I32)


def _dispatch_kernel(dest_ref, x_ref, xs_hbm, sem):
    tt = x_ref.shape[0]

    def body(t, c):
        for k in range(TOP_K):
            row = dest_ref[k, t]
            pltpu.make_async_copy(x_ref.at[pl.ds(t, 1), :], xs_hbm.at[pl.ds(row, 1), :], sem).start(priority=k % 2)
        return c

    lax.fori_loop(0, tt, body, 0)
    done = xs_hbm.at[pl.ds(0, TOP_K * tt), :]
    pltpu.make_async_copy(done, done, sem).wait()


def _dispatch(x1p, dest):
    m, dp = x1p.shape
    tt = min(DISPATCH_TT, m)
    return pl.pallas_call(
        _dispatch_kernel,
        grid=(m // tt,),
        in_specs=[pl.BlockSpec((TOP_K, tt), lambda i: (0, i), memory_space=pltpu.SMEM),
                  pl.BlockSpec((tt, dp), lambda i: (i, 0))],
        out_specs=pl.BlockSpec(memory_space=pl.ANY),
        out_shape=jax.ShapeDtypeStruct((m * TOP_K, dp), U32),
        scratch_shapes=[pltpu.SemaphoreType.DMA],
        compiler_params=_params("arbitrary"),
        name="dispatch",
    )(dest, x1p)


def _ffn_kernel(ve_ref, vt_ref, lo_ref, hi_ref, xs_ref, wg_ref, wu_ref, wd_ref, ys_ref,
                wg_b, wu_b, wd_b, acc):
    v = pl.program_id(0)
    lo = lo_ref[v]
    hi = hi_ref[v]
    tm = xs_ref.shape[0]

    @pl.when((v == 0) | (ve_ref[v] != ve_ref[jnp.maximum(v - 1, 0)]))
    def _():
        wg_b[...] = wg_ref[0].astype(BF16)
        wu_b[...] = wu_ref[0].astype(BF16)
        wd_b[...] = wd_ref[0].astype(BF16)

    @pl.when(hi > lo)
    def _():
        x = _unpack_rows(xs_ref[...])
        hg = jnp.dot(x, wg_b[...], preferred_element_type=F32)
        hu = jnp.dot(x, wu_b[...], preferred_element_type=F32)
        r = lax.broadcasted_iota(I32, hg.shape, 0)
        mine = (r >= lo) & (r < hi)
        hmid = jnp.where(mine, hg * _sigmoid(hg) * hu, 0.0).astype(BF16)
        y = jnp.dot(hmid, wd_b[...], preferred_element_type=F32)

        @pl.when(lo == 0)
        def _():
            acc[...] = y

        @pl.when(lo > 0)
        def _():
            acc[...] += y

        @pl.when(hi == tm)
        def _():
            a = acc[...]
            half = a.shape[1] // 2
            ys_ref[...] = _pack_pair(a[:, :half], a[:, half:])


def _ffn(xs, meta, wg, wu, wd):
    n_rows, dp = xs.shape
    n_e, d, f = wg.shape
    tm = FFN_TM
    nv = n_rows // tm + N_EXPERTS - 1
    grid_spec = pltpu.PrefetchScalarGridSpec(
        num_scalar_prefetch=4,
        grid=(nv,),
        in_specs=[pl.BlockSpec((tm, dp), lambda v, ve, vt, lo, hi: (vt[v], 0)),
                  pl.BlockSpec((1, d, f), lambda v, ve, vt, lo, hi: (ve[v], 0, 0)),
                  pl.BlockSpec((1, d, f), lambda v, ve, vt, lo, hi: (ve[v], 0, 0)),
                  pl.BlockSpec((1, f, d), lambda v, ve, vt, lo, hi: (ve[v], 0, 0))],
        out_specs=pl.BlockSpec((tm, dp), lambda v, ve, vt, lo, hi: (vt[v], 0)),
        scratch_shapes=[pltpu.VMEM((d, f), BF16), pltpu.VMEM((d, f), BF16), pltpu.VMEM((f, d), BF16),
                        pltpu.VMEM((tm, d), F32)],
    )
    return pl.pallas_call(
        _ffn_kernel,
        grid_spec=grid_spec,
        out_shape=jax.ShapeDtypeStruct((n_rows, dp), U32),
        compiler_params=_params("arbitrary"),
        name="ffn",
    )(*meta, xs, wg, wu, wd)


def _combine_kernel(dest_ref, ys_hbm, x1_ref, x1p_ref, wtm_ref, p_ref, wsg_ref, wsu_ref, wsd_ref,
                    wpg_ref, wpp_ref, g_ref, b_ref, x2_ref, x2b_ref, buf, sem):
    tt = x1_ref.shape[0]

    def body(t, c):
        for k in range(TOP_K):
            row = dest_ref[k, t]
            pltpu.make_async_copy(ys_hbm.at[pl.ds(row, 1), :], buf.at[k, pl.ds(t, 1), :], sem).start(priority=k % 2)
        return c

    lax.fori_loop(0, tt, body, 0)

    xb = _unpack_rows(x1p_ref[...])
    hg = jnp.dot(xb, wsg_ref[...], preferred_element_type=F32)
    hu = jnp.dot(xb, wsu_ref[...], preferred_element_type=F32)
    shared = jnp.dot((hg * _sigmoid(hg) * hu).astype(BF16), wsd_ref[...], preferred_element_type=F32)
    pgate = _sigmoid(jnp.dot(xb, wpg_ref[...], preferred_element_type=F32))
    pproj = jnp.dot(p_ref[...].astype(BF16), wpp_ref[...], preferred_element_type=F32)
    rest = ALPHA * x1_ref[...] + shared + pgate * pproj

    pltpu.make_async_copy(buf, buf, sem).wait()
    wt = wtm_ref[...]
    r_lo = jnp.zeros((tt, buf.shape[2]), F32)
    r_hi = jnp.zeros((tt, buf.shape[2]), F32)
    for k in range(TOP_K):
        lo, hi = _unpack_pair(buf[k])
        wk = wt[:, k:k + 1]
        r_lo = r_lo + wk * lo
        r_hi = r_hi + wk * hi
    routed = jnp.concatenate([r_lo, r_hi], axis=1)
    x2 = _layer_norm(rest + routed, g_ref[...], b_ref[...])
    x2_ref[...] = x2
    x2b_ref[...] = x2.astype(BF16)


def _combine(dest, ys, x1, x1p, wtm, p, wsg, wsu, wsd, wpg, wpp, g, b):
    m, d = x1.shape
    dp = x1p.shape[1]
    tt = min(COMBINE_TT, m)

    def full(a):
        return pl.BlockSpec(a.shape, lambda i: (0,) * a.ndim)

    def rows(a):
        return pl.BlockSpec((tt, a.shape[1]), lambda i: (i, 0))

    return pl.pallas_call(
        _combine_kernel,
        grid=(m // tt,),
        in_specs=[pl.BlockSpec((TOP_K, tt), lambda i: (0, i), memory_space=pltpu.SMEM),
                  pl.BlockSpec(memory_space=pl.ANY),
                  rows(x1), rows(x1p), rows(wtm), rows(p),
                  full(wsg), full(wsu), full(wsd), full(wpg), full(wpp), full(g), full(b)],
        out_specs=[rows(x1), rows(x1)],
        out_shape=[jax.ShapeDtypeStruct((m, d), F32), jax.ShapeDtypeStruct((m, d), BF16)],
        scratch_shapes=[pltpu.VMEM((TOP_K, tt, dp), U32), pltpu.SemaphoreType.DMA],
        compiler_params=_params("arbitrary"),
        name="combine",
    )(dest, ys, x1, x1p, wtm, p, wsg, wsu, wsd, wpg, wpp, g, b)


def kernel(x, p, w_in, conv_w, b_igate, b_fgate, mlstm_norm_g, w_branch_a, w_branch_b, w_out, ln1_g, ln1_b, w_router, router_bias, w_exp_gate, w_exp_up, w_exp_down, w_sh_gate, w_sh_up, w_sh_down, w_ple_gate, w_ple_proj, ln2_g, ln2_b):
    bsz, seq, d = x.shape
    m = bsz * seq
    depth = w_in.shape[0]
    hw = HEADS * HEAD_DIM
    if_lo = 3 * d + 4 * hw
    if_hi = if_lo + 2 * HEADS

    xf = x.reshape(m, d)
    xb = xf.astype(BF16)
    w_in_b = w_in.astype(BF16)
    for i in range(depth):
        w_merge = w_in_b[i][:, if_hi:]
        w_if = jnp.pad(w_in_b[i][:, if_lo:if_hi], ((0, 0), (0, LANES - 2 * HEADS)))
        gate_bias = jnp.pad(jnp.concatenate([b_igate[i], b_fgate[i]]), (0, LANES - 2 * HEADS)).reshape(1, LANES)

        z = _matmul(xb, w_in_b, BF16, INPROJ_TM, INPROJ_TN, "inproj", layer=i, n=if_lo).reshape(bsz, seq, -1)
        zm = _matmul(xb, w_merge, BF16, INPROJ_TM, INPROJ_TN, "mergeproj").reshape(bsz, seq, -1)
        gates = _matmul(xb, w_if, F32, INPROJ_TM, LANES, "gateproj").reshape(bsz, seq, LANES)
        y_b = _mlstm(z, gates, gate_bias, mlstm_norm_g[i].reshape(1, hw))
        x1, x1p = _mix(z, zm, y_b, xf.reshape(bsz, seq, d), conv_w[i],
                       w_branch_a[i].astype(BF16), w_branch_b[i].astype(BF16), w_out[i].astype(BF16),
                       ln1_g[i].reshape(1, d), ln1_b[i].reshape(1, d))
        x1 = x1.reshape(m, d)
        x1p = x1p.reshape(m, d // 2)

        idx, rank, wtm, cnt = _router(x1, w_router[i].T, router_bias[i].reshape(N_EXPERTS, 1))
        counts = cnt[:, 0].astype(I32)
        row_start = jnp.cumsum(counts) - counts
        expert_ids = jnp.arange(N_EXPERTS, dtype=I32)[:, None, None]
        dest = rank + jnp.sum(jnp.where(idx[None] == expert_ids, row_start[:, None, None], 0), axis=0)
        meta = _visit_metadata(counts, FFN_TM, m * TOP_K)
        xs = _dispatch(x1p, dest)
        ys = _ffn(xs, meta, w_exp_gate[i], w_exp_up[i], w_exp_down[i])
        xf, xb = _combine(dest, ys, x1, x1p, wtm, p[i].reshape(m, -1),
                          w_sh_gate[i].astype(BF16), w_sh_up[i].astype(BF16), w_sh_down[i].astype(BF16),
                          w_ple_gate[i].astype(BF16), w_ple_proj[i].astype(BF16),
                          ln2_g[i].reshape(1, d), ln2_b[i].reshape(1, d))
    return xf.reshape(bsz, seq, d)
```

```python
import jax
import jax.numpy as jnp
from jax import lax
from jax.experimental import pallas as pl
from jax.experimental.pallas import tpu as pltpu
from jax.experimental.pallas import tpu_sc as plsc

F32 = jnp.float32
BF16 = jnp.bfloat16
U32 = jnp.uint32
I32 = jnp.int32

HEADS = 8
HEAD_DIM = 128
N_EXPERTS = 64
N_GROUPS = 8
GROUP_SIZE = N_EXPERTS // N_GROUPS
TOPK_GROUPS = 4
TOP_K = 8
ROUTED_SCALE = 2.5
DEPTH = 4
ALPHA = (2 * DEPTH) ** 0.25
LN_EPS = 1e-5
RMS_EPS = 1e-6
QK_SCALE = HEAD_DIM ** -0.5

LANES = 128
VMEM_LIMIT = 56 * 1024 * 1024
NEG_INF = float("-inf")

MLSTM_CHUNK = 256
INPROJ_TM, INPROJ_TN = 2048, 1024
MIX_TS = 512
ROUTER_TT = 512
FFN_TM = 512
COMBINE_TT = 256

SC_CORES = 2
SC_SUBCORES = 16
SC_WORKERS = SC_CORES * SC_SUBCORES
SC_ROWS = 128


def _params(*sem):
    return pltpu.CompilerParams(dimension_semantics=sem, vmem_limit_bytes=VMEM_LIMIT)


def _sigmoid(x):
    return 1.0 / (1.0 + jnp.exp(-x))


def _layer_norm(r, g, b):
    mu = jnp.mean(r, axis=-1, keepdims=True)
    d = r - mu
    var = jnp.mean(d * d, axis=-1, keepdims=True)
    return d * lax.rsqrt(var + LN_EPS) * g + b


def _pack_pair(lo, hi):
    lo_w = lax.bitcast_convert_type(lo.astype(BF16).astype(F32), U32) >> 16
    hi_w = lax.bitcast_convert_type(hi.astype(BF16).astype(F32), U32) & jnp.uint32(0xFFFF0000)
    return lax.bitcast_convert_type(hi_w | lo_w, I32)


def _unpack_pair(w):
    w = lax.bitcast_convert_type(w, U32)
    lo = lax.bitcast_convert_type(w << 16, F32)
    hi = lax.bitcast_convert_type(w & jnp.uint32(0xFFFF0000), F32)
    return lo, hi


def _unpack_rows(w):
    lo, hi = _unpack_pair(w)
    return jnp.concatenate([lo.astype(BF16), hi.astype(BF16)], axis=1)


def _mm_kernel(x_ref, w_ref, o_ref):
    o_ref[...] = jnp.dot(x_ref[...], w_ref[...], preferred_element_type=F32).astype(o_ref.dtype)


def _matmul(x, w, out_dtype, tm, tn, name, layer=None, n=None):
    m, k = x.shape
    n = w.shape[-1] if n is None else n
    tm = min(tm, m)
    if layer is None:
        w_spec = pl.BlockSpec((k, tn), lambda i, j: (0, j))
    else:
        w_spec = pl.BlockSpec((None, k, tn), lambda i, j: (layer, 0, j))
    return pl.pallas_call(
        _mm_kernel,
        grid=(m // tm, n // tn),
        in_specs=[pl.BlockSpec((tm, k), lambda i, j: (i, 0)), w_spec],
        out_specs=pl.BlockSpec((tm, tn), lambda i, j: (i, j)),
        out_shape=jax.ShapeDtypeStruct((m, n), out_dtype),
        compiler_params=_params("parallel", "parallel"),
        name=name,
    )(x, w)


def _mlstm_kernel(q_ref, k_ref, v_ref, o_ref, g_ref, gb_ref, ng_ref, y_ref, c_scr, m_scr):
    L = q_ref.shape[1]

    @pl.when(pl.program_id(1) == 0)
    def _():
        c_scr[...] = jnp.zeros_like(c_scr)
        m_scr[...] = jnp.zeros_like(m_scr)

    g = g_ref[0] + gb_ref[...]
    ig = g
    fg = pltpu.roll(g, LANES - HEADS, axis=1)
    log_f = jnp.minimum(fg, 0.0) - jnp.log1p(jnp.exp(-jnp.abs(fg)))
    row = lax.broadcasted_iota(I32, (L, L), 0)
    col = lax.broadcasted_iota(I32, (L, L), 1)
    causal = col <= row
    tri = jnp.where(causal, 1.0, 0.0).astype(F32)
    b = jnp.dot(tri, log_f, precision=lax.Precision.HIGHEST, preferred_element_type=F32)
    b_last = b[L - 1:L, :]
    m_prev = m_scr[...]
    a = b_last - b + ig
    m_loc = jnp.max(a, axis=0, keepdims=True)
    w_loc = jnp.exp(a - m_loc)
    m_new = jnp.maximum(b_last + m_prev, m_loc)
    sp = jnp.exp(b_last + m_prev - m_new)
    sl = jnp.exp(m_loc - m_new)
    log_inter = b + m_prev
    r_t = jnp.transpose(ig - b)

    ones_blk = jnp.ones((L, HEAD_DIM), BF16)
    for h in range(HEADS):
        sl_h = slice(h * HEAD_DIM, (h + 1) * HEAD_DIM)
        qh = q_ref[0, :, sl_h]
        kh = k_ref[0, :, sl_h]
        vh = v_ref[0, :, sl_h]
        v_aug = jnp.concatenate([vh, ones_blk], axis=1)
        s1 = lax.dot_general(qh, kh, (((1,), (1,)), ((), ())), preferred_element_type=F32)
        log_d = jnp.where(causal, b[:, h:h + 1] + r_t[h:h + 1, :], NEG_INF)
        li = log_inter[:, h:h + 1]
        m_out = jnp.maximum(li, jnp.max(log_d, axis=1, keepdims=True))
        d = jnp.exp(log_d - m_out)
        s = (s1 * QK_SCALE * d).astype(BF16)
        intra = jnp.dot(s, v_aug, preferred_element_type=F32)
        c_prev = c_scr[h]
        inter = jnp.dot(qh, c_prev.astype(BF16), preferred_element_type=F32) * QK_SCALE
        tot = intra + jnp.exp(li - m_out) * inter
        num = tot[:, :HEAD_DIM]
        den = tot[:, HEAD_DIM:]
        hh = num / jnp.maximum(jnp.abs(den), jnp.exp(-m_out))
        hh = hh * lax.rsqrt(jnp.mean(hh * hh, axis=1, keepdims=True) + RMS_EPS)
        og = _sigmoid(o_ref[0, :, sl_h].astype(F32))
        y_ref[0, :, sl_h] = (hh * ng_ref[:, sl_h] * og).astype(y_ref.dtype)
        kw = (kh.astype(F32) * w_loc[:, h:h + 1]).astype(BF16)
        c_loc = lax.dot_general(kw, v_aug, (((0,), (0,)), ((), ())), preferred_element_type=F32)
        c_scr[h] = sp[:, h:h + 1] * c_prev + sl[:, h:h + 1] * c_loc
    m_scr[...] = m_new


def _mlstm(z, gates, gate_bias, norm_g):
    bsz, seq, _ = z.shape
    L = MLSTM_CHUNK
    hw = HEADS * HEAD_DIM

    def zspec(cb):
        return pl.BlockSpec((1, L, hw), lambda b, c, cb=cb: (b, c, cb))

    return pl.pallas_call(
        _mlstm_kernel,
        grid=(bsz, seq // L),
        in_specs=[zspec(3), zspec(4), zspec(5), zspec(6),
                  pl.BlockSpec((1, L, LANES), lambda b, c: (b, c, 0)),
                  pl.BlockSpec((1, LANES), lambda b, c: (0, 0)),
                  pl.BlockSpec((1, hw), lambda b, c: (0, 0))],
        out_specs=pl.BlockSpec((1, L, hw), lambda b, c: (b, c, 0)),
        out_shape=jax.ShapeDtypeStruct((bsz, seq, hw), BF16),
        scratch_shapes=[pltpu.VMEM((HEADS, HEAD_DIM, 2 * HEAD_DIM), F32),
                        pltpu.VMEM((1, LANES), F32)],
        compiler_params=_params("parallel", "arbitrary"),
        name="mlstm",
    )(z, z, z, z, gates, gate_bias, norm_g)


def _mix_kernel(cin_ref, cout_ref, cval_ref, mg0_ref, mg1_ref, yb_ref, x_ref, cw_ref,
                wa_ref, wb_ref, wo_ref, g_ref, b_ref, x1_ref, x1p_ref, carry):
    ts = x_ref.shape[1]

    @pl.when(pl.program_id(1) == 0)
    def _():
        carry[...] = jnp.zeros_like(carry)

    u = cin_ref[0].astype(F32) * cval_ref[0].astype(F32)
    prev = carry[...]
    carry[...] = u[ts - 8:, :]
    r8 = lax.broadcasted_iota(I32, (8, u.shape[1]), 0)

    def shifted(k):
        body = pltpu.roll(u, k, axis=0)
        head = jnp.where(r8 < k, pltpu.roll(prev, k, axis=0), body[:8, :])
        return jnp.concatenate([head, body[8:, :]], axis=0)

    cw = cw_ref[...]
    conv = cw[0:1, :] * shifted(2) + cw[1:2, :] * shifted(1) + cw[2:3, :] * u
    y_a = (cout_ref[0].astype(F32) * conv).astype(BF16)
    pa = jnp.dot(y_a, wa_ref[...], preferred_element_type=F32)
    pb = jnp.dot(yb_ref[0], wb_ref[...], preferred_element_type=F32)
    mixed = _sigmoid(mg0_ref[0].astype(F32)) * pa + _sigmoid(mg1_ref[0].astype(F32)) * pb
    hmix = jnp.dot(mixed.astype(BF16), wo_ref[...], preferred_element_type=F32)
    x1 = _layer_norm(ALPHA * x_ref[0] + hmix, g_ref[...], b_ref[...])
    x1_ref[0] = x1
    half = x1.shape[1] // 2
    x1p_ref[0] = _pack_pair(x1[:, :half], x1[:, half:])


def _mix(z, zm, y_b, x, conv_w, wa, wb, wo, g, b):
    bsz, seq, d = x.shape
    ts = min(MIX_TS, seq)

    def zspec(cb):
        return pl.BlockSpec((1, ts, d), lambda i, j, cb=cb: (i, j, cb))

    def full(shape):
        return pl.BlockSpec(shape, lambda i, j: (0,) * len(shape))

    tile = pl.BlockSpec((1, ts, d), lambda i, j: (i, j, 0))
    ptile = pl.BlockSpec((1, ts, d // 2), lambda i, j: (i, j, 0))
    return pl.pallas_call(
        _mix_kernel,
        grid=(bsz, seq // ts),
        in_specs=[zspec(0), zspec(1), zspec(2), zspec(0), zspec(1), tile, tile,
                  full(conv_w.shape), full(wa.shape), full(wb.shape), full(wo.shape),
                  full(g.shape), full(b.shape)],
        out_specs=[tile, ptile],
        out_shape=[jax.ShapeDtypeStruct((bsz, seq, d), F32),
                   jax.ShapeDtypeStruct((bsz, seq, d // 2), I32)],
        scratch_shapes=[pltpu.VMEM((8, d), F32)],
        compiler_params=_params("parallel", "arbitrary"),
        name="mix",
    )(z, z, z, zm, zm, y_b, x, conv_w, wa, wb, wo, g, b)


def _router_kernel(x_ref, wrt_ref, rb_ref, idx_ref, rank_ref, wtm_ref, cnt_ref, carry):
    t = x_ref.shape[0]

    @pl.when(pl.program_id(0) == 0)
    def _():
        carry[...] = jnp.zeros_like(carry)

    logits = lax.dot_general(wrt_ref[...], x_ref[...], (((1,), (1,)), ((), ())),
                             precision=lax.Precision.HIGHEST, preferred_element_type=F32)
    scores = _sigmoid(logits)
    sel = (scores + rb_ref[...]).reshape(N_GROUPS, GROUP_SIZE, t)
    scores = scores.reshape(N_GROUPS, GROUP_SIZE, t)
    shape3 = (N_GROUPS, GROUP_SIZE, t)
    sub = lax.broadcasted_iota(I32, shape3, 1)
    grp = lax.broadcasted_iota(I32, shape3, 0)
    eidx = grp * GROUP_SIZE + sub
    m1 = jnp.max(sel, axis=1, keepdims=True)
    first = jnp.min(jnp.where(sel == m1, sub, GROUP_SIZE), axis=1, keepdims=True)
    m2 = jnp.max(jnp.where(sub == first, NEG_INF, sel), axis=1, keepdims=True)
    rem = m1 + m2
    gidx = lax.broadcasted_iota(I32, rem.shape, 0)
    gmask = jnp.zeros(rem.shape, F32)
    for _ in range(TOPK_GROUPS):
        mx = jnp.max(rem, axis=0, keepdims=True)
        pick = gidx == jnp.min(jnp.where(rem == mx, gidx, N_GROUPS), axis=0, keepdims=True)
        gmask = jnp.where(pick, 1.0, gmask)
        rem = jnp.where(pick, NEG_INF, rem)
    masked = jnp.where(jnp.broadcast_to(gmask, shape3) > 0.5, sel, NEG_INF)
    chosen = jnp.zeros(shape3, F32)
    picks = []
    for _ in range(TOP_K):
        mx = jnp.max(jnp.max(masked, axis=1, keepdims=True), axis=0, keepdims=True)
        cand = jnp.where(masked == mx, eidx, N_EXPERTS)
        fi = jnp.min(jnp.min(cand, axis=1, keepdims=True), axis=0, keepdims=True)
        pick = eidx == fi
        picks.append((fi, pick))
        chosen = jnp.where(pick, 1.0, chosen)
        masked = jnp.where(pick, NEG_INF, masked)
    w = chosen * scores
    denom = jnp.sum(jnp.sum(w, axis=1, keepdims=True), axis=0, keepdims=True)
    gate3 = w / denom * ROUTED_SCALE

    chosen2 = chosen.reshape(N_EXPERTS, t).astype(BF16)
    tok_r = lax.broadcasted_iota(I32, (t, t), 0)
    tok_c = lax.broadcasted_iota(I32, (t, t), 1)
    before = jnp.where(tok_r < tok_c, 1.0, 0.0).astype(BF16)
    prefix = (jnp.dot(chosen2, before, preferred_element_type=F32) + carry[...]).reshape(shape3)
    carry[...] += jnp.dot(chosen2, jnp.ones((t, t), BF16), preferred_element_type=F32)
    cnt_ref[...] = carry[:, :LANES]

    def pick_sum(pick, val):
        return jnp.sum(jnp.sum(jnp.where(pick, val, 0.0), axis=1, keepdims=True), axis=0, keepdims=True)

    sub8 = lax.broadcasted_iota(I32, (TOP_K, t), 0)
    idx8 = jnp.zeros((TOP_K, t), I32)
    rank8 = jnp.zeros((TOP_K, t), F32)
    w8 = jnp.zeros((TOP_K, t), F32)
    for k, (fi, pick) in enumerate(picks):
        idx8 = jnp.where(sub8 == k, jnp.broadcast_to(fi.reshape(1, t), (TOP_K, t)), idx8)
        rank8 = jnp.where(sub8 == k, jnp.broadcast_to(pick_sum(pick, prefix).reshape(1, t), (TOP_K, t)), rank8)
        w8 = jnp.where(sub8 == k, jnp.broadcast_to(pick_sum(pick, gate3).reshape(1, t), (TOP_K, t)), w8)
    idx_ref[...] = idx8
    rank_ref[...] = rank8.astype(I32)
    pad = jnp.zeros((LANES - TOP_K, t), F32)
    wtm_ref[...] = jnp.transpose(jnp.concatenate([w8, pad], axis=0))


def _router(x1, w_router_t, router_bias):
    m, d = x1.shape
    tt = min(ROUTER_TT, m)
    kt = pl.BlockSpec((TOP_K, tt), lambda i: (0, i))
    return pl.pallas_call(
        _router_kernel,
        grid=(m // tt,),
        in_specs=[pl.BlockSpec((tt, d), lambda i: (i, 0)),
                  pl.BlockSpec((N_EXPERTS, d), lambda i: (0, 0)),
                  pl.BlockSpec((N_EXPERTS, 1), lambda i: (0, 0))],
        out_specs=[kt, kt,
                   pl.BlockSpec((tt, LANES), lambda i: (i, 0)),
                   pl.BlockSpec((N_EXPERTS, LANES), lambda i: (0, 0))],
        out_shape=[jax.ShapeDtypeStruct((TOP_K, m), I32),
                   jax.ShapeDtypeStruct((TOP_K, m), I32),
                   jax.ShapeDtypeStruct((m, LANES), F32),
                   jax.ShapeDtypeStruct((N_EXPERTS, LANES), F32)],
        scratch_shapes=[pltpu.VMEM((N_EXPERTS, tt), F32)],
        compiler_params=_params("arbitrary"),
        name="router",
    )(x1, w_router_t, router_bias)


def _visit_metadata(counts, tm, n_rows):
    nt = n_rows // tm
    nv = nt + N_EXPERTS - 1
    ends = jnp.cumsum(counts)
    starts = ends - counts
    first_tile = starts // tm
    ntiles = jnp.where(counts > 0, (ends - 1) // tm - first_tile + 1, 0)
    vend = jnp.cumsum(ntiles)
    vstart = vend - ntiles
    v = jnp.arange(nv, dtype=I32)
    valid = v < vend[-1]
    ve = jnp.minimum(jnp.sum((v[:, None] >= vend[None, :]).astype(I32), axis=1), N_EXPERTS - 1)
    ve = jnp.where(valid, ve, ve[jnp.maximum(vend[-1] - 1, 0)])
    vt = jnp.where(valid, first_tile[ve] + v - vstart[ve], nt - 1)
    lo = jnp.where(valid, jnp.clip(starts[ve] - vt * tm, 0, tm), 0)
    hi = jnp.where(valid, jnp.clip(ends[ve] - vt * tm, 0, tm), 0)
    return ve.astype(I32), vt.astype(I32), lo.astype(I32), hi.astype(I32)


def _sc_mesh():
    return plsc.VectorSubcoreMesh(core_axis_name="c", subcore_axis_name="s")


def _sc_worker():
    return lax.axis_index("s") * SC_CORES + lax.axis_index("c")


def _dispatch(x1p, dest):
    m, dp = x1p.shape
    blocks = m // SC_ROWS
    per_worker = blocks // SC_WORKERS
    table = dest.reshape(TOP_K, blocks, SC_ROWS).transpose(1, 0, 2).reshape(blocks * TOP_K, SC_ROWS)

    def body(x_hbm, idx_hbm, xs_hbm, idx_v, rows_v, sem):
        first = _sc_worker() * per_worker

        @pl.loop(0, per_worker)
        def _(j):
            b = first + j
            pltpu.sync_copy(idx_hbm.at[pl.ds(pl.multiple_of(b * TOP_K, TOP_K), TOP_K)], idx_v)
            pltpu.sync_copy(x_hbm.at[pl.ds(pl.multiple_of(b * SC_ROWS, SC_ROWS), SC_ROWS)], rows_v)
            copies = [pltpu.async_copy(rows_v, xs_hbm.at[idx_v.at[k]], sem) for k in range(TOP_K)]
            for cp in copies:
                cp.wait()

    return pl.kernel(
        body,
        out_type=jax.ShapeDtypeStruct((m * TOP_K, dp), x1p.dtype),
        mesh=_sc_mesh(),
        scratch_types=[pltpu.VMEM((TOP_K, SC_ROWS), I32), pltpu.VMEM((SC_ROWS, dp), x1p.dtype),
                       pltpu.SemaphoreType.DMA],
        name="dispatch",
    )(x1p, table)


def _gather_rows(ys, dest):
    n_rows, dp = ys.shape
    idx = dest.reshape(n_rows // SC_ROWS, SC_ROWS)
    per_worker = n_rows // SC_ROWS // SC_WORKERS

    def body(ys_hbm, idx_hbm, out_hbm, idx_v, rows_v, sem):
        first = _sc_worker() * per_worker

        @pl.loop(0, per_worker)
        def _(j):
            b = first + j
            pltpu.sync_copy(idx_hbm.at[pl.ds(b, 1)], idx_v)
            pltpu.async_copy(ys_hbm.at[idx_v.at[0]], rows_v, sem).wait()
            pltpu.sync_copy(rows_v, out_hbm.at[pl.ds(pl.multiple_of(b * SC_ROWS, SC_ROWS), SC_ROWS)])

    return pl.kernel(
        body,
        out_type=jax.ShapeDtypeStruct((n_rows, dp), ys.dtype),
        mesh=_sc_mesh(),
        scratch_types=[pltpu.VMEM((1, SC_ROWS), I32), pltpu.VMEM((SC_ROWS, dp), ys.dtype),
                       pltpu.SemaphoreType.DMA],
        name="gather_rows",
    )(ys, idx)


def _ffn_kernel(ve_ref, vt_ref, lo_ref, hi_ref, xs_ref, wg_ref, wu_ref, wd_ref, ys_ref,
                wg_b, wu_b, wd_b, acc):
    v = pl.program_id(0)
    lo = lo_ref[v]
    hi = hi_ref[v]
    tm = xs_ref.shape[0]

    @pl.when((v == 0) | (ve_ref[v] != ve_ref[jnp.maximum(v - 1, 0)]))
    def _():
        wg_b[...] = wg_ref[...].astype(BF16)
        wu_b[...] = wu_ref[...].astype(BF16)
        wd_b[...] = wd_ref[...].astype(BF16)

    @pl.when(hi > lo)
    def _():
        x = _unpack_rows(xs_ref[...])
        hg = jnp.dot(x, wg_b[...], preferred_element_type=F32)
        hu = jnp.dot(x, wu_b[...], preferred_element_type=F32)
        r = lax.broadcasted_iota(I32, hg.shape, 0)
        mine = (r >= lo) & (r < hi)
        hmid = jnp.where(mine, hg * _sigmoid(hg) * hu, 0.0).astype(BF16)
        y = jnp.dot(hmid, wd_b[...], preferred_element_type=F32)

        @pl.when(lo == 0)
        def _():
            acc[...] = y

        @pl.when(lo > 0)
        def _():
            acc[...] += y

        @pl.when(hi == tm)
        def _():
            a = acc[...]
            half = a.shape[1] // 2
            ys_ref[...] = _pack_pair(a[:, :half], a[:, half:])


def _ffn(xs, meta, wg, wu, wd, layer):
    n_rows, dp = xs.shape
    _, n_e, d, f = wg.shape
    tm = FFN_TM
    nv = n_rows // tm + N_EXPERTS - 1
    grid_spec = pltpu.PrefetchScalarGridSpec(
        num_scalar_prefetch=4,
        grid=(nv,),
        in_specs=[pl.BlockSpec((tm, dp), lambda v, ve, vt, lo, hi: (vt[v], 0)),
                  pl.BlockSpec((None, None, d, f), lambda v, ve, vt, lo, hi: (layer, ve[v], 0, 0)),
                  pl.BlockSpec((None, None, d, f), lambda v, ve, vt, lo, hi: (layer, ve[v], 0, 0)),
                  pl.BlockSpec((None, None, f, d), lambda v, ve, vt, lo, hi: (layer, ve[v], 0, 0))],
        out_specs=pl.BlockSpec((tm, dp), lambda v, ve, vt, lo, hi: (vt[v], 0)),
        scratch_shapes=[pltpu.VMEM((d, f), BF16), pltpu.VMEM((d, f), BF16), pltpu.VMEM((f, d), BF16),
                        pltpu.VMEM((tm, d), F32)],
    )
    return pl.pallas_call(
        _ffn_kernel,
        grid_spec=grid_spec,
        out_shape=jax.ShapeDtypeStruct((n_rows, dp), I32),
        compiler_params=_params("arbitrary"),
        name="ffn",
    )(*meta, xs, wg, wu, wd)


def _combine_kernel(yk_ref, x1_ref, x1p_ref, wtm_ref, p_ref, wsg_ref, wsu_ref, wsd_ref,
                    wpg_ref, wpp_ref, g_ref, b_ref, x2_ref, x2b_ref):
    tt = x1_ref.shape[0]
    xb = _unpack_rows(x1p_ref[...])
    hg = jnp.dot(xb, wsg_ref[...], preferred_element_type=F32)
    hu = jnp.dot(xb, wsu_ref[...], preferred_element_type=F32)
    shared = jnp.dot((hg * _sigmoid(hg) * hu).astype(BF16), wsd_ref[...], preferred_element_type=F32)
    pgate = _sigmoid(jnp.dot(xb, wpg_ref[...], preferred_element_type=F32))
    pproj = jnp.dot(p_ref[...].astype(BF16), wpp_ref[...], preferred_element_type=F32)
    rest = ALPHA * x1_ref[...] + shared + pgate * pproj

    wt = wtm_ref[...]
    r_lo = jnp.zeros((tt, yk_ref.shape[2]), F32)
    r_hi = jnp.zeros((tt, yk_ref.shape[2]), F32)
    for k in range(TOP_K):
        lo, hi = _unpack_pair(yk_ref[k])
        wk = wt[:, k:k + 1]
        r_lo = r_lo + wk * lo
        r_hi = r_hi + wk * hi
    routed = jnp.concatenate([r_lo, r_hi], axis=1)
    x2 = _layer_norm(rest + routed, g_ref[...], b_ref[...])
    x2_ref[...] = x2
    x2b_ref[...] = x2.astype(BF16)


def _combine(yk, x1, x1p, wtm, p, wsg, wsu, wsd, wpg, wpp, g, b):
    m, d = x1.shape
    dp = x1p.shape[1]
    tt = min(COMBINE_TT, m)
    yk = yk.reshape(TOP_K, m, dp)

    def full(a):
        return pl.BlockSpec(a.shape, lambda i: (0,) * a.ndim)

    def rows(a):
        return pl.BlockSpec((tt, a.shape[1]), lambda i: (i, 0))

    return pl.pallas_call(
        _combine_kernel,
        grid=(m // tt,),
        in_specs=[pl.BlockSpec((TOP_K, tt, dp), lambda i: (0, i, 0)),
                  rows(x1), rows(x1p), rows(wtm), rows(p),
                  full(wsg), full(wsu), full(wsd), full(wpg), full(wpp), full(g), full(b)],
        out_specs=[rows(x1), rows(x1)],
        out_shape=[jax.ShapeDtypeStruct((m, d), F32), jax.ShapeDtypeStruct((m, d), BF16)],
        compiler_params=_params("parallel"),
        name="combine",
    )(yk, x1, x1p, wtm, p, wsg, wsu, wsd, wpg, wpp, g, b)


def kernel(x, p, w_in, conv_w, b_igate, b_fgate, mlstm_norm_g, w_branch_a, w_branch_b, w_out, ln1_g, ln1_b, w_router, router_bias, w_exp_gate, w_exp_up, w_exp_down, w_sh_gate, w_sh_up, w_sh_down, w_ple_gate, w_ple_proj, ln2_g, ln2_b):
    bsz, seq, d = x.shape
    m = bsz * seq
    depth = w_in.shape[0]
    hw = HEADS * HEAD_DIM
    if_lo = 3 * d + 4 * hw
    if_hi = if_lo + 2 * HEADS

    xf = x.reshape(m, d)
    xb = xf.astype(BF16)
    w_in_b = w_in.astype(BF16)
    for i in range(depth):
        w_merge = w_in_b[i][:, if_hi:]
        w_if = jnp.pad(w_in_b[i][:, if_lo:if_hi], ((0, 0), (0, LANES - 2 * HEADS)))
        gate_bias = jnp.pad(jnp.concatenate([b_igate[i], b_fgate[i]]), (0, LANES - 2 * HEADS)).reshape(1, LANES)

        z = _matmul(xb, w_in_b, BF16, INPROJ_TM, INPROJ_TN, "inproj", layer=i, n=if_lo).reshape(bsz, seq, -1)
        zm = _matmul(xb, w_merge, BF16, INPROJ_TM, INPROJ_TN, "mergeproj").reshape(bsz, seq, -1)
        gates = _matmul(xb, w_if, F32, INPROJ_TM, LANES, "gateproj").reshape(bsz, seq, LANES)
        y_b = _mlstm(z, gates, gate_bias, mlstm_norm_g[i].reshape(1, hw))
        x1, x1p = _mix(z, zm, y_b, xf.reshape(bsz, seq, d), conv_w[i],
                       w_branch_a[i].astype(BF16), w_branch_b[i].astype(BF16), w_out[i].astype(BF16),
                       ln1_g[i].reshape(1, d), ln1_b[i].reshape(1, d))
        x1 = x1.reshape(m, d)
        x1p = x1p.reshape(m, d // 2)

        idx, rank, wtm, cnt = _router(x1, w_router[i].T, router_bias[i].reshape(N_EXPERTS, 1))
        counts = cnt[:, 0].astype(I32)
        row_start = jnp.cumsum(counts) - counts
        expert_ids = jnp.arange(N_EXPERTS, dtype=I32)[:, None, None]
        dest = rank + jnp.sum(jnp.where(idx[None] == expert_ids, row_start[:, None, None], 0), axis=0)
        meta = _visit_metadata(counts, FFN_TM, m * TOP_K)
        xs = _dispatch(x1p, dest)
        ys = _ffn(xs, meta, w_exp_gate, w_exp_up, w_exp_down, i)
        yk = _gather_rows(ys, dest)
        xf, xb = _combine(yk, x1, x1p, wtm, p[i].reshape(m, -1),
                          w_sh_gate[i].astype(BF16), w_sh_up[i].astype(BF16), w_sh_down[i].astype(BF16),
                          w_ple_gate[i].astype(BF16), w_ple_proj[i].astype(BF16),
                          ln2_g[i].reshape(1, d), ln2_b[i].reshape(1, d))
    return xf.reshape(bsz, seq, d)
```

```python
import math

import jax
import jax.numpy as jnp
from jax import lax
from jax.experimental import pallas as pl
from jax.experimental.pallas import tpu as pltpu
from jax.experimental.pallas import tpu_sc as plsc

F32 = jnp.float32
BF16 = jnp.bfloat16
U32 = jnp.uint32
I32 = jnp.int32

HEADS = 8
HEAD_DIM = 128
N_EXPERTS = 64
N_GROUPS = 8
GROUP_SIZE = N_EXPERTS // N_GROUPS
TOPK_GROUPS = 4
TOP_K = 8
ROUTED_SCALE = 2.5
DEPTH = 4
ALPHA = (2 * DEPTH) ** 0.25
LN_EPS = 1e-5
RMS_EPS = 1e-6
QK_SCALE = HEAD_DIM ** -0.5
LOG_QK_SCALE = math.log(QK_SCALE)

LANES = 128
VMEM_LIMIT = 56 * 1024 * 1024
NEG_INF = float("-inf")

MLSTM_CHUNK = 256
INPROJ_TM, INPROJ_TN = 2048, 1024
MIX_TS = 512
ROUTER_TT = 512
FFN_TM = 512
COMBINE_TT = 256

SC_CORES = 2
SC_SUBCORES = 16
SC_WORKERS = SC_CORES * SC_SUBCORES
SC_ROWS = 128


def _params(*sem):
    return pltpu.CompilerParams(dimension_semantics=sem, vmem_limit_bytes=VMEM_LIMIT)


def _sigmoid(x):
    return 1.0 / (1.0 + jnp.exp(-x))


def _layer_norm(r, g, b):
    mu = jnp.mean(r, axis=-1, keepdims=True)
    d = r - mu
    var = jnp.mean(d * d, axis=-1, keepdims=True)
    return d * lax.rsqrt(var + LN_EPS) * g + b


def _pack_pair(lo, hi):
    return lax.bitcast_convert_type(pltpu.pack_elementwise([lo, hi], packed_dtype=BF16), I32)


def _unpack_pair(w):
    w = lax.bitcast_convert_type(w, U32)
    lo = pltpu.unpack_elementwise(w, index=0, packed_dtype=BF16, unpacked_dtype=F32)
    hi = pltpu.unpack_elementwise(w, index=1, packed_dtype=BF16, unpacked_dtype=F32)
    return lo, hi


def _unpack_rows(w):
    lo, hi = _unpack_pair(w)
    return jnp.concatenate([lo.astype(BF16), hi.astype(BF16)], axis=1)


def _mm_kernel(x_ref, w_ref, o_ref):
    o_ref[...] = jnp.dot(x_ref[...], w_ref[...], preferred_element_type=F32).astype(o_ref.dtype)


def _matmul(x, w, out_dtype, tm, tn, name, layer=None, n=None):
    m, k = x.shape
    n = w.shape[-1] if n is None else n
    tm = min(tm, m)
    if layer is None:
        w_spec = pl.BlockSpec((k, tn), lambda i, j: (0, j))
    else:
        w_spec = pl.BlockSpec((None, k, tn), lambda i, j: (layer, 0, j))
    return pl.pallas_call(
        _mm_kernel,
        grid=(m // tm, n // tn),
        in_specs=[pl.BlockSpec((tm, k), lambda i, j: (i, 0)), w_spec],
        out_specs=pl.BlockSpec((tm, tn), lambda i, j: (i, j)),
        out_shape=jax.ShapeDtypeStruct((m, n), out_dtype),
        compiler_params=_params("parallel", "parallel"),
        name=name,
    )(x, w)


def _mlstm_kernel(q_ref, k_ref, v_ref, o_ref, g_ref, gb_ref, ng_ref, y_ref, c_scr, m_scr):
    L = q_ref.shape[1]

    @pl.when(pl.program_id(1) == 0)
    def _():
        c_scr[...] = jnp.zeros_like(c_scr)
        m_scr[...] = jnp.zeros_like(m_scr)

    g = g_ref[0] + gb_ref[...]
    ig = g
    fg = pltpu.roll(g, LANES - HEADS, axis=1)
    log_f = jnp.minimum(fg, 0.0) - jnp.log1p(jnp.exp(-jnp.abs(fg)))
    row = lax.broadcasted_iota(I32, (L, L), 0)
    col = lax.broadcasted_iota(I32, (L, L), 1)
    causal = col <= row
    tri = jnp.where(causal, 1.0, 0.0).astype(F32)
    b = jnp.dot(tri, log_f, precision=lax.Precision.HIGHEST, preferred_element_type=F32)
    b_last = b[L - 1:L, :]
    m_prev = m_scr[...]
    a = b_last - b + ig
    m_loc = jnp.max(a, axis=0, keepdims=True)
    w_loc = jnp.exp(a - m_loc)
    m_new = jnp.maximum(b_last + m_prev, m_loc)
    sp = jnp.exp(b_last + m_prev - m_new)
    sl = jnp.exp(m_loc - m_new)
    log_inter = b + m_prev
    r = ig - b
    r_t = jnp.transpose(r)
    cm = r
    rows = lax.broadcasted_iota(I32, cm.shape, 0)
    step = 1
    while step < L:
        cm = jnp.maximum(cm, jnp.where(rows >= step, pltpu.roll(cm, step, axis=0), NEG_INF))
        step *= 2
    m_out = jnp.maximum(log_inter, b + cm)
    u = b - m_out + LOG_QK_SCALE
    e_inter = jnp.exp(log_inter - m_out + LOG_QK_SCALE)
    e_floor = jnp.exp(-m_out)

    ones_blk = jnp.ones((L, HEAD_DIM), BF16)
    for h in range(HEADS):
        sl_h = slice(h * HEAD_DIM, (h + 1) * HEAD_DIM)
        qh = q_ref[0, :, sl_h]
        kh = k_ref[0, :, sl_h]
        vh = v_ref[0, :, sl_h]
        v_aug = jnp.concatenate([vh, ones_blk], axis=1)
        s1 = lax.dot_general(qh, kh, (((1,), (1,)), ((), ())), preferred_element_type=F32)
        d = jnp.where(causal, jnp.exp(u[:, h:h + 1] + r_t[h:h + 1, :]), 0.0)
        s = (s1 * d).astype(BF16)
        intra = jnp.dot(s, v_aug, preferred_element_type=F32)
        c_prev = c_scr[h]
        inter = jnp.dot(qh, c_prev.astype(BF16), preferred_element_type=F32)
        tot = intra + e_inter[:, h:h + 1] * inter
        num = tot[:, :HEAD_DIM]
        den = tot[:, HEAD_DIM:]
        hh = num / jnp.maximum(jnp.abs(den), e_floor[:, h:h + 1])
        hh = hh * lax.rsqrt(jnp.mean(hh * hh, axis=1, keepdims=True) + RMS_EPS)
        og = _sigmoid(o_ref[0, :, sl_h].astype(F32))
        y_ref[0, :, sl_h] = (hh * ng_ref[:, sl_h] * og).astype(y_ref.dtype)
        kw = (kh.astype(F32) * w_loc[:, h:h + 1]).astype(BF16)
        c_loc = lax.dot_general(kw, v_aug, (((0,), (0,)), ((), ())), preferred_element_type=F32)
        c_scr[h] = sp[:, h:h + 1] * c_prev + sl[:, h:h + 1] * c_loc
    m_scr[...] = m_new


def _mlstm(z, gates, gate_bias, norm_g):
    bsz, seq, _ = z.shape
    L = MLSTM_CHUNK
    hw = HEADS * HEAD_DIM

    def zspec(cb):
        return pl.BlockSpec((1, L, hw), lambda b, c, cb=cb: (b, c, cb))

    return pl.pallas_call(
        _mlstm_kernel,
        grid=(bsz, seq // L),
        in_specs=[zspec(3), zspec(4), zspec(5), zspec(6),
                  pl.BlockSpec((1, L, LANES), lambda b, c: (b, c, 0)),
                  pl.BlockSpec((1, LANES), lambda b, c: (0, 0)),
                  pl.BlockSpec((1, hw), lambda b, c: (0, 0))],
        out_specs=pl.BlockSpec((1, L, hw), lambda b, c: (b, c, 0)),
        out_shape=jax.ShapeDtypeStruct((bsz, seq, hw), BF16),
        scratch_shapes=[pltpu.VMEM((HEADS, HEAD_DIM, 2 * HEAD_DIM), F32),
                        pltpu.VMEM((1, LANES), F32)],
        compiler_params=_params("parallel", "arbitrary"),
        name="mlstm",
    )(z, z, z, z, gates, gate_bias, norm_g)


def _mix_kernel(cin_ref, cout_ref, cval_ref, mg0_ref, mg1_ref, yb_ref, x_ref, cw_ref,
                wa_ref, wb_ref, wo_ref, g_ref, b_ref, x1_ref, x1p_ref, carry):
    ts = x_ref.shape[1]

    @pl.when(pl.program_id(1) == 0)
    def _():
        carry[...] = jnp.zeros_like(carry)

    u = cin_ref[0].astype(F32) * cval_ref[0].astype(F32)
    prev = carry[...]
    carry[...] = u[ts - 8:, :]
    r8 = lax.broadcasted_iota(I32, (8, u.shape[1]), 0)

    def shifted(k):
        body = pltpu.roll(u, k, axis=0)
        head = jnp.where(r8 < k, pltpu.roll(prev, k, axis=0), body[:8, :])
        return jnp.concatenate([head, body[8:, :]], axis=0)

    cw = cw_ref[...]
    conv = cw[0:1, :] * shifted(2) + cw[1:2, :] * shifted(1) + cw[2:3, :] * u
    y_a = (cout_ref[0].astype(F32) * conv).astype(BF16)
    pa = jnp.dot(y_a, wa_ref[...], preferred_element_type=F32)
    pb = jnp.dot(yb_ref[0], wb_ref[...], preferred_element_type=F32)
    mixed = _sigmoid(mg0_ref[0].astype(F32)) * pa + _sigmoid(mg1_ref[0].astype(F32)) * pb
    hmix = jnp.dot(mixed.astype(BF16), wo_ref[...], preferred_element_type=F32)
    x1 = _layer_norm(ALPHA * x_ref[0] + hmix, g_ref[...], b_ref[...])
    x1_ref[0] = x1
    half = x1.shape[1] // 2
    x1p_ref[0] = _pack_pair(x1[:, :half], x1[:, half:])


def _mix(z, zm, y_b, x, conv_w, wa, wb, wo, g, b):
    bsz, seq, d = x.shape
    ts = min(MIX_TS, seq)

    def zspec(cb):
        return pl.BlockSpec((1, ts, d), lambda i, j, cb=cb: (i, j, cb))

    def full(shape):
        return pl.BlockSpec(shape, lambda i, j: (0,) * len(shape))

    tile = pl.BlockSpec((1, ts, d), lambda i, j: (i, j, 0))
    ptile = pl.BlockSpec((1, ts, d // 2), lambda i, j: (i, j, 0))
    return pl.pallas_call(
        _mix_kernel,
        grid=(bsz, seq // ts),
        in_specs=[zspec(0), zspec(1), zspec(2), zspec(0), zspec(1), tile, tile,
                  full(conv_w.shape), full(wa.shape), full(wb.shape), full(wo.shape),
                  full(g.shape), full(b.shape)],
        out_specs=[tile, ptile],
        out_shape=[jax.ShapeDtypeStruct((bsz, seq, d), F32),
                   jax.ShapeDtypeStruct((bsz, seq, d // 2), I32)],
        scratch_shapes=[pltpu.VMEM((8, d), F32)],
        compiler_params=_params("parallel", "arbitrary"),
        name="mix",
    )(z, z, z, zm, zm, y_b, x, conv_w, wa, wb, wo, g, b)


def _router_kernel(x_ref, wrt_ref, rb_ref, idx_ref, rank_ref, wtm_ref, cnt_ref, carry):
    t = x_ref.shape[0]

    @pl.when(pl.program_id(0) == 0)
    def _():
        carry[...] = jnp.zeros_like(carry)

    logits = lax.dot_general(wrt_ref[...], x_ref[...], (((1,), (1,)), ((), ())),
                             precision=lax.Precision.HIGHEST, preferred_element_type=F32)
    scores = _sigmoid(logits)
    sel = (scores + rb_ref[...]).reshape(N_GROUPS, GROUP_SIZE, t)
    scores = scores.reshape(N_GROUPS, GROUP_SIZE, t)
    shape3 = (N_GROUPS, GROUP_SIZE, t)
    sub = lax.broadcasted_iota(I32, shape3, 1)
    grp = lax.broadcasted_iota(I32, shape3, 0)
    eidx = grp * GROUP_SIZE + sub
    m1 = jnp.max(sel, axis=1, keepdims=True)
    first = jnp.min(jnp.where(sel == m1, sub, GROUP_SIZE), axis=1, keepdims=True)
    m2 = jnp.max(jnp.where(sub == first, NEG_INF, sel), axis=1, keepdims=True)
    rem = m1 + m2
    gidx = lax.broadcasted_iota(I32, rem.shape, 0)
    gmask = jnp.zeros(rem.shape, F32)
    for _ in range(TOPK_GROUPS):
        mx = jnp.max(rem, axis=0, keepdims=True)
        pick = gidx == jnp.min(jnp.where(rem == mx, gidx, N_GROUPS), axis=0, keepdims=True)
        gmask = jnp.where(pick, 1.0, gmask)
        rem = jnp.where(pick, NEG_INF, rem)
    masked = jnp.where(jnp.broadcast_to(gmask, shape3) > 0.5, sel, NEG_INF)
    chosen = jnp.zeros(shape3, F32)
    picks = []
    for _ in range(TOP_K):
        mx = jnp.max(jnp.max(masked, axis=1, keepdims=True), axis=0, keepdims=True)
        cand = jnp.where(masked == mx, eidx, N_EXPERTS)
        fi = jnp.min(jnp.min(cand, axis=1, keepdims=True), axis=0, keepdims=True)
        pick = eidx == fi
        picks.append((fi, pick))
        chosen = jnp.where(pick, 1.0, chosen)
        masked = jnp.where(pick, NEG_INF, masked)
    w = chosen * scores
    denom = jnp.sum(jnp.sum(w, axis=1, keepdims=True), axis=0, keepdims=True)
    gate3 = w / denom * ROUTED_SCALE

    chosen2 = chosen.reshape(N_EXPERTS, t).astype(BF16)
    tok_r = lax.broadcasted_iota(I32, (t, t), 0)
    tok_c = lax.broadcasted_iota(I32, (t, t), 1)
    before = jnp.where(tok_r < tok_c, 1.0, 0.0).astype(BF16)
    prefix = (jnp.dot(chosen2, before, preferred_element_type=F32) + carry[...]).reshape(shape3)
    carry[...] += jnp.dot(chosen2, jnp.ones((t, t), BF16), preferred_element_type=F32)
    cnt_ref[...] = carry[:, :LANES]

    def pick_sum(pick, val):
        return jnp.sum(jnp.sum(jnp.where(pick, val, 0.0), axis=1, keepdims=True), axis=0, keepdims=True)

    sub8 = lax.broadcasted_iota(I32, (TOP_K, t), 0)
    idx8 = jnp.zeros((TOP_K, t), I32)
    rank8 = jnp.zeros((TOP_K, t), F32)
    w8 = jnp.zeros((TOP_K, t), F32)
    for k, (fi, pick) in enumerate(picks):
        idx8 = jnp.where(sub8 == k, jnp.broadcast_to(fi.reshape(1, t), (TOP_K, t)), idx8)
        rank8 = jnp.where(sub8 == k, jnp.broadcast_to(pick_sum(pick, prefix).reshape(1, t), (TOP_K, t)), rank8)
        w8 = jnp.where(sub8 == k, jnp.broadcast_to(pick_sum(pick, gate3).reshape(1, t), (TOP_K, t)), w8)
    idx_ref[...] = idx8
    rank_ref[...] = rank8.astype(I32)
    pad = jnp.zeros((LANES - TOP_K, t), F32)
    wtm_ref[...] = jnp.transpose(jnp.concatenate([w8, pad], axis=0))


def _router(x1, w_router_t, router_bias):
    m, d = x1.shape
    tt = min(ROUTER_TT, m)
    kt = pl.BlockSpec((TOP_K, tt), lambda i: (0, i))
    return pl.pallas_call(
        _router_kernel,
        grid=(m // tt,),
        in_specs=[pl.BlockSpec((tt, d), lambda i: (i, 0)),
                  pl.BlockSpec((N_EXPERTS, d), lambda i: (0, 0)),
                  pl.BlockSpec((N_EXPERTS, 1), lambda i: (0, 0))],
        out_specs=[kt, kt,
                   pl.BlockSpec((tt, LANES), lambda i: (i, 0)),
                   pl.BlockSpec((N_EXPERTS, LANES), lambda i: (0, 0))],
        out_shape=[jax.ShapeDtypeStruct((TOP_K, m), I32),
                   jax.ShapeDtypeStruct((TOP_K, m), I32),
                   jax.ShapeDtypeStruct((m, LANES), F32),
                   jax.ShapeDtypeStruct((N_EXPERTS, LANES), F32)],
        scratch_shapes=[pltpu.VMEM((N_EXPERTS, tt), F32)],
        compiler_params=_params("arbitrary"),
        name="router",
    )(x1, w_router_t, router_bias)


def _visit_metadata(counts, tm, n_rows):
    nt = n_rows // tm
    nv = nt + N_EXPERTS - 1
    ends = jnp.cumsum(counts)
    starts = ends - counts
    first_tile = starts // tm
    ntiles = jnp.where(counts > 0, (ends - 1) // tm - first_tile + 1, 0)
    vend = jnp.cumsum(ntiles)
    vstart = vend - ntiles
    v = jnp.arange(nv, dtype=I32)
    valid = v < vend[-1]
    ve = jnp.minimum(jnp.sum((v[:, None] >= vend[None, :]).astype(I32), axis=1), N_EXPERTS - 1)
    ve = jnp.where(valid, ve, ve[jnp.maximum(vend[-1] - 1, 0)])
    vt = jnp.where(valid, first_tile[ve] + v - vstart[ve], nt - 1)
    lo = jnp.where(valid, jnp.clip(starts[ve] - vt * tm, 0, tm), 0)
    hi = jnp.where(valid, jnp.clip(ends[ve] - vt * tm, 0, tm), 0)
    return ve.astype(I32), vt.astype(I32), lo.astype(I32), hi.astype(I32)


def _sc_mesh():
    return plsc.VectorSubcoreMesh(core_axis_name="c", subcore_axis_name="s")


def _sc_worker():
    return lax.axis_index("s") * SC_CORES + lax.axis_index("c")


def _dispatch(x1p, dest):
    m, dp = x1p.shape
    blocks = m // SC_ROWS
    per_worker = blocks // SC_WORKERS
    table = dest.reshape(TOP_K, blocks, SC_ROWS).transpose(1, 0, 2).reshape(blocks * TOP_K, SC_ROWS)

    def body(x_hbm, idx_hbm, xs_hbm, idx_v, rows_v, sem):
        first = _sc_worker() * per_worker

        @pl.loop(0, per_worker)
        def _(j):
            b = first + j
            pltpu.sync_copy(idx_hbm.at[pl.ds(pl.multiple_of(b * TOP_K, TOP_K), TOP_K)], idx_v)
            pltpu.sync_copy(x_hbm.at[pl.ds(pl.multiple_of(b * SC_ROWS, SC_ROWS), SC_ROWS)], rows_v)
            copies = [pltpu.async_copy(rows_v, xs_hbm.at[idx_v.at[k]], sem) for k in range(TOP_K)]
            for cp in copies:
                cp.wait()

    return pl.kernel(
        body,
        out_type=jax.ShapeDtypeStruct((m * TOP_K, dp), x1p.dtype),
        mesh=_sc_mesh(),
        scratch_types=[pltpu.VMEM((TOP_K, SC_ROWS), I32), pltpu.VMEM((SC_ROWS, dp), x1p.dtype),
                       pltpu.SemaphoreType.DMA],
        name="dispatch",
    )(x1p, table)


def _gather_rows(ys, dest):
    n_rows, dp = ys.shape
    idx = dest.reshape(n_rows // SC_ROWS, SC_ROWS)
    per_worker = n_rows // SC_ROWS // SC_WORKERS

    def body(ys_hbm, idx_hbm, out_hbm, idx_v, rows_v, sem):
        first = _sc_worker() * per_worker

        @pl.loop(0, per_worker)
        def _(j):
            b = first + j
            pltpu.sync_copy(idx_hbm.at[pl.ds(b, 1)], idx_v)
            pltpu.async_copy(ys_hbm.at[idx_v.at[0]], rows_v, sem).wait()
            pltpu.sync_copy(rows_v, out_hbm.at[pl.ds(pl.multiple_of(b * SC_ROWS, SC_ROWS), SC_ROWS)])

    return pl.kernel(
        body,
        out_type=jax.ShapeDtypeStruct((n_rows, dp), ys.dtype),
        mesh=_sc_mesh(),
        scratch_types=[pltpu.VMEM((1, SC_ROWS), I32), pltpu.VMEM((SC_ROWS, dp), ys.dtype),
                       pltpu.SemaphoreType.DMA],
        name="gather_rows",
    )(ys, idx)


def _ffn_kernel(ve_ref, vt_ref, lo_ref, hi_ref, xs_ref, wg_ref, wu_ref, wd_ref, ys_ref,
                wgu_b, wd_b, acc):
    v = pl.program_id(0)
    lo = lo_ref[v]
    hi = hi_ref[v]
    tm = xs_ref.shape[0]
    f = wg_ref.shape[1]

    @pl.when((v == 0) | (ve_ref[v] != ve_ref[jnp.maximum(v - 1, 0)]))
    def _():
        wgu_b[:, :f] = wg_ref[...].astype(BF16)
        wgu_b[:, f:] = wu_ref[...].astype(BF16)
        wd_b[...] = wd_ref[...].astype(BF16)

    def pack_rows(a):
        half = a.shape[1] // 2
        return _pack_pair(a[:, :half], a[:, half:])

    @pl.when(hi > lo)
    def _():
        x = _unpack_rows(xs_ref[...])
        h2 = jnp.dot(x, wgu_b[...], preferred_element_type=F32)
        hg = h2[:, :f]
        hu = h2[:, f:]
        r = lax.broadcasted_iota(I32, hg.shape, 0)
        mine = (r >= lo) & (r < hi)
        hmid = jnp.where(mine, hg * _sigmoid(hg) * hu, 0.0).astype(BF16)
        y = jnp.dot(hmid, wd_b[...], preferred_element_type=F32)

        @pl.when((lo == 0) & (hi == tm))
        def _():
            ys_ref[...] = pack_rows(y)

        @pl.when((lo == 0) & (hi < tm))
        def _():
            acc[...] = y

        @pl.when(lo > 0)
        def _():
            acc[...] += y

        @pl.when((lo > 0) & (hi == tm))
        def _():
            ys_ref[...] = pack_rows(acc[...])


def _ffn(xs, meta, wg, wu, wd, layer):
    n_rows, dp = xs.shape
    _, n_e, d, f = wg.shape
    tm = FFN_TM
    nv = n_rows // tm + N_EXPERTS - 1
    grid_spec = pltpu.PrefetchScalarGridSpec(
        num_scalar_prefetch=4,
        grid=(nv,),
        in_specs=[pl.BlockSpec((tm, dp), lambda v, ve, vt, lo, hi: (vt[v], 0)),
                  pl.BlockSpec((None, None, d, f), lambda v, ve, vt, lo, hi: (layer, ve[v], 0, 0)),
                  pl.BlockSpec((None, None, d, f), lambda v, ve, vt, lo, hi: (layer, ve[v], 0, 0)),
                  pl.BlockSpec((None, None, f, d), lambda v, ve, vt, lo, hi: (layer, ve[v], 0, 0))],
        out_specs=pl.BlockSpec((tm, dp), lambda v, ve, vt, lo, hi: (vt[v], 0)),
        scratch_shapes=[pltpu.VMEM((d, 2 * f), BF16), pltpu.VMEM((f, d), BF16), pltpu.VMEM((tm, d), F32)],
    )
    return pl.pallas_call(
        _ffn_kernel,
        grid_spec=grid_spec,
        out_shape=jax.ShapeDtypeStruct((n_rows, dp), I32),
        compiler_params=_params("arbitrary"),
        name="ffn",
    )(*meta, xs, wg, wu, wd)


def _combine_kernel(yk_ref, x1_ref, x1p_ref, wtm_ref, p_ref, wsg_ref, wsu_ref, wsd_ref,
                    wpg_ref, wpp_ref, g_ref, b_ref, x2_ref, x2b_ref):
    tt = x1_ref.shape[0]
    xb = _unpack_rows(x1p_ref[...])
    hg = jnp.dot(xb, wsg_ref[...], preferred_element_type=F32)
    hu = jnp.dot(xb, wsu_ref[...], preferred_element_type=F32)
    shared = jnp.dot((hg * _sigmoid(hg) * hu).astype(BF16), wsd_ref[...], preferred_element_type=F32)
    pgate = _sigmoid(jnp.dot(xb, wpg_ref[...], preferred_element_type=F32))
    pproj = jnp.dot(p_ref[...].astype(BF16), wpp_ref[...], preferred_element_type=F32)
    rest = ALPHA * x1_ref[...] + shared + pgate * pproj

    wt = wtm_ref[...]
    r_lo = jnp.zeros((tt, yk_ref.shape[2]), F32)
    r_hi = jnp.zeros((tt, yk_ref.shape[2]), F32)
    for k in range(TOP_K):
        lo, hi = _unpack_pair(yk_ref[k])
        wk = wt[:, k:k + 1]
        r_lo = r_lo + wk * lo
        r_hi = r_hi + wk * hi
    routed = jnp.concatenate([r_lo, r_hi], axis=1)
    x2 = _layer_norm(rest + routed, g_ref[...], b_ref[...])
    x2_ref[...] = x2
    x2b_ref[...] = x2.astype(BF16)


def _combine(yk, x1, x1p, wtm, p, wsg, wsu, wsd, wpg, wpp, g, b):
    m, d = x1.shape
    dp = x1p.shape[1]
    tt = min(COMBINE_TT, m)
    yk = yk.reshape(TOP_K, m, dp)

    def full(a):
        return pl.BlockSpec(a.shape, lambda i: (0,) * a.ndim)

    def rows(a):
        return pl.BlockSpec((tt, a.shape[1]), lambda i: (i, 0))

    return pl.pallas_call(
        _combine_kernel,
        grid=(m // tt,),
        in_specs=[pl.BlockSpec((TOP_K, tt, dp), lambda i: (0, i, 0)),
                  rows(x1), rows(x1p), rows(wtm), rows(p),
                  full(wsg), full(wsu), full(wsd), full(wpg), full(wpp), full(g), full(b)],
        out_specs=[rows(x1), rows(x1)],
        out_shape=[jax.ShapeDtypeStruct((m, d), F32), jax.ShapeDtypeStruct((m, d), BF16)],
        compiler_params=_params("parallel"),
        name="combine",
    )(yk, x1, x1p, wtm, p, wsg, wsu, wsd, wpg, wpp, g, b)


def kernel(x, p, w_in, conv_w, b_igate, b_fgate, mlstm_norm_g, w_branch_a, w_branch_b, w_out, ln1_g, ln1_b, w_router, router_bias, w_exp_gate, w_exp_up, w_exp_down, w_sh_gate, w_sh_up, w_sh_down, w_ple_gate, w_ple_proj, ln2_g, ln2_b):
    bsz, seq, d = x.shape
    m = bsz * seq
    depth = w_in.shape[0]
    hw = HEADS * HEAD_DIM
    if_lo = 3 * d + 4 * hw
    if_hi = if_lo + 2 * HEADS

    xf = x.reshape(m, d)
    xb = xf.astype(BF16)
    w_in_b = w_in.astype(BF16)
    for i in range(depth):
        w_merge = w_in_b[i][:, if_hi:]
        w_if = jnp.pad(w_in_b[i][:, if_lo:if_hi], ((0, 0), (0, LANES - 2 * HEADS)))
        gate_bias = jnp.pad(jnp.concatenate([b_igate[i], b_fgate[i]]), (0, LANES - 2 * HEADS)).reshape(1, LANES)

        z = _matmul(xb, w_in_b, BF16, INPROJ_TM, INPROJ_TN, "inproj", layer=i, n=if_lo).reshape(bsz, seq, -1)
        zm = _matmul(xb, w_merge, BF16, INPROJ_TM, INPROJ_TN, "mergeproj").reshape(bsz, seq, -1)
        gates = _matmul(xb, w_if, F32, INPROJ_TM, LANES, "gateproj").reshape(bsz, seq, LANES)
        y_b = _mlstm(z, gates, gate_bias, mlstm_norm_g[i].reshape(1, hw))
        x1, x1p = _mix(z, zm, y_b, xf.reshape(bsz, seq, d), conv_w[i],
                       w_branch_a[i].astype(BF16), w_branch_b[i].astype(BF16), w_out[i].astype(BF16),
                       ln1_g[i].reshape(1, d), ln1_b[i].reshape(1, d))
        x1 = x1.reshape(m, d)
        x1p = x1p.reshape(m, d // 2)

        idx, rank, wtm, cnt = _router(x1, w_router[i].T, router_bias[i].reshape(N_EXPERTS, 1))
        counts = cnt[:, 0].astype(I32)
        row_start = jnp.cumsum(counts) - counts
        expert_ids = jnp.arange(N_EXPERTS, dtype=I32)[:, None, None]
        dest = rank + jnp.sum(jnp.where(idx[None] == expert_ids, row_start[:, None, None], 0), axis=0)
        meta = _visit_metadata(counts, FFN_TM, m * TOP_K)
        xs = _dispatch(x1p, dest)
        ys = _ffn(xs, meta, w_exp_gate, w_exp_up, w_exp_down, i)
        yk = _gather_rows(ys, dest)
        xf, xb = _combine(yk, x1, x1p, wtm, p[i].reshape(m, -1),
                          w_sh_gate[i].astype(BF16), w_sh_up[i].astype(BF16), w_sh_down[i].astype(BF16),
                          w_ple_gate[i].astype(BF16), w_ple_proj[i].astype(BF16),
                          ln2_g[i].reshape(1, d), ln2_b[i].reshape(1, d))
    return xf.reshape(bsz, seq, d)
```

```python
import math

import jax
import jax.numpy as jnp
from jax import lax
from jax.experimental import pallas as pl
from jax.experimental.pallas import tpu as pltpu
from jax.experimental.pallas import tpu_sc as plsc

F32 = jnp.float32
BF16 = jnp.bfloat16
U32 = jnp.uint32
I32 = jnp.int32

HEADS = 8
HEAD_DIM = 128
N_EXPERTS = 64
N_GROUPS = 8
GROUP_SIZE = N_EXPERTS // N_GROUPS
TOPK_GROUPS = 4
TOP_K = 8
ROUTED_SCALE = 2.5
DEPTH = 4
ALPHA = (2 * DEPTH) ** 0.25
LN_EPS = 1e-5
RMS_EPS = 1e-6
QK_SCALE = HEAD_DIM ** -0.5
LOG_QK_SCALE = math.log(QK_SCALE)

LANES = 128
VMEM_LIMIT = 56 * 1024 * 1024
NEG_INF = float("-inf")

MLSTM_CHUNK = 256
INPROJ_TM, INPROJ_TN = 2048, 1024
MIX_TS = 512
ROUTER_TT = 512
FFN_TM = 1024
FFN_SUB = 256
COMBINE_TT = 256

SC_CORES = 2
SC_SUBCORES = 16
SC_WORKERS = SC_CORES * SC_SUBCORES
SC_ROWS = 128


def _params(*sem):
    return pltpu.CompilerParams(dimension_semantics=sem, vmem_limit_bytes=VMEM_LIMIT)


def _sigmoid(x):
    return 1.0 / (1.0 + jnp.exp(-x))


def _layer_norm(r, g, b):
    mu = jnp.mean(r, axis=-1, keepdims=True)
    d = r - mu
    var = jnp.mean(d * d, axis=-1, keepdims=True)
    return d * lax.rsqrt(var + LN_EPS) * g + b


def _pack_pair(lo, hi):
    return lax.bitcast_convert_type(pltpu.pack_elementwise([lo, hi], packed_dtype=BF16), I32)


def _unpack_pair(w):
    w = lax.bitcast_convert_type(w, U32)
    lo = pltpu.unpack_elementwise(w, index=0, packed_dtype=BF16, unpacked_dtype=F32)
    hi = pltpu.unpack_elementwise(w, index=1, packed_dtype=BF16, unpacked_dtype=F32)
    return lo, hi


def _unpack_rows(w):
    lo, hi = _unpack_pair(w)
    return jnp.concatenate([lo.astype(BF16), hi.astype(BF16)], axis=1)


def _mm_kernel(x_ref, w_ref, o_ref):
    o_ref[...] = jnp.dot(x_ref[...], w_ref[...], preferred_element_type=F32).astype(o_ref.dtype)


def _matmul(x, w, out_dtype, tm, tn, name, layer=None, n=None):
    m, k = x.shape
    n = w.shape[-1] if n is None else n
    tm = min(tm, m)
    if layer is None:
        w_spec = pl.BlockSpec((k, tn), lambda i, j: (0, j))
    else:
        w_spec = pl.BlockSpec((None, k, tn), lambda i, j: (layer, 0, j))
    return pl.pallas_call(
        _mm_kernel,
        grid=(m // tm, n // tn),
        in_specs=[pl.BlockSpec((tm, k), lambda i, j: (i, 0)), w_spec],
        out_specs=pl.BlockSpec((tm, tn), lambda i, j: (i, j)),
        out_shape=jax.ShapeDtypeStruct((m, n), out_dtype),
        compiler_params=_params("parallel", "parallel"),
        name=name,
    )(x, w)


def _mlstm_kernel(q_ref, k_ref, v_ref, o_ref, g_ref, gb_ref, ng_ref, y_ref, c_scr, m_scr):
    L = q_ref.shape[1]

    @pl.when(pl.program_id(1) == 0)
    def _():
        c_scr[...] = jnp.zeros_like(c_scr)
        m_scr[...] = jnp.zeros_like(m_scr)

    g = g_ref[0] + gb_ref[...]
    ig = g
    fg = pltpu.roll(g, LANES - HEADS, axis=1)
    log_f = jnp.minimum(fg, 0.0) - jnp.log1p(jnp.exp(-jnp.abs(fg)))
    row = lax.broadcasted_iota(I32, (L, L), 0)
    col = lax.broadcasted_iota(I32, (L, L), 1)
    causal = col <= row
    tri = jnp.where(causal, 1.0, 0.0).astype(F32)
    b = jnp.dot(tri, log_f, precision=lax.Precision.HIGHEST, preferred_element_type=F32)
    b_last = b[L - 1:L, :]
    m_prev = m_scr[...]
    a = b_last - b + ig
    m_loc = jnp.max(a, axis=0, keepdims=True)
    w_loc = jnp.exp(a - m_loc)
    m_new = jnp.maximum(b_last + m_prev, m_loc)
    sp = jnp.exp(b_last + m_prev - m_new)
    sl = jnp.exp(m_loc - m_new)
    log_inter = b + m_prev
    r = ig - b
    r_t = jnp.transpose(r)
    cm = r
    rows = lax.broadcasted_iota(I32, cm.shape, 0)
    step = 1
    while step < L:
        cm = jnp.maximum(cm, jnp.where(rows >= step, pltpu.roll(cm, step, axis=0), NEG_INF))
        step *= 2
    m_out = jnp.maximum(log_inter, b + cm)
    u = b - m_out + LOG_QK_SCALE
    e_inter = jnp.exp(log_inter - m_out + LOG_QK_SCALE)
    e_floor = jnp.exp(-m_out)

    ones_blk = jnp.ones((L, HEAD_DIM), BF16)
    heads = range(HEADS)
    lanes = [slice(h * HEAD_DIM, (h + 1) * HEAD_DIM) for h in heads]
    q = [q_ref[0, :, lanes[h]] for h in heads]
    k = [k_ref[0, :, lanes[h]] for h in heads]
    v_aug = [jnp.concatenate([v_ref[0, :, lanes[h]], ones_blk], axis=1) for h in heads]
    s1 = [lax.dot_general(q[h], k[h], (((1,), (1,)), ((), ())), preferred_element_type=F32) for h in heads]
    c_prev = [c_scr[h] for h in heads]
    inter = [jnp.dot(q[h], c_prev[h].astype(BF16), preferred_element_type=F32) for h in heads]
    s = [(s1[h] * jnp.where(causal, jnp.exp(u[:, h:h + 1] + r_t[h:h + 1, :]), 0.0)).astype(BF16) for h in heads]
    intra = [jnp.dot(s[h], v_aug[h], preferred_element_type=F32) for h in heads]
    tot = [intra[h] + e_inter[:, h:h + 1] * inter[h] for h in heads]
    hh = [tot[h][:, :HEAD_DIM] / jnp.maximum(jnp.abs(tot[h][:, HEAD_DIM:]), e_floor[:, h:h + 1]) for h in heads]
    hh = [hh[h] * lax.rsqrt(jnp.mean(hh[h] * hh[h], axis=1, keepdims=True) + RMS_EPS) for h in heads]
    for h in heads:
        og = _sigmoid(o_ref[0, :, lanes[h]].astype(F32))
        y_ref[0, :, lanes[h]] = (hh[h] * ng_ref[:, lanes[h]] * og).astype(y_ref.dtype)
    kw = [(k[h].astype(F32) * w_loc[:, h:h + 1]).astype(BF16) for h in heads]
    c_loc = [lax.dot_general(kw[h], v_aug[h], (((0,), (0,)), ((), ())), preferred_element_type=F32) for h in heads]
    for h in heads:
        c_scr[h] = sp[:, h:h + 1] * c_prev[h] + sl[:, h:h + 1] * c_loc[h]
    m_scr[...] = m_new


def _mlstm(z, gates, gate_bias, norm_g):
    bsz, seq, _ = z.shape
    L = MLSTM_CHUNK
    hw = HEADS * HEAD_DIM

    def zspec(cb):
        return pl.BlockSpec((1, L, hw), lambda b, c, cb=cb: (b, c, cb))

    return pl.pallas_call(
        _mlstm_kernel,
        grid=(bsz, seq // L),
        in_specs=[zspec(3), zspec(4), zspec(5), zspec(6),
                  pl.BlockSpec((1, L, LANES), lambda b, c: (b, c, 0)),
                  pl.BlockSpec((1, LANES), lambda b, c: (0, 0)),
                  pl.BlockSpec((1, hw), lambda b, c: (0, 0))],
        out_specs=pl.BlockSpec((1, L, hw), lambda b, c: (b, c, 0)),
        out_shape=jax.ShapeDtypeStruct((bsz, seq, hw), BF16),
        scratch_shapes=[pltpu.VMEM((HEADS, HEAD_DIM, 2 * HEAD_DIM), F32),
                        pltpu.VMEM((1, LANES), F32)],
        compiler_params=_params("parallel", "arbitrary"),
        name="mlstm",
    )(z, z, z, z, gates, gate_bias, norm_g)


def _mix_kernel(cin_ref, cout_ref, cval_ref, mg0_ref, mg1_ref, yb_ref, x_ref, cw_ref,
                wa_ref, wb_ref, wo_ref, g_ref, b_ref, x1_ref, x1p_ref, carry):
    ts = x_ref.shape[1]

    @pl.when(pl.program_id(1) == 0)
    def _():
        carry[...] = jnp.zeros_like(carry)

    u = cin_ref[0].astype(F32) * cval_ref[0].astype(F32)
    prev = carry[...]
    carry[...] = u[ts - 8:, :]
    r8 = lax.broadcasted_iota(I32, (8, u.shape[1]), 0)

    def shifted(k):
        body = pltpu.roll(u, k, axis=0)
        head = jnp.where(r8 < k, pltpu.roll(prev, k, axis=0), body[:8, :])
        return jnp.concatenate([head, body[8:, :]], axis=0)

    cw = cw_ref[...]
    conv = cw[0:1, :] * shifted(2) + cw[1:2, :] * shifted(1) + cw[2:3, :] * u
    y_a = (cout_ref[0].astype(F32) * conv).astype(BF16)
    pa = jnp.dot(y_a, wa_ref[...], preferred_element_type=F32)
    pb = jnp.dot(yb_ref[0], wb_ref[...], preferred_element_type=F32)
    mixed = _sigmoid(mg0_ref[0].astype(F32)) * pa + _sigmoid(mg1_ref[0].astype(F32)) * pb
    hmix = jnp.dot(mixed.astype(BF16), wo_ref[...], preferred_element_type=F32)
    x1 = _layer_norm(ALPHA * x_ref[0] + hmix, g_ref[...], b_ref[...])
    x1_ref[0] = x1
    half = x1.shape[1] // 2
    x1p_ref[0] = _pack_pair(x1[:, :half], x1[:, half:])


def _mix(z, zm, y_b, x, conv_w, wa, wb, wo, g, b):
    bsz, seq, d = x.shape
    ts = min(MIX_TS, seq)

    def zspec(cb):
        return pl.BlockSpec((1, ts, d), lambda i, j, cb=cb: (i, j, cb))

    def full(shape):
        return pl.BlockSpec(shape, lambda i, j: (0,) * len(shape))

    tile = pl.BlockSpec((1, ts, d), lambda i, j: (i, j, 0))
    ptile = pl.BlockSpec((1, ts, d // 2), lambda i, j: (i, j, 0))
    return pl.pallas_call(
        _mix_kernel,
        grid=(bsz, seq // ts),
        in_specs=[zspec(0), zspec(1), zspec(2), zspec(0), zspec(1), tile, tile,
                  full(conv_w.shape), full(wa.shape), full(wb.shape), full(wo.shape),
                  full(g.shape), full(b.shape)],
        out_specs=[tile, ptile],
        out_shape=[jax.ShapeDtypeStruct((bsz, seq, d), F32),
                   jax.ShapeDtypeStruct((bsz, seq, d // 2), I32)],
        scratch_shapes=[pltpu.VMEM((8, d), F32)],
        compiler_params=_params("parallel", "arbitrary"),
        name="mix",
    )(z, z, z, zm, zm, y_b, x, conv_w, wa, wb, wo, g, b)


def _router_kernel(x_ref, wrt_ref, rb_ref, idx_ref, rank_ref, wtm_ref, cnt_ref, carry):
    t = x_ref.shape[0]

    @pl.when(pl.program_id(0) == 0)
    def _():
        carry[...] = jnp.zeros_like(carry)

    logits = lax.dot_general(wrt_ref[...], x_ref[...], (((1,), (1,)), ((), ())),
                             precision=lax.Precision.HIGHEST, preferred_element_type=F32)
    scores = _sigmoid(logits)
    sel = (scores + rb_ref[...]).reshape(N_GROUPS, GROUP_SIZE, t)
    scores = scores.reshape(N_GROUPS, GROUP_SIZE, t)
    shape3 = (N_GROUPS, GROUP_SIZE, t)
    sub = lax.broadcasted_iota(I32, shape3, 1)
    grp = lax.broadcasted_iota(I32, shape3, 0)
    eidx = grp * GROUP_SIZE + sub
    m1 = jnp.max(sel, axis=1, keepdims=True)
    first = jnp.min(jnp.where(sel == m1, sub, GROUP_SIZE), axis=1, keepdims=True)
    m2 = jnp.max(jnp.where(sub == first, NEG_INF, sel), axis=1, keepdims=True)
    rem = m1 + m2
    gidx = lax.broadcasted_iota(I32, rem.shape, 0)
    gmask = jnp.zeros(rem.shape, F32)
    for _ in range(TOPK_GROUPS):
        mx = jnp.max(rem, axis=0, keepdims=True)
        pick = gidx == jnp.min(jnp.where(rem == mx, gidx, N_GROUPS), axis=0, keepdims=True)
        gmask = jnp.where(pick, 1.0, gmask)
        rem = jnp.where(pick, NEG_INF, rem)
    masked = jnp.where(jnp.broadcast_to(gmask, shape3) > 0.5, sel, NEG_INF)
    chosen = jnp.zeros(shape3, F32)
    picks = []
    for _ in range(TOP_K):
        mx = jnp.max(jnp.max(masked, axis=1, keepdims=True), axis=0, keepdims=True)
        cand = jnp.where(masked == mx, eidx, N_EXPERTS)
        fi = jnp.min(jnp.min(cand, axis=1, keepdims=True), axis=0, keepdims=True)
        pick = eidx == fi
        picks.append((fi, pick))
        chosen = jnp.where(pick, 1.0, chosen)
        masked = jnp.where(pick, NEG_INF, masked)
    w = chosen * scores
    denom = jnp.sum(jnp.sum(w, axis=1, keepdims=True), axis=0, keepdims=True)
    gate3 = w / denom * ROUTED_SCALE

    chosen2 = chosen.reshape(N_EXPERTS, t).astype(BF16)
    tok_r = lax.broadcasted_iota(I32, (t, t), 0)
    tok_c = lax.broadcasted_iota(I32, (t, t), 1)
    before = jnp.where(tok_r < tok_c, 1.0, 0.0).astype(BF16)
    prefix = (jnp.dot(chosen2, before, preferred_element_type=F32) + carry[...]).reshape(shape3)
    carry[...] += jnp.dot(chosen2, jnp.ones((t, t), BF16), preferred_element_type=F32)
    cnt_ref[...] = carry[:, :LANES]

    def pick_sum(pick, val):
        return jnp.sum(jnp.sum(jnp.where(pick, val, 0.0), axis=1, keepdims=True), axis=0, keepdims=True)

    sub8 = lax.broadcasted_iota(I32, (TOP_K, t), 0)
    idx8 = jnp.zeros((TOP_K, t), I32)
    rank8 = jnp.zeros((TOP_K, t), F32)
    w8 = jnp.zeros((TOP_K, t), F32)
    for k, (fi, pick) in enumerate(picks):
        idx8 = jnp.where(sub8 == k, jnp.broadcast_to(fi.reshape(1, t), (TOP_K, t)), idx8)
        rank8 = jnp.where(sub8 == k, jnp.broadcast_to(pick_sum(pick, prefix).reshape(1, t), (TOP_K, t)), rank8)
        w8 = jnp.where(sub8 == k, jnp.broadcast_to(pick_sum(pick, gate3).reshape(1, t), (TOP_K, t)), w8)
    idx_ref[...] = idx8
    rank_ref[...] = rank8.astype(I32)
    pad = jnp.zeros((LANES - TOP_K, t), F32)
    wtm_ref[...] = jnp.transpose(jnp.concatenate([w8, pad], axis=0))


def _router(x1, w_router_t, router_bias):
    m, d = x1.shape
    tt = min(ROUTER_TT, m)
    kt = pl.BlockSpec((TOP_K, tt), lambda i: (0, i))
    return pl.pallas_call(
        _router_kernel,
        grid=(m // tt,),
        in_specs=[pl.BlockSpec((tt, d), lambda i: (i, 0)),
                  pl.BlockSpec((N_EXPERTS, d), lambda i: (0, 0)),
                  pl.BlockSpec((N_EXPERTS, 1), lambda i: (0, 0))],
        out_specs=[kt, kt,
                   pl.BlockSpec((tt, LANES), lambda i: (i, 0)),
                   pl.BlockSpec((N_EXPERTS, LANES), lambda i: (0, 0))],
        out_shape=[jax.ShapeDtypeStruct((TOP_K, m), I32),
                   jax.ShapeDtypeStruct((TOP_K, m), I32),
                   jax.ShapeDtypeStruct((m, LANES), F32),
                   jax.ShapeDtypeStruct((N_EXPERTS, LANES), F32)],
        scratch_shapes=[pltpu.VMEM((N_EXPERTS, tt), F32)],
        compiler_params=_params("arbitrary"),
        name="router",
    )(x1, w_router_t, router_bias)


def _visit_metadata(counts, tm, n_rows):
    nt = n_rows // tm
    nv = nt + N_EXPERTS - 1
    ends = jnp.cumsum(counts)
    starts = ends - counts
    first_tile = starts // tm
    ntiles = jnp.where(counts > 0, (ends - 1) // tm - first_tile + 1, 0)
    vend = jnp.cumsum(ntiles)
    vstart = vend - ntiles
    v = jnp.arange(nv, dtype=I32)
    valid = v < vend[-1]
    ve = jnp.minimum(jnp.sum((v[:, None] >= vend[None, :]).astype(I32), axis=1), N_EXPERTS - 1)
    ve = jnp.where(valid, ve, ve[jnp.maximum(vend[-1] - 1, 0)])
    vt = jnp.where(valid, first_tile[ve] + v - vstart[ve], nt - 1)
    lo = jnp.where(valid, jnp.clip(starts[ve] - vt * tm, 0, tm), 0)
    hi = jnp.where(valid, jnp.clip(ends[ve] - vt * tm, 0, tm), 0)
    return ve.astype(I32), vt.astype(I32), lo.astype(I32), hi.astype(I32)


def _sc_mesh():
    return plsc.VectorSubcoreMesh(core_axis_name="c", subcore_axis_name="s")


def _sc_worker():
    return lax.axis_index("s") * SC_CORES + lax.axis_index("c")


def _dispatch(x1p, dest):
    m, dp = x1p.shape
    blocks = m // SC_ROWS
    per_worker = blocks // SC_WORKERS
    table = dest.reshape(TOP_K, blocks, SC_ROWS).transpose(1, 0, 2).reshape(blocks * TOP_K, SC_ROWS)

    def body(x_hbm, idx_hbm, xs_hbm, idx_v, rows_v, sem):
        first = _sc_worker() * per_worker

        @pl.loop(0, per_worker)
        def _(j):
            b = first + j
            pltpu.sync_copy(idx_hbm.at[pl.ds(pl.multiple_of(b * TOP_K, TOP_K), TOP_K)], idx_v)
            pltpu.sync_copy(x_hbm.at[pl.ds(pl.multiple_of(b * SC_ROWS, SC_ROWS), SC_ROWS)], rows_v)
            copies = [pltpu.async_copy(rows_v, xs_hbm.at[idx_v.at[k]], sem) for k in range(TOP_K)]
            for cp in copies:
                cp.wait()

    return pl.kernel(
        body,
        out_type=jax.ShapeDtypeStruct((m * TOP_K, dp), x1p.dtype),
        mesh=_sc_mesh(),
        scratch_types=[pltpu.VMEM((TOP_K, SC_ROWS), I32), pltpu.VMEM((SC_ROWS, dp), x1p.dtype),
                       pltpu.SemaphoreType.DMA],
        name="dispatch",
    )(x1p, table)


def _gather_rows(ys, dest):
    n_rows, dp = ys.shape
    idx = dest.reshape(n_rows // SC_ROWS, SC_ROWS)
    per_worker = n_rows // SC_ROWS // SC_WORKERS

    def body(ys_hbm, idx_hbm, out_hbm, idx_v, rows_v, sem):
        first = _sc_worker() * per_worker

        @pl.loop(0, per_worker)
        def _(j):
            b = first + j
            pltpu.sync_copy(idx_hbm.at[pl.ds(b, 1)], idx_v)
            pltpu.async_copy(ys_hbm.at[idx_v.at[0]], rows_v, sem).wait()
            pltpu.sync_copy(rows_v, out_hbm.at[pl.ds(pl.multiple_of(b * SC_ROWS, SC_ROWS), SC_ROWS)])

    return pl.kernel(
        body,
        out_type=jax.ShapeDtypeStruct((n_rows, dp), ys.dtype),
        mesh=_sc_mesh(),
        scratch_types=[pltpu.VMEM((1, SC_ROWS), I32), pltpu.VMEM((SC_ROWS, dp), ys.dtype),
                       pltpu.SemaphoreType.DMA],
        name="gather_rows",
    )(ys, idx)


def _ffn_kernel(ve_ref, vt_ref, lo_ref, hi_ref, xs_ref, wg_ref, wu_ref, wd_ref, ys_ref,
                wgu_b, wd_b, acc):
    v = pl.program_id(0)
    lo = lo_ref[v]
    hi = hi_ref[v]
    tm = xs_ref.shape[0]
    f = wg_ref.shape[1]

    @pl.when((v == 0) | (ve_ref[v] != ve_ref[jnp.maximum(v - 1, 0)]))
    def _():
        wgu_b[:, :f] = wg_ref[...].astype(BF16)
        wgu_b[:, f:] = wu_ref[...].astype(BF16)
        wd_b[...] = wd_ref[...].astype(BF16)

    def pack_rows(a):
        half = a.shape[1] // 2
        return _pack_pair(a[:, :half], a[:, half:])

    def sub_block(sb, carry):
        s0 = pl.multiple_of(sb * FFN_SUB, FFN_SUB)
        rows = pl.ds(s0, FFN_SUB)
        lo_s = jnp.clip(lo - s0, 0, FFN_SUB)
        hi_s = jnp.clip(hi - s0, 0, FFN_SUB)

        @pl.when(hi_s > lo_s)
        def _():
            x = _unpack_rows(xs_ref[rows, :])
            h2 = jnp.dot(x, wgu_b[...], preferred_element_type=F32)
            hg = h2[:, :f]
            hu = h2[:, f:]
            r = lax.broadcasted_iota(I32, hg.shape, 0)
            mine = (r >= lo_s) & (r < hi_s)
            hmid = jnp.where(mine, hg * _sigmoid(hg) * hu, 0.0).astype(BF16)
            y = jnp.dot(hmid, wd_b[...], preferred_element_type=F32)

            @pl.when((lo_s == 0) & (hi_s == FFN_SUB))
            def _():
                ys_ref[rows, :] = pack_rows(y)

            @pl.when((lo_s == 0) & (hi_s < FFN_SUB))
            def _():
                acc[rows, :] = y

            @pl.when(lo_s > 0)
            def _():
                acc[rows, :] += y

            @pl.when((lo_s > 0) & (hi_s == FFN_SUB))
            def _():
                ys_ref[rows, :] = pack_rows(acc[rows, :])

        return carry

    lax.fori_loop(0, tm // FFN_SUB, sub_block, 0)


def _ffn(xs, meta, wg, wu, wd, layer):
    n_rows, dp = xs.shape
    _, n_e, d, f = wg.shape
    tm = FFN_TM
    nv = n_rows // tm + N_EXPERTS - 1
    grid_spec = pltpu.PrefetchScalarGridSpec(
        num_scalar_prefetch=4,
        grid=(nv,),
        in_specs=[pl.BlockSpec((tm, dp), lambda v, ve, vt, lo, hi: (vt[v], 0)),
                  pl.BlockSpec((None, None, d, f), lambda v, ve, vt, lo, hi: (layer, ve[v], 0, 0)),
                  pl.BlockSpec((None, None, d, f), lambda v, ve, vt, lo, hi: (layer, ve[v], 0, 0)),
                  pl.BlockSpec((None, None, f, d), lambda v, ve, vt, lo, hi: (layer, ve[v], 0, 0))],
        out_specs=pl.BlockSpec((tm, dp), lambda v, ve, vt, lo, hi: (vt[v], 0)),
        scratch_shapes=[pltpu.VMEM((d, 2 * f), BF16), pltpu.VMEM((f, d), BF16), pltpu.VMEM((tm, d), F32)],
    )
    return pl.pallas_call(
        _ffn_kernel,
        grid_spec=grid_spec,
        out_shape=jax.ShapeDtypeStruct((n_rows, dp), I32),
        compiler_params=_params("arbitrary"),
        name="ffn",
    )(*meta, xs, wg, wu, wd)


def _combine_kernel(yk_ref, x1_ref, x1p_ref, wtm_ref, p_ref, wsg_ref, wsu_ref, wsd_ref,
                    wpg_ref, wpp_ref, g_ref, b_ref, x2_ref, x2b_ref):
    tt = x1_ref.shape[0]
    xb = _unpack_rows(x1p_ref[...])
    hg = jnp.dot(xb, wsg_ref[...], preferred_element_type=F32)
    hu = jnp.dot(xb, wsu_ref[...], preferred_element_type=F32)
    shared = jnp.dot((hg * _sigmoid(hg) * hu).astype(BF16), wsd_ref[...], preferred_element_type=F32)
    pgate = _sigmoid(jnp.dot(xb, wpg_ref[...], preferred_element_type=F32))
    pproj = jnp.dot(p_ref[...].astype(BF16), wpp_ref[...], preferred_element_type=F32)
    rest = ALPHA * x1_ref[...] + shared + pgate * pproj

    wt = wtm_ref[...]
    r_lo = jnp.zeros((tt, yk_ref.shape[2]), F32)
    r_hi = jnp.zeros((tt, yk_ref.shape[2]), F32)
    for k in range(TOP_K):
        lo, hi = _unpack_pair(yk_ref[k])
        wk = wt[:, k:k + 1]
        r_lo = r_lo + wk * lo
        r_hi = r_hi + wk * hi
    routed = jnp.concatenate([r_lo, r_hi], axis=1)
    x2 = _layer_norm(rest + routed, g_ref[...], b_ref[...])
    x2_ref[...] = x2
    x2b_ref[...] = x2.astype(BF16)


def _combine(yk, x1, x1p, wtm, p, wsg, wsu, wsd, wpg, wpp, g, b):
    m, d = x1.shape
    dp = x1p.shape[1]
    tt = min(COMBINE_TT, m)
    yk = yk.reshape(TOP_K, m, dp)

    def full(a):
        return pl.BlockSpec(a.shape, lambda i: (0,) * a.ndim)

    def rows(a):
        return pl.BlockSpec((tt, a.shape[1]), lambda i: (i, 0))

    return pl.pallas_call(
        _combine_kernel,
        grid=(m // tt,),
        in_specs=[pl.BlockSpec((TOP_K, tt, dp), lambda i: (0, i, 0)),
                  rows(x1), rows(x1p), rows(wtm), rows(p),
                  full(wsg), full(wsu), full(wsd), full(wpg), full(wpp), full(g), full(b)],
        out_specs=[rows(x1), rows(x1)],
        out_shape=[jax.ShapeDtypeStruct((m, d), F32), jax.ShapeDtypeStruct((m, d), BF16)],
        compiler_params=_params("parallel"),
        name="combine",
    )(yk, x1, x1p, wtm, p, wsg, wsu, wsd, wpg, wpp, g, b)


def kernel(x, p, w_in, conv_w, b_igate, b_fgate, mlstm_norm_g, w_branch_a, w_branch_b, w_out, ln1_g, ln1_b, w_router, router_bias, w_exp_gate, w_exp_up, w_exp_down, w_sh_gate, w_sh_up, w_sh_down, w_ple_gate, w_ple_proj, ln2_g, ln2_b):
    bsz, seq, d = x.shape
    m = bsz * seq
    depth = w_in.shape[0]
    hw = HEADS * HEAD_DIM
    if_lo = 3 * d + 4 * hw
    if_hi = if_lo + 2 * HEADS

    xf = x.reshape(m, d)
    xb = xf.astype(BF16)
    w_in_b = w_in.astype(BF16)
    for i in range(depth):
        w_merge = w_in_b[i][:, if_hi:]
        w_if = jnp.pad(w_in_b[i][:, if_lo:if_hi], ((0, 0), (0, LANES - 2 * HEADS)))
        gate_bias = jnp.pad(jnp.concatenate([b_igate[i], b_fgate[i]]), (0, LANES - 2 * HEADS)).reshape(1, LANES)

        z = _matmul(xb, w_in_b, BF16, INPROJ_TM, INPROJ_TN, "inproj", layer=i, n=if_lo).reshape(bsz, seq, -1)
        zm = _matmul(xb, w_merge, BF16, INPROJ_TM, INPROJ_TN, "mergeproj").reshape(bsz, seq, -1)
        gates = _matmul(xb, w_if, F32, INPROJ_TM, LANES, "gateproj").reshape(bsz, seq, LANES)
        y_b = _mlstm(z, gates, gate_bias, mlstm_norm_g[i].reshape(1, hw))
        x1, x1p = _mix(z, zm, y_b, xf.reshape(bsz, seq, d), conv_w[i],
                       w_branch_a[i].astype(BF16), w_branch_b[i].astype(BF16), w_out[i].astype(BF16),
                       ln1_g[i].reshape(1, d), ln1_b[i].reshape(1, d))
        x1 = x1.reshape(m, d)
        x1p = x1p.reshape(m, d // 2)

        idx, rank, wtm, cnt = _router(x1, w_router[i].T, router_bias[i].reshape(N_EXPERTS, 1))
        counts = cnt[:, 0].astype(I32)
        row_start = jnp.cumsum(counts) - counts
        expert_ids = jnp.arange(N_EXPERTS, dtype=I32)[:, None, None]
        dest = rank + jnp.sum(jnp.where(idx[None] == expert_ids, row_start[:, None, None], 0), axis=0)
        meta = _visit_metadata(counts, FFN_TM, m * TOP_K)
        xs = _dispatch(x1p, dest)
        ys = _ffn(xs, meta, w_exp_gate, w_exp_up, w_exp_down, i)
        yk = _gather_rows(ys, dest)
        xf, xb = _combine(yk, x1, x1p, wtm, p[i].reshape(m, -1),
                          w_sh_gate[i].astype(BF16), w_sh_up[i].astype(BF16), w_sh_down[i].astype(BF16),
                          w_ple_gate[i].astype(BF16), w_ple_proj[i].astype(BF16),
                          ln2_g[i].reshape(1, d), ln2_b[i].reshape(1, d))
    return xf.reshape(bsz, seq, d)
```

```python
import math

import jax
import jax.numpy as jnp
from jax import lax
from jax.experimental import pallas as pl
from jax.experimental.pallas import tpu as pltpu
from jax.experimental.pallas import tpu_sc as plsc

F32 = jnp.float32
BF16 = jnp.bfloat16
U32 = jnp.uint32
I32 = jnp.int32

HEADS = 8
HEAD_DIM = 128
N_EXPERTS = 64
N_GROUPS = 8
GROUP_SIZE = N_EXPERTS // N_GROUPS
TOPK_GROUPS = 4
TOP_K = 8
ROUTED_SCALE = 2.5
DEPTH = 4
ALPHA = (2 * DEPTH) ** 0.25
LN_EPS = 1e-5
RMS_EPS = 1e-6
QK_SCALE = HEAD_DIM ** -0.5
LOG_QK_SCALE = math.log(QK_SCALE)

LANES = 128
VMEM_LIMIT = 56 * 1024 * 1024
NEG_INF = float("-inf")

MLSTM_CHUNK = 256
INPROJ_TM, INPROJ_TN = 2048, 1024
MIX_TS = 512
ROUTER_TT = 512
FFN_TM = 2048
FFN_SUB = 512
COMBINE_TT = 256

SC_CORES = 2
SC_SUBCORES = 16
SC_WORKERS = SC_CORES * SC_SUBCORES
SC_ROWS = 128


def _params(*sem):
    return pltpu.CompilerParams(dimension_semantics=sem, vmem_limit_bytes=VMEM_LIMIT)


def _sigmoid(x):
    return 1.0 / (1.0 + jnp.exp(-x))


def _layer_norm(r, g, b):
    mu = jnp.mean(r, axis=-1, keepdims=True)
    d = r - mu
    var = jnp.mean(d * d, axis=-1, keepdims=True)
    return d * lax.rsqrt(var + LN_EPS) * g + b


def _pack_pair(lo, hi):
    return lax.bitcast_convert_type(pltpu.pack_elementwise([lo, hi], packed_dtype=BF16), I32)


def _unpack_pair(w):
    w = lax.bitcast_convert_type(w, U32)
    lo = pltpu.unpack_elementwise(w, index=0, packed_dtype=BF16, unpacked_dtype=F32)
    hi = pltpu.unpack_elementwise(w, index=1, packed_dtype=BF16, unpacked_dtype=F32)
    return lo, hi


def _unpack_rows(w):
    lo, hi = _unpack_pair(w)
    return jnp.concatenate([lo.astype(BF16), hi.astype(BF16)], axis=1)


def _mm_kernel(x_ref, w_ref, o_ref):
    o_ref[...] = jnp.dot(x_ref[...], w_ref[...].astype(BF16), preferred_element_type=F32).astype(o_ref.dtype)


def _matmul(x, w, out_dtype, tm, tn, name, layer=None, n=None):
    m, k = x.shape
    n = w.shape[-1] if n is None else n
    tm = min(tm, m)
    if layer is None:
        w_spec = pl.BlockSpec((k, tn), lambda i, j: (0, j))
    else:
        w_spec = pl.BlockSpec((None, k, tn), lambda i, j: (layer, 0, j))
    return pl.pallas_call(
        _mm_kernel,
        grid=(m // tm, n // tn),
        in_specs=[pl.BlockSpec((tm, k), lambda i, j: (i, 0)), w_spec],
        out_specs=pl.BlockSpec((tm, tn), lambda i, j: (i, j)),
        out_shape=jax.ShapeDtypeStruct((m, n), out_dtype),
        compiler_params=_params("parallel", "parallel"),
        name=name,
    )(x, w)


def _mlstm_kernel(q_ref, k_ref, v_ref, o_ref, g_ref, gb_ref, ng_ref, y_ref, c_scr, m_scr):
    L = q_ref.shape[1]

    @pl.when(pl.program_id(1) == 0)
    def _():
        c_scr[...] = jnp.zeros_like(c_scr)
        m_scr[...] = jnp.zeros_like(m_scr)

    g = g_ref[0] + gb_ref[...]
    ig = g
    fg = pltpu.roll(g, LANES - HEADS, axis=1)
    log_f = jnp.minimum(fg, 0.0) - jnp.log(1.0 + jnp.exp(-jnp.abs(fg)))
    row = lax.broadcasted_iota(I32, (L, L), 0)
    col = lax.broadcasted_iota(I32, (L, L), 1)
    causal = col <= row
    rows = lax.broadcasted_iota(I32, log_f.shape, 0)

    def scan_rows(x, op, identity):
        step = 1
        while step < L:
            x = op(x, jnp.where(rows >= step, pltpu.roll(x, step, axis=0), identity))
            step *= 2
        return x

    b = scan_rows(log_f, jnp.add, 0.0)
    b_last = b[L - 1:L, :]
    m_prev = m_scr[...]
    a = b_last - b + ig
    m_loc = jnp.max(a, axis=0, keepdims=True)
    w_loc = jnp.exp(a - m_loc)
    m_new = jnp.maximum(b_last + m_prev, m_loc)
    sp = jnp.exp(b_last + m_prev - m_new)
    sl = jnp.exp(m_loc - m_new)
    log_inter = b + m_prev
    r = ig - b
    r_t = jnp.transpose(r)
    m_out = jnp.maximum(log_inter, b + scan_rows(r, jnp.maximum, NEG_INF))
    u = b - m_out + LOG_QK_SCALE
    e_inter = jnp.exp(log_inter - m_out + LOG_QK_SCALE)
    e_floor = jnp.exp(-m_out)

    ones_blk = jnp.ones((L, HEAD_DIM), BF16)
    heads = range(HEADS)
    lanes = [slice(h * HEAD_DIM, (h + 1) * HEAD_DIM) for h in heads]
    q = [q_ref[0, :, lanes[h]] for h in heads]
    k = [k_ref[0, :, lanes[h]] for h in heads]
    v_aug = [jnp.concatenate([v_ref[0, :, lanes[h]], ones_blk], axis=1) for h in heads]
    s1 = [lax.dot_general(q[h], k[h], (((1,), (1,)), ((), ())), preferred_element_type=F32) for h in heads]
    c_prev = [c_scr[h] for h in heads]
    inter = [jnp.dot(q[h], c_prev[h].astype(BF16), preferred_element_type=F32) for h in heads]
    s = [(s1[h] * jnp.where(causal, jnp.exp(u[:, h:h + 1] + r_t[h:h + 1, :]), 0.0)).astype(BF16) for h in heads]
    intra = [jnp.dot(s[h], v_aug[h], preferred_element_type=F32) for h in heads]
    tot = [intra[h] + e_inter[:, h:h + 1] * inter[h] for h in heads]
    hh = [tot[h][:, :HEAD_DIM] / jnp.maximum(jnp.abs(tot[h][:, HEAD_DIM:]), e_floor[:, h:h + 1]) for h in heads]
    hh = [hh[h] * lax.rsqrt(jnp.mean(hh[h] * hh[h], axis=1, keepdims=True) + RMS_EPS) for h in heads]
    for h in heads:
        og = _sigmoid(o_ref[0, :, lanes[h]].astype(F32))
        y_ref[0, :, lanes[h]] = (hh[h] * ng_ref[:, lanes[h]] * og).astype(y_ref.dtype)
    kw = [(k[h].astype(F32) * w_loc[:, h:h + 1]).astype(BF16) for h in heads]
    c_loc = [lax.dot_general(kw[h], v_aug[h], (((0,), (0,)), ((), ())), preferred_element_type=F32) for h in heads]
    for h in heads:
        c_scr[h] = sp[:, h:h + 1] * c_prev[h] + sl[:, h:h + 1] * c_loc[h]
    m_scr[...] = m_new


def _mlstm(z, gates, gate_bias, norm_g):
    bsz, seq, _ = z.shape
    L = MLSTM_CHUNK
    hw = HEADS * HEAD_DIM

    def zspec(cb):
        return pl.BlockSpec((1, L, hw), lambda b, c, cb=cb: (b, c, cb))

    return pl.pallas_call(
        _mlstm_kernel,
        grid=(bsz, seq // L),
        in_specs=[zspec(3), zspec(4), zspec(5), zspec(6),
                  pl.BlockSpec((1, L, LANES), lambda b, c: (b, c, 0)),
                  pl.BlockSpec((1, LANES), lambda b, c: (0, 0)),
                  pl.BlockSpec((1, hw), lambda b, c: (0, 0))],
        out_specs=pl.BlockSpec((1, L, hw), lambda b, c: (b, c, 0)),
        out_shape=jax.ShapeDtypeStruct((bsz, seq, hw), BF16),
        scratch_shapes=[pltpu.VMEM((HEADS, HEAD_DIM, 2 * HEAD_DIM), F32),
                        pltpu.VMEM((1, LANES), F32)],
        compiler_params=_params("parallel", "arbitrary"),
        name="mlstm",
    )(z, z, z, z, gates, gate_bias, norm_g)


def _mix_kernel(cin_ref, cout_ref, cval_ref, mg0_ref, mg1_ref, yb_ref, x_ref, cw_ref,
                wa_ref, wb_ref, wo_ref, g_ref, b_ref, x1_ref, x1p_ref, carry):
    ts = x_ref.shape[1]

    @pl.when(pl.program_id(1) == 0)
    def _():
        carry[...] = jnp.zeros_like(carry)

    u = cin_ref[0].astype(F32) * cval_ref[0].astype(F32)
    prev = carry[...]
    carry[...] = u[ts - 8:, :]
    r8 = lax.broadcasted_iota(I32, (8, u.shape[1]), 0)

    def shifted(k):
        body = pltpu.roll(u, k, axis=0)
        head = jnp.where(r8 < k, pltpu.roll(prev, k, axis=0), body[:8, :])
        return jnp.concatenate([head, body[8:, :]], axis=0)

    cw = cw_ref[...]
    conv = cw[0:1, :] * shifted(2) + cw[1:2, :] * shifted(1) + cw[2:3, :] * u
    y_a = (cout_ref[0].astype(F32) * conv).astype(BF16)
    pa = jnp.dot(y_a, wa_ref[...], preferred_element_type=F32)
    pb = jnp.dot(yb_ref[0], wb_ref[...], preferred_element_type=F32)
    mixed = _sigmoid(mg0_ref[0].astype(F32)) * pa + _sigmoid(mg1_ref[0].astype(F32)) * pb
    hmix = jnp.dot(mixed.astype(BF16), wo_ref[...], preferred_element_type=F32)
    x1 = _layer_norm(ALPHA * x_ref[0] + hmix, g_ref[...], b_ref[...])
    x1_ref[0] = x1
    half = x1.shape[1] // 2
    x1p_ref[0] = _pack_pair(x1[:, :half], x1[:, half:])


def _mix(z, zm, y_b, x, conv_w, wa, wb, wo, g, b):
    bsz, seq, d = x.shape
    ts = min(MIX_TS, seq)

    def zspec(cb):
        return pl.BlockSpec((1, ts, d), lambda i, j, cb=cb: (i, j, cb))

    def full(shape):
        return pl.BlockSpec(shape, lambda i, j: (0,) * len(shape))

    tile = pl.BlockSpec((1, ts, d), lambda i, j: (i, j, 0))
    ptile = pl.BlockSpec((1, ts, d // 2), lambda i, j: (i, j, 0))
    return pl.pallas_call(
        _mix_kernel,
        grid=(bsz, seq // ts),
        in_specs=[zspec(0), zspec(1), zspec(2), zspec(0), zspec(1), tile, tile,
                  full(conv_w.shape), full(wa.shape), full(wb.shape), full(wo.shape),
                  full(g.shape), full(b.shape)],
        out_specs=[tile, ptile],
        out_shape=[jax.ShapeDtypeStruct((bsz, seq, d), F32),
                   jax.ShapeDtypeStruct((bsz, seq, d // 2), I32)],
        scratch_shapes=[pltpu.VMEM((8, d), F32)],
        compiler_params=_params("parallel", "arbitrary"),
        name="mix",
    )(z, z, z, zm, zm, y_b, x, conv_w, wa, wb, wo, g, b)


def _router_kernel(x_ref, wrt_ref, rb_ref, idx_ref, rank_ref, wtm_ref, cnt_ref, carry):
    t = x_ref.shape[0]

    @pl.when(pl.program_id(0) == 0)
    def _():
        carry[...] = jnp.zeros_like(carry)

    logits = lax.dot_general(wrt_ref[...], x_ref[...], (((1,), (1,)), ((), ())),
                             precision=lax.Precision.HIGHEST, preferred_element_type=F32)
    scores = _sigmoid(logits)
    sel = (scores + rb_ref[...]).reshape(N_GROUPS, GROUP_SIZE, t)
    scores = scores.reshape(N_GROUPS, GROUP_SIZE, t)
    shape3 = (N_GROUPS, GROUP_SIZE, t)
    sub = lax.broadcasted_iota(I32, shape3, 1)
    grp = lax.broadcasted_iota(I32, shape3, 0)
    eidx = grp * GROUP_SIZE + sub
    m1 = jnp.max(sel, axis=1, keepdims=True)
    first = jnp.min(jnp.where(sel == m1, sub, GROUP_SIZE), axis=1, keepdims=True)
    m2 = jnp.max(jnp.where(sub == first, NEG_INF, sel), axis=1, keepdims=True)
    rem = m1 + m2
    gidx = lax.broadcasted_iota(I32, rem.shape, 0)
    gmask = jnp.zeros(rem.shape, F32)
    for _ in range(TOPK_GROUPS):
        mx = jnp.max(rem, axis=0, keepdims=True)
        pick = gidx == jnp.min(jnp.where(rem == mx, gidx, N_GROUPS), axis=0, keepdims=True)
        gmask = jnp.where(pick, 1.0, gmask)
        rem = jnp.where(pick, NEG_INF, rem)
    masked = jnp.where(jnp.broadcast_to(gmask, shape3) > 0.5, sel, NEG_INF)
    chosen = jnp.zeros(shape3, F32)
    picks = []
    for _ in range(TOP_K):
        mx = jnp.max(jnp.max(masked, axis=1, keepdims=True), axis=0, keepdims=True)
        cand = jnp.where(masked == mx, eidx, N_EXPERTS)
        fi = jnp.min(jnp.min(cand, axis=1, keepdims=True), axis=0, keepdims=True)
        pick = eidx == fi
        picks.append((fi, pick))
        chosen = jnp.where(pick, 1.0, chosen)
        masked = jnp.where(pick, NEG_INF, masked)
    w = chosen * scores
    denom = jnp.sum(jnp.sum(w, axis=1, keepdims=True), axis=0, keepdims=True)
    gate3 = w / denom * ROUTED_SCALE

    chosen2 = chosen.reshape(N_EXPERTS, t).astype(BF16)
    tok_r = lax.broadcasted_iota(I32, (t, t), 0)
    tok_c = lax.broadcasted_iota(I32, (t, t), 1)
    before = jnp.where(tok_r < tok_c, 1.0, 0.0).astype(BF16)
    prefix = (jnp.dot(chosen2, before, preferred_element_type=F32) + carry[...]).reshape(shape3)
    carry[...] += jnp.dot(chosen2, jnp.ones((t, t), BF16), preferred_element_type=F32)
    cnt_ref[...] = carry[:, :LANES]

    def pick_sum(pick, val):
        return jnp.sum(jnp.sum(jnp.where(pick, val, 0.0), axis=1, keepdims=True), axis=0, keepdims=True)

    sub8 = lax.broadcasted_iota(I32, (TOP_K, t), 0)
    idx8 = jnp.zeros((TOP_K, t), I32)
    rank8 = jnp.zeros((TOP_K, t), F32)
    w8 = jnp.zeros((TOP_K, t), F32)
    for k, (fi, pick) in enumerate(picks):
        idx8 = jnp.where(sub8 == k, jnp.broadcast_to(fi.reshape(1, t), (TOP_K, t)), idx8)
        rank8 = jnp.where(sub8 == k, jnp.broadcast_to(pick_sum(pick, prefix).reshape(1, t), (TOP_K, t)), rank8)
        w8 = jnp.where(sub8 == k, jnp.broadcast_to(pick_sum(pick, gate3).reshape(1, t), (TOP_K, t)), w8)
    idx_ref[...] = idx8
    rank_ref[...] = rank8.astype(I32)
    pad = jnp.zeros((LANES - TOP_K, t), F32)
    wtm_ref[...] = jnp.transpose(jnp.concatenate([w8, pad], axis=0))


def _router(x1, w_router_t, router_bias):
    m, d = x1.shape
    tt = min(ROUTER_TT, m)
    kt = pl.BlockSpec((TOP_K, tt), lambda i: (0, i))
    return pl.pallas_call(
        _router_kernel,
        grid=(m // tt,),
        in_specs=[pl.BlockSpec((tt, d), lambda i: (i, 0)),
                  pl.BlockSpec((N_EXPERTS, d), lambda i: (0, 0)),
                  pl.BlockSpec((N_EXPERTS, 1), lambda i: (0, 0))],
        out_specs=[kt, kt,
                   pl.BlockSpec((tt, LANES), lambda i: (i, 0)),
                   pl.BlockSpec((N_EXPERTS, LANES), lambda i: (0, 0))],
        out_shape=[jax.ShapeDtypeStruct((TOP_K, m), I32),
                   jax.ShapeDtypeStruct((TOP_K, m), I32),
                   jax.ShapeDtypeStruct((m, LANES), F32),
                   jax.ShapeDtypeStruct((N_EXPERTS, LANES), F32)],
        scratch_shapes=[pltpu.VMEM((N_EXPERTS, tt), F32)],
        compiler_params=_params("arbitrary"),
        name="router",
    )(x1, w_router_t, router_bias)


def _visit_metadata(counts, tm, n_rows):
    nt = n_rows // tm
    nv = nt + N_EXPERTS - 1
    ends = jnp.cumsum(counts)
    starts = ends - counts
    first_tile = starts // tm
    ntiles = jnp.where(counts > 0, (ends - 1) // tm - first_tile + 1, 0)
    vend = jnp.cumsum(ntiles)
    vstart = vend - ntiles
    v = jnp.arange(nv, dtype=I32)
    valid = v < vend[-1]
    ve = jnp.minimum(jnp.sum((v[:, None] >= vend[None, :]).astype(I32), axis=1), N_EXPERTS - 1)
    ve = jnp.where(valid, ve, ve[jnp.maximum(vend[-1] - 1, 0)])
    vt = jnp.where(valid, first_tile[ve] + v - vstart[ve], nt - 1)
    lo = jnp.where(valid, jnp.clip(starts[ve] - vt * tm, 0, tm), 0)
    hi = jnp.where(valid, jnp.clip(ends[ve] - vt * tm, 0, tm), 0)
    return ve.astype(I32), vt.astype(I32), lo.astype(I32), hi.astype(I32)


def _sc_mesh():
    return plsc.VectorSubcoreMesh(core_axis_name="c", subcore_axis_name="s")


def _sc_worker():
    return lax.axis_index("s") * SC_CORES + lax.axis_index("c")


def _dispatch(x1p, dest):
    m, dp = x1p.shape
    blocks = m // SC_ROWS
    per_worker = blocks // SC_WORKERS
    table = dest.reshape(TOP_K, blocks, SC_ROWS).transpose(1, 0, 2).reshape(blocks * TOP_K, SC_ROWS)

    def body(x_hbm, idx_hbm, xs_hbm, idx_v, rows_v, sem):
        first = _sc_worker() * per_worker

        @pl.loop(0, per_worker)
        def _(j):
            b = first + j
            pltpu.sync_copy(idx_hbm.at[pl.ds(pl.multiple_of(b * TOP_K, TOP_K), TOP_K)], idx_v)
            pltpu.sync_copy(x_hbm.at[pl.ds(pl.multiple_of(b * SC_ROWS, SC_ROWS), SC_ROWS)], rows_v)
            copies = [pltpu.async_copy(rows_v, xs_hbm.at[idx_v.at[k]], sem) for k in range(TOP_K)]
            for cp in copies:
                cp.wait()

    return pl.kernel(
        body,
        out_type=jax.ShapeDtypeStruct((m * TOP_K, dp), x1p.dtype),
        mesh=_sc_mesh(),
        scratch_types=[pltpu.VMEM((TOP_K, SC_ROWS), I32), pltpu.VMEM((SC_ROWS, dp), x1p.dtype),
                       pltpu.SemaphoreType.DMA],
        name="dispatch",
    )(x1p, table)


def _gather_rows(ys, dest):
    n_rows, dp = ys.shape
    idx = dest.reshape(n_rows // SC_ROWS, SC_ROWS)
    per_worker = n_rows // SC_ROWS // SC_WORKERS

    def body(ys_hbm, idx_hbm, out_hbm, idx_v, rows_v, sem):
        first = _sc_worker() * per_worker

        @pl.loop(0, per_worker)
        def _(j):
            b = first + j
            pltpu.sync_copy(idx_hbm.at[pl.ds(b, 1)], idx_v)
            pltpu.async_copy(ys_hbm.at[idx_v.at[0]], rows_v, sem).wait()
            pltpu.sync_copy(rows_v, out_hbm.at[pl.ds(pl.multiple_of(b * SC_ROWS, SC_ROWS), SC_ROWS)])

    return pl.kernel(
        body,
        out_type=jax.ShapeDtypeStruct((n_rows, dp), ys.dtype),
        mesh=_sc_mesh(),
        scratch_types=[pltpu.VMEM((1, SC_ROWS), I32), pltpu.VMEM((SC_ROWS, dp), ys.dtype),
                       pltpu.SemaphoreType.DMA],
        name="gather_rows",
    )(ys, idx)


def _ffn_kernel(ve_ref, vt_ref, lo_ref, hi_ref, xs_ref, wg_ref, wu_ref, wd_ref, ys_ref,
                wgu_b, wd_b, acc):
    v = pl.program_id(0)
    lo = lo_ref[v]
    hi = hi_ref[v]
    tm = xs_ref.shape[0]
    f = wg_ref.shape[1]

    @pl.when((v == 0) | (ve_ref[v] != ve_ref[jnp.maximum(v - 1, 0)]))
    def _():
        wgu_b[:, :f] = wg_ref[...].astype(BF16)
        wgu_b[:, f:] = wu_ref[...].astype(BF16)
        wd_b[...] = wd_ref[...].astype(BF16)

    def pack_rows(a):
        half = a.shape[1] // 2
        return _pack_pair(a[:, :half], a[:, half:])

    for sb in range(tm // FFN_SUB):
        s0 = sb * FFN_SUB
        rows = pl.ds(s0, FFN_SUB)
        lo_s = jnp.clip(lo - s0, 0, FFN_SUB)
        hi_s = jnp.clip(hi - s0, 0, FFN_SUB)

        @pl.when(hi_s > lo_s)
        def _():
            x = _unpack_rows(xs_ref[rows, :])
            h2 = jnp.dot(x, wgu_b[...], preferred_element_type=F32)
            hg = h2[:, :f]
            hu = h2[:, f:]
            r = lax.broadcasted_iota(I32, hg.shape, 0)
            mine = (r >= lo_s) & (r < hi_s)
            hmid = jnp.where(mine, hg * _sigmoid(hg) * hu, 0.0).astype(BF16)
            y = jnp.dot(hmid, wd_b[...], preferred_element_type=F32)

            @pl.when((lo_s == 0) & (hi_s == FFN_SUB))
            def _():
                ys_ref[rows, :] = pack_rows(y)

            @pl.when((lo_s == 0) & (hi_s < FFN_SUB))
            def _():
                acc[rows, :] = y

            @pl.when(lo_s > 0)
            def _():
                acc[rows, :] += y

            @pl.when((lo_s > 0) & (hi_s == FFN_SUB))
            def _():
                ys_ref[rows, :] = pack_rows(acc[rows, :])


def _ffn(xs, meta, wg, wu, wd, layer):
    n_rows, dp = xs.shape
    _, n_e, d, f = wg.shape
    tm = FFN_TM
    nv = n_rows // tm + N_EXPERTS - 1
    grid_spec = pltpu.PrefetchScalarGridSpec(
        num_scalar_prefetch=4,
        grid=(nv,),
        in_specs=[pl.BlockSpec((tm, dp), lambda v, ve, vt, lo, hi: (vt[v], 0)),
                  pl.BlockSpec((None, None, d, f), lambda v, ve, vt, lo, hi: (layer, ve[v], 0, 0)),
                  pl.BlockSpec((None, None, d, f), lambda v, ve, vt, lo, hi: (layer, ve[v], 0, 0)),
                  pl.BlockSpec((None, None, f, d), lambda v, ve, vt, lo, hi: (layer, ve[v], 0, 0))],
        out_specs=pl.BlockSpec((tm, dp), lambda v, ve, vt, lo, hi: (vt[v], 0)),
        scratch_shapes=[pltpu.VMEM((d, 2 * f), BF16), pltpu.VMEM((f, d), BF16), pltpu.VMEM((tm, d), F32)],
    )
    return pl.pallas_call(
        _ffn_kernel,
        grid_spec=grid_spec,
        out_shape=jax.ShapeDtypeStruct((n_rows, dp), I32),
        compiler_params=_params("arbitrary"),
        name="ffn",
    )(*meta, xs, wg, wu, wd)


def _combine_kernel(yk_ref, x1_ref, x1p_ref, wtm_ref, p_ref, wsg_ref, wsu_ref, wsd_ref,
                    wpg_ref, wpp_ref, g_ref, b_ref, x2_ref, x2b_ref):
    tt = x1_ref.shape[0]
    xb = _unpack_rows(x1p_ref[...])
    hg = jnp.dot(xb, wsg_ref[...], preferred_element_type=F32)
    hu = jnp.dot(xb, wsu_ref[...], preferred_element_type=F32)
    shared = jnp.dot((hg * _sigmoid(hg) * hu).astype(BF16), wsd_ref[...], preferred_element_type=F32)
    pgate = _sigmoid(jnp.dot(xb, wpg_ref[...], preferred_element_type=F32))
    pproj = jnp.dot(p_ref[...].astype(BF16), wpp_ref[...], preferred_element_type=F32)
    rest = ALPHA * x1_ref[...] + shared + pgate * pproj

    wt = wtm_ref[...]
    r_lo = jnp.zeros((tt, yk_ref.shape[2]), F32)
    r_hi = jnp.zeros((tt, yk_ref.shape[2]), F32)
    for k in range(TOP_K):
        lo, hi = _unpack_pair(yk_ref[k])
        wk = wt[:, k:k + 1]
        r_lo = r_lo + wk * lo
        r_hi = r_hi + wk * hi
    routed = jnp.concatenate([r_lo, r_hi], axis=1)
    x2 = _layer_norm(rest + routed, g_ref[...], b_ref[...])
    x2_ref[...] = x2
    x2b_ref[...] = x2.astype(BF16)


def _combine(yk, x1, x1p, wtm, p, wsg, wsu, wsd, wpg, wpp, g, b):
    m, d = x1.shape
    dp = x1p.shape[1]
    tt = min(COMBINE_TT, m)
    yk = yk.reshape(TOP_K, m, dp)

    def full(a):
        return pl.BlockSpec(a.shape, lambda i: (0,) * a.ndim)

    def rows(a):
        return pl.BlockSpec((tt, a.shape[1]), lambda i: (i, 0))

    return pl.pallas_call(
        _combine_kernel,
        grid=(m // tt,),
        in_specs=[pl.BlockSpec((TOP_K, tt, dp), lambda i: (0, i, 0)),
                  rows(x1), rows(x1p), rows(wtm), rows(p),
                  full(wsg), full(wsu), full(wsd), full(wpg), full(wpp), full(g), full(b)],
        out_specs=[rows(x1), rows(x1)],
        out_shape=[jax.ShapeDtypeStruct((m, d), F32), jax.ShapeDtypeStruct((m, d), BF16)],
        compiler_params=_params("parallel"),
        name="combine",
    )(yk, x1, x1p, wtm, p, wsg, wsu, wsd, wpg, wpp, g, b)


def kernel(x, p, w_in, conv_w, b_igate, b_fgate, mlstm_norm_g, w_branch_a, w_branch_b, w_out, ln1_g, ln1_b, w_router, router_bias, w_exp_gate, w_exp_up, w_exp_down, w_sh_gate, w_sh_up, w_sh_down, w_ple_gate, w_ple_proj, ln2_g, ln2_b):
    bsz, seq, d = x.shape
    m = bsz * seq
    depth = w_in.shape[0]
    hw = HEADS * HEAD_DIM
    if_lo = 3 * d + 4 * hw
    if_hi = if_lo + 2 * HEADS

    xf = x.reshape(m, d)
    xb = xf.astype(BF16)
    for i in range(depth):
        w_merge = w_in[i][:, if_hi:]
        w_if = jnp.pad(w_in[i][:, if_lo:if_hi], ((0, 0), (0, LANES - 2 * HEADS)))
        gate_bias = jnp.pad(jnp.concatenate([b_igate[i], b_fgate[i]]), (0, LANES - 2 * HEADS)).reshape(1, LANES)

        z = _matmul(xb, w_in, BF16, INPROJ_TM, INPROJ_TN, "inproj", layer=i, n=if_lo).reshape(bsz, seq, -1)
        zm = _matmul(xb, w_merge, BF16, INPROJ_TM, INPROJ_TN, "mergeproj").reshape(bsz, seq, -1)
        gates = _matmul(xb, w_if, F32, INPROJ_TM, LANES, "gateproj").reshape(bsz, seq, LANES)
        y_b = _mlstm(z, gates, gate_bias, mlstm_norm_g[i].reshape(1, hw))
        x1, x1p = _mix(z, zm, y_b, xf.reshape(bsz, seq, d), conv_w[i],
                       w_branch_a[i].astype(BF16), w_branch_b[i].astype(BF16), w_out[i].astype(BF16),
                       ln1_g[i].reshape(1, d), ln1_b[i].reshape(1, d))
        x1 = x1.reshape(m, d)
        x1p = x1p.reshape(m, d // 2)

        idx, rank, wtm, cnt = _router(x1, w_router[i].T, router_bias[i].reshape(N_EXPERTS, 1))
        counts = cnt[:, 0].astype(I32)
        row_start = jnp.cumsum(counts) - counts
        expert_ids = jnp.arange(N_EXPERTS, dtype=I32)[:, None, None]
        dest = rank + jnp.sum(jnp.where(idx[None] == expert_ids, row_start[:, None, None], 0), axis=0)
        meta = _visit_metadata(counts, FFN_TM, m * TOP_K)
        xs = _dispatch(x1p, dest)
        ys = _ffn(xs, meta, w_exp_gate, w_exp_up, w_exp_down, i)
        yk = _gather_rows(ys, dest)
        xf, xb = _combine(yk, x1, x1p, wtm, p[i].reshape(m, -1),
                          w_sh_gate[i].astype(BF16), w_sh_up[i].astype(BF16), w_sh_down[i].astype(BF16),
                          w_ple_gate[i].astype(BF16), w_ple_proj[i].astype(BF16),
                          ln2_g[i].reshape(1, d), ln2_b[i].reshape(1, d))
    return xf.reshape(bsz, seq, d)
```

```python
import math

import jax
import jax.numpy as jnp
from jax import lax
from jax.experimental import pallas as pl
from jax.experimental.pallas import tpu as pltpu
from jax.experimental.pallas import tpu_sc as plsc

F32 = jnp.float32
BF16 = jnp.bfloat16
U32 = jnp.uint32
I32 = jnp.int32

HEADS = 8
HEAD_DIM = 128
N_EXPERTS = 64
N_GROUPS = 8
GROUP_SIZE = N_EXPERTS // N_GROUPS
TOPK_GROUPS = 4
TOP_K = 8
ROUTED_SCALE = 2.5
DEPTH = 4
ALPHA = (2 * DEPTH) ** 0.25
LN_EPS = 1e-5
RMS_EPS = 1e-6
QK_SCALE = HEAD_DIM ** -0.5
LOG_QK_SCALE = math.log(QK_SCALE)

LANES = 128
VMEM_LIMIT = 56 * 1024 * 1024
NEG_INF = float("-inf")

MLSTM_CHUNK = 256
INPROJ_TM, INPROJ_TN = 2048, 1024
MIX_TS = 512
ROUTER_TT = 512
FFN_TM = 2048
FFN_SUB = 512
COMBINE_TT = 256

SC_CORES = 2
SC_SUBCORES = 16
SC_WORKERS = SC_CORES * SC_SUBCORES
SC_ROWS = 128


def _params(*sem):
    return pltpu.CompilerParams(dimension_semantics=sem, vmem_limit_bytes=VMEM_LIMIT)


def _sigmoid(x):
    return 1.0 / (1.0 + jnp.exp(-x))


def _layer_norm(r, g, b):
    mu = jnp.mean(r, axis=-1, keepdims=True)
    d = r - mu
    var = jnp.mean(d * d, axis=-1, keepdims=True)
    return d * lax.rsqrt(var + LN_EPS) * g + b


def _pack_pair(lo, hi):
    return lax.bitcast_convert_type(pltpu.pack_elementwise([lo, hi], packed_dtype=BF16), I32)


def _unpack_pair(w):
    w = lax.bitcast_convert_type(w, U32)
    lo = pltpu.unpack_elementwise(w, index=0, packed_dtype=BF16, unpacked_dtype=F32)
    hi = pltpu.unpack_elementwise(w, index=1, packed_dtype=BF16, unpacked_dtype=F32)
    return lo, hi


def _unpack_rows(w):
    lo, hi = _unpack_pair(w)
    return jnp.concatenate([lo.astype(BF16), hi.astype(BF16)], axis=1)


def _mm_kernel(x_ref, wt_ref, o_ref):
    o_ref[...] = lax.dot_general(x_ref[...], wt_ref[...].astype(BF16), (((1,), (1,)), ((), ())),
                                 preferred_element_type=F32).astype(o_ref.dtype)


def _matmul(x, wt, out_dtype, tm, tn, name, layer=None, n=None):
    m, k = x.shape
    n = wt.shape[-2] if n is None else n
    tm = min(tm, m)
    if layer is None:
        w_spec = pl.BlockSpec((tn, k), lambda i, j: (j, 0))
    else:
        w_spec = pl.BlockSpec((None, tn, k), lambda i, j: (layer, j, 0))
    return pl.pallas_call(
        _mm_kernel,
        grid=(m // tm, n // tn),
        in_specs=[pl.BlockSpec((tm, k), lambda i, j: (i, 0)), w_spec],
        out_specs=pl.BlockSpec((tm, tn), lambda i, j: (i, j)),
        out_shape=jax.ShapeDtypeStruct((m, n), out_dtype),
        compiler_params=_params("parallel", "parallel"),
        name=name,
    )(x, wt)


def _mlstm_kernel(q_ref, k_ref, v_ref, o_ref, g_ref, gb_ref, ng_ref, y_ref, c_scr, m_scr):
    L = q_ref.shape[1]

    @pl.when(pl.program_id(1) == 0)
    def _():
        c_scr[...] = jnp.zeros_like(c_scr)
        m_scr[...] = jnp.zeros_like(m_scr)

    g = g_ref[0] + gb_ref[...]
    ig = g
    fg = pltpu.roll(g, LANES - HEADS, axis=1)
    log_f = jnp.minimum(fg, 0.0) - jnp.log(1.0 + jnp.exp(-jnp.abs(fg)))
    row = lax.broadcasted_iota(I32, (L, L), 0)
    col = lax.broadcasted_iota(I32, (L, L), 1)
    causal = col <= row
    rows = lax.broadcasted_iota(I32, log_f.shape, 0)

    def scan_rows(x, op, identity):
        step = 1
        while step < L:
            x = op(x, jnp.where(rows >= step, pltpu.roll(x, step, axis=0), identity))
            step *= 2
        return x

    b = scan_rows(log_f, jnp.add, 0.0)
    b_last = b[L - 1:L, :]
    m_prev = m_scr[...]
    a = b_last - b + ig
    m_loc = jnp.max(a, axis=0, keepdims=True)
    w_loc = jnp.exp(a - m_loc)
    m_new = jnp.maximum(b_last + m_prev, m_loc)
    sp = jnp.exp(b_last + m_prev - m_new)
    sl = jnp.exp(m_loc - m_new)
    log_inter = b + m_prev
    r = ig - b
    r_t = jnp.transpose(r)
    m_out = jnp.maximum(log_inter, b + scan_rows(r, jnp.maximum, NEG_INF))
    u = b - m_out + LOG_QK_SCALE
    e_inter = jnp.exp(log_inter - m_out + LOG_QK_SCALE)
    e_floor = jnp.exp(-m_out)

    ones_blk = jnp.ones((L, HEAD_DIM), BF16)
    heads = range(HEADS)
    lanes = [slice(h * HEAD_DIM, (h + 1) * HEAD_DIM) for h in heads]
    q = [q_ref[0, :, lanes[h]] for h in heads]
    k = [k_ref[0, :, lanes[h]] for h in heads]
    v_aug = [jnp.concatenate([v_ref[0, :, lanes[h]], ones_blk], axis=1) for h in heads]
    s1 = [lax.dot_general(q[h], k[h], (((1,), (1,)), ((), ())), preferred_element_type=F32) for h in heads]
    c_prev = [c_scr[h] for h in heads]
    inter = [jnp.dot(q[h], c_prev[h].astype(BF16), preferred_element_type=F32) for h in heads]
    s = [(s1[h] * jnp.where(causal, jnp.exp(u[:, h:h + 1] + r_t[h:h + 1, :]), 0.0)).astype(BF16) for h in heads]
    intra = [jnp.dot(s[h], v_aug[h], preferred_element_type=F32) for h in heads]
    tot = [intra[h] + e_inter[:, h:h + 1] * inter[h] for h in heads]
    hh = [tot[h][:, :HEAD_DIM] / jnp.maximum(jnp.abs(tot[h][:, HEAD_DIM:]), e_floor[:, h:h + 1]) for h in heads]
    hh = [hh[h] * lax.rsqrt(jnp.mean(hh[h] * hh[h], axis=1, keepdims=True) + RMS_EPS) for h in heads]
    for h in heads:
        og = _sigmoid(o_ref[0, :, lanes[h]].astype(F32))
        y_ref[0, :, lanes[h]] = (hh[h] * ng_ref[:, lanes[h]] * og).astype(y_ref.dtype)
    kw = [(k[h].astype(F32) * w_loc[:, h:h + 1]).astype(BF16) for h in heads]
    c_loc = [lax.dot_general(kw[h], v_aug[h], (((0,), (0,)), ((), ())), preferred_element_type=F32) for h in heads]
    for h in heads:
        c_scr[h] = sp[:, h:h + 1] * c_prev[h] + sl[:, h:h + 1] * c_loc[h]
    m_scr[...] = m_new


def _mlstm(z, gates, gate_bias, norm_g):
    bsz, seq, _ = z.shape
    L = MLSTM_CHUNK
    hw = HEADS * HEAD_DIM

    def zspec(cb):
        return pl.BlockSpec((1, L, hw), lambda b, c, cb=cb: (b, c, cb))

    return pl.pallas_call(
        _mlstm_kernel,
        grid=(bsz, seq // L),
        in_specs=[zspec(3), zspec(4), zspec(5), zspec(6),
                  pl.BlockSpec((1, L, LANES), lambda b, c: (b, c, 0)),
                  pl.BlockSpec((1, LANES), lambda b, c: (0, 0)),
                  pl.BlockSpec((1, hw), lambda b, c: (0, 0))],
        out_specs=pl.BlockSpec((1, L, hw), lambda b, c: (b, c, 0)),
        out_shape=jax.ShapeDtypeStruct((bsz, seq, hw), BF16),
        scratch_shapes=[pltpu.VMEM((HEADS, HEAD_DIM, 2 * HEAD_DIM), F32),
                        pltpu.VMEM((1, LANES), F32)],
        compiler_params=_params("parallel", "arbitrary"),
        name="mlstm",
    )(z, z, z, z, gates, gate_bias, norm_g)


def _mix_kernel(cin_ref, cout_ref, cval_ref, mg0_ref, mg1_ref, yb_ref, x_ref, cw_ref,
                wa_ref, wb_ref, wo_ref, g_ref, b_ref, x1_ref, x1p_ref, carry):
    ts = x_ref.shape[1]

    @pl.when(pl.program_id(1) == 0)
    def _():
        carry[...] = jnp.zeros_like(carry)

    u = cin_ref[0].astype(F32) * cval_ref[0].astype(F32)
    prev = carry[...]
    carry[...] = u[ts - 8:, :]
    r8 = lax.broadcasted_iota(I32, (8, u.shape[1]), 0)

    def shifted(k):
        body = pltpu.roll(u, k, axis=0)
        head = jnp.where(r8 < k, pltpu.roll(prev, k, axis=0), body[:8, :])
        return jnp.concatenate([head, body[8:, :]], axis=0)

    cw = cw_ref[...]
    conv = cw[0:1, :] * shifted(2) + cw[1:2, :] * shifted(1) + cw[2:3, :] * u
    y_a = (cout_ref[0].astype(F32) * conv).astype(BF16)
    pa = jnp.dot(y_a, wa_ref[...], preferred_element_type=F32)
    pb = jnp.dot(yb_ref[0], wb_ref[...], preferred_element_type=F32)
    mixed = _sigmoid(mg0_ref[0].astype(F32)) * pa + _sigmoid(mg1_ref[0].astype(F32)) * pb
    hmix = jnp.dot(mixed.astype(BF16), wo_ref[...], preferred_element_type=F32)
    x1 = _layer_norm(ALPHA * x_ref[0] + hmix, g_ref[...], b_ref[...])
    x1_ref[0] = x1
    half = x1.shape[1] // 2
    x1p_ref[0] = _pack_pair(x1[:, :half], x1[:, half:])


def _mix(z, zm, y_b, x, conv_w, wa, wb, wo, g, b):
    bsz, seq, d = x.shape
    ts = min(MIX_TS, seq)

    def zspec(cb):
        return pl.BlockSpec((1, ts, d), lambda i, j, cb=cb: (i, j, cb))

    def full(shape):
        return pl.BlockSpec(shape, lambda i, j: (0,) * len(shape))

    tile = pl.BlockSpec((1, ts, d), lambda i, j: (i, j, 0))
    ptile = pl.BlockSpec((1, ts, d // 2), lambda i, j: (i, j, 0))
    return pl.pallas_call(
        _mix_kernel,
        grid=(bsz, seq // ts),
        in_specs=[zspec(0), zspec(1), zspec(2), zspec(0), zspec(1), tile, tile,
                  full(conv_w.shape), full(wa.shape), full(wb.shape), full(wo.shape),
                  full(g.shape), full(b.shape)],
        out_specs=[tile, ptile],
        out_shape=[jax.ShapeDtypeStruct((bsz, seq, d), F32),
                   jax.ShapeDtypeStruct((bsz, seq, d // 2), I32)],
        scratch_shapes=[pltpu.VMEM((8, d), F32)],
        compiler_params=_params("parallel", "arbitrary"),
        name="mix",
    )(z, z, z, zm, zm, y_b, x, conv_w, wa, wb, wo, g, b)


def _router_kernel(x_ref, wrt_ref, rb_ref, idx_ref, rank_ref, wtm_ref, cnt_ref, carry):
    t = x_ref.shape[0]

    @pl.when(pl.program_id(0) == 0)
    def _():
        carry[...] = jnp.zeros_like(carry)

    logits = lax.dot_general(wrt_ref[...], x_ref[...], (((1,), (1,)), ((), ())),
                             precision=lax.Precision.HIGHEST, preferred_element_type=F32)
    scores = _sigmoid(logits)
    sel = (scores + rb_ref[...]).reshape(N_GROUPS, GROUP_SIZE, t)
    scores = scores.reshape(N_GROUPS, GROUP_SIZE, t)
    shape3 = (N_GROUPS, GROUP_SIZE, t)
    sub = lax.broadcasted_iota(I32, shape3, 1)
    grp = lax.broadcasted_iota(I32, shape3, 0)
    eidx = grp * GROUP_SIZE + sub
    m1 = jnp.max(sel, axis=1, keepdims=True)
    first = jnp.min(jnp.where(sel == m1, sub, GROUP_SIZE), axis=1, keepdims=True)
    m2 = jnp.max(jnp.where(sub == first, NEG_INF, sel), axis=1, keepdims=True)
    rem = m1 + m2
    gidx = lax.broadcasted_iota(I32, rem.shape, 0)
    gmask = jnp.zeros(rem.shape, F32)
    for _ in range(TOPK_GROUPS):
        mx = jnp.max(rem, axis=0, keepdims=True)
        pick = gidx == jnp.min(jnp.where(rem == mx, gidx, N_GROUPS), axis=0, keepdims=True)
        gmask = jnp.where(pick, 1.0, gmask)
        rem = jnp.where(pick, NEG_INF, rem)
    masked = jnp.where(jnp.broadcast_to(gmask, shape3) > 0.5, sel, NEG_INF)
    chosen = jnp.zeros(shape3, F32)
    picks = []
    for _ in range(TOP_K):
        mx = jnp.max(jnp.max(masked, axis=1, keepdims=True), axis=0, keepdims=True)
        cand = jnp.where(masked == mx, eidx, N_EXPERTS)
        fi = jnp.min(jnp.min(cand, axis=1, keepdims=True), axis=0, keepdims=True)
        pick = eidx == fi
        picks.append((fi, pick))
        chosen = jnp.where(pick, 1.0, chosen)
        masked = jnp.where(pick, NEG_INF, masked)
    w = chosen * scores
    denom = jnp.sum(jnp.sum(w, axis=1, keepdims=True), axis=0, keepdims=True)
    gate3 = w / denom * ROUTED_SCALE

    chosen2 = chosen.reshape(N_EXPERTS, t).astype(BF16)
    tok_r = lax.broadcasted_iota(I32, (t, t), 0)
    tok_c = lax.broadcasted_iota(I32, (t, t), 1)
    before = jnp.where(tok_r < tok_c, 1.0, 0.0).astype(BF16)
    prefix = (jnp.dot(chosen2, before, preferred_element_type=F32) + carry[...]).reshape(shape3)
    carry[...] += jnp.dot(chosen2, jnp.ones((t, t), BF16), preferred_element_type=F32)
    cnt_ref[...] = carry[:, :LANES]

    def pick_sum(pick, val):
        return jnp.sum(jnp.sum(jnp.where(pick, val, 0.0), axis=1, keepdims=True), axis=0, keepdims=True)

    sub8 = lax.broadcasted_iota(I32, (TOP_K, t), 0)
    idx8 = jnp.zeros((TOP_K, t), I32)
    rank8 = jnp.zeros((TOP_K, t), F32)
    w8 = jnp.zeros((TOP_K, t), F32)
    for k, (fi, pick) in enumerate(picks):
        idx8 = jnp.where(sub8 == k, jnp.broadcast_to(fi.reshape(1, t), (TOP_K, t)), idx8)
        rank8 = jnp.where(sub8 == k, jnp.broadcast_to(pick_sum(pick, prefix).reshape(1, t), (TOP_K, t)), rank8)
        w8 = jnp.where(sub8 == k, jnp.broadcast_to(pick_sum(pick, gate3).reshape(1, t), (TOP_K, t)), w8)
    idx_ref[...] = idx8
    rank_ref[...] = rank8.astype(I32)
    pad = jnp.zeros((LANES - TOP_K, t), F32)
    wtm_ref[...] = jnp.transpose(jnp.concatenate([w8, pad], axis=0))


def _router(x1, w_router_t, router_bias):
    m, d = x1.shape
    tt = min(ROUTER_TT, m)
    kt = pl.BlockSpec((TOP_K, tt), lambda i: (0, i))
    return pl.pallas_call(
        _router_kernel,
        grid=(m // tt,),
        in_specs=[pl.BlockSpec((tt, d), lambda i: (i, 0)),
                  pl.BlockSpec((N_EXPERTS, d), lambda i: (0, 0)),
                  pl.BlockSpec((N_EXPERTS, 1), lambda i: (0, 0))],
        out_specs=[kt, kt,
                   pl.BlockSpec((tt, LANES), lambda i: (i, 0)),
                   pl.BlockSpec((N_EXPERTS, LANES), lambda i: (0, 0))],
        out_shape=[jax.ShapeDtypeStruct((TOP_K, m), I32),
                   jax.ShapeDtypeStruct((TOP_K, m), I32),
                   jax.ShapeDtypeStruct((m, LANES), F32),
                   jax.ShapeDtypeStruct((N_EXPERTS, LANES), F32)],
        scratch_shapes=[pltpu.VMEM((N_EXPERTS, tt), F32)],
        compiler_params=_params("arbitrary"),
        name="router",
    )(x1, w_router_t, router_bias)


def _visit_metadata(counts, tm, n_rows):
    nt = n_rows // tm
    nv = nt + N_EXPERTS - 1
    ends = jnp.cumsum(counts)
    starts = ends - counts
    first_tile = starts // tm
    ntiles = jnp.where(counts > 0, (ends - 1) // tm - first_tile + 1, 0)
    vend = jnp.cumsum(ntiles)
    vstart = vend - ntiles
    v = jnp.arange(nv, dtype=I32)
    valid = v < vend[-1]
    ve = jnp.minimum(jnp.sum((v[:, None] >= vend[None, :]).astype(I32), axis=1), N_EXPERTS - 1)
    ve = jnp.where(valid, ve, ve[jnp.maximum(vend[-1] - 1, 0)])
    vt = jnp.where(valid, first_tile[ve] + v - vstart[ve], nt - 1)
    lo = jnp.where(valid, jnp.clip(starts[ve] - vt * tm, 0, tm), 0)
    hi = jnp.where(valid, jnp.clip(ends[ve] - vt * tm, 0, tm), 0)
    return ve.astype(I32), vt.astype(I32), lo.astype(I32), hi.astype(I32)


def _sc_mesh():
    return plsc.VectorSubcoreMesh(core_axis_name="c", subcore_axis_name="s")


def _sc_worker():
    return lax.axis_index("s") * SC_CORES + lax.axis_index("c")


def _dispatch(x1p, dest):
    m, dp = x1p.shape
    blocks = m // SC_ROWS
    per_worker = blocks // SC_WORKERS
    table = dest.reshape(TOP_K, blocks, SC_ROWS).transpose(1, 0, 2).reshape(blocks * TOP_K, SC_ROWS)

    def body(x_hbm, idx_hbm, xs_hbm, idx_v, rows_v, sem):
        first = _sc_worker() * per_worker

        @pl.loop(0, per_worker)
        def _(j):
            b = first + j
            pltpu.sync_copy(idx_hbm.at[pl.ds(pl.multiple_of(b * TOP_K, TOP_K), TOP_K)], idx_v)
            pltpu.sync_copy(x_hbm.at[pl.ds(pl.multiple_of(b * SC_ROWS, SC_ROWS), SC_ROWS)], rows_v)
            copies = [pltpu.async_copy(rows_v, xs_hbm.at[idx_v.at[k]], sem) for k in range(TOP_K)]
            for cp in copies:
                cp.wait()

    return pl.kernel(
        body,
        out_type=jax.ShapeDtypeStruct((m * TOP_K, dp), x1p.dtype),
        mesh=_sc_mesh(),
        scratch_types=[pltpu.VMEM((TOP_K, SC_ROWS), I32), pltpu.VMEM((SC_ROWS, dp), x1p.dtype),
                       pltpu.SemaphoreType.DMA],
        name="dispatch",
    )(x1p, table)


def _gather_rows(ys, dest):
    n_rows, dp = ys.shape
    idx = dest.reshape(n_rows // SC_ROWS, SC_ROWS)
    per_worker = n_rows // SC_ROWS // SC_WORKERS

    def body(ys_hbm, idx_hbm, out_hbm, idx_v, rows_v, sem):
        first = _sc_worker() * per_worker

        @pl.loop(0, per_worker)
        def _(j):
            b = first + j
            pltpu.sync_copy(idx_hbm.at[pl.ds(b, 1)], idx_v)
            pltpu.async_copy(ys_hbm.at[idx_v.at[0]], rows_v, sem).wait()
            pltpu.sync_copy(rows_v, out_hbm.at[pl.ds(pl.multiple_of(b * SC_ROWS, SC_ROWS), SC_ROWS)])

    return pl.kernel(
        body,
        out_type=jax.ShapeDtypeStruct((n_rows, dp), ys.dtype),
        mesh=_sc_mesh(),
        scratch_types=[pltpu.VMEM((1, SC_ROWS), I32), pltpu.VMEM((SC_ROWS, dp), ys.dtype),
                       pltpu.SemaphoreType.DMA],
        name="gather_rows",
    )(ys, idx)


def _ffn_kernel(ve_ref, vt_ref, lo_ref, hi_ref, xs_ref, wg_ref, wu_ref, wd_ref, ys_ref,
                wgu_b, wd_b, acc):
    v = pl.program_id(0)
    lo = lo_ref[v]
    hi = hi_ref[v]
    tm = xs_ref.shape[0]
    f = wg_ref.shape[1]

    @pl.when((v == 0) | (ve_ref[v] != ve_ref[jnp.maximum(v - 1, 0)]))
    def _():
        wgu_b[:, :f] = wg_ref[...].astype(BF16)
        wgu_b[:, f:] = wu_ref[...].astype(BF16)
        wd_b[...] = wd_ref[...].astype(BF16)

    def pack_rows(a):
        half = a.shape[1] // 2
        return _pack_pair(a[:, :half], a[:, half:])

    for sb in range(tm // FFN_SUB):
        s0 = sb * FFN_SUB
        rows = pl.ds(s0, FFN_SUB)
        lo_s = jnp.clip(lo - s0, 0, FFN_SUB)
        hi_s = jnp.clip(hi - s0, 0, FFN_SUB)

        @pl.when(hi_s > lo_s)
        def _():
            x = _unpack_rows(xs_ref[rows, :])
            h2 = jnp.dot(x, wgu_b[...], preferred_element_type=F32)
            hg = h2[:, :f]
            hu = h2[:, f:]
            r = lax.broadcasted_iota(I32, hg.shape, 0)
            mine = (r >= lo_s) & (r < hi_s)
            hmid = jnp.where(mine, hg * _sigmoid(hg) * hu, 0.0).astype(BF16)
            y = jnp.dot(hmid, wd_b[...], preferred_element_type=F32)

            @pl.when((lo_s == 0) & (hi_s == FFN_SUB))
            def _():
                ys_ref[rows, :] = pack_rows(y)

            @pl.when((lo_s == 0) & (hi_s < FFN_SUB))
            def _():
                acc[rows, :] = y

            @pl.when(lo_s > 0)
            def _():
                acc[rows, :] += y

            @pl.when((lo_s > 0) & (hi_s == FFN_SUB))
            def _():
                ys_ref[rows, :] = pack_rows(acc[rows, :])


def _ffn(xs, meta, wg, wu, wd, layer):
    n_rows, dp = xs.shape
    _, n_e, d, f = wg.shape
    tm = FFN_TM
    nv = n_rows // tm + N_EXPERTS - 1
    grid_spec = pltpu.PrefetchScalarGridSpec(
        num_scalar_prefetch=4,
        grid=(nv,),
        in_specs=[pl.BlockSpec((tm, dp), lambda v, ve, vt, lo, hi: (vt[v], 0)),
                  pl.BlockSpec((None, None, d, f), lambda v, ve, vt, lo, hi: (layer, ve[v], 0, 0)),
                  pl.BlockSpec((None, None, d, f), lambda v, ve, vt, lo, hi: (layer, ve[v], 0, 0)),
                  pl.BlockSpec((None, None, f, d), lambda v, ve, vt, lo, hi: (layer, ve[v], 0, 0))],
        out_specs=pl.BlockSpec((tm, dp), lambda v, ve, vt, lo, hi: (vt[v], 0)),
        scratch_shapes=[pltpu.VMEM((d, 2 * f), BF16), pltpu.VMEM((f, d), BF16), pltpu.VMEM((tm, d), F32)],
    )
    return pl.pallas_call(
        _ffn_kernel,
        grid_spec=grid_spec,
        out_shape=jax.ShapeDtypeStruct((n_rows, dp), I32),
        compiler_params=_params("arbitrary"),
        name="ffn",
    )(*meta, xs, wg, wu, wd)


def _combine_kernel(yk_ref, x1_ref, x1p_ref, wtm_ref, p_ref, wsg_ref, wsu_ref, wsd_ref,
                    wpg_ref, wpp_ref, g_ref, b_ref, x2_ref, x2b_ref):
    tt = x1_ref.shape[0]
    xb = _unpack_rows(x1p_ref[...])
    hg = jnp.dot(xb, wsg_ref[...], preferred_element_type=F32)
    hu = jnp.dot(xb, wsu_ref[...], preferred_element_type=F32)
    shared = jnp.dot((hg * _sigmoid(hg) * hu).astype(BF16), wsd_ref[...], preferred_element_type=F32)
    pgate = _sigmoid(jnp.dot(xb, wpg_ref[...], preferred_element_type=F32))
    pproj = jnp.dot(p_ref[...].astype(BF16), wpp_ref[...], preferred_element_type=F32)
    rest = ALPHA * x1_ref[...] + shared + pgate * pproj

    wt = wtm_ref[...]
    r_lo = jnp.zeros((tt, yk_ref.shape[2]), F32)
    r_hi = jnp.zeros((tt, yk_ref.shape[2]), F32)
    for k in range(TOP_K):
        lo, hi = _unpack_pair(yk_ref[k])
        wk = wt[:, k:k + 1]
        r_lo = r_lo + wk * lo
        r_hi = r_hi + wk * hi
    routed = jnp.concatenate([r_lo, r_hi], axis=1)
    x2 = _layer_norm(rest + routed, g_ref[...], b_ref[...])
    x2_ref[...] = x2
    x2b_ref[...] = x2.astype(BF16)


def _combine(yk, x1, x1p, wtm, p, layer, wsg, wsu, wsd, wpg, wpp, g, b):
    m, d = x1.shape
    dp = x1p.shape[1]
    tt = min(COMBINE_TT, m)
    yk = yk.reshape(TOP_K, m, dp)

    def full(a):
        return pl.BlockSpec(a.shape, lambda i: (0,) * a.ndim)

    def rows(a):
        return pl.BlockSpec((tt, a.shape[1]), lambda i: (i, 0))

    return pl.pallas_call(
        _combine_kernel,
        grid=(m // tt,),
        in_specs=[pl.BlockSpec((TOP_K, tt, dp), lambda i: (0, i, 0)),
                  rows(x1), rows(x1p), rows(wtm),
                  pl.BlockSpec((None, tt, p.shape[2]), lambda i: (layer, i, 0)),
                  full(wsg), full(wsu), full(wsd), full(wpg), full(wpp), full(g), full(b)],
        out_specs=[rows(x1), rows(x1)],
        out_shape=[jax.ShapeDtypeStruct((m, d), F32), jax.ShapeDtypeStruct((m, d), BF16)],
        compiler_params=_params("parallel"),
        name="combine",
    )(yk, x1, x1p, wtm, p, wsg, wsu, wsd, wpg, wpp, g, b)


def kernel(x, p, w_in, conv_w, b_igate, b_fgate, mlstm_norm_g, w_branch_a, w_branch_b, w_out, ln1_g, ln1_b, w_router, router_bias, w_exp_gate, w_exp_up, w_exp_down, w_sh_gate, w_sh_up, w_sh_down, w_ple_gate, w_ple_proj, ln2_g, ln2_b):
    bsz, seq, d = x.shape
    m = bsz * seq
    depth = w_in.shape[0]
    hw = HEADS * HEAD_DIM
    if_lo = 3 * d + 4 * hw
    if_hi = if_lo + 2 * HEADS

    xf = x.reshape(m, d)
    xb = xf.astype(BF16)
    w_in_t = jnp.swapaxes(w_in, 1, 2)
    p = p.reshape(depth, m, -1)
    for i in range(depth):
        w_merge = w_in_t[i, if_hi:]
        w_if = jnp.pad(w_in_t[i, if_lo:if_hi], ((0, LANES - 2 * HEADS), (0, 0)))
        gate_bias = jnp.pad(jnp.concatenate([b_igate[i], b_fgate[i]]), (0, LANES - 2 * HEADS)).reshape(1, LANES)

        z = _matmul(xb, w_in_t, BF16, INPROJ_TM, INPROJ_TN, "inproj", layer=i, n=if_lo).reshape(bsz, seq, -1)
        zm = _matmul(xb, w_merge, BF16, INPROJ_TM, INPROJ_TN, "mergeproj").reshape(bsz, seq, -1)
        gates = _matmul(xb, w_if, F32, INPROJ_TM, LANES, "gateproj").reshape(bsz, seq, LANES)
        y_b = _mlstm(z, gates, gate_bias, mlstm_norm_g[i].reshape(1, hw))
        x1, x1p = _mix(z, zm, y_b, xf.reshape(bsz, seq, d), conv_w[i],
                       w_branch_a[i].astype(BF16), w_branch_b[i].astype(BF16), w_out[i].astype(BF16),
                       ln1_g[i].reshape(1, d), ln1_b[i].reshape(1, d))
        x1 = x1.reshape(m, d)
        x1p = x1p.reshape(m, d // 2)

        idx, rank, wtm, cnt = _router(x1, w_router[i].T, router_bias[i].reshape(N_EXPERTS, 1))
        counts = cnt[:, 0].astype(I32)
        row_start = jnp.cumsum(counts) - counts
        expert_ids = jnp.arange(N_EXPERTS, dtype=I32)[:, None, None]
        dest = rank + jnp.sum(jnp.where(idx[None] == expert_ids, row_start[:, None, None], 0), axis=0)
        meta = _visit_metadata(counts, FFN_TM, m * TOP_K)
        xs = _dispatch(x1p, dest)
        ys = _ffn(xs, meta, w_exp_gate, w_exp_up, w_exp_down, i)
        yk = _gather_rows(ys, dest)
        xf, xb = _combine(yk, x1, x1p, wtm, p, i,
                          w_sh_gate[i].astype(BF16), w_sh_up[i].astype(BF16), w_sh_down[i].astype(BF16),
                          w_ple_gate[i].astype(BF16), w_ple_proj[i].astype(BF16),
                          ln2_g[i].reshape(1, d), ln2_b[i].reshape(1, d))
    return xf.reshape(bsz, seq, d)
```

```python
import functools
import math

import jax
import jax.numpy as jnp
from jax import lax
from jax.experimental import pallas as pl
from jax.experimental.pallas import tpu as pltpu
from jax.experimental.pallas import tpu_sc as plsc

F32 = jnp.float32
BF16 = jnp.bfloat16
U32 = jnp.uint32
I32 = jnp.int32

HEADS = 8
HEAD_DIM = 128
N_EXPERTS = 64
N_GROUPS = 8
GROUP_SIZE = N_EXPERTS // N_GROUPS
TOPK_GROUPS = 4
TOP_K = 8
ROUTED_SCALE = 2.5
DEPTH = 4
ALPHA = (2 * DEPTH) ** 0.25
LN_EPS = 1e-5
RMS_EPS = 1e-6
QK_SCALE = HEAD_DIM ** -0.5
LOG_QK_SCALE = math.log(QK_SCALE)

LANES = 128
VMEM_LIMIT = 56 * 1024 * 1024
NEG_INF = float("-inf")

MLSTM_CHUNK = 256
INPROJ_TM, INPROJ_TN = 2048, 1024
MIX_TS = 512
ROUTER_TT = 512
FFN_TM = 2048
FFN_SUB = 512
COMBINE_TT = 256
MOE_STREAMS = 2

SC_CORES = 2
SC_SUBCORES = 16
SC_WORKERS = SC_CORES * SC_SUBCORES
SC_ROWS = 128


def _params(*sem):
    return pltpu.CompilerParams(dimension_semantics=sem, vmem_limit_bytes=VMEM_LIMIT)


def _sigmoid(x):
    return 1.0 / (1.0 + jnp.exp(-x))


def _layer_norm(r, g, b):
    mu = jnp.mean(r, axis=-1, keepdims=True)
    d = r - mu
    var = jnp.mean(d * d, axis=-1, keepdims=True)
    return d * lax.rsqrt(var + LN_EPS) * g + b


def _pack_pair(lo, hi):
    return lax.bitcast_convert_type(pltpu.pack_elementwise([lo, hi], packed_dtype=BF16), I32)


def _unpack_pair(w):
    w = lax.bitcast_convert_type(w, U32)
    lo = pltpu.unpack_elementwise(w, index=0, packed_dtype=BF16, unpacked_dtype=F32)
    hi = pltpu.unpack_elementwise(w, index=1, packed_dtype=BF16, unpacked_dtype=F32)
    return lo, hi


def _unpack_rows(w):
    lo, hi = _unpack_pair(w)
    return jnp.concatenate([lo.astype(BF16), hi.astype(BF16)], axis=1)


def _mm_kernel(x_ref, wt_ref, o_ref):
    o_ref[...] = lax.dot_general(x_ref[...], wt_ref[...].astype(BF16), (((1,), (1,)), ((), ())),
                                 preferred_element_type=F32).astype(o_ref.dtype)


def _matmul(x, wt, out_dtype, tm, tn, name, layer=None, n=None):
    m, k = x.shape
    n = wt.shape[-2] if n is None else n
    tm = min(tm, m)
    if layer is None:
        w_spec = pl.BlockSpec((tn, k), lambda i, j: (j, 0))
    else:
        w_spec = pl.BlockSpec((None, tn, k), lambda i, j: (layer, j, 0))
    return pl.pallas_call(
        _mm_kernel,
        grid=(m // tm, n // tn),
        in_specs=[pl.BlockSpec((tm, k), lambda i, j: (i, 0)), w_spec],
        out_specs=pl.BlockSpec((tm, tn), lambda i, j: (i, j)),
        out_shape=jax.ShapeDtypeStruct((m, n), out_dtype),
        compiler_params=_params("parallel", "parallel"),
        name=name,
    )(x, wt)


def _mlstm_kernel(q_ref, k_ref, v_ref, o_ref, g_ref, gb_ref, ng_ref, y_ref, c_scr, m_scr):
    L = q_ref.shape[1]

    @pl.when(pl.program_id(1) == 0)
    def _():
        c_scr[...] = jnp.zeros_like(c_scr)
        m_scr[...] = jnp.zeros_like(m_scr)

    g = g_ref[0] + gb_ref[...]
    ig = g
    fg = pltpu.roll(g, LANES - HEADS, axis=1)
    log_f = jnp.minimum(fg, 0.0) - jnp.log(1.0 + jnp.exp(-jnp.abs(fg)))
    row = lax.broadcasted_iota(I32, (L, L), 0)
    col = lax.broadcasted_iota(I32, (L, L), 1)
    causal = col <= row
    rows = lax.broadcasted_iota(I32, log_f.shape, 0)

    def scan_rows(x, op, identity):
        step = 1
        while step < L:
            x = op(x, jnp.where(rows >= step, pltpu.roll(x, step, axis=0), identity))
            step *= 2
        return x

    b = scan_rows(log_f, jnp.add, 0.0)
    b_last = b[L - 1:L, :]
    m_prev = m_scr[...]
    a = b_last - b + ig
    m_loc = jnp.max(a, axis=0, keepdims=True)
    w_loc = jnp.exp(a - m_loc)
    m_new = jnp.maximum(b_last + m_prev, m_loc)
    sp = jnp.exp(b_last + m_prev - m_new)
    sl = jnp.exp(m_loc - m_new)
    log_inter = b + m_prev
    r = ig - b
    r_t = jnp.transpose(r)
    m_out = jnp.maximum(log_inter, b + scan_rows(r, jnp.maximum, NEG_INF))
    u = b - m_out + LOG_QK_SCALE
    e_inter = jnp.exp(log_inter - m_out + LOG_QK_SCALE)
    e_floor = jnp.exp(-m_out)

    ones_blk = jnp.ones((L, HEAD_DIM), BF16)
    heads = range(HEADS)
    lanes = [slice(h * HEAD_DIM, (h + 1) * HEAD_DIM) for h in heads]
    q = [q_ref[0, :, lanes[h]] for h in heads]
    k = [k_ref[0, :, lanes[h]] for h in heads]
    v_aug = [jnp.concatenate([v_ref[0, :, lanes[h]], ones_blk], axis=1) for h in heads]
    s1 = [lax.dot_general(q[h], k[h], (((1,), (1,)), ((), ())), preferred_element_type=F32) for h in heads]
    c_prev = [c_scr[h] for h in heads]
    inter = [jnp.dot(q[h], c_prev[h].astype(BF16), preferred_element_type=F32) for h in heads]
    s = [(s1[h] * jnp.where(causal, jnp.exp(u[:, h:h + 1] + r_t[h:h + 1, :]), 0.0)).astype(BF16) for h in heads]
    intra = [jnp.dot(s[h], v_aug[h], preferred_element_type=F32) for h in heads]
    tot = [intra[h] + e_inter[:, h:h + 1] * inter[h] for h in heads]
    hh = [tot[h][:, :HEAD_DIM] / jnp.maximum(jnp.abs(tot[h][:, HEAD_DIM:]), e_floor[:, h:h + 1]) for h in heads]
    hh = [hh[h] * lax.rsqrt(jnp.mean(hh[h] * hh[h], axis=1, keepdims=True) + RMS_EPS) for h in heads]
    for h in heads:
        og = _sigmoid(o_ref[0, :, lanes[h]].astype(F32))
        y_ref[0, :, lanes[h]] = (hh[h] * ng_ref[:, lanes[h]] * og).astype(y_ref.dtype)
    kw = [(k[h].astype(F32) * w_loc[:, h:h + 1]).astype(BF16) for h in heads]
    c_loc = [lax.dot_general(kw[h], v_aug[h], (((0,), (0,)), ((), ())), preferred_element_type=F32) for h in heads]
    for h in heads:
        c_scr[h] = sp[:, h:h + 1] * c_prev[h] + sl[:, h:h + 1] * c_loc[h]
    m_scr[...] = m_new


def _mlstm(z, gates, gate_bias, norm_g):
    bsz, seq, _ = z.shape
    L = MLSTM_CHUNK
    hw = HEADS * HEAD_DIM

    def zspec(cb):
        return pl.BlockSpec((1, L, hw), lambda b, c, cb=cb: (b, c, cb))

    return pl.pallas_call(
        _mlstm_kernel,
        grid=(bsz, seq // L),
        in_specs=[zspec(3), zspec(4), zspec(5), zspec(6),
                  pl.BlockSpec((1, L, LANES), lambda b, c: (b, c, 0)),
                  pl.BlockSpec((1, LANES), lambda b, c: (0, 0)),
                  pl.BlockSpec((1, hw), lambda b, c: (0, 0))],
        out_specs=pl.BlockSpec((1, L, hw), lambda b, c: (b, c, 0)),
        out_shape=jax.ShapeDtypeStruct((bsz, seq, hw), BF16),
        scratch_shapes=[pltpu.VMEM((HEADS, HEAD_DIM, 2 * HEAD_DIM), F32),
                        pltpu.VMEM((1, LANES), F32)],
        compiler_params=_params("parallel", "arbitrary"),
        name="mlstm",
    )(z, z, z, z, gates, gate_bias, norm_g)


def _mix_kernel(cin_ref, cout_ref, cval_ref, mg0_ref, mg1_ref, yb_ref, x_ref, cw_ref,
                wa_ref, wb_ref, wo_ref, g_ref, b_ref, x1_ref, x1p_ref, carry):
    ts = x_ref.shape[1]

    @pl.when(pl.program_id(1) == 0)
    def _():
        carry[...] = jnp.zeros_like(carry)

    u = cin_ref[0].astype(F32) * cval_ref[0].astype(F32)
    prev = carry[...]
    carry[...] = u[ts - 8:, :]
    r8 = lax.broadcasted_iota(I32, (8, u.shape[1]), 0)

    def shifted(k):
        body = pltpu.roll(u, k, axis=0)
        head = jnp.where(r8 < k, pltpu.roll(prev, k, axis=0), body[:8, :])
        return jnp.concatenate([head, body[8:, :]], axis=0)

    cw = cw_ref[...]
    conv = cw[0:1, :] * shifted(2) + cw[1:2, :] * shifted(1) + cw[2:3, :] * u
    y_a = (cout_ref[0].astype(F32) * conv).astype(BF16)
    pa = jnp.dot(y_a, wa_ref[...], preferred_element_type=F32)
    pb = jnp.dot(yb_ref[0], wb_ref[...], preferred_element_type=F32)
    mixed = _sigmoid(mg0_ref[0].astype(F32)) * pa + _sigmoid(mg1_ref[0].astype(F32)) * pb
    hmix = jnp.dot(mixed.astype(BF16), wo_ref[...], preferred_element_type=F32)
    x1 = _layer_norm(ALPHA * x_ref[0] + hmix, g_ref[...], b_ref[...])
    x1_ref[0] = x1
    half = x1.shape[1] // 2
    x1p_ref[0] = _pack_pair(x1[:, :half], x1[:, half:])


def _mix(z, zm, y_b, x, conv_w, wa, wb, wo, g, b):
    bsz, seq, d = x.shape
    ts = min(MIX_TS, seq)

    def zspec(cb):
        return pl.BlockSpec((1, ts, d), lambda i, j, cb=cb: (i, j, cb))

    def full(shape):
        return pl.BlockSpec(shape, lambda i, j: (0,) * len(shape))

    tile = pl.BlockSpec((1, ts, d), lambda i, j: (i, j, 0))
    ptile = pl.BlockSpec((1, ts, d // 2), lambda i, j: (i, j, 0))
    return pl.pallas_call(
        _mix_kernel,
        grid=(bsz, seq // ts),
        in_specs=[zspec(0), zspec(1), zspec(2), zspec(0), zspec(1), tile, tile,
                  full(conv_w.shape), full(wa.shape), full(wb.shape), full(wo.shape),
                  full(g.shape), full(b.shape)],
        out_specs=[tile, ptile],
        out_shape=[jax.ShapeDtypeStruct((bsz, seq, d), F32),
                   jax.ShapeDtypeStruct((bsz, seq, d // 2), I32)],
        scratch_shapes=[pltpu.VMEM((8, d), F32)],
        compiler_params=_params("parallel", "arbitrary"),
        name="mix",
    )(z, z, z, zm, zm, y_b, x, conv_w, wa, wb, wo, g, b)


def _router_kernel(x_ref, wrt_ref, rb_ref, idx_ref, rank_ref, wtm_ref, cnt_ref, carry):
    t = x_ref.shape[0]

    @pl.when(pl.program_id(0) == 0)
    def _():
        carry[...] = jnp.zeros_like(carry)

    logits = lax.dot_general(wrt_ref[...], x_ref[...], (((1,), (1,)), ((), ())),
                             precision=lax.Precision.HIGHEST, preferred_element_type=F32)
    scores = _sigmoid(logits)
    sel = (scores + rb_ref[...]).reshape(N_GROUPS, GROUP_SIZE, t)
    scores = scores.reshape(N_GROUPS, GROUP_SIZE, t)
    shape3 = (N_GROUPS, GROUP_SIZE, t)
    sub = lax.broadcasted_iota(I32, shape3, 1)
    grp = lax.broadcasted_iota(I32, shape3, 0)
    eidx = grp * GROUP_SIZE + sub
    m1 = jnp.max(sel, axis=1, keepdims=True)
    first = jnp.min(jnp.where(sel == m1, sub, GROUP_SIZE), axis=1, keepdims=True)
    m2 = jnp.max(jnp.where(sub == first, NEG_INF, sel), axis=1, keepdims=True)
    rem = m1 + m2
    gidx = lax.broadcasted_iota(I32, rem.shape, 0)
    gmask = jnp.zeros(rem.shape, F32)
    for _ in range(TOPK_GROUPS):
        mx = jnp.max(rem, axis=0, keepdims=True)
        pick = gidx == jnp.min(jnp.where(rem == mx, gidx, N_GROUPS), axis=0, keepdims=True)
        gmask = jnp.where(pick, 1.0, gmask)
        rem = jnp.where(pick, NEG_INF, rem)
    masked = jnp.where(jnp.broadcast_to(gmask, shape3) > 0.5, sel, NEG_INF)
    chosen = jnp.zeros(shape3, F32)
    picks = []
    for _ in range(TOP_K):
        mx = jnp.max(jnp.max(masked, axis=1, keepdims=True), axis=0, keepdims=True)
        cand = jnp.where(masked == mx, eidx, N_EXPERTS)
        fi = jnp.min(jnp.min(cand, axis=1, keepdims=True), axis=0, keepdims=True)
        pick = eidx == fi
        picks.append((fi, pick))
        chosen = jnp.where(pick, 1.0, chosen)
        masked = jnp.where(pick, NEG_INF, masked)
    w = chosen * scores
    denom = jnp.sum(jnp.sum(w, axis=1, keepdims=True), axis=0, keepdims=True)
    gate3 = w / denom * ROUTED_SCALE

    chosen2 = chosen.reshape(N_EXPERTS, t).astype(BF16)
    tok_r = lax.broadcasted_iota(I32, (t, t), 0)
    tok_c = lax.broadcasted_iota(I32, (t, t), 1)
    before = jnp.where(tok_r < tok_c, 1.0, 0.0).astype(BF16)
    prefix = (jnp.dot(chosen2, before, preferred_element_type=F32) + carry[...]).reshape(shape3)
    carry[...] += jnp.dot(chosen2, jnp.ones((t, t), BF16), preferred_element_type=F32)
    cnt_ref[...] = carry[:, :LANES]

    def pick_sum(pick, val):
        return jnp.sum(jnp.sum(jnp.where(pick, val, 0.0), axis=1, keepdims=True), axis=0, keepdims=True)

    sub8 = lax.broadcasted_iota(I32, (TOP_K, t), 0)
    idx8 = jnp.zeros((TOP_K, t), I32)
    rank8 = jnp.zeros((TOP_K, t), F32)
    w8 = jnp.zeros((TOP_K, t), F32)
    for k, (fi, pick) in enumerate(picks):
        idx8 = jnp.where(sub8 == k, jnp.broadcast_to(fi.reshape(1, t), (TOP_K, t)), idx8)
        rank8 = jnp.where(sub8 == k, jnp.broadcast_to(pick_sum(pick, prefix).reshape(1, t), (TOP_K, t)), rank8)
        w8 = jnp.where(sub8 == k, jnp.broadcast_to(pick_sum(pick, gate3).reshape(1, t), (TOP_K, t)), w8)
    idx_ref[...] = idx8
    rank_ref[...] = rank8.astype(I32)
    pad = jnp.zeros((LANES - TOP_K, t), F32)
    wtm_ref[...] = jnp.transpose(jnp.concatenate([w8, pad], axis=0))


def _router(x1, w_router_t, router_bias, part):
    d = x1.shape[1]
    m = x1.shape[0] // MOE_STREAMS
    tt = min(ROUTER_TT, m)
    first = part * (m // tt)
    kt = pl.BlockSpec((TOP_K, tt), lambda i: (0, i))
    return pl.pallas_call(
        _router_kernel,
        grid=(m // tt,),
        in_specs=[pl.BlockSpec((tt, d), lambda i: (first + i, 0)),
                  pl.BlockSpec((N_EXPERTS, d), lambda i: (0, 0)),
                  pl.BlockSpec((N_EXPERTS, 1), lambda i: (0, 0))],
        out_specs=[kt, kt,
                   pl.BlockSpec((tt, LANES), lambda i: (i, 0)),
                   pl.BlockSpec((N_EXPERTS, LANES), lambda i: (0, 0))],
        out_shape=[jax.ShapeDtypeStruct((TOP_K, m), I32),
                   jax.ShapeDtypeStruct((TOP_K, m), I32),
                   jax.ShapeDtypeStruct((m, LANES), F32),
                   jax.ShapeDtypeStruct((N_EXPERTS, LANES), F32)],
        scratch_shapes=[pltpu.VMEM((N_EXPERTS, tt), F32)],
        compiler_params=_params("arbitrary"),
        name="router",
    )(x1, w_router_t, router_bias)


def _visit_metadata(counts, tm, n_rows):
    nt = n_rows // tm
    nv = nt + N_EXPERTS - 1
    ends = jnp.cumsum(counts)
    starts = ends - counts
    first_tile = starts // tm
    ntiles = jnp.where(counts > 0, (ends - 1) // tm - first_tile + 1, 0)
    vend = jnp.cumsum(ntiles)
    vstart = vend - ntiles
    v = jnp.arange(nv, dtype=I32)
    valid = v < vend[-1]
    ve = jnp.minimum(jnp.sum((v[:, None] >= vend[None, :]).astype(I32), axis=1), N_EXPERTS - 1)
    ve = jnp.where(valid, ve, ve[jnp.maximum(vend[-1] - 1, 0)])
    vt = jnp.where(valid, first_tile[ve] + v - vstart[ve], nt - 1)
    lo = jnp.where(valid, jnp.clip(starts[ve] - vt * tm, 0, tm), 0)
    hi = jnp.where(valid, jnp.clip(ends[ve] - vt * tm, 0, tm), 0)
    return ve.astype(I32), vt.astype(I32), lo.astype(I32), hi.astype(I32)


def _sc_mesh():
    return plsc.VectorSubcoreMesh(core_axis_name="c", subcore_axis_name="s")


def _sc_worker():
    return lax.axis_index("s") * SC_CORES + lax.axis_index("c")


def _dispatch(x1p, dest, part):
    dp = x1p.shape[1]
    m = dest.shape[1]
    blocks = m // SC_ROWS
    per_worker = blocks // SC_WORKERS
    x_first = part * blocks
    table = dest.reshape(TOP_K, blocks, SC_ROWS).transpose(1, 0, 2).reshape(blocks * TOP_K, SC_ROWS)

    def body(x_hbm, idx_hbm, xs_hbm, idx_v, rows_v, sem):
        first = _sc_worker() * per_worker

        @pl.loop(0, per_worker)
        def _(j):
            b = first + j
            pltpu.sync_copy(idx_hbm.at[pl.ds(pl.multiple_of(b * TOP_K, TOP_K), TOP_K)], idx_v)
            pltpu.sync_copy(x_hbm.at[pl.ds(pl.multiple_of((x_first + b) * SC_ROWS, SC_ROWS), SC_ROWS)], rows_v)
            copies = [pltpu.async_copy(rows_v, xs_hbm.at[idx_v.at[k]], sem) for k in range(TOP_K)]
            for cp in copies:
                cp.wait()

    return pl.kernel(
        body,
        out_type=jax.ShapeDtypeStruct((m * TOP_K, dp), x1p.dtype),
        mesh=_sc_mesh(),
        scratch_types=[pltpu.VMEM((TOP_K, SC_ROWS), I32), pltpu.VMEM((SC_ROWS, dp), x1p.dtype),
                       pltpu.SemaphoreType.DMA],
        name="dispatch",
    )(x1p, table)


def _gather_rows(ys, dest):
    n_rows, dp = ys.shape
    idx = dest.reshape(n_rows // SC_ROWS, SC_ROWS)
    per_worker = n_rows // SC_ROWS // SC_WORKERS

    def body(ys_hbm, idx_hbm, out_hbm, idx_v, rows_v, sem):
        first = _sc_worker() * per_worker

        @pl.loop(0, per_worker)
        def _(j):
            b = first + j
            pltpu.sync_copy(idx_hbm.at[pl.ds(b, 1)], idx_v)
            pltpu.async_copy(ys_hbm.at[idx_v.at[0]], rows_v, sem).wait()
            pltpu.sync_copy(rows_v, out_hbm.at[pl.ds(pl.multiple_of(b * SC_ROWS, SC_ROWS), SC_ROWS)])

    return pl.kernel(
        body,
        out_type=jax.ShapeDtypeStruct((n_rows, dp), ys.dtype),
        mesh=_sc_mesh(),
        scratch_types=[pltpu.VMEM((1, SC_ROWS), I32), pltpu.VMEM((SC_ROWS, dp), ys.dtype),
                       pltpu.SemaphoreType.DMA],
        name="gather_rows",
    )(ys, idx)


def _ffn_kernel(ve_ref, vt_ref, lo_ref, hi_ref, xs_ref, wg_ref, wu_ref, wd_ref, ys_ref,
                wgu_b, wd_b, acc):
    v = pl.program_id(0)
    lo = lo_ref[v]
    hi = hi_ref[v]
    tm = xs_ref.shape[0]
    f = wg_ref.shape[1]

    @pl.when((v == 0) | (ve_ref[v] != ve_ref[jnp.maximum(v - 1, 0)]))
    def _():
        wgu_b[:, :f] = wg_ref[...].astype(BF16)
        wgu_b[:, f:] = wu_ref[...].astype(BF16)
        wd_b[...] = wd_ref[...].astype(BF16)

    def pack_rows(a):
        half = a.shape[1] // 2
        return _pack_pair(a[:, :half], a[:, half:])

    for sb in range(tm // FFN_SUB):
        s0 = sb * FFN_SUB
        rows = pl.ds(s0, FFN_SUB)
        lo_s = jnp.clip(lo - s0, 0, FFN_SUB)
        hi_s = jnp.clip(hi - s0, 0, FFN_SUB)

        @pl.when(hi_s > lo_s)
        def _():
            x = _unpack_rows(xs_ref[rows, :])
            h2 = jnp.dot(x, wgu_b[...], preferred_element_type=F32)
            hg = h2[:, :f]
            hu = h2[:, f:]
            r = lax.broadcasted_iota(I32, hg.shape, 0)
            mine = (r >= lo_s) & (r < hi_s)
            hmid = jnp.where(mine, hg * _sigmoid(hg) * hu, 0.0).astype(BF16)
            y = jnp.dot(hmid, wd_b[...], preferred_element_type=F32)

            @pl.when((lo_s == 0) & (hi_s == FFN_SUB))
            def _():
                ys_ref[rows, :] = pack_rows(y)

            @pl.when((lo_s == 0) & (hi_s < FFN_SUB))
            def _():
                acc[rows, :] = y

            @pl.when(lo_s > 0)
            def _():
                acc[rows, :] += y

            @pl.when((lo_s > 0) & (hi_s == FFN_SUB))
            def _():
                ys_ref[rows, :] = pack_rows(acc[rows, :])


def _ffn(xs, meta, wg, wu, wd, layer):
    n_rows, dp = xs.shape
    _, n_e, d, f = wg.shape
    tm = FFN_TM
    nv = n_rows // tm + N_EXPERTS - 1
    grid_spec = pltpu.PrefetchScalarGridSpec(
        num_scalar_prefetch=4,
        grid=(nv,),
        in_specs=[pl.BlockSpec((tm, dp), lambda v, ve, vt, lo, hi: (vt[v], 0)),
                  pl.BlockSpec((None, None, d, f), lambda v, ve, vt, lo, hi: (layer, ve[v], 0, 0)),
                  pl.BlockSpec((None, None, d, f), lambda v, ve, vt, lo, hi: (layer, ve[v], 0, 0)),
                  pl.BlockSpec((None, None, f, d), lambda v, ve, vt, lo, hi: (layer, ve[v], 0, 0))],
        out_specs=pl.BlockSpec((tm, dp), lambda v, ve, vt, lo, hi: (vt[v], 0)),
        scratch_shapes=[pltpu.VMEM((d, 2 * f), BF16), pltpu.VMEM((f, d), BF16), pltpu.VMEM((tm, d), F32)],
    )
    return pl.pallas_call(
        _ffn_kernel,
        grid_spec=grid_spec,
        out_shape=jax.ShapeDtypeStruct((n_rows, dp), I32),
        compiler_params=_params("arbitrary"),
        name="ffn",
    )(*meta, xs, wg, wu, wd)


def _combine_kernel(*refs, n_prev):
    (yk_ref, x1_ref, wtm_ref, p_ref, wsg_ref, wsu_ref, wsd_ref,
     wpg_ref, wpp_ref, g_ref, b_ref, x2_ref, x2b_ref) = refs[n_prev:]
    tt = x1_ref.shape[0]
    xb = x1_ref[...].astype(BF16)
    hg = jnp.dot(xb, wsg_ref[...], preferred_element_type=F32)
    hu = jnp.dot(xb, wsu_ref[...], preferred_element_type=F32)
    shared = jnp.dot((hg * _sigmoid(hg) * hu).astype(BF16), wsd_ref[...], preferred_element_type=F32)
    pgate = _sigmoid(jnp.dot(xb, wpg_ref[...], preferred_element_type=F32))
    pproj = jnp.dot(p_ref[...].astype(BF16), wpp_ref[...], preferred_element_type=F32)
    rest = ALPHA * x1_ref[...] + shared + pgate * pproj

    wt = wtm_ref[...]
    r_lo = jnp.zeros((tt, yk_ref.shape[2]), F32)
    r_hi = jnp.zeros((tt, yk_ref.shape[2]), F32)
    for k in range(TOP_K):
        lo, hi = _unpack_pair(yk_ref[k])
        wk = wt[:, k:k + 1]
        r_lo = r_lo + wk * lo
        r_hi = r_hi + wk * hi
    routed = jnp.concatenate([r_lo, r_hi], axis=1)
    x2 = _layer_norm(rest + routed, g_ref[...], b_ref[...])
    x2_ref[...] = x2
    x2b_ref[...] = x2.astype(BF16)


def _combine(yk, x1, wtm, p, layer, part, prev, wsg, wsu, wsd, wpg, wpp, g, b):
    m_all, d = x1.shape
    m = wtm.shape[0]
    tt = min(COMBINE_TT, m)
    first = part * (m // tt)
    yk = yk.reshape(TOP_K, m, d // 2)

    def full(a):
        return pl.BlockSpec(a.shape, lambda i: (0,) * a.ndim)

    token_rows = pl.BlockSpec((tt, d), lambda i: (first + i, 0))
    untouched = pl.BlockSpec(memory_space=pl.ANY)
    return pl.pallas_call(
        functools.partial(_combine_kernel, n_prev=len(prev)),
        grid=(m // tt,),
        in_specs=[untouched] * len(prev) + [
            pl.BlockSpec((TOP_K, tt, d // 2), lambda i: (0, i, 0)),
            token_rows,
            pl.BlockSpec((tt, LANES), lambda i: (i, 0)),
            pl.BlockSpec((None, tt, p.shape[2]), lambda i: (layer, first + i, 0)),
            full(wsg), full(wsu), full(wsd), full(wpg), full(wpp), full(g), full(b)],
        out_specs=[token_rows, token_rows],
        out_shape=[jax.ShapeDtypeStruct((m_all, d), F32), jax.ShapeDtypeStruct((m_all, d), BF16)],
        input_output_aliases={j: j for j in range(len(prev))},
        compiler_params=_params("parallel"),
        name="combine",
    )(*prev, yk, x1, wtm, p, wsg, wsu, wsd, wpg, wpp, g, b)


def kernel(x, p, w_in, conv_w, b_igate, b_fgate, mlstm_norm_g, w_branch_a, w_branch_b, w_out, ln1_g, ln1_b, w_router, router_bias, w_exp_gate, w_exp_up, w_exp_down, w_sh_gate, w_sh_up, w_sh_down, w_ple_gate, w_ple_proj, ln2_g, ln2_b):
    bsz, seq, d = x.shape
    m = bsz * seq
    depth = w_in.shape[0]
    hw = HEADS * HEAD_DIM
    if_lo = 3 * d + 4 * hw
    if_hi = if_lo + 2 * HEADS

    xf = x.reshape(m, d)
    xb = xf.astype(BF16)
    w_in_t = jnp.swapaxes(w_in, 1, 2)
    p = p.reshape(depth, m, -1)
    for i in range(depth):
        w_merge = w_in_t[i, if_hi:]
        w_if = jnp.pad(w_in_t[i, if_lo:if_hi], ((0, LANES - 2 * HEADS), (0, 0)))
        gate_bias = jnp.pad(jnp.concatenate([b_igate[i], b_fgate[i]]), (0, LANES - 2 * HEADS)).reshape(1, LANES)

        z = _matmul(xb, w_in_t, BF16, INPROJ_TM, INPROJ_TN, "inproj", layer=i, n=if_lo).reshape(bsz, seq, -1)
        zm = _matmul(xb, w_merge, BF16, INPROJ_TM, INPROJ_TN, "mergeproj").reshape(bsz, seq, -1)
        gates = _matmul(xb, w_if, F32, INPROJ_TM, LANES, "gateproj").reshape(bsz, seq, LANES)
        y_b = _mlstm(z, gates, gate_bias, mlstm_norm_g[i].reshape(1, hw))
        x1, x1p = _mix(z, zm, y_b, xf.reshape(bsz, seq, d), conv_w[i],
                       w_branch_a[i].astype(BF16), w_branch_b[i].astype(BF16), w_out[i].astype(BF16),
                       ln1_g[i].reshape(1, d), ln1_b[i].reshape(1, d))
        x1 = x1.reshape(m, d)
        x1p = x1p.reshape(m, d // 2)

        ms = m // MOE_STREAMS
        shared_w = (w_sh_gate[i].astype(BF16), w_sh_up[i].astype(BF16), w_sh_down[i].astype(BF16),
                    w_ple_gate[i].astype(BF16), w_ple_proj[i].astype(BF16),
                    ln2_g[i].reshape(1, d), ln2_b[i].reshape(1, d))
        w_router_t = w_router[i].T
        rb = router_bias[i].reshape(N_EXPERTS, 1)
        expert_ids = jnp.arange(N_EXPERTS, dtype=I32)[:, None, None]
        outs = ()
        for part in range(MOE_STREAMS):
            idx, rank, wtm, cnt = _router(x1, w_router_t, rb, part)
            counts = cnt[:, 0].astype(I32)
            row_start = jnp.cumsum(counts) - counts
            dest = rank + jnp.sum(jnp.where(idx[None] == expert_ids, row_start[:, None, None], 0), axis=0)
            meta = _visit_metadata(counts, FFN_TM, ms * TOP_K)
            xs = _dispatch(x1p, dest, part)
            ys = _ffn(xs, meta, w_exp_gate, w_exp_up, w_exp_down, i)
            yk = _gather_rows(ys, dest)
            outs = tuple(_combine(yk, x1, wtm, p, i, part, outs, *shared_w))
        xf, xb = outs
    return xf.reshape(bsz, seq, d)
```

```python
import functools
import math

import jax
import jax.numpy as jnp
from jax import lax
from jax.experimental import pallas as pl
from jax.experimental.pallas import tpu as pltpu
from jax.experimental.pallas import tpu_sc as plsc

F32 = jnp.float32
BF16 = jnp.bfloat16
U32 = jnp.uint32
I32 = jnp.int32

HEADS = 8
HEAD_DIM = 128
N_EXPERTS = 64
N_GROUPS = 8
GROUP_SIZE = N_EXPERTS // N_GROUPS
TOPK_GROUPS = 4
TOP_K = 8
ROUTED_SCALE = 2.5
DEPTH = 4
ALPHA = (2 * DEPTH) ** 0.25
LN_EPS = 1e-5
RMS_EPS = 1e-6
QK_SCALE = HEAD_DIM ** -0.5
LOG_QK_SCALE = math.log(QK_SCALE)

LANES = 128
VMEM_LIMIT = 56 * 1024 * 1024
NEG_INF = float("-inf")

MLSTM_CHUNK = 256
INPROJ_TM, INPROJ_TN = 2048, 1024
MIX_TS = 512
ROUTER_TT = 512
FFN_TM = 2048
FFN_SUB = 512
COMBINE_TT = 512
MOE_STREAMS = 1

SC_CORES = 2
SC_SUBCORES = 16
SC_WORKERS = SC_CORES * SC_SUBCORES
SC_ROWS = 128


def _params(*sem):
    return pltpu.CompilerParams(dimension_semantics=sem, vmem_limit_bytes=VMEM_LIMIT)


def _sigmoid(x):
    return 1.0 / (1.0 + jnp.exp(-x))


def _layer_norm(r, g, b):
    mu = jnp.mean(r, axis=-1, keepdims=True)
    d = r - mu
    var = jnp.mean(d * d, axis=-1, keepdims=True)
    return d * lax.rsqrt(var + LN_EPS) * g + b


def _pack_pair(lo, hi):
    return lax.bitcast_convert_type(pltpu.pack_elementwise([lo, hi], packed_dtype=BF16), I32)


def _unpack_pair(w):
    w = lax.bitcast_convert_type(w, U32)
    lo = pltpu.unpack_elementwise(w, index=0, packed_dtype=BF16, unpacked_dtype=F32)
    hi = pltpu.unpack_elementwise(w, index=1, packed_dtype=BF16, unpacked_dtype=F32)
    return lo, hi


def _unpack_rows(w):
    lo, hi = _unpack_pair(w)
    return jnp.concatenate([lo.astype(BF16), hi.astype(BF16)], axis=1)


def _mm_kernel(x_ref, wt_ref, o_ref):
    o_ref[...] = lax.dot_general(x_ref[...], wt_ref[...].astype(BF16), (((1,), (1,)), ((), ())),
                                 preferred_element_type=F32).astype(o_ref.dtype)


def _matmul(x, wt, out_dtype, tm, tn, name, layer=None, n=None):
    m, k = x.shape
    n = wt.shape[-2] if n is None else n
    tm = min(tm, m)
    if layer is None:
        w_spec = pl.BlockSpec((tn, k), lambda i, j: (j, 0))
    else:
        w_spec = pl.BlockSpec((None, tn, k), lambda i, j: (layer, j, 0))
    return pl.pallas_call(
        _mm_kernel,
        grid=(m // tm, n // tn),
        in_specs=[pl.BlockSpec((tm, k), lambda i, j: (i, 0)), w_spec],
        out_specs=pl.BlockSpec((tm, tn), lambda i, j: (i, j)),
        out_shape=jax.ShapeDtypeStruct((m, n), out_dtype),
        compiler_params=_params("parallel", "parallel"),
        name=name,
    )(x, wt)


def _mlstm_kernel(q_ref, k_ref, v_ref, o_ref, g_ref, gb_ref, ng_ref, y_ref, c_scr, m_scr):
    L = q_ref.shape[1]

    @pl.when(pl.program_id(1) == 0)
    def _():
        c_scr[...] = jnp.zeros_like(c_scr)
        m_scr[...] = jnp.zeros_like(m_scr)

    g = g_ref[0] + gb_ref[...]
    ig = g
    fg = pltpu.roll(g, LANES - HEADS, axis=1)
    log_f = jnp.minimum(fg, 0.0) - jnp.log(1.0 + jnp.exp(-jnp.abs(fg)))
    row = lax.broadcasted_iota(I32, (L, L), 0)
    col = lax.broadcasted_iota(I32, (L, L), 1)
    causal = col <= row
    rows = lax.broadcasted_iota(I32, log_f.shape, 0)

    def scan_rows(x, op, identity):
        step = 1
        while step < L:
            x = op(x, jnp.where(rows >= step, pltpu.roll(x, step, axis=0), identity))
            step *= 2
        return x

    b = scan_rows(log_f, jnp.add, 0.0)
    b_last = b[L - 1:L, :]
    m_prev = m_scr[...]
    a = b_last - b + ig
    m_loc = jnp.max(a, axis=0, keepdims=True)
    w_loc = jnp.exp(a - m_loc)
    m_new = jnp.maximum(b_last + m_prev, m_loc)
    sp = jnp.exp(b_last + m_prev - m_new)
    sl = jnp.exp(m_loc - m_new)
    log_inter = b + m_prev
    r = ig - b
    r_t = jnp.transpose(r)
    m_out = jnp.maximum(log_inter, b + scan_rows(r, jnp.maximum, NEG_INF))
    u = b - m_out + LOG_QK_SCALE
    e_inter = jnp.exp(log_inter - m_out + LOG_QK_SCALE)
    e_floor = jnp.exp(-m_out)

    ones_blk = jnp.ones((L, HEAD_DIM), BF16)
    heads = range(HEADS)
    lanes = [slice(h * HEAD_DIM, (h + 1) * HEAD_DIM) for h in heads]
    q = [q_ref[0, :, lanes[h]] for h in heads]
    k = [k_ref[0, :, lanes[h]] for h in heads]
    v_aug = [jnp.concatenate([v_ref[0, :, lanes[h]], ones_blk], axis=1) for h in heads]
    s1 = [lax.dot_general(q[h], k[h], (((1,), (1,)), ((), ())), preferred_element_type=F32) for h in heads]
    c_prev = [c_scr[h] for h in heads]
    inter = [jnp.dot(q[h], c_prev[h].astype(BF16), preferred_element_type=F32) for h in heads]
    s = [(s1[h] * jnp.where(causal, jnp.exp(u[:, h:h + 1] + r_t[h:h + 1, :]), 0.0)).astype(BF16) for h in heads]
    intra = [jnp.dot(s[h], v_aug[h], preferred_element_type=F32) for h in heads]
    tot = [intra[h] + e_inter[:, h:h + 1] * inter[h] for h in heads]
    hh = [tot[h][:, :HEAD_DIM] / jnp.maximum(jnp.abs(tot[h][:, HEAD_DIM:]), e_floor[:, h:h + 1]) for h in heads]
    hh = [hh[h] * lax.rsqrt(jnp.mean(hh[h] * hh[h], axis=1, keepdims=True) + RMS_EPS) for h in heads]
    for h in heads:
        og = _sigmoid(o_ref[0, :, lanes[h]].astype(F32))
        y_ref[0, :, lanes[h]] = (hh[h] * ng_ref[:, lanes[h]] * og).astype(y_ref.dtype)
    kw = [(k[h].astype(F32) * w_loc[:, h:h + 1]).astype(BF16) for h in heads]
    c_loc = [lax.dot_general(kw[h], v_aug[h], (((0,), (0,)), ((), ())), preferred_element_type=F32) for h in heads]
    for h in heads:
        c_scr[h] = sp[:, h:h + 1] * c_prev[h] + sl[:, h:h + 1] * c_loc[h]
    m_scr[...] = m_new


def _mlstm(z, gates, gate_bias, norm_g):
    bsz, seq, _ = z.shape
    L = MLSTM_CHUNK
    hw = HEADS * HEAD_DIM

    def zspec(cb):
        return pl.BlockSpec((1, L, hw), lambda b, c, cb=cb: (b, c, cb))

    return pl.pallas_call(
        _mlstm_kernel,
        grid=(bsz, seq // L),
        in_specs=[zspec(3), zspec(4), zspec(5), zspec(6),
                  pl.BlockSpec((1, L, LANES), lambda b, c: (b, c, 0)),
                  pl.BlockSpec((1, LANES), lambda b, c: (0, 0)),
                  pl.BlockSpec((1, hw), lambda b, c: (0, 0))],
        out_specs=pl.BlockSpec((1, L, hw), lambda b, c: (b, c, 0)),
        out_shape=jax.ShapeDtypeStruct((bsz, seq, hw), BF16),
        scratch_shapes=[pltpu.VMEM((HEADS, HEAD_DIM, 2 * HEAD_DIM), F32),
                        pltpu.VMEM((1, LANES), F32)],
        compiler_params=_params("parallel", "arbitrary"),
        name="mlstm",
    )(z, z, z, z, gates, gate_bias, norm_g)


def _mix_kernel(cin_ref, cout_ref, cval_ref, mg0_ref, mg1_ref, yb_ref, x_ref, cw_ref,
                wa_ref, wb_ref, wo_ref, g_ref, b_ref, x1_ref, x1p_ref, carry):
    ts = x_ref.shape[1]

    @pl.when(pl.program_id(1) == 0)
    def _():
        carry[...] = jnp.zeros_like(carry)

    u = cin_ref[0].astype(F32) * cval_ref[0].astype(F32)
    prev = carry[...]
    carry[...] = u[ts - 8:, :]
    r8 = lax.broadcasted_iota(I32, (8, u.shape[1]), 0)

    def shifted(k):
        body = pltpu.roll(u, k, axis=0)
        head = jnp.where(r8 < k, pltpu.roll(prev, k, axis=0), body[:8, :])
        return jnp.concatenate([head, body[8:, :]], axis=0)

    cw = cw_ref[...]
    conv = cw[0:1, :] * shifted(2) + cw[1:2, :] * shifted(1) + cw[2:3, :] * u
    y_a = (cout_ref[0].astype(F32) * conv).astype(BF16)
    pa = jnp.dot(y_a, wa_ref[...], preferred_element_type=F32)
    pb = jnp.dot(yb_ref[0], wb_ref[...], preferred_element_type=F32)
    mixed = _sigmoid(mg0_ref[0].astype(F32)) * pa + _sigmoid(mg1_ref[0].astype(F32)) * pb
    hmix = jnp.dot(mixed.astype(BF16), wo_ref[...], preferred_element_type=F32)
    x1 = _layer_norm(ALPHA * x_ref[0] + hmix, g_ref[...], b_ref[...])
    x1_ref[0] = x1
    half = x1.shape[1] // 2
    x1p_ref[0] = _pack_pair(x1[:, :half], x1[:, half:])


def _mix(z, zm, y_b, x, conv_w, wa, wb, wo, g, b):
    bsz, seq, d = x.shape
    ts = min(MIX_TS, seq)

    def zspec(cb):
        return pl.BlockSpec((1, ts, d), lambda i, j, cb=cb: (i, j, cb))

    def full(shape):
        return pl.BlockSpec(shape, lambda i, j: (0,) * len(shape))

    tile = pl.BlockSpec((1, ts, d), lambda i, j: (i, j, 0))
    ptile = pl.BlockSpec((1, ts, d // 2), lambda i, j: (i, j, 0))
    return pl.pallas_call(
        _mix_kernel,
        grid=(bsz, seq // ts),
        in_specs=[zspec(0), zspec(1), zspec(2), zspec(0), zspec(1), tile, tile,
                  full(conv_w.shape), full(wa.shape), full(wb.shape), full(wo.shape),
                  full(g.shape), full(b.shape)],
        out_specs=[tile, ptile],
        out_shape=[jax.ShapeDtypeStruct((bsz, seq, d), F32),
                   jax.ShapeDtypeStruct((bsz, seq, d // 2), I32)],
        scratch_shapes=[pltpu.VMEM((8, d), F32)],
        compiler_params=_params("parallel", "arbitrary"),
        name="mix",
    )(z, z, z, zm, zm, y_b, x, conv_w, wa, wb, wo, g, b)


def _router_kernel(x_ref, wrt_ref, rb_ref, idx_ref, rank_ref, wtm_ref, cnt_ref, carry):
    t = x_ref.shape[0]

    @pl.when(pl.program_id(0) == 0)
    def _():
        carry[...] = jnp.zeros_like(carry)

    logits = lax.dot_general(wrt_ref[...], x_ref[...], (((1,), (1,)), ((), ())),
                             precision=lax.Precision.HIGHEST, preferred_element_type=F32)
    scores = _sigmoid(logits)
    shape3 = (GROUP_SIZE, N_GROUPS, t)
    sel = (scores + rb_ref[...]).reshape(shape3)
    scores = scores.reshape(shape3)
    mem = lax.broadcasted_iota(I32, shape3, 0)
    grp = lax.broadcasted_iota(I32, shape3, 1)
    eidx = grp * GROUP_SIZE + mem
    m1 = jnp.max(sel, axis=0, keepdims=True)
    first = jnp.min(jnp.where(sel == m1, mem, GROUP_SIZE), axis=0, keepdims=True)
    m2 = jnp.max(jnp.where(mem == first, NEG_INF, sel), axis=0, keepdims=True)
    rem = m1 + m2
    gidx = lax.broadcasted_iota(I32, rem.shape, 1)
    gmask = jnp.zeros(rem.shape, F32)
    for _ in range(TOPK_GROUPS):
        mx = jnp.max(rem, axis=1, keepdims=True)
        pick = gidx == jnp.min(jnp.where(rem == mx, gidx, N_GROUPS), axis=1, keepdims=True)
        gmask = jnp.where(pick, 1.0, gmask)
        rem = jnp.where(pick, NEG_INF, rem)
    masked = jnp.where(jnp.broadcast_to(gmask, shape3) > 0.5, sel, NEG_INF)
    chosen = jnp.zeros(shape3, F32)
    picks = []
    for _ in range(TOP_K):
        mx = jnp.max(jnp.max(masked, axis=0, keepdims=True), axis=1, keepdims=True)
        cand = jnp.where(masked == mx, eidx, N_EXPERTS)
        fi = jnp.min(jnp.min(cand, axis=0, keepdims=True), axis=1, keepdims=True)
        pick = eidx == fi
        picks.append((fi, pick))
        chosen = jnp.where(pick, 1.0, chosen)
        masked = jnp.where(pick, NEG_INF, masked)
    w = chosen * scores
    denom = jnp.sum(jnp.sum(w, axis=0, keepdims=True), axis=1, keepdims=True)
    gate3 = w / denom * ROUTED_SCALE

    chosen2 = chosen.reshape(N_EXPERTS, t).astype(BF16)
    tok_r = lax.broadcasted_iota(I32, (t, t), 0)
    tok_c = lax.broadcasted_iota(I32, (t, t), 1)
    before = jnp.where(tok_r < tok_c, 1.0, 0.0).astype(BF16)
    prefix = (jnp.dot(chosen2, before, preferred_element_type=F32) + carry[...]).reshape(shape3)
    carry[...] += jnp.dot(chosen2, jnp.ones((t, t), BF16), preferred_element_type=F32)
    cnt_ref[...] = carry[:, :LANES]

    def pick_sum(pick, val):
        return jnp.sum(jnp.sum(jnp.where(pick, val, 0.0), axis=0, keepdims=True), axis=1, keepdims=True)

    sub8 = lax.broadcasted_iota(I32, (TOP_K, t), 0)
    idx8 = jnp.zeros((TOP_K, t), I32)
    rank8 = jnp.zeros((TOP_K, t), F32)
    w8 = jnp.zeros((TOP_K, t), F32)
    for k, (fi, pick) in enumerate(picks):
        idx8 = jnp.where(sub8 == k, jnp.broadcast_to(fi.reshape(1, t), (TOP_K, t)), idx8)
        rank8 = jnp.where(sub8 == k, jnp.broadcast_to(pick_sum(pick, prefix).reshape(1, t), (TOP_K, t)), rank8)
        w8 = jnp.where(sub8 == k, jnp.broadcast_to(pick_sum(pick, gate3).reshape(1, t), (TOP_K, t)), w8)
    idx_ref[...] = idx8
    rank_ref[...] = rank8.astype(I32)
    pad = jnp.zeros((LANES - TOP_K, t), F32)
    wtm_ref[...] = jnp.transpose(jnp.concatenate([w8, pad], axis=0))


def _router(x1, w_router_t, router_bias, part):
    d = x1.shape[1]
    m = x1.shape[0] // MOE_STREAMS
    tt = min(ROUTER_TT, m)
    first = part * (m // tt)
    kt = pl.BlockSpec((TOP_K, tt), lambda i: (0, i))
    return pl.pallas_call(
        _router_kernel,
        grid=(m // tt,),
        in_specs=[pl.BlockSpec((tt, d), lambda i: (first + i, 0)),
                  pl.BlockSpec((N_EXPERTS, d), lambda i: (0, 0)),
                  pl.BlockSpec((N_EXPERTS, 1), lambda i: (0, 0))],
        out_specs=[kt, kt,
                   pl.BlockSpec((tt, LANES), lambda i: (i, 0)),
                   pl.BlockSpec((N_EXPERTS, LANES), lambda i: (0, 0))],
        out_shape=[jax.ShapeDtypeStruct((TOP_K, m), I32),
                   jax.ShapeDtypeStruct((TOP_K, m), I32),
                   jax.ShapeDtypeStruct((m, LANES), F32),
                   jax.ShapeDtypeStruct((N_EXPERTS, LANES), F32)],
        scratch_shapes=[pltpu.VMEM((N_EXPERTS, tt), F32)],
        compiler_params=_params("arbitrary"),
        name="router",
    )(x1, w_router_t, router_bias)


def _visit_metadata(counts, tm, n_rows):
    nt = n_rows // tm
    nv = nt + N_EXPERTS - 1
    ends = jnp.cumsum(counts)
    starts = ends - counts
    first_tile = starts // tm
    ntiles = jnp.where(counts > 0, (ends - 1) // tm - first_tile + 1, 0)
    vend = jnp.cumsum(ntiles)
    vstart = vend - ntiles
    v = jnp.arange(nv, dtype=I32)
    valid = v < vend[-1]
    ve = jnp.minimum(jnp.sum((v[:, None] >= vend[None, :]).astype(I32), axis=1), N_EXPERTS - 1)
    ve = jnp.where(valid, ve, ve[jnp.maximum(vend[-1] - 1, 0)])
    vt = jnp.where(valid, first_tile[ve] + v - vstart[ve], nt - 1)
    lo = jnp.where(valid, jnp.clip(starts[ve] - vt * tm, 0, tm), 0)
    hi = jnp.where(valid, jnp.clip(ends[ve] - vt * tm, 0, tm), 0)
    return ve.astype(I32), vt.astype(I32), lo.astype(I32), hi.astype(I32)


def _sc_mesh():
    return plsc.VectorSubcoreMesh(core_axis_name="c", subcore_axis_name="s")


def _sc_worker():
    return lax.axis_index("s") * SC_CORES + lax.axis_index("c")


def _dispatch(x1p, dest, part):
    dp = x1p.shape[1]
    m = dest.shape[1]
    blocks = m // SC_ROWS
    per_worker = blocks // SC_WORKERS
    x_first = part * blocks
    table = dest.reshape(TOP_K, blocks, SC_ROWS).transpose(1, 0, 2).reshape(blocks * TOP_K, SC_ROWS)

    def body(x_hbm, idx_hbm, xs_hbm, idx_v, rows_v, sem):
        first = _sc_worker() * per_worker

        @pl.loop(0, per_worker)
        def _(j):
            b = first + j
            pltpu.sync_copy(idx_hbm.at[pl.ds(pl.multiple_of(b * TOP_K, TOP_K), TOP_K)], idx_v)
            pltpu.sync_copy(x_hbm.at[pl.ds(pl.multiple_of((x_first + b) * SC_ROWS, SC_ROWS), SC_ROWS)], rows_v)
            copies = [pltpu.async_copy(rows_v, xs_hbm.at[idx_v.at[k]], sem) for k in range(TOP_K)]
            for cp in copies:
                cp.wait()

    return pl.kernel(
        body,
        out_type=jax.ShapeDtypeStruct((m * TOP_K, dp), x1p.dtype),
        mesh=_sc_mesh(),
        scratch_types=[pltpu.VMEM((TOP_K, SC_ROWS), I32), pltpu.VMEM((SC_ROWS, dp), x1p.dtype),
                       pltpu.SemaphoreType.DMA],
        name="dispatch",
    )(x1p, table)


def _gather_rows(ys, dest):
    n_rows, dp = ys.shape
    idx = dest.reshape(n_rows // SC_ROWS, SC_ROWS)
    per_worker = n_rows // SC_ROWS // SC_WORKERS

    def body(ys_hbm, idx_hbm, out_hbm, idx_v, rows_v, sem):
        first = _sc_worker() * per_worker

        @pl.loop(0, per_worker)
        def _(j):
            b = first + j
            pltpu.sync_copy(idx_hbm.at[pl.ds(b, 1)], idx_v)
            pltpu.async_copy(ys_hbm.at[idx_v.at[0]], rows_v, sem).wait()
            pltpu.sync_copy(rows_v, out_hbm.at[pl.ds(pl.multiple_of(b * SC_ROWS, SC_ROWS), SC_ROWS)])

    return pl.kernel(
        body,
        out_type=jax.ShapeDtypeStruct((n_rows, dp), ys.dtype),
        mesh=_sc_mesh(),
        scratch_types=[pltpu.VMEM((1, SC_ROWS), I32), pltpu.VMEM((SC_ROWS, dp), ys.dtype),
                       pltpu.SemaphoreType.DMA],
        name="gather_rows",
    )(ys, idx)


def _ffn_kernel(ve_ref, vt_ref, lo_ref, hi_ref, xs_ref, wg_ref, wu_ref, wd_ref, ys_ref,
                wgu_b, wd_b, acc):
    v = pl.program_id(0)
    lo = lo_ref[v]
    hi = hi_ref[v]
    tm = xs_ref.shape[0]
    f = wg_ref.shape[1]

    @pl.when((v == 0) | (ve_ref[v] != ve_ref[jnp.maximum(v - 1, 0)]))
    def _():
        wgu_b[:, :f] = wg_ref[...].astype(BF16)
        wgu_b[:, f:] = wu_ref[...].astype(BF16)
        wd_b[...] = wd_ref[...].astype(BF16)

    def pack_rows(a):
        half = a.shape[1] // 2
        return _pack_pair(a[:, :half], a[:, half:])

    for sb in range(tm // FFN_SUB):
        s0 = sb * FFN_SUB
        rows = pl.ds(s0, FFN_SUB)
        lo_s = jnp.clip(lo - s0, 0, FFN_SUB)
        hi_s = jnp.clip(hi - s0, 0, FFN_SUB)

        @pl.when(hi_s > lo_s)
        def _():
            x = _unpack_rows(xs_ref[rows, :])
            h2 = jnp.dot(x, wgu_b[...], preferred_element_type=F32)
            hg = h2[:, :f]
            hu = h2[:, f:]
            r = lax.broadcasted_iota(I32, hg.shape, 0)
            mine = (r >= lo_s) & (r < hi_s)
            hmid = jnp.where(mine, hg * _sigmoid(hg) * hu, 0.0).astype(BF16)
            y = jnp.dot(hmid, wd_b[...], preferred_element_type=F32)

            @pl.when((lo_s == 0) & (hi_s == FFN_SUB))
            def _():
                ys_ref[rows, :] = pack_rows(y)

            @pl.when((lo_s == 0) & (hi_s < FFN_SUB))
            def _():
                acc[rows, :] = y

            @pl.when(lo_s > 0)
            def _():
                acc[rows, :] += y

            @pl.when((lo_s > 0) & (hi_s == FFN_SUB))
            def _():
                ys_ref[rows, :] = pack_rows(acc[rows, :])


def _ffn(xs, meta, wg, wu, wd, layer):
    n_rows, dp = xs.shape
    _, n_e, d, f = wg.shape
    tm = FFN_TM
    nv = n_rows // tm + N_EXPERTS - 1
    grid_spec = pltpu.PrefetchScalarGridSpec(
        num_scalar_prefetch=4,
        grid=(nv,),
        in_specs=[pl.BlockSpec((tm, dp), lambda v, ve, vt, lo, hi: (vt[v], 0)),
                  pl.BlockSpec((None, None, d, f), lambda v, ve, vt, lo, hi: (layer, ve[v], 0, 0)),
                  pl.BlockSpec((None, None, d, f), lambda v, ve, vt, lo, hi: (layer, ve[v], 0, 0)),
                  pl.BlockSpec((None, None, f, d), lambda v, ve, vt, lo, hi: (layer, ve[v], 0, 0))],
        out_specs=pl.BlockSpec((tm, dp), lambda v, ve, vt, lo, hi: (vt[v], 0)),
        scratch_shapes=[pltpu.VMEM((d, 2 * f), BF16), pltpu.VMEM((f, d), BF16), pltpu.VMEM((tm, d), F32)],
    )
    return pl.pallas_call(
        _ffn_kernel,
        grid_spec=grid_spec,
        out_shape=jax.ShapeDtypeStruct((n_rows, dp), I32),
        compiler_params=_params("arbitrary"),
        name="ffn",
    )(*meta, xs, wg, wu, wd)


def _combine_kernel(*refs, n_prev):
    (yk_ref, x1_ref, wtm_ref, p_ref, wsg_ref, wsu_ref, wsd_ref,
     wpg_ref, wpp_ref, g_ref, b_ref, x2_ref, x2b_ref) = refs[n_prev:]
    tt = x1_ref.shape[0]
    xb = x1_ref[...].astype(BF16)
    hg = jnp.dot(xb, wsg_ref[...], preferred_element_type=F32)
    hu = jnp.dot(xb, wsu_ref[...], preferred_element_type=F32)
    shared = jnp.dot((hg * _sigmoid(hg) * hu).astype(BF16), wsd_ref[...], preferred_element_type=F32)
    pgate = _sigmoid(jnp.dot(xb, wpg_ref[...], preferred_element_type=F32))
    pproj = jnp.dot(p_ref[...].astype(BF16), wpp_ref[...], preferred_element_type=F32)
    rest = ALPHA * x1_ref[...] + shared + pgate * pproj

    wt = wtm_ref[...]
    r_lo = jnp.zeros((tt, yk_ref.shape[2]), F32)
    r_hi = jnp.zeros((tt, yk_ref.shape[2]), F32)
    for k in range(TOP_K):
        lo, hi = _unpack_pair(yk_ref[k])
        wk = wt[:, k:k + 1]
        r_lo = r_lo + wk * lo
        r_hi = r_hi + wk * hi
    routed = jnp.concatenate([r_lo, r_hi], axis=1)
    x2 = _layer_norm(rest + routed, g_ref[...], b_ref[...])
    x2_ref[...] = x2
    x2b_ref[...] = x2.astype(BF16)


def _combine(yk, x1, wtm, p, layer, part, prev, wsg, wsu, wsd, wpg, wpp, g, b):
    m_all, d = x1.shape
    m = wtm.shape[0]
    tt = min(COMBINE_TT, m)
    first = part * (m // tt)
    yk = yk.reshape(TOP_K, m, d // 2)

    def full(a):
        return pl.BlockSpec(a.shape, lambda i: (0,) * a.ndim)

    token_rows = pl.BlockSpec((tt, d), lambda i: (first + i, 0))
    untouched = pl.BlockSpec(memory_space=pl.ANY)
    return pl.pallas_call(
        functools.partial(_combine_kernel, n_prev=len(prev)),
        grid=(m // tt,),
        in_specs=[untouched] * len(prev) + [
            pl.BlockSpec((TOP_K, tt, d // 2), lambda i: (0, i, 0)),
            token_rows,
            pl.BlockSpec((tt, LANES), lambda i: (i, 0)),
            pl.BlockSpec((None, tt, p.shape[2]), lambda i: (layer, first + i, 0)),
            full(wsg), full(wsu), full(wsd), full(wpg), full(wpp), full(g), full(b)],
        out_specs=[token_rows, token_rows],
        out_shape=[jax.ShapeDtypeStruct((m_all, d), F32), jax.ShapeDtypeStruct((m_all, d), BF16)],
        input_output_aliases={j: j for j in range(len(prev))},
        compiler_params=_params("parallel"),
        name="combine",
    )(*prev, yk, x1, wtm, p, wsg, wsu, wsd, wpg, wpp, g, b)


def kernel(x, p, w_in, conv_w, b_igate, b_fgate, mlstm_norm_g, w_branch_a, w_branch_b, w_out, ln1_g, ln1_b, w_router, router_bias, w_exp_gate, w_exp_up, w_exp_down, w_sh_gate, w_sh_up, w_sh_down, w_ple_gate, w_ple_proj, ln2_g, ln2_b):
    bsz, seq, d = x.shape
    m = bsz * seq
    depth = w_in.shape[0]
    hw = HEADS * HEAD_DIM
    if_lo = 3 * d + 4 * hw
    if_hi = if_lo + 2 * HEADS

    xf = x.reshape(m, d)
    xb = xf.astype(BF16)
    w_in_t = jnp.swapaxes(w_in, 1, 2)
    p = p.reshape(depth, m, -1)
    for i in range(depth):
        w_merge = w_in_t[i, if_hi:]
        w_if = jnp.pad(w_in_t[i, if_lo:if_hi], ((0, LANES - 2 * HEADS), (0, 0)))
        gate_bias = jnp.pad(jnp.concatenate([b_igate[i], b_fgate[i]]), (0, LANES - 2 * HEADS)).reshape(1, LANES)

        z = _matmul(xb, w_in_t, BF16, INPROJ_TM, INPROJ_TN, "inproj", layer=i, n=if_lo).reshape(bsz, seq, -1)
        zm = _matmul(xb, w_merge, BF16, INPROJ_TM, INPROJ_TN, "mergeproj").reshape(bsz, seq, -1)
        gates = _matmul(xb, w_if, F32, INPROJ_TM, LANES, "gateproj").reshape(bsz, seq, LANES)
        y_b = _mlstm(z, gates, gate_bias, mlstm_norm_g[i].reshape(1, hw))
        x1, x1p = _mix(z, zm, y_b, xf.reshape(bsz, seq, d), conv_w[i],
                       w_branch_a[i].astype(BF16), w_branch_b[i].astype(BF16), w_out[i].astype(BF16),
                       ln1_g[i].reshape(1, d), ln1_b[i].reshape(1, d))
        x1 = x1.reshape(m, d)
        x1p = x1p.reshape(m, d // 2)

        ms = m // MOE_STREAMS
        shared_w = (w_sh_gate[i].astype(BF16), w_sh_up[i].astype(BF16), w_sh_down[i].astype(BF16),
                    w_ple_gate[i].astype(BF16), w_ple_proj[i].astype(BF16),
                    ln2_g[i].reshape(1, d), ln2_b[i].reshape(1, d))
        w_router_t = w_router[i].T.reshape(N_GROUPS, GROUP_SIZE, d).swapaxes(0, 1).reshape(N_EXPERTS, d)
        rb = router_bias[i].reshape(N_GROUPS, GROUP_SIZE).T.reshape(N_EXPERTS, 1)
        expert_ids = jnp.arange(N_EXPERTS, dtype=I32)[:, None, None]
        outs = ()
        for part in range(MOE_STREAMS):
            idx, rank, wtm, cnt = _router(x1, w_router_t, rb, part)
            counts = cnt[:, 0].astype(I32).reshape(GROUP_SIZE, N_GROUPS).T.reshape(N_EXPERTS)
            row_start = jnp.cumsum(counts) - counts
            dest = rank + jnp.sum(jnp.where(idx[None] == expert_ids, row_start[:, None, None], 0), axis=0)
            meta = _visit_metadata(counts, FFN_TM, ms * TOP_K)
            xs = _dispatch(x1p, dest, part)
            ys = _ffn(xs, meta, w_exp_gate, w_exp_up, w_exp_down, i)
            yk = _gather_rows(ys, dest)
            outs = tuple(_combine(yk, x1, wtm, p, i, part, outs, *shared_w))
        xf, xb = outs
    return xf.reshape(bsz, seq, d)
```

```python
import functools
import math

import jax
import jax.numpy as jnp
from jax import lax
from jax.experimental import pallas as pl
from jax.experimental.pallas import tpu as pltpu
from jax.experimental.pallas import tpu_sc as plsc

F32 = jnp.float32
BF16 = jnp.bfloat16
U32 = jnp.uint32
I32 = jnp.int32

HEADS = 8
HEAD_DIM = 128
N_EXPERTS = 64
N_GROUPS = 8
GROUP_SIZE = N_EXPERTS // N_GROUPS
TOPK_GROUPS = 4
TOP_K = 8
ROUTED_SCALE = 2.5
DEPTH = 4
ALPHA = (2 * DEPTH) ** 0.25
LN_EPS = 1e-5
RMS_EPS = 1e-6
QK_SCALE = HEAD_DIM ** -0.5
LOG_QK_SCALE = math.log(QK_SCALE)

LANES = 128
VMEM_LIMIT = 56 * 1024 * 1024
NEG_INF = float("-inf")

MLSTM_CHUNK = 256
MLSTM_STEP_CHUNKS = 2
INPROJ_TM, INPROJ_TN = 2048, 1024
MIX_TS = 512
ROUTER_TT = 512
FFN_TM = 2048
FFN_SUB = 512
COMBINE_TT = 512
MOE_STREAMS = 1

SC_CORES = 2
SC_SUBCORES = 16
SC_WORKERS = SC_CORES * SC_SUBCORES
SC_ROWS = 128


def _params(*sem):
    return pltpu.CompilerParams(dimension_semantics=sem, vmem_limit_bytes=VMEM_LIMIT)


def _sigmoid(x):
    return 1.0 / (1.0 + jnp.exp(-x))


def _layer_norm(r, g, b):
    mu = jnp.mean(r, axis=-1, keepdims=True)
    d = r - mu
    var = jnp.mean(d * d, axis=-1, keepdims=True)
    return d * lax.rsqrt(var + LN_EPS) * g + b


def _pack_pair(lo, hi):
    return lax.bitcast_convert_type(pltpu.pack_elementwise([lo, hi], packed_dtype=BF16), I32)


def _unpack_pair(w):
    w = lax.bitcast_convert_type(w, U32)
    lo = pltpu.unpack_elementwise(w, index=0, packed_dtype=BF16, unpacked_dtype=F32)
    hi = pltpu.unpack_elementwise(w, index=1, packed_dtype=BF16, unpacked_dtype=F32)
    return lo, hi


def _unpack_rows(w):
    lo, hi = _unpack_pair(w)
    return jnp.concatenate([lo.astype(BF16), hi.astype(BF16)], axis=1)


def _mm_kernel(x_ref, wt_ref, o_ref):
    o_ref[...] = lax.dot_general(x_ref[...], wt_ref[...].astype(BF16), (((1,), (1,)), ((), ())),
                                 preferred_element_type=F32).astype(o_ref.dtype)


def _matmul(x, wt, out_dtype, tm, tn, name, layer=None, n=None):
    m, k = x.shape
    n = wt.shape[-2] if n is None else n
    tm = min(tm, m)
    if layer is None:
        w_spec = pl.BlockSpec((tn, k), lambda i, j: (j, 0))
    else:
        w_spec = pl.BlockSpec((None, tn, k), lambda i, j: (layer, j, 0))
    return pl.pallas_call(
        _mm_kernel,
        grid=(m // tm, n // tn),
        in_specs=[pl.BlockSpec((tm, k), lambda i, j: (i, 0)), w_spec],
        out_specs=pl.BlockSpec((tm, tn), lambda i, j: (i, j)),
        out_shape=jax.ShapeDtypeStruct((m, n), out_dtype),
        compiler_params=_params("parallel", "parallel"),
        name=name,
    )(x, wt)


def _mlstm_kernel(q_ref, k_ref, v_ref, o_ref, g_ref, gb_ref, ng_ref, y_ref, c_scr, m_scr):
    L = MLSTM_CHUNK
    chunks = [pl.ds(j * L, L) for j in range(q_ref.shape[1] // L)]

    @pl.when(pl.program_id(1) == 0)
    def _():
        c_scr[...] = jnp.zeros_like(c_scr)
        m_scr[...] = jnp.zeros_like(m_scr)

    row = lax.broadcasted_iota(I32, (L, L), 0)
    col = lax.broadcasted_iota(I32, (L, L), 1)
    causal = col <= row
    rows = lax.broadcasted_iota(I32, (L, LANES), 0)

    def scan_rows(x, op, identity):
        step = 1
        while step < L:
            x = op(x, jnp.where(rows >= step, pltpu.roll(x, step, axis=0), identity))
            step *= 2
        return x

    gate = []
    m_prev = m_scr[...]
    for ch in chunks:
        g = g_ref[0, ch, :] + gb_ref[...]
        ig = g
        fg = pltpu.roll(g, LANES - HEADS, axis=1)
        log_f = jnp.minimum(fg, 0.0) - jnp.log(1.0 + jnp.exp(-jnp.abs(fg)))
        b = scan_rows(log_f, jnp.add, 0.0)
        b_last = b[L - 1:L, :]
        a = b_last - b + ig
        m_loc = jnp.max(a, axis=0, keepdims=True)
        m_new = jnp.maximum(b_last + m_prev, m_loc)
        log_inter = b + m_prev
        r = ig - b
        m_out = jnp.maximum(log_inter, b + scan_rows(r, jnp.maximum, NEG_INF))
        gate.append(dict(
            w_loc=jnp.exp(a - m_loc),
            sp=jnp.exp(b_last + m_prev - m_new),
            sl=jnp.exp(m_loc - m_new),
            r_t=jnp.transpose(r),
            u=b - m_out + LOG_QK_SCALE,
            e_inter=jnp.exp(log_inter - m_out + LOG_QK_SCALE),
            e_floor=jnp.exp(-m_out)))
        m_prev = m_new
    m_scr[...] = m_prev

    ones_blk = jnp.ones((L, HEAD_DIM), BF16)
    lanes = [slice(h * HEAD_DIM, (h + 1) * HEAD_DIM) for h in range(HEADS)]
    pairs = [(j, h) for j in range(len(chunks)) for h in range(HEADS)]

    def col_of(name, j, h):
        return gate[j][name][:, h:h + 1]

    q = {(j, h): q_ref[0, chunks[j], lanes[h]] for j, h in pairs}
    k = {(j, h): k_ref[0, chunks[j], lanes[h]] for j, h in pairs}
    v_aug = {(j, h): jnp.concatenate([v_ref[0, chunks[j], lanes[h]], ones_blk], axis=1) for j, h in pairs}
    s1 = {p: lax.dot_general(q[p], k[p], (((1,), (1,)), ((), ())), preferred_element_type=F32) for p in pairs}
    kw = {(j, h): (k[j, h].astype(F32) * col_of("w_loc", j, h)).astype(BF16) for j, h in pairs}
    c_loc = {p: lax.dot_general(kw[p], v_aug[p], (((0,), (0,)), ((), ())), preferred_element_type=F32)
             for p in pairs}
    c_seen = {}
    for h in range(HEADS):
        c = c_scr[h]
        for j in range(len(chunks)):
            c_seen[j, h] = c
            c = col_of("sp", j, h) * c + col_of("sl", j, h) * c_loc[j, h]
        c_scr[h] = c
    inter = {p: jnp.dot(q[p], c_seen[p].astype(BF16), preferred_element_type=F32) for p in pairs}
    s = {(j, h): (s1[j, h] * jnp.where(causal, jnp.exp(col_of("u", j, h) + gate[j]["r_t"][h:h + 1, :]), 0.0)
                  ).astype(BF16) for j, h in pairs}
    intra = {p: jnp.dot(s[p], v_aug[p], preferred_element_type=F32) for p in pairs}
    tot = {(j, h): intra[j, h] + col_of("e_inter", j, h) * inter[j, h] for j, h in pairs}
    hh = {(j, h): tot[j, h][:, :HEAD_DIM] / jnp.maximum(jnp.abs(tot[j, h][:, HEAD_DIM:]), col_of("e_floor", j, h))
          for j, h in pairs}
    hh = {p: hh[p] * lax.rsqrt(jnp.mean(hh[p] * hh[p], axis=1, keepdims=True) + RMS_EPS) for p in pairs}
    for j, h in pairs:
        og = _sigmoid(o_ref[0, chunks[j], lanes[h]].astype(F32))
        y_ref[0, chunks[j], lanes[h]] = (hh[j, h] * ng_ref[:, lanes[h]] * og).astype(y_ref.dtype)


def _mlstm(z, gates, gate_bias, norm_g):
    bsz, seq, _ = z.shape
    L = min(MLSTM_STEP_CHUNKS * MLSTM_CHUNK, seq)
    hw = HEADS * HEAD_DIM

    def zspec(cb):
        return pl.BlockSpec((1, L, hw), lambda b, c, cb=cb: (b, c, cb))

    return pl.pallas_call(
        _mlstm_kernel,
        grid=(bsz, seq // L),
        in_specs=[zspec(3), zspec(4), zspec(5), zspec(6),
                  pl.BlockSpec((1, L, LANES), lambda b, c: (b, c, 0)),
                  pl.BlockSpec((1, LANES), lambda b, c: (0, 0)),
                  pl.BlockSpec((1, hw), lambda b, c: (0, 0))],
        out_specs=pl.BlockSpec((1, L, hw), lambda b, c: (b, c, 0)),
        out_shape=jax.ShapeDtypeStruct((bsz, seq, hw), BF16),
        scratch_shapes=[pltpu.VMEM((HEADS, HEAD_DIM, 2 * HEAD_DIM), F32),
                        pltpu.VMEM((1, LANES), F32)],
        compiler_params=_params("parallel", "arbitrary"),
        name="mlstm",
    )(z, z, z, z, gates, gate_bias, norm_g)


def _mix_kernel(cin_ref, cout_ref, cval_ref, mg0_ref, mg1_ref, yb_ref, x_ref, cw_ref,
                wa_ref, wb_ref, wo_ref, g_ref, b_ref, x1_ref, x1p_ref, carry):
    ts = x_ref.shape[1]

    @pl.when(pl.program_id(1) == 0)
    def _():
        carry[...] = jnp.zeros_like(carry)

    n_parts = 2 if ts % 16 == 0 else 1
    tp = ts // n_parts
    parts = [pl.ds(j * tp, tp) for j in range(n_parts)]
    u = [cin_ref[0, pr, :].astype(F32) * cval_ref[0, pr, :].astype(F32) for pr in parts]
    prev = [carry[...]] + [uj[tp - 8:, :] for uj in u[:-1]]
    carry[...] = u[-1][tp - 8:, :]
    r8 = lax.broadcasted_iota(I32, (8, u[0].shape[1]), 0)

    def shifted(uj, pj, k):
        body = pltpu.roll(uj, k, axis=0)
        head = jnp.where(r8 < k, pltpu.roll(pj, k, axis=0), body[:8, :])
        return jnp.concatenate([head, body[8:, :]], axis=0)

    cw = cw_ref[...]
    conv = [cw[0:1, :] * shifted(uj, pj, 2) + cw[1:2, :] * shifted(uj, pj, 1) + cw[2:3, :] * uj
            for uj, pj in zip(u, prev)]
    y_a = [(cout_ref[0, pr, :].astype(F32) * cj).astype(BF16) for pr, cj in zip(parts, conv)]
    pa = [jnp.dot(yj, wa_ref[...], preferred_element_type=F32) for yj in y_a]
    pb = [jnp.dot(yb_ref[0, pr, :], wb_ref[...], preferred_element_type=F32) for pr in parts]
    mixed = [(_sigmoid(mg0_ref[0, pr, :].astype(F32)) * paj
              + _sigmoid(mg1_ref[0, pr, :].astype(F32)) * pbj).astype(BF16)
             for pr, paj, pbj in zip(parts, pa, pb)]
    hmix = [jnp.dot(mj, wo_ref[...], preferred_element_type=F32) for mj in mixed]
    for pr, hj in zip(parts, hmix):
        x1 = _layer_norm(ALPHA * x_ref[0, pr, :] + hj, g_ref[...], b_ref[...])
        x1_ref[0, pr, :] = x1
        half = x1.shape[1] // 2
        x1p_ref[0, pr, :] = _pack_pair(x1[:, :half], x1[:, half:])


def _mix(z, zm, y_b, x, conv_w, wa, wb, wo, g, b):
    bsz, seq, d = x.shape
    ts = min(MIX_TS, seq)

    def zspec(cb):
        return pl.BlockSpec((1, ts, d), lambda i, j, cb=cb: (i, j, cb))

    def full(shape):
        return pl.BlockSpec(shape, lambda i, j: (0,) * len(shape))

    tile = pl.BlockSpec((1, ts, d), lambda i, j: (i, j, 0))
    ptile = pl.BlockSpec((1, ts, d // 2), lambda i, j: (i, j, 0))
    return pl.pallas_call(
        _mix_kernel,
        grid=(bsz, seq // ts),
        in_specs=[zspec(0), zspec(1), zspec(2), zspec(0), zspec(1), tile, tile,
                  full(conv_w.shape), full(wa.shape), full(wb.shape), full(wo.shape),
                  full(g.shape), full(b.shape)],
        out_specs=[tile, ptile],
        out_shape=[jax.ShapeDtypeStruct((bsz, seq, d), F32),
                   jax.ShapeDtypeStruct((bsz, seq, d // 2), I32)],
        scratch_shapes=[pltpu.VMEM((8, d), F32)],
        compiler_params=_params("parallel", "arbitrary"),
        name="mix",
    )(z, z, z, zm, zm, y_b, x, conv_w, wa, wb, wo, g, b)


def _router_kernel(x_ref, wrt_ref, rb_ref, idx_ref, rank_ref, wtm_ref, cnt_ref, carry):
    t = x_ref.shape[0]

    @pl.when(pl.program_id(0) == 0)
    def _():
        carry[...] = jnp.zeros_like(carry)

    logits = lax.dot_general(wrt_ref[...], x_ref[...], (((1,), (1,)), ((), ())),
                             precision=lax.Precision.HIGHEST, preferred_element_type=F32)
    scores = _sigmoid(logits)
    shape3 = (GROUP_SIZE, N_GROUPS, t)
    sel = (scores + rb_ref[...]).reshape(shape3)
    scores = scores.reshape(shape3)
    mem = lax.broadcasted_iota(I32, shape3, 0)
    grp = lax.broadcasted_iota(I32, shape3, 1)
    eidx = grp * GROUP_SIZE + mem
    m1 = jnp.max(sel, axis=0, keepdims=True)
    first = jnp.min(jnp.where(sel == m1, mem, GROUP_SIZE), axis=0, keepdims=True)
    m2 = jnp.max(jnp.where(mem == first, NEG_INF, sel), axis=0, keepdims=True)
    rem = m1 + m2
    gidx = lax.broadcasted_iota(I32, rem.shape, 1)
    gmask = jnp.zeros(rem.shape, F32)
    for _ in range(TOPK_GROUPS):
        mx = jnp.max(rem, axis=1, keepdims=True)
        pick = gidx == jnp.min(jnp.where(rem == mx, gidx, N_GROUPS), axis=1, keepdims=True)
        gmask = jnp.where(pick, 1.0, gmask)
        rem = jnp.where(pick, NEG_INF, rem)
    masked = jnp.where(jnp.broadcast_to(gmask, shape3) > 0.5, sel, NEG_INF)
    chosen = jnp.zeros(shape3, F32)
    picks = []
    for _ in range(TOP_K):
        mx = jnp.max(jnp.max(masked, axis=0, keepdims=True), axis=1, keepdims=True)
        cand = jnp.where(masked == mx, eidx, N_EXPERTS)
        fi = jnp.min(jnp.min(cand, axis=0, keepdims=True), axis=1, keepdims=True)
        pick = eidx == fi
        picks.append((fi, pick))
        chosen = jnp.where(pick, 1.0, chosen)
        masked = jnp.where(pick, NEG_INF, masked)
    w = chosen * scores
    denom = jnp.sum(jnp.sum(w, axis=0, keepdims=True), axis=1, keepdims=True)
    gate3 = w / denom * ROUTED_SCALE

    chosen2 = chosen.reshape(N_EXPERTS, t).astype(BF16)
    tok_r = lax.broadcasted_iota(I32, (t, t), 0)
    tok_c = lax.broadcasted_iota(I32, (t, t), 1)
    before = jnp.where(tok_r < tok_c, 1.0, 0.0).astype(BF16)
    prefix = (jnp.dot(chosen2, before, preferred_element_type=F32) + carry[...]).reshape(shape3)
    carry[...] += jnp.dot(chosen2, jnp.ones((t, t), BF16), preferred_element_type=F32)
    cnt_ref[...] = carry[:, :LANES]

    def pick_sum(pick, val):
        return jnp.sum(jnp.sum(jnp.where(pick, val, 0.0), axis=0, keepdims=True), axis=1, keepdims=True)

    sub8 = lax.broadcasted_iota(I32, (TOP_K, t), 0)
    idx8 = jnp.zeros((TOP_K, t), I32)
    rank8 = jnp.zeros((TOP_K, t), F32)
    w8 = jnp.zeros((TOP_K, t), F32)
    for k, (fi, pick) in enumerate(picks):
        idx8 = jnp.where(sub8 == k, jnp.broadcast_to(fi.reshape(1, t), (TOP_K, t)), idx8)
        rank8 = jnp.where(sub8 == k, jnp.broadcast_to(pick_sum(pick, prefix).reshape(1, t), (TOP_K, t)), rank8)
        w8 = jnp.where(sub8 == k, jnp.broadcast_to(pick_sum(pick, gate3).reshape(1, t), (TOP_K, t)), w8)
    idx_ref[...] = idx8
    rank_ref[...] = rank8.astype(I32)
    pad = jnp.zeros((LANES - TOP_K, t), F32)
    wtm_ref[...] = jnp.transpose(jnp.concatenate([w8, pad], axis=0))


def _router(x1, w_router_t, router_bias, part):
    d = x1.shape[1]
    m = x1.shape[0] // MOE_STREAMS
    tt = min(ROUTER_TT, m)
    first = part * (m // tt)
    kt = pl.BlockSpec((TOP_K, tt), lambda i: (0, i))
    return pl.pallas_call(
        _router_kernel,
        grid=(m // tt,),
        in_specs=[pl.BlockSpec((tt, d), lambda i: (first + i, 0)),
                  pl.BlockSpec((N_EXPERTS, d), lambda i: (0, 0)),
                  pl.BlockSpec((N_EXPERTS, 1), lambda i: (0, 0))],
        out_specs=[kt, kt,
                   pl.BlockSpec((tt, LANES), lambda i: (i, 0)),
                   pl.BlockSpec((N_EXPERTS, LANES), lambda i: (0, 0))],
        out_shape=[jax.ShapeDtypeStruct((TOP_K, m), I32),
                   jax.ShapeDtypeStruct((TOP_K, m), I32),
                   jax.ShapeDtypeStruct((m, LANES), F32),
                   jax.ShapeDtypeStruct((N_EXPERTS, LANES), F32)],
        scratch_shapes=[pltpu.VMEM((N_EXPERTS, tt), F32)],
        compiler_params=_params("arbitrary"),
        name="router",
    )(x1, w_router_t, router_bias)


def _visit_metadata(counts, tm, n_rows):
    nt = n_rows // tm
    nv = nt + N_EXPERTS - 1
    ends = jnp.cumsum(counts)
    starts = ends - counts
    first_tile = starts // tm
    ntiles = jnp.where(counts > 0, (ends - 1) // tm - first_tile + 1, 0)
    vend = jnp.cumsum(ntiles)
    vstart = vend - ntiles
    v = jnp.arange(nv, dtype=I32)
    valid = v < vend[-1]
    ve = jnp.minimum(jnp.sum((v[:, None] >= vend[None, :]).astype(I32), axis=1), N_EXPERTS - 1)
    ve = jnp.where(valid, ve, ve[jnp.maximum(vend[-1] - 1, 0)])
    vt = jnp.where(valid, first_tile[ve] + v - vstart[ve], nt - 1)
    lo = jnp.where(valid, jnp.clip(starts[ve] - vt * tm, 0, tm), 0)
    hi = jnp.where(valid, jnp.clip(ends[ve] - vt * tm, 0, tm), 0)
    return ve.astype(I32), vt.astype(I32), lo.astype(I32), hi.astype(I32)


def _sc_mesh():
    return plsc.VectorSubcoreMesh(core_axis_name="c", subcore_axis_name="s")


def _sc_worker():
    return lax.axis_index("s") * SC_CORES + lax.axis_index("c")


def _dispatch(x1p, dest, part):
    dp = x1p.shape[1]
    m = dest.shape[1]
    blocks = m // SC_ROWS
    per_worker = blocks // SC_WORKERS
    x_first = part * blocks
    table = dest.reshape(TOP_K, blocks, SC_ROWS).transpose(1, 0, 2).reshape(blocks * TOP_K, SC_ROWS)

    def body(x_hbm, idx_hbm, xs_hbm, idx_v, rows_v, sem):
        first = _sc_worker() * per_worker

        @pl.loop(0, per_worker)
        def _(j):
            b = first + j
            pltpu.sync_copy(idx_hbm.at[pl.ds(pl.multiple_of(b * TOP_K, TOP_K), TOP_K)], idx_v)
            pltpu.sync_copy(x_hbm.at[pl.ds(pl.multiple_of((x_first + b) * SC_ROWS, SC_ROWS), SC_ROWS)], rows_v)
            copies = [pltpu.async_copy(rows_v, xs_hbm.at[idx_v.at[k]], sem) for k in range(TOP_K)]
            for cp in copies:
                cp.wait()

    return pl.kernel(
        body,
        out_type=jax.ShapeDtypeStruct((m * TOP_K, dp), x1p.dtype),
        mesh=_sc_mesh(),
        scratch_types=[pltpu.VMEM((TOP_K, SC_ROWS), I32), pltpu.VMEM((SC_ROWS, dp), x1p.dtype),
                       pltpu.SemaphoreType.DMA],
        name="dispatch",
    )(x1p, table)


def _gather_rows(ys, dest):
    n_rows, dp = ys.shape
    idx = dest.reshape(n_rows // SC_ROWS, SC_ROWS)
    per_worker = n_rows // SC_ROWS // SC_WORKERS

    def body(ys_hbm, idx_hbm, out_hbm, idx_v, rows_v, sem):
        first = _sc_worker() * per_worker

        @pl.loop(0, per_worker)
        def _(j):
            b = first + j
            pltpu.sync_copy(idx_hbm.at[pl.ds(b, 1)], idx_v)
            pltpu.async_copy(ys_hbm.at[idx_v.at[0]], rows_v, sem).wait()
            pltpu.sync_copy(rows_v, out_hbm.at[pl.ds(pl.multiple_of(b * SC_ROWS, SC_ROWS), SC_ROWS)])

    return pl.kernel(
        body,
        out_type=jax.ShapeDtypeStruct((n_rows, dp), ys.dtype),
        mesh=_sc_mesh(),
        scratch_types=[pltpu.VMEM((1, SC_ROWS), I32), pltpu.VMEM((SC_ROWS, dp), ys.dtype),
                       pltpu.SemaphoreType.DMA],
        name="gather_rows",
    )(ys, idx)


def _ffn_kernel(ve_ref, vt_ref, lo_ref, hi_ref, xs_ref, wg_ref, wu_ref, wd_ref, ys_ref,
                wgu_b, wd_b, acc):
    v = pl.program_id(0)
    lo = lo_ref[v]
    hi = hi_ref[v]
    tm = xs_ref.shape[0]
    f = wg_ref.shape[1]

    @pl.when((v == 0) | (ve_ref[v] != ve_ref[jnp.maximum(v - 1, 0)]))
    def _():
        wgu_b[:, :f] = wg_ref[...].astype(BF16)
        wgu_b[:, f:] = wu_ref[...].astype(BF16)
        wd_b[...] = wd_ref[...].astype(BF16)

    def pack_rows(a):
        half = a.shape[1] // 2
        return _pack_pair(a[:, :half], a[:, half:])

    for sb in range(tm // FFN_SUB):
        s0 = sb * FFN_SUB
        rows = pl.ds(s0, FFN_SUB)
        lo_s = jnp.clip(lo - s0, 0, FFN_SUB)
        hi_s = jnp.clip(hi - s0, 0, FFN_SUB)

        whole = (lo_s == 0) & (hi_s == FFN_SUB)

        @pl.when(whole)
        def _():
            halves = [pl.ds(s0 + j * (FFN_SUB // 2), FFN_SUB // 2) for j in range(2)]
            x = [_unpack_rows(xs_ref[hr, :]) for hr in halves]
            h2 = [jnp.dot(xj, wgu_b[...], preferred_element_type=F32) for xj in x]
            act = [(hj[:, :f] * _sigmoid(hj[:, :f]) * hj[:, f:]).astype(BF16) for hj in h2]
            y = [jnp.dot(aj, wd_b[...], preferred_element_type=F32) for aj in act]
            for hr, yj in zip(halves, y):
                ys_ref[hr, :] = pack_rows(yj)

        @pl.when((hi_s > lo_s) & jnp.logical_not(whole))
        def _():
            x = _unpack_rows(xs_ref[rows, :])
            h2 = jnp.dot(x, wgu_b[...], preferred_element_type=F32)
            hg = h2[:, :f]
            hu = h2[:, f:]
            r = lax.broadcasted_iota(I32, hg.shape, 0)
            mine = (r >= lo_s) & (r < hi_s)
            hmid = jnp.where(mine, hg * _sigmoid(hg) * hu, 0.0).astype(BF16)
            y = jnp.dot(hmid, wd_b[...], preferred_element_type=F32)

            @pl.when(lo_s == 0)
            def _():
                acc[rows, :] = y

            @pl.when(lo_s > 0)
            def _():
                acc[rows, :] += y

            @pl.when((lo_s > 0) & (hi_s == FFN_SUB))
            def _():
                ys_ref[rows, :] = pack_rows(acc[rows, :])


def _ffn(xs, meta, wg, wu, wd, layer):
    n_rows, dp = xs.shape
    _, n_e, d, f = wg.shape
    tm = FFN_TM
    nv = n_rows // tm + N_EXPERTS - 1
    grid_spec = pltpu.PrefetchScalarGridSpec(
        num_scalar_prefetch=4,
        grid=(nv,),
        in_specs=[pl.BlockSpec((tm, dp), lambda v, ve, vt, lo, hi: (vt[v], 0)),
                  pl.BlockSpec((None, None, d, f), lambda v, ve, vt, lo, hi: (layer, ve[v], 0, 0)),
                  pl.BlockSpec((None, None, d, f), lambda v, ve, vt, lo, hi: (layer, ve[v], 0, 0)),
                  pl.BlockSpec((None, None, f, d), lambda v, ve, vt, lo, hi: (layer, ve[v], 0, 0))],
        out_specs=pl.BlockSpec((tm, dp), lambda v, ve, vt, lo, hi: (vt[v], 0)),
        scratch_shapes=[pltpu.VMEM((d, 2 * f), BF16), pltpu.VMEM((f, d), BF16), pltpu.VMEM((tm, d), F32)],
    )
    return pl.pallas_call(
        _ffn_kernel,
        grid_spec=grid_spec,
        out_shape=jax.ShapeDtypeStruct((n_rows, dp), I32),
        compiler_params=_params("arbitrary"),
        name="ffn",
    )(*meta, xs, wg, wu, wd)


def _combine_kernel(*refs, n_prev):
    (yk_ref, x1_ref, wtm_ref, p_ref, wsg_ref, wsu_ref, wsd_ref,
     wpg_ref, wpp_ref, g_ref, b_ref, x2_ref, x2b_ref) = refs[n_prev:]
    tt = x1_ref.shape[0]
    xb = x1_ref[...].astype(BF16)
    hg = jnp.dot(xb, wsg_ref[...], preferred_element_type=F32)
    hu = jnp.dot(xb, wsu_ref[...], preferred_element_type=F32)
    shared = jnp.dot((hg * _sigmoid(hg) * hu).astype(BF16), wsd_ref[...], preferred_element_type=F32)
    pgate = _sigmoid(jnp.dot(xb, wpg_ref[...], preferred_element_type=F32))
    pproj = jnp.dot(p_ref[...].astype(BF16), wpp_ref[...], preferred_element_type=F32)
    rest = ALPHA * x1_ref[...] + shared + pgate * pproj

    wt = wtm_ref[...]
    r_lo = jnp.zeros((tt, yk_ref.shape[2]), F32)
    r_hi = jnp.zeros((tt, yk_ref.shape[2]), F32)
    for k in range(TOP_K):
        lo, hi = _unpack_pair(yk_ref[k])
        wk = wt[:, k:k + 1]
        r_lo = r_lo + wk * lo
        r_hi = r_hi + wk * hi
    routed = jnp.concatenate([r_lo, r_hi], axis=1)
    x2 = _layer_norm(rest + routed, g_ref[...], b_ref[...])
    x2_ref[...] = x2
    x2b_ref[...] = x2.astype(BF16)


def _combine(yk, x1, wtm, p, layer, part, prev, wsg, wsu, wsd, wpg, wpp, g, b):
    m_all, d = x1.shape
    m = wtm.shape[0]
    tt = min(COMBINE_TT, m)
    first = part * (m // tt)
    yk = yk.reshape(TOP_K, m, d // 2)

    def full(a):
        return pl.BlockSpec(a.shape, lambda i: (0,) * a.ndim)

    token_rows = pl.BlockSpec((tt, d), lambda i: (first + i, 0))
    untouched = pl.BlockSpec(memory_space=pl.ANY)
    return pl.pallas_call(
        functools.partial(_combine_kernel, n_prev=len(prev)),
        grid=(m // tt,),
        in_specs=[untouched] * len(prev) + [
            pl.BlockSpec((TOP_K, tt, d // 2), lambda i: (0, i, 0)),
            token_rows,
            pl.BlockSpec((tt, LANES), lambda i: (i, 0)),
            pl.BlockSpec((None, tt, p.shape[2]), lambda i: (layer, first + i, 0)),
            full(wsg), full(wsu), full(wsd), full(wpg), full(wpp), full(g), full(b)],
        out_specs=[token_rows, token_rows],
        out_shape=[jax.ShapeDtypeStruct((m_all, d), F32), jax.ShapeDtypeStruct((m_all, d), BF16)],
        input_output_aliases={j: j for j in range(len(prev))},
        compiler_params=_params("parallel"),
        name="combine",
    )(*prev, yk, x1, wtm, p, wsg, wsu, wsd, wpg, wpp, g, b)


def kernel(x, p, w_in, conv_w, b_igate, b_fgate, mlstm_norm_g, w_branch_a, w_branch_b, w_out, ln1_g, ln1_b, w_router, router_bias, w_exp_gate, w_exp_up, w_exp_down, w_sh_gate, w_sh_up, w_sh_down, w_ple_gate, w_ple_proj, ln2_g, ln2_b):
    bsz, seq, d = x.shape
    m = bsz * seq
    depth = w_in.shape[0]
    hw = HEADS * HEAD_DIM
    if_lo = 3 * d + 4 * hw
    if_hi = if_lo + 2 * HEADS

    xf = x.reshape(m, d)
    xb = xf.astype(BF16)
    w_in_t = jnp.swapaxes(w_in, 1, 2)
    p = p.reshape(depth, m, -1)
    for i in range(depth):
        w_merge = w_in_t[i, if_hi:]
        w_if = jnp.pad(w_in_t[i, if_lo:if_hi], ((0, LANES - 2 * HEADS), (0, 0)))
        gate_bias = jnp.pad(jnp.concatenate([b_igate[i], b_fgate[i]]), (0, LANES - 2 * HEADS)).reshape(1, LANES)

        z = _matmul(xb, w_in_t, BF16, INPROJ_TM, INPROJ_TN, "inproj", layer=i, n=if_lo).reshape(bsz, seq, -1)
        zm = _matmul(xb, w_merge, BF16, INPROJ_TM, INPROJ_TN, "mergeproj").reshape(bsz, seq, -1)
        gates = _matmul(xb, w_if, F32, INPROJ_TM, LANES, "gateproj").reshape(bsz, seq, LANES)
        y_b = _mlstm(z, gates, gate_bias, mlstm_norm_g[i].reshape(1, hw))
        x1, x1p = _mix(z, zm, y_b, xf.reshape(bsz, seq, d), conv_w[i],
                       w_branch_a[i].astype(BF16), w_branch_b[i].astype(BF16), w_out[i].astype(BF16),
                       ln1_g[i].reshape(1, d), ln1_b[i].reshape(1, d))
        x1 = x1.reshape(m, d)
        x1p = x1p.reshape(m, d // 2)

        ms = m // MOE_STREAMS
        shared_w = (w_sh_gate[i].astype(BF16), w_sh_up[i].astype(BF16), w_sh_down[i].astype(BF16),
                    w_ple_gate[i].astype(BF16), w_ple_proj[i].astype(BF16),
                    ln2_g[i].reshape(1, d), ln2_b[i].reshape(1, d))
        w_router_t = w_router[i].T.reshape(N_GROUPS, GROUP_SIZE, d).swapaxes(0, 1).reshape(N_EXPERTS, d)
        rb = router_bias[i].reshape(N_GROUPS, GROUP_SIZE).T.reshape(N_EXPERTS, 1)
        expert_ids = jnp.arange(N_EXPERTS, dtype=I32)[:, None, None]
        outs = ()
        for part in range(MOE_STREAMS):
            idx, rank, wtm, cnt = _router(x1, w_router_t, rb, part)
            counts = cnt[:, 0].astype(I32).reshape(GROUP_SIZE, N_GROUPS).T.reshape(N_EXPERTS)
            row_start = jnp.cumsum(counts) - counts
            dest = rank + jnp.sum(jnp.where(idx[None] == expert_ids, row_start[:, None, None], 0), axis=0)
            meta = _visit_metadata(counts, FFN_TM, ms * TOP_K)
            xs = _dispatch(x1p, dest, part)
            ys = _ffn(xs, meta, w_exp_gate, w_exp_up, w_exp_down, i)
            yk = _gather_rows(ys, dest)
            outs = tuple(_combine(yk, x1, wtm, p, i, part, outs, *shared_w))
        xf, xb = outs
    return xf.reshape(bsz, seq, d)
```

```python
import functools
import math

import jax
import jax.numpy as jnp
from jax import lax
from jax.experimental import pallas as pl
from jax.experimental.pallas import tpu as pltpu
from jax.experimental.pallas import tpu_sc as plsc

F32 = jnp.float32
BF16 = jnp.bfloat16
U32 = jnp.uint32
I32 = jnp.int32

HEADS = 8
HEAD_DIM = 128
N_EXPERTS = 64
N_GROUPS = 8
GROUP_SIZE = N_EXPERTS // N_GROUPS
TOPK_GROUPS = 4
TOP_K = 8
ROUTED_SCALE = 2.5
DEPTH = 4
ALPHA = (2 * DEPTH) ** 0.25
LN_EPS = 1e-5
RMS_EPS = 1e-6
QK_SCALE = HEAD_DIM ** -0.5
LOG_QK_SCALE = math.log(QK_SCALE)

LANES = 128
VMEM_LIMIT = 56 * 1024 * 1024
NEG_INF = float("-inf")

MLSTM_CHUNK = 256
MLSTM_STEP_CHUNKS = 2
INPROJ_TM, INPROJ_TN = 2048, 1024
ROUTER_TT = 512
FFN_TM = 2048
FFN_SUB = 512
COMBINE_TT = 512
MOE_STREAMS = 1

SC_CORES = 2
SC_SUBCORES = 16
SC_WORKERS = SC_CORES * SC_SUBCORES
SC_ROWS = 128


def _params(*sem):
    return pltpu.CompilerParams(dimension_semantics=sem, vmem_limit_bytes=VMEM_LIMIT)


def _sigmoid(x):
    return 1.0 / (1.0 + jnp.exp(-x))


def _layer_norm(r, g, b):
    mu = jnp.mean(r, axis=-1, keepdims=True)
    d = r - mu
    var = jnp.mean(d * d, axis=-1, keepdims=True)
    return d * lax.rsqrt(var + LN_EPS) * g + b


def _pack_pair(lo, hi):
    return lax.bitcast_convert_type(pltpu.pack_elementwise([lo, hi], packed_dtype=BF16), I32)


def _unpack_pair(w):
    w = lax.bitcast_convert_type(w, U32)
    lo = pltpu.unpack_elementwise(w, index=0, packed_dtype=BF16, unpacked_dtype=F32)
    hi = pltpu.unpack_elementwise(w, index=1, packed_dtype=BF16, unpacked_dtype=F32)
    return lo, hi


def _unpack_rows(w):
    lo, hi = _unpack_pair(w)
    return jnp.concatenate([lo.astype(BF16), hi.astype(BF16)], axis=1)


def _mm_kernel(x_ref, wt_ref, o_ref):
    o_ref[...] = lax.dot_general(x_ref[...], wt_ref[...].astype(BF16), (((1,), (1,)), ((), ())),
                                 preferred_element_type=F32).astype(o_ref.dtype)


def _matmul(x, wt, out_dtype, tm, tn, name, layer=None, n=None):
    m, k = x.shape
    n = wt.shape[-2] if n is None else n
    tm = min(tm, m)
    if layer is None:
        w_spec = pl.BlockSpec((tn, k), lambda i, j: (j, 0))
    else:
        w_spec = pl.BlockSpec((None, tn, k), lambda i, j: (layer, j, 0))
    return pl.pallas_call(
        _mm_kernel,
        grid=(m // tm, n // tn),
        in_specs=[pl.BlockSpec((tm, k), lambda i, j: (i, 0)), w_spec],
        out_specs=pl.BlockSpec((tm, tn), lambda i, j: (i, j)),
        out_shape=jax.ShapeDtypeStruct((m, n), out_dtype),
        compiler_params=_params("parallel", "parallel"),
        name=name,
    )(x, wt)


def _mlstm_rows(q_ref, k_ref, v_ref, o_ref, gates, ng_ref, c_scr, m_scr):
    L = MLSTM_CHUNK
    chunks = [pl.ds(j * L, L) for j in range(q_ref.shape[1] // L)]
    row = lax.broadcasted_iota(I32, (L, L), 0)
    col = lax.broadcasted_iota(I32, (L, L), 1)
    causal = col <= row
    rows = lax.broadcasted_iota(I32, (L, LANES), 0)

    def scan_rows(x, op, identity):
        step = 1
        while step < L:
            x = op(x, jnp.where(rows >= step, pltpu.roll(x, step, axis=0), identity))
            step *= 2
        return x

    gate = []
    m_prev = m_scr[...]
    for j in range(len(chunks)):
        g = gates[j * L:(j + 1) * L, :]
        ig = g
        fg = pltpu.roll(g, LANES - HEADS, axis=1)
        log_f = jnp.minimum(fg, 0.0) - jnp.log(1.0 + jnp.exp(-jnp.abs(fg)))
        b = scan_rows(log_f, jnp.add, 0.0)
        b_last = b[L - 1:L, :]
        a = b_last - b + ig
        m_loc = jnp.max(a, axis=0, keepdims=True)
        m_new = jnp.maximum(b_last + m_prev, m_loc)
        log_inter = b + m_prev
        r = ig - b
        m_out = jnp.maximum(log_inter, b + scan_rows(r, jnp.maximum, NEG_INF))
        gate.append(dict(
            w_loc=jnp.exp(a - m_loc),
            sp=jnp.exp(b_last + m_prev - m_new),
            sl=jnp.exp(m_loc - m_new),
            r_t=jnp.transpose(r),
            u=b - m_out + LOG_QK_SCALE,
            e_inter=jnp.exp(log_inter - m_out + LOG_QK_SCALE),
            e_floor=jnp.exp(-m_out)))
        m_prev = m_new
    m_scr[...] = m_prev

    ones_blk = jnp.ones((L, HEAD_DIM), BF16)
    lanes = [slice(h * HEAD_DIM, (h + 1) * HEAD_DIM) for h in range(HEADS)]
    pairs = [(j, h) for j in range(len(chunks)) for h in range(HEADS)]

    def col_of(name, j, h):
        return gate[j][name][:, h:h + 1]

    q = {(j, h): q_ref[0, chunks[j], lanes[h]] for j, h in pairs}
    k = {(j, h): k_ref[0, chunks[j], lanes[h]] for j, h in pairs}
    v_aug = {(j, h): jnp.concatenate([v_ref[0, chunks[j], lanes[h]], ones_blk], axis=1) for j, h in pairs}
    s1 = {p: lax.dot_general(q[p], k[p], (((1,), (1,)), ((), ())), preferred_element_type=F32) for p in pairs}
    kw = {(j, h): (k[j, h].astype(F32) * col_of("w_loc", j, h)).astype(BF16) for j, h in pairs}
    c_loc = {p: lax.dot_general(kw[p], v_aug[p], (((0,), (0,)), ((), ())), preferred_element_type=F32)
             for p in pairs}
    c_seen = {}
    for h in range(HEADS):
        c = c_scr[h]
        for j in range(len(chunks)):
            c_seen[j, h] = c
            c = col_of("sp", j, h) * c + col_of("sl", j, h) * c_loc[j, h]
        c_scr[h] = c
    inter = {p: jnp.dot(q[p], c_seen[p].astype(BF16), preferred_element_type=F32) for p in pairs}
    s = {(j, h): (s1[j, h] * jnp.where(causal, jnp.exp(col_of("u", j, h) + gate[j]["r_t"][h:h + 1, :]), 0.0)
                  ).astype(BF16) for j, h in pairs}
    intra = {p: jnp.dot(s[p], v_aug[p], preferred_element_type=F32) for p in pairs}
    tot = {(j, h): intra[j, h] + col_of("e_inter", j, h) * inter[j, h] for j, h in pairs}
    hh = {(j, h): tot[j, h][:, :HEAD_DIM] / jnp.maximum(jnp.abs(tot[j, h][:, HEAD_DIM:]), col_of("e_floor", j, h))
          for j, h in pairs}
    hh = {p: hh[p] * lax.rsqrt(jnp.mean(hh[p] * hh[p], axis=1, keepdims=True) + RMS_EPS) for p in pairs}
    y_b = {(j, h): (hh[j, h] * ng_ref[:, lanes[h]] * _sigmoid(o_ref[0, chunks[j], lanes[h]].astype(F32))).astype(BF16)
           for j, h in pairs}
    return [jnp.concatenate([y_b[j, h] for h in range(HEADS)], axis=1) for j in range(len(chunks))]


def _mixer_kernel(q_ref, k_ref, v_ref, o_ref, gates_ref, gb_ref, ng_ref,
                  cin_ref, cout_ref, cval_ref, mg0_ref, mg1_ref, x_ref, cw_ref,
                  wa_ref, wb_ref, wo_ref, g_ref, b_ref, x1_ref, x1p_ref, c_scr, m_scr, carry):
    @pl.when(pl.program_id(1) == 0)
    def _():
        c_scr[...] = jnp.zeros_like(c_scr)
        m_scr[...] = jnp.zeros_like(m_scr)
        carry[...] = jnp.zeros_like(carry)

    tp = MLSTM_CHUNK
    parts = [pl.ds(j * tp, tp) for j in range(x_ref.shape[1] // tp)]
    u = [cin_ref[0, pr, :].astype(F32) * cval_ref[0, pr, :].astype(F32) for pr in parts]
    prev = [carry[...]] + [uj[tp - 8:, :] for uj in u[:-1]]
    carry[...] = u[-1][tp - 8:, :]
    r8 = lax.broadcasted_iota(I32, (8, u[0].shape[1]), 0)

    def shifted(uj, pj, k):
        body = pltpu.roll(uj, k, axis=0)
        head = jnp.where(r8 < k, pltpu.roll(pj, k, axis=0), body[:8, :])
        return jnp.concatenate([head, body[8:, :]], axis=0)

    cw = cw_ref[...]
    conv = [cw[0:1, :] * shifted(uj, pj, 2) + cw[1:2, :] * shifted(uj, pj, 1) + cw[2:3, :] * uj
            for uj, pj in zip(u, prev)]
    y_a = [(cout_ref[0, pr, :].astype(F32) * cj).astype(BF16) for pr, cj in zip(parts, conv)]
    pa = [jnp.dot(yj, wa_ref[...], preferred_element_type=F32) for yj in y_a]
    gated_a = [_sigmoid(mg0_ref[0, pr, :].astype(F32)) * paj for pr, paj in zip(parts, pa)]
    y_b = _mlstm_rows(q_ref, k_ref, v_ref, o_ref, gates_ref[0] + gb_ref[...], ng_ref, c_scr, m_scr)
    pb = [jnp.dot(yj, wb_ref[...], preferred_element_type=F32) for yj in y_b]
    mixed = [(gaj + _sigmoid(mg1_ref[0, pr, :].astype(F32)) * pbj).astype(BF16)
             for pr, gaj, pbj in zip(parts, gated_a, pb)]
    hmix = [jnp.dot(mj, wo_ref[...], preferred_element_type=F32) for mj in mixed]
    for pr, hj in zip(parts, hmix):
        x1 = _layer_norm(ALPHA * x_ref[0, pr, :] + hj, g_ref[...], b_ref[...])
        x1_ref[0, pr, :] = x1
        half = x1.shape[1] // 2
        x1p_ref[0, pr, :] = _pack_pair(x1[:, :half], x1[:, half:])


def _mixer(z, zm, gates, gate_bias, norm_g, x, conv_w, wa, wb, wo, g, b):
    bsz, seq, d = x.shape
    ts = min(MLSTM_STEP_CHUNKS * MLSTM_CHUNK, seq)

    def zspec(cb):
        return pl.BlockSpec((1, ts, d), lambda i, j, cb=cb: (i, j, cb))

    def full(shape):
        return pl.BlockSpec(shape, lambda i, j: (0,) * len(shape))

    tile = pl.BlockSpec((1, ts, d), lambda i, j: (i, j, 0))
    ptile = pl.BlockSpec((1, ts, d // 2), lambda i, j: (i, j, 0))
    return pl.pallas_call(
        _mixer_kernel,
        grid=(bsz, seq // ts),
        in_specs=[zspec(3), zspec(4), zspec(5), zspec(6),
                  pl.BlockSpec((1, ts, LANES), lambda i, j: (i, j, 0)),
                  full(gate_bias.shape), full(norm_g.shape),
                  zspec(0), zspec(1), zspec(2), zspec(0), zspec(1), tile,
                  full(conv_w.shape), full(wa.shape), full(wb.shape), full(wo.shape),
                  full(g.shape), full(b.shape)],
        out_specs=[tile, ptile],
        out_shape=[jax.ShapeDtypeStruct((bsz, seq, d), F32),
                   jax.ShapeDtypeStruct((bsz, seq, d // 2), I32)],
        scratch_shapes=[pltpu.VMEM((HEADS, HEAD_DIM, 2 * HEAD_DIM), F32),
                        pltpu.VMEM((1, LANES), F32),
                        pltpu.VMEM((8, d), F32)],
        compiler_params=_params("parallel", "arbitrary"),
        name="mixer",
    )(z, z, z, z, gates, gate_bias, norm_g, z, z, z, zm, zm, x, conv_w, wa, wb, wo, g, b)


def _router_kernel(x_ref, wrt_ref, rb_ref, idx_ref, rank_ref, wtm_ref, cnt_ref, carry):
    t = x_ref.shape[0]

    @pl.when(pl.program_id(0) == 0)
    def _():
        carry[...] = jnp.zeros_like(carry)

    logits = lax.dot_general(wrt_ref[...], x_ref[...], (((1,), (1,)), ((), ())),
                             precision=lax.Precision.HIGHEST, preferred_element_type=F32)
    scores = _sigmoid(logits)
    shape3 = (GROUP_SIZE, N_GROUPS, t)
    sel = (scores + rb_ref[...]).reshape(shape3)
    scores = scores.reshape(shape3)
    mem = lax.broadcasted_iota(I32, shape3, 0)
    grp = lax.broadcasted_iota(I32, shape3, 1)
    eidx = grp * GROUP_SIZE + mem
    m1 = jnp.max(sel, axis=0, keepdims=True)
    first = jnp.min(jnp.where(sel == m1, mem, GROUP_SIZE), axis=0, keepdims=True)
    m2 = jnp.max(jnp.where(mem == first, NEG_INF, sel), axis=0, keepdims=True)
    rem = m1 + m2
    gidx = lax.broadcasted_iota(I32, rem.shape, 1)
    gmask = jnp.zeros(rem.shape, F32)
    for _ in range(TOPK_GROUPS):
        mx = jnp.max(rem, axis=1, keepdims=True)
        pick = gidx == jnp.min(jnp.where(rem == mx, gidx, N_GROUPS), axis=1, keepdims=True)
        gmask = jnp.where(pick, 1.0, gmask)
        rem = jnp.where(pick, NEG_INF, rem)
    masked = jnp.where(jnp.broadcast_to(gmask, shape3) > 0.5, sel, NEG_INF)
    chosen = jnp.zeros(shape3, F32)
    picks = []
    for _ in range(TOP_K):
        mx = jnp.max(jnp.max(masked, axis=0, keepdims=True), axis=1, keepdims=True)
        cand = jnp.where(masked == mx, eidx, N_EXPERTS)
        fi = jnp.min(jnp.min(cand, axis=0, keepdims=True), axis=1, keepdims=True)
        pick = eidx == fi
        picks.append((fi, pick))
        chosen = jnp.where(pick, 1.0, chosen)
        masked = jnp.where(pick, NEG_INF, masked)
    w = chosen * scores
    denom = jnp.sum(jnp.sum(w, axis=0, keepdims=True), axis=1, keepdims=True)
    gate3 = w / denom * ROUTED_SCALE

    chosen2 = chosen.reshape(N_EXPERTS, t).astype(BF16)
    tok_r = lax.broadcasted_iota(I32, (t, t), 0)
    tok_c = lax.broadcasted_iota(I32, (t, t), 1)
    before = jnp.where(tok_r < tok_c, 1.0, 0.0).astype(BF16)
    prefix = (jnp.dot(chosen2, before, preferred_element_type=F32) + carry[...]).reshape(shape3)
    carry[...] += jnp.dot(chosen2, jnp.ones((t, t), BF16), preferred_element_type=F32)
    cnt_ref[...] = carry[:, :LANES]

    def pick_sum(pick, val):
        return jnp.sum(jnp.sum(jnp.where(pick, val, 0.0), axis=0, keepdims=True), axis=1, keepdims=True)

    sub8 = lax.broadcasted_iota(I32, (TOP_K, t), 0)
    idx8 = jnp.zeros((TOP_K, t), I32)
    rank8 = jnp.zeros((TOP_K, t), F32)
    w8 = jnp.zeros((TOP_K, t), F32)
    for k, (fi, pick) in enumerate(picks):
        idx8 = jnp.where(sub8 == k, jnp.broadcast_to(fi.reshape(1, t), (TOP_K, t)), idx8)
        rank8 = jnp.where(sub8 == k, jnp.broadcast_to(pick_sum(pick, prefix).reshape(1, t), (TOP_K, t)), rank8)
        w8 = jnp.where(sub8 == k, jnp.broadcast_to(pick_sum(pick, gate3).reshape(1, t), (TOP_K, t)), w8)
    idx_ref[...] = idx8
    rank_ref[...] = rank8.astype(I32)
    pad = jnp.zeros((LANES - TOP_K, t), F32)
    wtm_ref[...] = jnp.transpose(jnp.concatenate([w8, pad], axis=0))


def _router(x1, w_router_t, router_bias, part):
    d = x1.shape[1]
    m = x1.shape[0] // MOE_STREAMS
    tt = min(ROUTER_TT, m)
    first = part * (m // tt)
    kt = pl.BlockSpec((TOP_K, tt), lambda i: (0, i))
    return pl.pallas_call(
        _router_kernel,
        grid=(m // tt,),
        in_specs=[pl.BlockSpec((tt, d), lambda i: (first + i, 0)),
                  pl.BlockSpec((N_EXPERTS, d), lambda i: (0, 0)),
                  pl.BlockSpec((N_EXPERTS, 1), lambda i: (0, 0))],
        out_specs=[kt, kt,
                   pl.BlockSpec((tt, LANES), lambda i: (i, 0)),
                   pl.BlockSpec((N_EXPERTS, LANES), lambda i: (0, 0))],
        out_shape=[jax.ShapeDtypeStruct((TOP_K, m), I32),
                   jax.ShapeDtypeStruct((TOP_K, m), I32),
                   jax.ShapeDtypeStruct((m, LANES), F32),
                   jax.ShapeDtypeStruct((N_EXPERTS, LANES), F32)],
        scratch_shapes=[pltpu.VMEM((N_EXPERTS, tt), F32)],
        compiler_params=_params("arbitrary"),
        name="router",
    )(x1, w_router_t, router_bias)


def _visit_metadata(counts, tm, n_rows):
    nt = n_rows // tm
    nv = nt + N_EXPERTS - 1
    ends = jnp.cumsum(counts)
    starts = ends - counts
    first_tile = starts // tm
    ntiles = jnp.where(counts > 0, (ends - 1) // tm - first_tile + 1, 0)
    vend = jnp.cumsum(ntiles)
    vstart = vend - ntiles
    v = jnp.arange(nv, dtype=I32)
    valid = v < vend[-1]
    ve = jnp.minimum(jnp.sum((v[:, None] >= vend[None, :]).astype(I32), axis=1), N_EXPERTS - 1)
    ve = jnp.where(valid, ve, ve[jnp.maximum(vend[-1] - 1, 0)])
    vt = jnp.where(valid, first_tile[ve] + v - vstart[ve], nt - 1)
    lo = jnp.where(valid, jnp.clip(starts[ve] - vt * tm, 0, tm), 0)
    hi = jnp.where(valid, jnp.clip(ends[ve] - vt * tm, 0, tm), 0)
    return ve.astype(I32), vt.astype(I32), lo.astype(I32), hi.astype(I32)


def _sc_mesh():
    return plsc.VectorSubcoreMesh(core_axis_name="c", subcore_axis_name="s")


def _sc_worker():
    return lax.axis_index("s") * SC_CORES + lax.axis_index("c")


def _dispatch(x1p, dest, part):
    dp = x1p.shape[1]
    m = dest.shape[1]
    blocks = m // SC_ROWS
    per_worker = blocks // SC_WORKERS
    x_first = part * blocks
    table = dest.reshape(TOP_K, blocks, SC_ROWS).transpose(1, 0, 2).reshape(blocks * TOP_K, SC_ROWS)

    def body(x_hbm, idx_hbm, xs_hbm, idx_v, rows_v, sem):
        first = _sc_worker() * per_worker

        @pl.loop(0, per_worker)
        def _(j):
            b = first + j
            pltpu.sync_copy(idx_hbm.at[pl.ds(pl.multiple_of(b * TOP_K, TOP_K), TOP_K)], idx_v)
            pltpu.sync_copy(x_hbm.at[pl.ds(pl.multiple_of((x_first + b) * SC_ROWS, SC_ROWS), SC_ROWS)], rows_v)
            copies = [pltpu.async_copy(rows_v, xs_hbm.at[idx_v.at[k]], sem) for k in range(TOP_K)]
            for cp in copies:
                cp.wait()

    return pl.kernel(
        body,
        out_type=jax.ShapeDtypeStruct((m * TOP_K, dp), x1p.dtype),
        mesh=_sc_mesh(),
        scratch_types=[pltpu.VMEM((TOP_K, SC_ROWS), I32), pltpu.VMEM((SC_ROWS, dp), x1p.dtype),
                       pltpu.SemaphoreType.DMA],
        name="dispatch",
    )(x1p, table)


def _gather_rows(ys, dest):
    n_rows, dp = ys.shape
    idx = dest.reshape(n_rows // SC_ROWS, SC_ROWS)
    per_worker = n_rows // SC_ROWS // SC_WORKERS

    def body(ys_hbm, idx_hbm, out_hbm, idx_v, rows_v, sem):
        first = _sc_worker() * per_worker

        @pl.loop(0, per_worker)
        def _(j):
            b = first + j
            pltpu.sync_copy(idx_hbm.at[pl.ds(b, 1)], idx_v)
            pltpu.async_copy(ys_hbm.at[idx_v.at[0]], rows_v, sem).wait()
            pltpu.sync_copy(rows_v, out_hbm.at[pl.ds(pl.multiple_of(b * SC_ROWS, SC_ROWS), SC_ROWS)])

    return pl.kernel(
        body,
        out_type=jax.ShapeDtypeStruct((n_rows, dp), ys.dtype),
        mesh=_sc_mesh(),
        scratch_types=[pltpu.VMEM((1, SC_ROWS), I32), pltpu.VMEM((SC_ROWS, dp), ys.dtype),
                       pltpu.SemaphoreType.DMA],
        name="gather_rows",
    )(ys, idx)


def _ffn_kernel(ve_ref, vt_ref, lo_ref, hi_ref, xs_ref, wg_ref, wu_ref, wd_ref, ys_ref,
                wgu_b, wd_b, acc):
    v = pl.program_id(0)
    lo = lo_ref[v]
    hi = hi_ref[v]
    tm = xs_ref.shape[0]
    f = wg_ref.shape[1]

    @pl.when((v == 0) | (ve_ref[v] != ve_ref[jnp.maximum(v - 1, 0)]))
    def _():
        wgu_b[:, :f] = wg_ref[...].astype(BF16)
        wgu_b[:, f:] = wu_ref[...].astype(BF16)
        wd_b[...] = wd_ref[...].astype(BF16)

    def pack_rows(a):
        half = a.shape[1] // 2
        return _pack_pair(a[:, :half], a[:, half:])

    for sb in range(tm // FFN_SUB):
        s0 = sb * FFN_SUB
        rows = pl.ds(s0, FFN_SUB)
        lo_s = jnp.clip(lo - s0, 0, FFN_SUB)
        hi_s = jnp.clip(hi - s0, 0, FFN_SUB)

        whole = (lo_s == 0) & (hi_s == FFN_SUB)

        @pl.when(whole)
        def _():
            halves = [pl.ds(s0 + j * (FFN_SUB // 2), FFN_SUB // 2) for j in range(2)]
            x = [_unpack_rows(xs_ref[hr, :]) for hr in halves]
            h2 = [jnp.dot(xj, wgu_b[...], preferred_element_type=F32) for xj in x]
            act = [(hj[:, :f] * _sigmoid(hj[:, :f]) * hj[:, f:]).astype(BF16) for hj in h2]
            y = [jnp.dot(aj, wd_b[...], preferred_element_type=F32) for aj in act]
            for hr, yj in zip(halves, y):
                ys_ref[hr, :] = pack_rows(yj)

        @pl.when((hi_s > lo_s) & jnp.logical_not(whole))
        def _():
            x = _unpack_rows(xs_ref[rows, :])
            h2 = jnp.dot(x, wgu_b[...], preferred_element_type=F32)
            hg = h2[:, :f]
            hu = h2[:, f:]
            r = lax.broadcasted_iota(I32, hg.shape, 0)
            mine = (r >= lo_s) & (r < hi_s)
            hmid = jnp.where(mine, hg * _sigmoid(hg) * hu, 0.0).astype(BF16)
            y = jnp.dot(hmid, wd_b[...], preferred_element_type=F32)

            @pl.when(lo_s == 0)
            def _():
                acc[rows, :] = y

            @pl.when(lo_s > 0)
            def _():
                acc[rows, :] += y

            @pl.when((lo_s > 0) & (hi_s == FFN_SUB))
            def _():
                ys_ref[rows, :] = pack_rows(acc[rows, :])


def _ffn(xs, meta, wg, wu, wd, layer):
    n_rows, dp = xs.shape
    _, n_e, d, f = wg.shape
    tm = FFN_TM
    nv = n_rows // tm + N_EXPERTS - 1
    grid_spec = pltpu.PrefetchScalarGridSpec(
        num_scalar_prefetch=4,
        grid=(nv,),
        in_specs=[pl.BlockSpec((tm, dp), lambda v, ve, vt, lo, hi: (vt[v], 0)),
                  pl.BlockSpec((None, None, d, f), lambda v, ve, vt, lo, hi: (layer, ve[v], 0, 0)),
                  pl.BlockSpec((None, None, d, f), lambda v, ve, vt, lo, hi: (layer, ve[v], 0, 0)),
                  pl.BlockSpec((None, None, f, d), lambda v, ve, vt, lo, hi: (layer, ve[v], 0, 0))],
        out_specs=pl.BlockSpec((tm, dp), lambda v, ve, vt, lo, hi: (vt[v], 0)),
        scratch_shapes=[pltpu.VMEM((d, 2 * f), BF16), pltpu.VMEM((f, d), BF16), pltpu.VMEM((tm, d), F32)],
    )
    return pl.pallas_call(
        _ffn_kernel,
        grid_spec=grid_spec,
        out_shape=jax.ShapeDtypeStruct((n_rows, dp), I32),
        compiler_params=_params("arbitrary"),
        name="ffn",
    )(*meta, xs, wg, wu, wd)


def _combine_kernel(*refs, n_prev):
    (yk_ref, x1_ref, wtm_ref, p_ref, wsg_ref, wsu_ref, wsd_ref,
     wpg_ref, wpp_ref, g_ref, b_ref, x2_ref, x2b_ref) = refs[n_prev:]
    tt = x1_ref.shape[0]
    xb = x1_ref[...].astype(BF16)
    hg = jnp.dot(xb, wsg_ref[...], preferred_element_type=F32)
    hu = jnp.dot(xb, wsu_ref[...], preferred_element_type=F32)
    shared = jnp.dot((hg * _sigmoid(hg) * hu).astype(BF16), wsd_ref[...], preferred_element_type=F32)
    pgate = _sigmoid(jnp.dot(xb, wpg_ref[...], preferred_element_type=F32))
    pproj = jnp.dot(p_ref[...].astype(BF16), wpp_ref[...], preferred_element_type=F32)
    rest = ALPHA * x1_ref[...] + shared + pgate * pproj

    wt = wtm_ref[...]
    r_lo = jnp.zeros((tt, yk_ref.shape[2]), F32)
    r_hi = jnp.zeros((tt, yk_ref.shape[2]), F32)
    for k in range(TOP_K):
        lo, hi = _unpack_pair(yk_ref[k])
        wk = wt[:, k:k + 1]
        r_lo = r_lo + wk * lo
        r_hi = r_hi + wk * hi
    routed = jnp.concatenate([r_lo, r_hi], axis=1)
    x2 = _layer_norm(rest + routed, g_ref[...], b_ref[...])
    x2_ref[...] = x2
    x2b_ref[...] = x2.astype(BF16)


def _combine(yk, x1, wtm, p, layer, part, prev, wsg, wsu, wsd, wpg, wpp, g, b):
    m_all, d = x1.shape
    m = wtm.shape[0]
    tt = min(COMBINE_TT, m)
    first = part * (m // tt)
    yk = yk.reshape(TOP_K, m, d // 2)

    def full(a):
        return pl.BlockSpec(a.shape, lambda i: (0,) * a.ndim)

    token_rows = pl.BlockSpec((tt, d), lambda i: (first + i, 0))
    untouched = pl.BlockSpec(memory_space=pl.ANY)
    return pl.pallas_call(
        functools.partial(_combine_kernel, n_prev=len(prev)),
        grid=(m // tt,),
        in_specs=[untouched] * len(prev) + [
            pl.BlockSpec((TOP_K, tt, d // 2), lambda i: (0, i, 0)),
            token_rows,
            pl.BlockSpec((tt, LANES), lambda i: (i, 0)),
            pl.BlockSpec((None, tt, p.shape[2]), lambda i: (layer, first + i, 0)),
            full(wsg), full(wsu), full(wsd), full(wpg), full(wpp), full(g), full(b)],
        out_specs=[token_rows, token_rows],
        out_shape=[jax.ShapeDtypeStruct((m_all, d), F32), jax.ShapeDtypeStruct((m_all, d), BF16)],
        input_output_aliases={j: j for j in range(len(prev))},
        compiler_params=_params("parallel"),
        name="combine",
    )(*prev, yk, x1, wtm, p, wsg, wsu, wsd, wpg, wpp, g, b)


def kernel(x, p, w_in, conv_w, b_igate, b_fgate, mlstm_norm_g, w_branch_a, w_branch_b, w_out, ln1_g, ln1_b, w_router, router_bias, w_exp_gate, w_exp_up, w_exp_down, w_sh_gate, w_sh_up, w_sh_down, w_ple_gate, w_ple_proj, ln2_g, ln2_b):
    bsz, seq, d = x.shape
    m = bsz * seq
    depth = w_in.shape[0]
    hw = HEADS * HEAD_DIM
    if_lo = 3 * d + 4 * hw
    if_hi = if_lo + 2 * HEADS

    xf = x.reshape(m, d)
    xb = xf.astype(BF16)
    w_in_t = jnp.swapaxes(w_in, 1, 2)
    p = p.reshape(depth, m, -1)
    for i in range(depth):
        w_merge = w_in_t[i, if_hi:]
        w_if = jnp.pad(w_in_t[i, if_lo:if_hi], ((0, LANES - 2 * HEADS), (0, 0)))
        gate_bias = jnp.pad(jnp.concatenate([b_igate[i], b_fgate[i]]), (0, LANES - 2 * HEADS)).reshape(1, LANES)

        z = _matmul(xb, w_in_t, BF16, INPROJ_TM, INPROJ_TN, "inproj", layer=i, n=if_lo).reshape(bsz, seq, -1)
        zm = _matmul(xb, w_merge, BF16, INPROJ_TM, INPROJ_TN, "mergeproj").reshape(bsz, seq, -1)
        gates = _matmul(xb, w_if, F32, INPROJ_TM, LANES, "gateproj").reshape(bsz, seq, LANES)
        x1, x1p = _mixer(z, zm, gates, gate_bias, mlstm_norm_g[i].reshape(1, hw),
                         xf.reshape(bsz, seq, d), conv_w[i],
                         w_branch_a[i].astype(BF16), w_branch_b[i].astype(BF16), w_out[i].astype(BF16),
                         ln1_g[i].reshape(1, d), ln1_b[i].reshape(1, d))
        x1 = x1.reshape(m, d)
        x1p = x1p.reshape(m, d // 2)

        ms = m // MOE_STREAMS
        shared_w = (w_sh_gate[i].astype(BF16), w_sh_up[i].astype(BF16), w_sh_down[i].astype(BF16),
                    w_ple_gate[i].astype(BF16), w_ple_proj[i].astype(BF16),
                    ln2_g[i].reshape(1, d), ln2_b[i].reshape(1, d))
        w_router_t = w_router[i].T.reshape(N_GROUPS, GROUP_SIZE, d).swapaxes(0, 1).reshape(N_EXPERTS, d)
        rb = router_bias[i].reshape(N_GROUPS, GROUP_SIZE).T.reshape(N_EXPERTS, 1)
        expert_ids = jnp.arange(N_EXPERTS, dtype=I32)[:, None, None]
        outs = ()
        for part in range(MOE_STREAMS):
            idx, rank, wtm, cnt = _router(x1, w_router_t, rb, part)
            counts = cnt[:, 0].astype(I32).reshape(GROUP_SIZE, N_GROUPS).T.reshape(N_EXPERTS)
            row_start = jnp.cumsum(counts) - counts
            dest = rank + jnp.sum(jnp.where(idx[None] == expert_ids, row_start[:, None, None], 0), axis=0)
            meta = _visit_metadata(counts, FFN_TM, ms * TOP_K)
            xs = _dispatch(x1p, dest, part)
            ys = _ffn(xs, meta, w_exp_gate, w_exp_up, w_exp_down, i)
            yk = _gather_rows(ys, dest)
            outs = tuple(_combine(yk, x1, wtm, p, i, part, outs, *shared_w))
        xf, xb = outs
    return xf.reshape(bsz, seq, d)
```

```python
import functools
import math

import jax
import jax.numpy as jnp
from jax import lax
from jax.experimental import pallas as pl
from jax.experimental.pallas import tpu as pltpu
from jax.experimental.pallas import tpu_sc as plsc

F32 = jnp.float32
BF16 = jnp.bfloat16
U32 = jnp.uint32
I32 = jnp.int32

HEADS = 8
HEAD_DIM = 128
N_EXPERTS = 64
N_GROUPS = 8
GROUP_SIZE = N_EXPERTS // N_GROUPS
TOPK_GROUPS = 4
TOP_K = 8
ROUTED_SCALE = 2.5
DEPTH = 4
ALPHA = (2 * DEPTH) ** 0.25
LN_EPS = 1e-5
RMS_EPS = 1e-6
QK_SCALE = HEAD_DIM ** -0.5
LOG_QK_SCALE = math.log(QK_SCALE)

LANES = 128
VMEM_LIMIT = 56 * 1024 * 1024
NEG_INF = float("-inf")

MLSTM_CHUNK = 256
MLSTM_STEP_CHUNKS = 2
INPROJ_TM, INPROJ_TN = 2048, 1024
ROUTER_TT = 512
FFN_TM = 2048
FFN_SUB = 512
COMBINE_TT = 512
MOE_STREAMS = 1

SC_CORES = 2
SC_SUBCORES = 16
SC_WORKERS = SC_CORES * SC_SUBCORES
SC_ROWS = 128


def _params(*sem):
    return pltpu.CompilerParams(dimension_semantics=sem, vmem_limit_bytes=VMEM_LIMIT)


def _sigmoid(x):
    return 1.0 / (1.0 + jnp.exp(-x))


def _layer_norm(r, g, b):
    mu = jnp.mean(r, axis=-1, keepdims=True)
    d = r - mu
    var = jnp.mean(d * d, axis=-1, keepdims=True)
    return d * lax.rsqrt(var + LN_EPS) * g + b


def _pack_pair(lo, hi):
    return lax.bitcast_convert_type(pltpu.pack_elementwise([lo, hi], packed_dtype=BF16), I32)


def _unpack_pair(w):
    w = lax.bitcast_convert_type(w, U32)
    lo = pltpu.unpack_elementwise(w, index=0, packed_dtype=BF16, unpacked_dtype=F32)
    hi = pltpu.unpack_elementwise(w, index=1, packed_dtype=BF16, unpacked_dtype=F32)
    return lo, hi


def _unpack_rows(w):
    lo, hi = _unpack_pair(w)
    return jnp.concatenate([lo.astype(BF16), hi.astype(BF16)], axis=1)


def _mm_kernel(x_ref, wt_ref, o_ref):
    o_ref[...] = lax.dot_general(x_ref[...], wt_ref[...].astype(BF16), (((1,), (1,)), ((), ())),
                                 preferred_element_type=F32).astype(o_ref.dtype)


def _matmul(x, wt, out_dtype, tm, tn, name, layer=None, n=None):
    m, k = x.shape
    n = wt.shape[-2] if n is None else n
    tm = min(tm, m)
    if layer is None:
        w_spec = pl.BlockSpec((tn, k), lambda i, j: (j, 0))
    else:
        w_spec = pl.BlockSpec((None, tn, k), lambda i, j: (layer, j, 0))
    return pl.pallas_call(
        _mm_kernel,
        grid=(m // tm, n // tn),
        in_specs=[pl.BlockSpec((tm, k), lambda i, j: (i, 0)), w_spec],
        out_specs=pl.BlockSpec((tm, tn), lambda i, j: (i, j)),
        out_shape=jax.ShapeDtypeStruct((m, n), out_dtype),
        compiler_params=_params("parallel", "parallel"),
        name=name,
    )(x, wt)


def _mlstm_rows(q_ref, k_ref, v_ref, o_ref, gates, ng_ref, c_scr, m_scr):
    L = MLSTM_CHUNK
    chunks = [pl.ds(j * L, L) for j in range(q_ref.shape[1] // L)]
    row = lax.broadcasted_iota(I32, (L, L), 0)
    col = lax.broadcasted_iota(I32, (L, L), 1)
    causal = col <= row
    rows = lax.broadcasted_iota(I32, (L, LANES), 0)

    def scan_rows(x, op, identity):
        step = 1
        while step < L:
            x = op(x, jnp.where(rows >= step, pltpu.roll(x, step, axis=0), identity))
            step *= 2
        return x

    gate = []
    m_prev = m_scr[...]
    for j in range(len(chunks)):
        g = gates[j * L:(j + 1) * L, :]
        ig = g
        fg = pltpu.roll(g, LANES - HEADS, axis=1)
        log_f = jnp.minimum(fg, 0.0) - jnp.log(1.0 + jnp.exp(-jnp.abs(fg)))
        b = scan_rows(log_f, jnp.add, 0.0)
        b_last = b[L - 1:L, :]
        a = b_last - b + ig
        m_loc = jnp.max(a, axis=0, keepdims=True)
        m_new = jnp.maximum(b_last + m_prev, m_loc)
        log_inter = b + m_prev
        r = ig - b
        m_out = jnp.maximum(log_inter, b + scan_rows(r, jnp.maximum, NEG_INF))
        gate.append(dict(
            w_loc=jnp.exp(a - m_loc),
            sp=jnp.exp(b_last + m_prev - m_new),
            sl=jnp.exp(m_loc - m_new),
            r_t=jnp.transpose(r),
            u=b - m_out + LOG_QK_SCALE,
            e_inter=jnp.exp(log_inter - m_out + LOG_QK_SCALE),
            e_floor=jnp.exp(-m_out)))
        m_prev = m_new
    m_scr[...] = m_prev

    ones_blk = jnp.ones((L, HEAD_DIM), BF16)
    lanes = [slice(h * HEAD_DIM, (h + 1) * HEAD_DIM) for h in range(HEADS)]
    pairs = [(j, h) for j in range(len(chunks)) for h in range(HEADS)]

    def col_of(name, j, h):
        return gate[j][name][:, h:h + 1]

    q = {(j, h): q_ref[0, chunks[j], lanes[h]] for j, h in pairs}
    k = {(j, h): k_ref[0, chunks[j], lanes[h]] for j, h in pairs}
    v_aug = {(j, h): jnp.concatenate([v_ref[0, chunks[j], lanes[h]], ones_blk], axis=1) for j, h in pairs}
    s1 = {p: lax.dot_general(q[p], k[p], (((1,), (1,)), ((), ())), preferred_element_type=F32) for p in pairs}
    kw = {(j, h): (k[j, h].astype(F32) * col_of("w_loc", j, h)).astype(BF16) for j, h in pairs}
    c_loc = {p: lax.dot_general(kw[p], v_aug[p], (((0,), (0,)), ((), ())), preferred_element_type=F32)
             for p in pairs}
    c_seen = {}
    for h in range(HEADS):
        c = c_scr[h]
        for j in range(len(chunks)):
            c_seen[j, h] = c
            c = col_of("sp", j, h) * c + col_of("sl", j, h) * c_loc[j, h]
        c_scr[h] = c
    inter = {p: jnp.dot(q[p], c_seen[p].astype(BF16), preferred_element_type=F32) for p in pairs}
    s = {(j, h): (s1[j, h] * jnp.where(causal, jnp.exp(col_of("u", j, h) + gate[j]["r_t"][h:h + 1, :]), 0.0)
                  ).astype(BF16) for j, h in pairs}
    intra = {p: jnp.dot(s[p], v_aug[p], preferred_element_type=F32) for p in pairs}
    tot = {(j, h): intra[j, h] + col_of("e_inter", j, h) * inter[j, h] for j, h in pairs}
    hh = {(j, h): tot[j, h][:, :HEAD_DIM] / jnp.maximum(jnp.abs(tot[j, h][:, HEAD_DIM:]), col_of("e_floor", j, h))
          for j, h in pairs}
    hh = {p: hh[p] * lax.rsqrt(jnp.mean(hh[p] * hh[p], axis=1, keepdims=True) + RMS_EPS) for p in pairs}
    y_b = {(j, h): (hh[j, h] * ng_ref[:, lanes[h]] * _sigmoid(o_ref[0, chunks[j], lanes[h]].astype(F32))).astype(BF16)
           for j, h in pairs}
    return [jnp.concatenate([y_b[j, h] for h in range(HEADS)], axis=1) for j in range(len(chunks))]


def _mixer_kernel(q_ref, k_ref, v_ref, o_ref, gates_ref, gb_ref, ng_ref,
                  cin_ref, cout_ref, cval_ref, mg0_ref, mg1_ref, x_ref, cw_ref,
                  wa_ref, wb_ref, wo_ref, g_ref, b_ref, x1_ref, x1p_ref, c_scr, m_scr, carry):
    @pl.when(pl.program_id(1) == 0)
    def _():
        c_scr[...] = jnp.zeros_like(c_scr)
        m_scr[...] = jnp.zeros_like(m_scr)
        carry[...] = jnp.zeros_like(carry)

    tp = MLSTM_CHUNK
    parts = [pl.ds(j * tp, tp) for j in range(x_ref.shape[1] // tp)]
    u = [cin_ref[0, pr, :].astype(F32) * cval_ref[0, pr, :].astype(F32) for pr in parts]
    prev = [carry[...]] + [uj[tp - 8:, :] for uj in u[:-1]]
    carry[...] = u[-1][tp - 8:, :]
    r8 = lax.broadcasted_iota(I32, (8, u[0].shape[1]), 0)

    def shifted(uj, pj, k):
        body = pltpu.roll(uj, k, axis=0)
        head = jnp.where(r8 < k, pltpu.roll(pj, k, axis=0), body[:8, :])
        return jnp.concatenate([head, body[8:, :]], axis=0)

    cw = cw_ref[...]
    conv = [cw[0:1, :] * shifted(uj, pj, 2) + cw[1:2, :] * shifted(uj, pj, 1) + cw[2:3, :] * uj
            for uj, pj in zip(u, prev)]
    y_a = [(cout_ref[0, pr, :].astype(F32) * cj).astype(BF16) for pr, cj in zip(parts, conv)]
    pa = [jnp.dot(yj, wa_ref[...], preferred_element_type=F32) for yj in y_a]
    gated_a = [_sigmoid(mg0_ref[0, pr, :].astype(F32)) * paj for pr, paj in zip(parts, pa)]
    y_b = _mlstm_rows(q_ref, k_ref, v_ref, o_ref, gates_ref[0] + gb_ref[...], ng_ref, c_scr, m_scr)
    pb = [jnp.dot(yj, wb_ref[...], preferred_element_type=F32) for yj in y_b]
    mixed = [(gaj + _sigmoid(mg1_ref[0, pr, :].astype(F32)) * pbj).astype(BF16)
             for pr, gaj, pbj in zip(parts, gated_a, pb)]
    hmix = [jnp.dot(mj, wo_ref[...], preferred_element_type=F32) for mj in mixed]
    for pr, hj in zip(parts, hmix):
        x1 = _layer_norm(ALPHA * x_ref[0, pr, :] + hj, g_ref[...], b_ref[...])
        x1_ref[0, pr, :] = x1
        half = x1.shape[1] // 2
        x1p_ref[0, pr, :] = _pack_pair(x1[:, :half], x1[:, half:])


def _mixer(z, zm, gates, gate_bias, norm_g, x, conv_w, wa, wb, wo, g, b):
    bsz, seq, d = x.shape
    ts = min(MLSTM_STEP_CHUNKS * MLSTM_CHUNK, seq)

    def zspec(cb):
        return pl.BlockSpec((1, ts, d), lambda i, j, cb=cb: (i, j, cb))

    def full(shape):
        return pl.BlockSpec(shape, lambda i, j: (0,) * len(shape))

    tile = pl.BlockSpec((1, ts, d), lambda i, j: (i, j, 0))
    ptile = pl.BlockSpec((1, ts, d // 2), lambda i, j: (i, j, 0))
    return pl.pallas_call(
        _mixer_kernel,
        grid=(bsz, seq // ts),
        in_specs=[zspec(3), zspec(4), zspec(5), zspec(6),
                  pl.BlockSpec((1, ts, LANES), lambda i, j: (i, j, 0)),
                  full(gate_bias.shape), full(norm_g.shape),
                  zspec(0), zspec(1), zspec(2), zspec(0), zspec(1), tile,
                  full(conv_w.shape), full(wa.shape), full(wb.shape), full(wo.shape),
                  full(g.shape), full(b.shape)],
        out_specs=[tile, ptile],
        out_shape=[jax.ShapeDtypeStruct((bsz, seq, d), F32),
                   jax.ShapeDtypeStruct((bsz, seq, d // 2), I32)],
        scratch_shapes=[pltpu.VMEM((HEADS, HEAD_DIM, 2 * HEAD_DIM), F32),
                        pltpu.VMEM((1, LANES), F32),
                        pltpu.VMEM((8, d), F32)],
        compiler_params=_params("parallel", "arbitrary"),
        name="mixer",
    )(z, z, z, z, gates, gate_bias, norm_g, z, z, z, zm, zm, x, conv_w, wa, wb, wo, g, b)


def _router_kernel(x_ref, wrt_ref, rb_ref, idx_ref, rank_ref, wtm_ref, cnt_ref, carry):
    t = x_ref.shape[0]

    @pl.when(pl.program_id(0) == 0)
    def _():
        carry[...] = jnp.zeros_like(carry)

    def split(a):
        hi = a.astype(BF16)
        return hi, (a - hi.astype(F32)).astype(BF16)

    def dot_t(a, b):
        return lax.dot_general(a, b, (((1,), (1,)), ((), ())), preferred_element_type=F32)

    w_hi, w_lo = split(wrt_ref[...])
    x_hi, x_lo = split(x_ref[...])
    logits = dot_t(w_hi, x_hi) + (dot_t(w_hi, x_lo) + dot_t(w_lo, x_hi))
    scores = _sigmoid(logits)
    shape3 = (GROUP_SIZE, N_GROUPS, t)
    sel = (scores + rb_ref[...]).reshape(shape3)
    scores = scores.reshape(shape3)
    mem = lax.broadcasted_iota(I32, shape3, 0)
    grp = lax.broadcasted_iota(I32, shape3, 1)
    eidx = grp * GROUP_SIZE + mem
    m1 = jnp.max(sel, axis=0, keepdims=True)
    first = jnp.min(jnp.where(sel == m1, mem, GROUP_SIZE), axis=0, keepdims=True)
    m2 = jnp.max(jnp.where(mem == first, NEG_INF, sel), axis=0, keepdims=True)
    rem = m1 + m2
    gidx = lax.broadcasted_iota(I32, rem.shape, 1)
    gmask = jnp.zeros(rem.shape, F32)
    for _ in range(TOPK_GROUPS):
        mx = jnp.max(rem, axis=1, keepdims=True)
        pick = gidx == jnp.min(jnp.where(rem == mx, gidx, N_GROUPS), axis=1, keepdims=True)
        gmask = jnp.where(pick, 1.0, gmask)
        rem = jnp.where(pick, NEG_INF, rem)
    masked = jnp.where(jnp.broadcast_to(gmask, shape3) > 0.5, sel, NEG_INF)
    chosen = jnp.zeros(shape3, F32)
    picks = []
    for _ in range(TOP_K):
        mx = jnp.max(jnp.max(masked, axis=0, keepdims=True), axis=1, keepdims=True)
        cand = jnp.where(masked == mx, eidx, N_EXPERTS)
        fi = jnp.min(jnp.min(cand, axis=0, keepdims=True), axis=1, keepdims=True)
        pick = eidx == fi
        picks.append((fi, pick))
        chosen = jnp.where(pick, 1.0, chosen)
        masked = jnp.where(pick, NEG_INF, masked)
    w = chosen * scores
    denom = jnp.sum(jnp.sum(w, axis=0, keepdims=True), axis=1, keepdims=True)
    gate3 = w / denom * ROUTED_SCALE

    chosen2 = chosen.reshape(N_EXPERTS, t).astype(BF16)
    tok_r = lax.broadcasted_iota(I32, (t, t), 0)
    tok_c = lax.broadcasted_iota(I32, (t, t), 1)
    before = jnp.where(tok_r < tok_c, 1.0, 0.0).astype(BF16)
    prefix = (jnp.dot(chosen2, before, preferred_element_type=F32) + carry[...]).reshape(shape3)
    carry[...] += jnp.dot(chosen2, jnp.ones((t, t), BF16), preferred_element_type=F32)
    cnt_ref[...] = carry[:, :LANES]

    def pick_sum(pick, val):
        return jnp.sum(jnp.sum(jnp.where(pick, val, 0.0), axis=0, keepdims=True), axis=1, keepdims=True)

    sub8 = lax.broadcasted_iota(I32, (TOP_K, t), 0)
    idx8 = jnp.zeros((TOP_K, t), I32)
    rank8 = jnp.zeros((TOP_K, t), F32)
    w8 = jnp.zeros((TOP_K, t), F32)
    for k, (fi, pick) in enumerate(picks):
        idx8 = jnp.where(sub8 == k, jnp.broadcast_to(fi.reshape(1, t), (TOP_K, t)), idx8)
        rank8 = jnp.where(sub8 == k, jnp.broadcast_to(pick_sum(pick, prefix).reshape(1, t), (TOP_K, t)), rank8)
        w8 = jnp.where(sub8 == k, jnp.broadcast_to(pick_sum(pick, gate3).reshape(1, t), (TOP_K, t)), w8)
    idx_ref[...] = idx8
    rank_ref[...] = rank8.astype(I32)
    pad = jnp.zeros((LANES - TOP_K, t), F32)
    wtm_ref[...] = jnp.transpose(jnp.concatenate([w8, pad], axis=0))


def _router(x1, w_router_t, router_bias, part):
    d = x1.shape[1]
    m = x1.shape[0] // MOE_STREAMS
    tt = min(ROUTER_TT, m)
    first = part * (m // tt)
    kt = pl.BlockSpec((TOP_K, tt), lambda i: (0, i))
    return pl.pallas_call(
        _router_kernel,
        grid=(m // tt,),
        in_specs=[pl.BlockSpec((tt, d), lambda i: (first + i, 0)),
                  pl.BlockSpec((N_EXPERTS, d), lambda i: (0, 0)),
                  pl.BlockSpec((N_EXPERTS, 1), lambda i: (0, 0))],
        out_specs=[kt, kt,
                   pl.BlockSpec((tt, LANES), lambda i: (i, 0)),
                   pl.BlockSpec((N_EXPERTS, LANES), lambda i: (0, 0))],
        out_shape=[jax.ShapeDtypeStruct((TOP_K, m), I32),
                   jax.ShapeDtypeStruct((TOP_K, m), I32),
                   jax.ShapeDtypeStruct((m, LANES), F32),
                   jax.ShapeDtypeStruct((N_EXPERTS, LANES), F32)],
        scratch_shapes=[pltpu.VMEM((N_EXPERTS, tt), F32)],
        compiler_params=_params("arbitrary"),
        name="router",
    )(x1, w_router_t, router_bias)


def _visit_metadata(counts, tm, n_rows):
    nt = n_rows // tm
    nv = nt + N_EXPERTS - 1
    ends = jnp.cumsum(counts)
    starts = ends - counts
    first_tile = starts // tm
    ntiles = jnp.where(counts > 0, (ends - 1) // tm - first_tile + 1, 0)
    vend = jnp.cumsum(ntiles)
    vstart = vend - ntiles
    v = jnp.arange(nv, dtype=I32)
    valid = v < vend[-1]
    ve = jnp.minimum(jnp.sum((v[:, None] >= vend[None, :]).astype(I32), axis=1), N_EXPERTS - 1)
    ve = jnp.where(valid, ve, ve[jnp.maximum(vend[-1] - 1, 0)])
    vt = jnp.where(valid, first_tile[ve] + v - vstart[ve], nt - 1)
    lo = jnp.where(valid, jnp.clip(starts[ve] - vt * tm, 0, tm), 0)
    hi = jnp.where(valid, jnp.clip(ends[ve] - vt * tm, 0, tm), 0)
    return ve.astype(I32), vt.astype(I32), lo.astype(I32), hi.astype(I32)


def _sc_mesh():
    return plsc.VectorSubcoreMesh(core_axis_name="c", subcore_axis_name="s")


def _sc_worker():
    return lax.axis_index("s") * SC_CORES + lax.axis_index("c")


def _dispatch(x1p, dest, part):
    dp = x1p.shape[1]
    m = dest.shape[1]
    blocks = m // SC_ROWS
    per_worker = blocks // SC_WORKERS
    x_first = part * blocks
    table = dest.reshape(TOP_K, blocks, SC_ROWS).transpose(1, 0, 2).reshape(blocks * TOP_K, SC_ROWS)

    def body(x_hbm, idx_hbm, xs_hbm, idx_v, rows_v, sem):
        first = _sc_worker() * per_worker

        @pl.loop(0, per_worker)
        def _(j):
            b = first + j
            pltpu.sync_copy(idx_hbm.at[pl.ds(pl.multiple_of(b * TOP_K, TOP_K), TOP_K)], idx_v)
            pltpu.sync_copy(x_hbm.at[pl.ds(pl.multiple_of((x_first + b) * SC_ROWS, SC_ROWS), SC_ROWS)], rows_v)
            copies = [pltpu.async_copy(rows_v, xs_hbm.at[idx_v.at[k]], sem) for k in range(TOP_K)]
            for cp in copies:
                cp.wait()

    return pl.kernel(
        body,
        out_type=jax.ShapeDtypeStruct((m * TOP_K, dp), x1p.dtype),
        mesh=_sc_mesh(),
        scratch_types=[pltpu.VMEM((TOP_K, SC_ROWS), I32), pltpu.VMEM((SC_ROWS, dp), x1p.dtype),
                       pltpu.SemaphoreType.DMA],
        name="dispatch",
    )(x1p, table)


def _gather_rows(ys, dest):
    n_rows, dp = ys.shape
    idx = dest.reshape(n_rows // SC_ROWS, SC_ROWS)
    per_worker = n_rows // SC_ROWS // SC_WORKERS

    def body(ys_hbm, idx_hbm, out_hbm, idx_v, rows_v, sem):
        first = _sc_worker() * per_worker

        @pl.loop(0, per_worker)
        def _(j):
            b = first + j
            pltpu.sync_copy(idx_hbm.at[pl.ds(b, 1)], idx_v)
            pltpu.async_copy(ys_hbm.at[idx_v.at[0]], rows_v, sem).wait()
            pltpu.sync_copy(rows_v, out_hbm.at[pl.ds(pl.multiple_of(b * SC_ROWS, SC_ROWS), SC_ROWS)])

    return pl.kernel(
        body,
        out_type=jax.ShapeDtypeStruct((n_rows, dp), ys.dtype),
        mesh=_sc_mesh(),
        scratch_types=[pltpu.VMEM((1, SC_ROWS), I32), pltpu.VMEM((SC_ROWS, dp), ys.dtype),
                       pltpu.SemaphoreType.DMA],
        name="gather_rows",
    )(ys, idx)


def _ffn_kernel(ve_ref, vt_ref, lo_ref, hi_ref, xs_ref, wg_ref, wu_ref, wd_ref, ys_ref,
                wgu_b, wd_b, acc):
    v = pl.program_id(0)
    lo = lo_ref[v]
    hi = hi_ref[v]
    tm = xs_ref.shape[0]
    f = wg_ref.shape[1]

    @pl.when((v == 0) | (ve_ref[v] != ve_ref[jnp.maximum(v - 1, 0)]))
    def _():
        wgu_b[:, :f] = wg_ref[...].astype(BF16)
        wgu_b[:, f:] = wu_ref[...].astype(BF16)
        wd_b[...] = wd_ref[...].astype(BF16)

    def pack_rows(a):
        half = a.shape[1] // 2
        return _pack_pair(a[:, :half], a[:, half:])

    for sb in range(tm // FFN_SUB):
        s0 = sb * FFN_SUB
        rows = pl.ds(s0, FFN_SUB)
        lo_s = jnp.clip(lo - s0, 0, FFN_SUB)
        hi_s = jnp.clip(hi - s0, 0, FFN_SUB)

        whole = (lo_s == 0) & (hi_s == FFN_SUB)

        @pl.when(whole)
        def _():
            halves = [pl.ds(s0 + j * (FFN_SUB // 2), FFN_SUB // 2) for j in range(2)]
            x = [_unpack_rows(xs_ref[hr, :]) for hr in halves]
            h2 = [jnp.dot(xj, wgu_b[...], preferred_element_type=F32) for xj in x]
            act = [(hj[:, :f] * _sigmoid(hj[:, :f]) * hj[:, f:]).astype(BF16) for hj in h2]
            y = [jnp.dot(aj, wd_b[...], preferred_element_type=F32) for aj in act]
            for hr, yj in zip(halves, y):
                ys_ref[hr, :] = pack_rows(yj)

        half = FFN_SUB // 2
        for j in range(2):
            hrows = pl.ds(s0 + j * half, half)
            lo_h = jnp.clip(lo_s - j * half, 0, half)
            hi_h = jnp.clip(hi_s - j * half, 0, half)

            @pl.when((hi_h > lo_h) & jnp.logical_not(whole))
            def _(hrows=hrows, lo_h=lo_h, hi_h=hi_h):
                x = _unpack_rows(xs_ref[hrows, :])
                h2 = jnp.dot(x, wgu_b[...], preferred_element_type=F32)
                hg = h2[:, :f]
                hu = h2[:, f:]
                r = lax.broadcasted_iota(I32, hg.shape, 0)
                mine = (r >= lo_h) & (r < hi_h)
                hmid = jnp.where(mine, hg * _sigmoid(hg) * hu, 0.0).astype(BF16)
                y = jnp.dot(hmid, wd_b[...], preferred_element_type=F32)

                @pl.when((lo_h == 0) & (hi_h == half))
                def _():
                    ys_ref[hrows, :] = pack_rows(y)

                @pl.when((lo_h == 0) & (hi_h < half))
                def _():
                    acc[hrows, :] = y

                @pl.when(lo_h > 0)
                def _():
                    acc[hrows, :] += y

                @pl.when((lo_h > 0) & (hi_h == half))
                def _():
                    ys_ref[hrows, :] = pack_rows(acc[hrows, :])


def _ffn(xs, meta, wg, wu, wd, layer):
    n_rows, dp = xs.shape
    _, n_e, d, f = wg.shape
    tm = FFN_TM
    nv = n_rows // tm + N_EXPERTS - 1
    grid_spec = pltpu.PrefetchScalarGridSpec(
        num_scalar_prefetch=4,
        grid=(nv,),
        in_specs=[pl.BlockSpec((tm, dp), lambda v, ve, vt, lo, hi: (vt[v], 0)),
                  pl.BlockSpec((None, None, d, f), lambda v, ve, vt, lo, hi: (layer, ve[v], 0, 0)),
                  pl.BlockSpec((None, None, d, f), lambda v, ve, vt, lo, hi: (layer, ve[v], 0, 0)),
                  pl.BlockSpec((None, None, f, d), lambda v, ve, vt, lo, hi: (layer, ve[v], 0, 0))],
        out_specs=pl.BlockSpec((tm, dp), lambda v, ve, vt, lo, hi: (vt[v], 0)),
        scratch_shapes=[pltpu.VMEM((d, 2 * f), BF16), pltpu.VMEM((f, d), BF16), pltpu.VMEM((tm, d), F32)],
    )
    return pl.pallas_call(
        _ffn_kernel,
        grid_spec=grid_spec,
        out_shape=jax.ShapeDtypeStruct((n_rows, dp), I32),
        compiler_params=_params("arbitrary"),
        name="ffn",
    )(*meta, xs, wg, wu, wd)


def _combine_kernel(*refs, n_prev):
    (yk_ref, x1_ref, wtm_ref, p_ref, wsg_ref, wsu_ref, wsd_ref,
     wpg_ref, wpp_ref, g_ref, b_ref, x2_ref, x2b_ref) = refs[n_prev:]
    tt = x1_ref.shape[0]
    xb = x1_ref[...].astype(BF16)
    hg = jnp.dot(xb, wsg_ref[...], preferred_element_type=F32)
    hu = jnp.dot(xb, wsu_ref[...], preferred_element_type=F32)
    shared = jnp.dot((hg * _sigmoid(hg) * hu).astype(BF16), wsd_ref[...], preferred_element_type=F32)
    pgate = _sigmoid(jnp.dot(xb, wpg_ref[...], preferred_element_type=F32))
    pproj = jnp.dot(p_ref[...].astype(BF16), wpp_ref[...], preferred_element_type=F32)
    rest = ALPHA * x1_ref[...] + shared + pgate * pproj

    wt = wtm_ref[...]
    r_lo = jnp.zeros((tt, yk_ref.shape[2]), F32)
    r_hi = jnp.zeros((tt, yk_ref.shape[2]), F32)
    for k in range(TOP_K):
        lo, hi = _unpack_pair(yk_ref[k])
        wk = wt[:, k:k + 1]
        r_lo = r_lo + wk * lo
        r_hi = r_hi + wk * hi
    routed = jnp.concatenate([r_lo, r_hi], axis=1)
    x2 = _layer_norm(rest + routed, g_ref[...], b_ref[...])
    x2_ref[...] = x2
    x2b_ref[...] = x2.astype(BF16)


def _combine(yk, x1, wtm, p, layer, part, prev, wsg, wsu, wsd, wpg, wpp, g, b):
    m_all, d = x1.shape
    m = wtm.shape[0]
    tt = min(COMBINE_TT, m)
    first = part * (m // tt)
    yk = yk.reshape(TOP_K, m, d // 2)

    def full(a):
        return pl.BlockSpec(a.shape, lambda i: (0,) * a.ndim)

    token_rows = pl.BlockSpec((tt, d), lambda i: (first + i, 0))
    untouched = pl.BlockSpec(memory_space=pl.ANY)
    return pl.pallas_call(
        functools.partial(_combine_kernel, n_prev=len(prev)),
        grid=(m // tt,),
        in_specs=[untouched] * len(prev) + [
            pl.BlockSpec((TOP_K, tt, d // 2), lambda i: (0, i, 0)),
            token_rows,
            pl.BlockSpec((tt, LANES), lambda i: (i, 0)),
            pl.BlockSpec((None, tt, p.shape[2]), lambda i: (layer, first + i, 0)),
            full(wsg), full(wsu), full(wsd), full(wpg), full(wpp), full(g), full(b)],
        out_specs=[token_rows, token_rows],
        out_shape=[jax.ShapeDtypeStruct((m_all, d), F32), jax.ShapeDtypeStruct((m_all, d), BF16)],
        input_output_aliases={j: j for j in range(len(prev))},
        compiler_params=_params("parallel"),
        name="combine",
    )(*prev, yk, x1, wtm, p, wsg, wsu, wsd, wpg, wpp, g, b)


def kernel(x, p, w_in, conv_w, b_igate, b_fgate, mlstm_norm_g, w_branch_a, w_branch_b, w_out, ln1_g, ln1_b, w_router, router_bias, w_exp_gate, w_exp_up, w_exp_down, w_sh_gate, w_sh_up, w_sh_down, w_ple_gate, w_ple_proj, ln2_g, ln2_b):
    bsz, seq, d = x.shape
    m = bsz * seq
    depth = w_in.shape[0]
    hw = HEADS * HEAD_DIM
    if_lo = 3 * d + 4 * hw
    if_hi = if_lo + 2 * HEADS

    xf = x.reshape(m, d)
    xb = xf.astype(BF16)
    w_in_t = jnp.swapaxes(w_in, 1, 2)
    p = p.reshape(depth, m, -1)
    for i in range(depth):
        w_merge = w_in_t[i, if_hi:]
        w_if = jnp.pad(w_in_t[i, if_lo:if_hi], ((0, LANES - 2 * HEADS), (0, 0)))
        gate_bias = jnp.pad(jnp.concatenate([b_igate[i], b_fgate[i]]), (0, LANES - 2 * HEADS)).reshape(1, LANES)

        z = _matmul(xb, w_in_t, BF16, INPROJ_TM, INPROJ_TN, "inproj", layer=i, n=if_lo).reshape(bsz, seq, -1)
        zm = _matmul(xb, w_merge, BF16, INPROJ_TM, INPROJ_TN, "mergeproj").reshape(bsz, seq, -1)
        gates = _matmul(xb, w_if, F32, INPROJ_TM, LANES, "gateproj").reshape(bsz, seq, LANES)
        x1, x1p = _mixer(z, zm, gates, gate_bias, mlstm_norm_g[i].reshape(1, hw),
                         xf.reshape(bsz, seq, d), conv_w[i],
                         w_branch_a[i].astype(BF16), w_branch_b[i].astype(BF16), w_out[i].astype(BF16),
                         ln1_g[i].reshape(1, d), ln1_b[i].reshape(1, d))
        x1 = x1.reshape(m, d)
        x1p = x1p.reshape(m, d // 2)

        ms = m // MOE_STREAMS
        shared_w = (w_sh_gate[i].astype(BF16), w_sh_up[i].astype(BF16), w_sh_down[i].astype(BF16),
                    w_ple_gate[i].astype(BF16), w_ple_proj[i].astype(BF16),
                    ln2_g[i].reshape(1, d), ln2_b[i].reshape(1, d))
        w_router_t = w_router[i].T.reshape(N_GROUPS, GROUP_SIZE, d).swapaxes(0, 1).reshape(N_EXPERTS, d)
        rb = router_bias[i].reshape(N_GROUPS, GROUP_SIZE).T.reshape(N_EXPERTS, 1)
        expert_ids = jnp.arange(N_EXPERTS, dtype=I32)[:, None, None]
        outs = ()
        for part in range(MOE_STREAMS):
            idx, rank, wtm, cnt = _router(x1, w_router_t, rb, part)
            counts = cnt[:, 0].astype(I32).reshape(GROUP_SIZE, N_GROUPS).T.reshape(N_EXPERTS)
            row_start = jnp.cumsum(counts) - counts
            dest = rank + jnp.sum(jnp.where(idx[None] == expert_ids, row_start[:, None, None], 0), axis=0)
            meta = _visit_metadata(counts, FFN_TM, ms * TOP_K)
            xs = _dispatch(x1p, dest, part)
            ys = _ffn(xs, meta, w_exp_gate, w_exp_up, w_exp_down, i)
            yk = _gather_rows(ys, dest)
            outs = tuple(_combine(yk, x1, wtm, p, i, part, outs, *shared_w))
        xf, xb = outs
    return xf.reshape(bsz, seq, d)
```

```python
import math

import jax
import jax.numpy as jnp
from jax import lax
from jax.experimental import pallas as pl
from jax.experimental.pallas import tpu as pltpu
from jax.experimental.pallas import tpu_sc as plsc

F32 = jnp.float32
BF16 = jnp.bfloat16
U32 = jnp.uint32
I32 = jnp.int32

HEADS = 8
HEAD_DIM = 128
N_EXPERTS = 64
N_GROUPS = 8
GROUP_SIZE = N_EXPERTS // N_GROUPS
TOPK_GROUPS = 4
TOP_K = 8
ROUTED_SCALE = 2.5
DEPTH = 4
ALPHA = (2 * DEPTH) ** 0.25
LN_EPS = 1e-5
RMS_EPS = 1e-6
QK_SCALE = HEAD_DIM ** -0.5
LOG_QK_SCALE = math.log(QK_SCALE)

LANES = 128
VMEM_LIMIT = 56 * 1024 * 1024
NEG_INF = float("-inf")

MLSTM_CHUNK = 256
MLSTM_STEP_CHUNKS = 2
INPROJ_TM, INPROJ_TN = 2048, 1024
ROUTER_TT = 512
FFN_TM = 2048
FFN_SUB = 512
COMBINE_TT = 512

SC_CORES = 2
SC_SUBCORES = 16
SC_WORKERS = SC_CORES * SC_SUBCORES
SC_ROWS = 128


def _params(*sem):
    return pltpu.CompilerParams(dimension_semantics=sem, vmem_limit_bytes=VMEM_LIMIT)


def _sigmoid(x):
    return 1.0 / (1.0 + jnp.exp(-x))


def _layer_norm(r, g, b):
    mu = jnp.mean(r, axis=-1, keepdims=True)
    d = r - mu
    var = jnp.mean(d * d, axis=-1, keepdims=True)
    return d * lax.rsqrt(var + LN_EPS) * g + b


def _pack_pair(lo, hi):
    return lax.bitcast_convert_type(pltpu.pack_elementwise([lo, hi], packed_dtype=BF16), I32)


def _unpack_pair(w):
    w = lax.bitcast_convert_type(w, U32)
    lo = pltpu.unpack_elementwise(w, index=0, packed_dtype=BF16, unpacked_dtype=F32)
    hi = pltpu.unpack_elementwise(w, index=1, packed_dtype=BF16, unpacked_dtype=F32)
    return lo, hi


def _unpack_rows(w):
    lo, hi = _unpack_pair(w)
    return jnp.concatenate([lo.astype(BF16), hi.astype(BF16)], axis=1)


def _mm_kernel(x_ref, wt_ref, o_ref):
    o_ref[...] = lax.dot_general(x_ref[...], wt_ref[...].astype(BF16), (((1,), (1,)), ((), ())),
                                 preferred_element_type=F32).astype(o_ref.dtype)


def _matmul(x, wt, out_dtype, tm, tn, name, layer=None, n=None):
    m, k = x.shape
    n = wt.shape[-2] if n is None else n
    tm = min(tm, m)
    if layer is None:
        w_spec = pl.BlockSpec((tn, k), lambda i, j: (j, 0))
    else:
        w_spec = pl.BlockSpec((None, tn, k), lambda i, j: (layer, j, 0))
    return pl.pallas_call(
        _mm_kernel,
        grid=(m // tm, n // tn),
        in_specs=[pl.BlockSpec((tm, k), lambda i, j: (i, 0)), w_spec],
        out_specs=pl.BlockSpec((tm, tn), lambda i, j: (i, j)),
        out_shape=jax.ShapeDtypeStruct((m, n), out_dtype),
        compiler_params=_params("parallel", "parallel"),
        name=name,
    )(x, wt)


def _mlstm_rows(q_ref, k_ref, v_ref, o_ref, gates, ng_ref, c_scr, m_scr):
    L = MLSTM_CHUNK
    chunks = [pl.ds(j * L, L) for j in range(q_ref.shape[1] // L)]
    row = lax.broadcasted_iota(I32, (L, L), 0)
    col = lax.broadcasted_iota(I32, (L, L), 1)
    causal = col <= row
    rows = lax.broadcasted_iota(I32, (L, LANES), 0)

    def scan_rows(x, op, identity):
        step = 1
        while step < L:
            x = op(x, jnp.where(rows >= step, pltpu.roll(x, step, axis=0), identity))
            step *= 2
        return x

    gate = []
    m_prev = m_scr[...]
    for j in range(len(chunks)):
        g = gates[j * L:(j + 1) * L, :]
        ig = g
        fg = pltpu.roll(g, LANES - HEADS, axis=1)
        log_f = jnp.minimum(fg, 0.0) - jnp.log(1.0 + jnp.exp(-jnp.abs(fg)))
        b = scan_rows(log_f, jnp.add, 0.0)
        b_last = b[L - 1:L, :]
        a = b_last - b + ig
        m_loc = jnp.max(a, axis=0, keepdims=True)
        m_new = jnp.maximum(b_last + m_prev, m_loc)
        log_inter = b + m_prev
        r = ig - b
        m_out = jnp.maximum(log_inter, b + scan_rows(r, jnp.maximum, NEG_INF))
        gate.append(dict(
            w_loc=jnp.exp(a - m_loc),
            sp=jnp.exp(b_last + m_prev - m_new),
            sl=jnp.exp(m_loc - m_new),
            r_t=jnp.transpose(r),
            u=b - m_out + LOG_QK_SCALE,
            e_inter=jnp.exp(log_inter - m_out + LOG_QK_SCALE),
            e_floor=jnp.exp(-m_out)))
        m_prev = m_new
    m_scr[...] = m_prev

    ones_blk = jnp.ones((L, HEAD_DIM), BF16)
    lanes = [slice(h * HEAD_DIM, (h + 1) * HEAD_DIM) for h in range(HEADS)]
    pairs = [(j, h) for j in range(len(chunks)) for h in range(HEADS)]

    def col_of(name, j, h):
        return gate[j][name][:, h:h + 1]

    q = {(j, h): q_ref[0, chunks[j], lanes[h]] for j, h in pairs}
    k = {(j, h): k_ref[0, chunks[j], lanes[h]] for j, h in pairs}
    v_aug = {(j, h): jnp.concatenate([v_ref[0, chunks[j], lanes[h]], ones_blk], axis=1) for j, h in pairs}
    s1 = {p: lax.dot_general(q[p], k[p], (((1,), (1,)), ((), ())), preferred_element_type=F32) for p in pairs}
    kw = {(j, h): (k[j, h].astype(F32) * col_of("w_loc", j, h)).astype(BF16) for j, h in pairs}
    c_loc = {p: lax.dot_general(kw[p], v_aug[p], (((0,), (0,)), ((), ())), preferred_element_type=F32)
             for p in pairs}
    c_seen = {}
    for h in range(HEADS):
        c = c_scr[h]
        for j in range(len(chunks)):
            c_seen[j, h] = c
            c = col_of("sp", j, h) * c + col_of("sl", j, h) * c_loc[j, h]
        c_scr[h] = c
    inter = {p: jnp.dot(q[p], c_seen[p].astype(BF16), preferred_element_type=F32) for p in pairs}
    s = {(j, h): (s1[j, h] * jnp.where(causal, jnp.exp(col_of("u", j, h) + gate[j]["r_t"][h:h + 1, :]), 0.0)
                  ).astype(BF16) for j, h in pairs}
    intra = {p: jnp.dot(s[p], v_aug[p], preferred_element_type=F32) for p in pairs}
    tot = {(j, h): intra[j, h] + col_of("e_inter", j, h) * inter[j, h] for j, h in pairs}
    hh = {(j, h): tot[j, h][:, :HEAD_DIM] / jnp.maximum(jnp.abs(tot[j, h][:, HEAD_DIM:]), col_of("e_floor", j, h))
          for j, h in pairs}
    hh = {p: hh[p] * lax.rsqrt(jnp.mean(hh[p] * hh[p], axis=1, keepdims=True) + RMS_EPS) for p in pairs}
    y_b = {(j, h): (hh[j, h] * ng_ref[:, lanes[h]] * _sigmoid(o_ref[0, chunks[j], lanes[h]].astype(F32))).astype(BF16)
           for j, h in pairs}
    return [jnp.concatenate([y_b[j, h] for h in range(HEADS)], axis=1) for j in range(len(chunks))]


def _mixer_kernel(q_ref, k_ref, v_ref, o_ref, gates_ref, gb_ref, ng_ref,
                  cin_ref, cout_ref, cval_ref, mg0_ref, mg1_ref, x_ref, cw_ref,
                  wa_ref, wb_ref, wo_ref, g_ref, b_ref, x1_ref, x1p_ref, c_scr, m_scr, carry):
    @pl.when(pl.program_id(1) == 0)
    def _():
        c_scr[...] = jnp.zeros_like(c_scr)
        m_scr[...] = jnp.zeros_like(m_scr)
        carry[...] = jnp.zeros_like(carry)

    tp = MLSTM_CHUNK
    parts = [pl.ds(j * tp, tp) for j in range(x_ref.shape[1] // tp)]
    u = [cin_ref[0, pr, :].astype(F32) * cval_ref[0, pr, :].astype(F32) for pr in parts]
    prev = [carry[...]] + [uj[tp - 8:, :] for uj in u[:-1]]
    carry[...] = u[-1][tp - 8:, :]
    r8 = lax.broadcasted_iota(I32, (8, u[0].shape[1]), 0)

    def shifted(uj, pj, k):
        body = pltpu.roll(uj, k, axis=0)
        head = jnp.where(r8 < k, pltpu.roll(pj, k, axis=0), body[:8, :])
        return jnp.concatenate([head, body[8:, :]], axis=0)

    cw = cw_ref[...]
    conv = [cw[0:1, :] * shifted(uj, pj, 2) + cw[1:2, :] * shifted(uj, pj, 1) + cw[2:3, :] * uj
            for uj, pj in zip(u, prev)]
    y_a = [(cout_ref[0, pr, :].astype(F32) * cj).astype(BF16) for pr, cj in zip(parts, conv)]
    pa = [jnp.dot(yj, wa_ref[...], preferred_element_type=F32) for yj in y_a]
    gated_a = [_sigmoid(mg0_ref[0, pr, :].astype(F32)) * paj for pr, paj in zip(parts, pa)]
    y_b = _mlstm_rows(q_ref, k_ref, v_ref, o_ref, gates_ref[0] + gb_ref[...], ng_ref, c_scr, m_scr)
    pb = [jnp.dot(yj, wb_ref[...], preferred_element_type=F32) for yj in y_b]
    mixed = [(gaj + _sigmoid(mg1_ref[0, pr, :].astype(F32)) * pbj).astype(BF16)
             for pr, gaj, pbj in zip(parts, gated_a, pb)]
    hmix = [jnp.dot(mj, wo_ref[...], preferred_element_type=F32) for mj in mixed]
    for pr, hj in zip(parts, hmix):
        x1 = _layer_norm(ALPHA * x_ref[0, pr, :] + hj, g_ref[...], b_ref[...])
        x1_ref[0, pr, :] = x1
        half = x1.shape[1] // 2
        x1p_ref[0, pr, :] = _pack_pair(x1[:, :half], x1[:, half:])


def _mixer(z, zm, gates, gate_bias, norm_g, x, conv_w, wa, wb, wo, g, b):
    bsz, seq, d = x.shape
    ts = min(MLSTM_STEP_CHUNKS * MLSTM_CHUNK, seq)

    def zspec(cb):
        return pl.BlockSpec((1, ts, d), lambda i, j, cb=cb: (i, j, cb))

    def full(shape):
        return pl.BlockSpec(shape, lambda i, j: (0,) * len(shape))

    tile = pl.BlockSpec((1, ts, d), lambda i, j: (i, j, 0))
    ptile = pl.BlockSpec((1, ts, d // 2), lambda i, j: (i, j, 0))
    return pl.pallas_call(
        _mixer_kernel,
        grid=(bsz, seq // ts),
        in_specs=[zspec(3), zspec(4), zspec(5), zspec(6),
                  pl.BlockSpec((1, ts, LANES), lambda i, j: (i, j, 0)),
                  full(gate_bias.shape), full(norm_g.shape),
                  zspec(0), zspec(1), zspec(2), zspec(0), zspec(1), tile,
                  full(conv_w.shape), full(wa.shape), full(wb.shape), full(wo.shape),
                  full(g.shape), full(b.shape)],
        out_specs=[tile, ptile],
        out_shape=[jax.ShapeDtypeStruct((bsz, seq, d), F32),
                   jax.ShapeDtypeStruct((bsz, seq, d // 2), I32)],
        scratch_shapes=[pltpu.VMEM((HEADS, HEAD_DIM, 2 * HEAD_DIM), F32),
                        pltpu.VMEM((1, LANES), F32),
                        pltpu.VMEM((8, d), F32)],
        compiler_params=_params("parallel", "arbitrary"),
        name="mixer",
    )(z, z, z, z, gates, gate_bias, norm_g, z, z, z, zm, zm, x, conv_w, wa, wb, wo, g, b)


def _router_kernel(x_ref, wrt_ref, rb_ref, idx_ref, rank_ref, wtm_ref, cnt_ref, carry):
    t = x_ref.shape[0]

    @pl.when(pl.program_id(0) == 0)
    def _():
        carry[...] = jnp.zeros_like(carry)

    def split(a):
        hi = a.astype(BF16)
        return hi, (a - hi.astype(F32)).astype(BF16)

    def dot_t(a, b):
        return lax.dot_general(a, b, (((1,), (1,)), ((), ())), preferred_element_type=F32)

    w_hi, w_lo = split(wrt_ref[...])
    x_hi, x_lo = split(x_ref[...])
    logits = dot_t(w_hi, x_hi) + (dot_t(w_hi, x_lo) + dot_t(w_lo, x_hi))
    scores = _sigmoid(logits)
    shape3 = (GROUP_SIZE, N_GROUPS, t)
    sel = (scores + rb_ref[...]).reshape(shape3)
    scores = scores.reshape(shape3)
    mem = lax.broadcasted_iota(I32, shape3, 0)
    grp = lax.broadcasted_iota(I32, shape3, 1)
    eidx = grp * GROUP_SIZE + mem
    m1 = jnp.max(sel, axis=0, keepdims=True)
    first = jnp.min(jnp.where(sel == m1, mem, GROUP_SIZE), axis=0, keepdims=True)
    m2 = jnp.max(jnp.where(mem == first, NEG_INF, sel), axis=0, keepdims=True)
    rem = m1 + m2
    gidx = lax.broadcasted_iota(I32, rem.shape, 1)
    gmask = jnp.zeros(rem.shape, F32)
    for _ in range(TOPK_GROUPS):
        mx = jnp.max(rem, axis=1, keepdims=True)
        pick = gidx == jnp.min(jnp.where(rem == mx, gidx, N_GROUPS), axis=1, keepdims=True)
        gmask = jnp.where(pick, 1.0, gmask)
        rem = jnp.where(pick, NEG_INF, rem)
    masked = jnp.where(jnp.broadcast_to(gmask, shape3) > 0.5, sel, NEG_INF)
    chosen = jnp.zeros(shape3, F32)
    picks = []
    for _ in range(TOP_K):
        mx = jnp.max(jnp.max(masked, axis=0, keepdims=True), axis=1, keepdims=True)
        cand = jnp.where(masked == mx, eidx, N_EXPERTS)
        fi = jnp.min(jnp.min(cand, axis=0, keepdims=True), axis=1, keepdims=True)
        pick = eidx == fi
        picks.append((fi, pick))
        chosen = jnp.where(pick, 1.0, chosen)
        masked = jnp.where(pick, NEG_INF, masked)
    w = chosen * scores
    denom = jnp.sum(jnp.sum(w, axis=0, keepdims=True), axis=1, keepdims=True)
    gate3 = w / denom * ROUTED_SCALE

    chosen2 = chosen.reshape(N_EXPERTS, t).astype(BF16)
    tok_r = lax.broadcasted_iota(I32, (t, t), 0)
    tok_c = lax.broadcasted_iota(I32, (t, t), 1)
    before = jnp.where(tok_r < tok_c, 1.0, 0.0).astype(BF16)
    prefix = (jnp.dot(chosen2, before, preferred_element_type=F32) + carry[...]).reshape(shape3)
    carry[...] += jnp.dot(chosen2, jnp.ones((t, t), BF16), preferred_element_type=F32)
    cnt_ref[...] = carry[:, :LANES]

    def pick_sum(pick, val):
        return jnp.sum(jnp.sum(jnp.where(pick, val, 0.0), axis=0, keepdims=True), axis=1, keepdims=True)

    sub8 = lax.broadcasted_iota(I32, (TOP_K, t), 0)
    idx8 = jnp.zeros((TOP_K, t), I32)
    rank8 = jnp.zeros((TOP_K, t), F32)
    w8 = jnp.zeros((TOP_K, t), F32)
    for k, (fi, pick) in enumerate(picks):
        idx8 = jnp.where(sub8 == k, jnp.broadcast_to(fi.reshape(1, t), (TOP_K, t)), idx8)
        rank8 = jnp.where(sub8 == k, jnp.broadcast_to(pick_sum(pick, prefix).reshape(1, t), (TOP_K, t)), rank8)
        w8 = jnp.where(sub8 == k, jnp.broadcast_to(pick_sum(pick, gate3).reshape(1, t), (TOP_K, t)), w8)
    idx_ref[...] = idx8
    rank_ref[...] = rank8.astype(I32)
    pad = jnp.zeros((LANES - TOP_K, t), F32)
    wtm_ref[...] = jnp.transpose(jnp.concatenate([w8, pad], axis=0))


def _router(x1, w_router_t, router_bias):
    m, d = x1.shape
    tt = min(ROUTER_TT, m)
    kt = pl.BlockSpec((TOP_K, tt), lambda i: (0, i))
    return pl.pallas_call(
        _router_kernel,
        grid=(m // tt,),
        in_specs=[pl.BlockSpec((tt, d), lambda i: (i, 0)),
                  pl.BlockSpec((N_EXPERTS, d), lambda i: (0, 0)),
                  pl.BlockSpec((N_EXPERTS, 1), lambda i: (0, 0))],
        out_specs=[kt, kt,
                   pl.BlockSpec((tt, LANES), lambda i: (i, 0)),
                   pl.BlockSpec((N_EXPERTS, LANES), lambda i: (0, 0))],
        out_shape=[jax.ShapeDtypeStruct((TOP_K, m), I32),
                   jax.ShapeDtypeStruct((TOP_K, m), I32),
                   jax.ShapeDtypeStruct((m, LANES), F32),
                   jax.ShapeDtypeStruct((N_EXPERTS, LANES), F32)],
        scratch_shapes=[pltpu.VMEM((N_EXPERTS, tt), F32)],
        compiler_params=_params("arbitrary"),
        name="router",
    )(x1, w_router_t, router_bias)


def _visit_metadata(counts, tm, n_rows):
    nt = n_rows // tm
    nv = nt + N_EXPERTS - 1
    ends = jnp.cumsum(counts)
    starts = ends - counts
    first_tile = starts // tm
    ntiles = jnp.where(counts > 0, (ends - 1) // tm - first_tile + 1, 0)
    vend = jnp.cumsum(ntiles)
    vstart = vend - ntiles
    v = jnp.arange(nv, dtype=I32)
    valid = v < vend[-1]
    ve = jnp.minimum(jnp.sum((v[:, None] >= vend[None, :]).astype(I32), axis=1), N_EXPERTS - 1)
    ve = jnp.where(valid, ve, ve[jnp.maximum(vend[-1] - 1, 0)])
    vt = jnp.where(valid, first_tile[ve] + v - vstart[ve], nt - 1)
    lo = jnp.where(valid, jnp.clip(starts[ve] - vt * tm, 0, tm), 0)
    hi = jnp.where(valid, jnp.clip(ends[ve] - vt * tm, 0, tm), 0)
    return ve.astype(I32), vt.astype(I32), lo.astype(I32), hi.astype(I32)


def _sc_mesh():
    return plsc.VectorSubcoreMesh(core_axis_name="c", subcore_axis_name="s")


def _sc_worker():
    return lax.axis_index("s") * SC_CORES + lax.axis_index("c")


def _dispatch(x1p, dest):
    m, dp = x1p.shape
    blocks = m // SC_ROWS
    per_worker = blocks // SC_WORKERS
    table = dest.reshape(TOP_K, blocks, SC_ROWS).transpose(1, 0, 2).reshape(blocks * TOP_K, SC_ROWS)

    def body(x_hbm, idx_hbm, xs_hbm, idx_v, rows_v, sem):
        first = _sc_worker() * per_worker

        @pl.loop(0, per_worker)
        def _(j):
            b = first + j
            pltpu.sync_copy(idx_hbm.at[pl.ds(pl.multiple_of(b * TOP_K, TOP_K), TOP_K)], idx_v)
            pltpu.sync_copy(x_hbm.at[pl.ds(pl.multiple_of(b * SC_ROWS, SC_ROWS), SC_ROWS)], rows_v)
            copies = [pltpu.async_copy(rows_v, xs_hbm.at[idx_v.at[k]], sem) for k in range(TOP_K)]
            for cp in copies:
                cp.wait()

    return pl.kernel(
        body,
        out_type=jax.ShapeDtypeStruct((m * TOP_K, dp), x1p.dtype),
        mesh=_sc_mesh(),
        scratch_types=[pltpu.VMEM((TOP_K, SC_ROWS), I32), pltpu.VMEM((SC_ROWS, dp), x1p.dtype),
                       pltpu.SemaphoreType.DMA],
        name="dispatch",
    )(x1p, table)


def _gather_rows(ys, dest):
    n_rows, dp = ys.shape
    idx = dest.reshape(n_rows // SC_ROWS, SC_ROWS)
    per_worker = n_rows // SC_ROWS // SC_WORKERS

    def body(ys_hbm, idx_hbm, out_hbm, idx_v, rows_v, sem):
        first = _sc_worker() * per_worker

        @pl.loop(0, per_worker)
        def _(j):
            b = first + j
            pltpu.sync_copy(idx_hbm.at[pl.ds(b, 1)], idx_v)
            pltpu.async_copy(ys_hbm.at[idx_v.at[0]], rows_v, sem).wait()
            pltpu.sync_copy(rows_v, out_hbm.at[pl.ds(pl.multiple_of(b * SC_ROWS, SC_ROWS), SC_ROWS)])

    return pl.kernel(
        body,
        out_type=jax.ShapeDtypeStruct((n_rows, dp), ys.dtype),
        mesh=_sc_mesh(),
        scratch_types=[pltpu.VMEM((1, SC_ROWS), I32), pltpu.VMEM((SC_ROWS, dp), ys.dtype),
                       pltpu.SemaphoreType.DMA],
        name="gather_rows",
    )(ys, idx)


def _ffn_kernel(ve_ref, vt_ref, lo_ref, hi_ref, xs_ref, wg_ref, wu_ref, wd_ref, ys_ref,
                wgu_b, wd_b, acc):
    v = pl.program_id(0)
    lo = lo_ref[v]
    hi = hi_ref[v]
    tm = xs_ref.shape[0]
    f = wg_ref.shape[1]

    @pl.when((v == 0) | (ve_ref[v] != ve_ref[jnp.maximum(v - 1, 0)]))
    def _():
        wgu_b[:, :f] = wg_ref[...].astype(BF16)
        wgu_b[:, f:] = wu_ref[...].astype(BF16)
        wd_b[...] = wd_ref[...].astype(BF16)

    def pack_rows(a):
        half = a.shape[1] // 2
        return _pack_pair(a[:, :half], a[:, half:])

    for sb in range(tm // FFN_SUB):
        s0 = sb * FFN_SUB
        rows = pl.ds(s0, FFN_SUB)
        lo_s = jnp.clip(lo - s0, 0, FFN_SUB)
        hi_s = jnp.clip(hi - s0, 0, FFN_SUB)

        whole = (lo_s == 0) & (hi_s == FFN_SUB)

        @pl.when(whole)
        def _():
            halves = [pl.ds(s0 + j * (FFN_SUB // 2), FFN_SUB // 2) for j in range(2)]
            x = [_unpack_rows(xs_ref[hr, :]) for hr in halves]
            h2 = [jnp.dot(xj, wgu_b[...], preferred_element_type=F32) for xj in x]
            act = [(hj[:, :f] * _sigmoid(hj[:, :f]) * hj[:, f:]).astype(BF16) for hj in h2]
            y = [jnp.dot(aj, wd_b[...], preferred_element_type=F32) for aj in act]
            for hr, yj in zip(halves, y):
                ys_ref[hr, :] = pack_rows(yj)

        @pl.when((hi_s > lo_s) & jnp.logical_not(whole))
        def _():
            x = _unpack_rows(xs_ref[rows, :])
            h2 = jnp.dot(x, wgu_b[...], preferred_element_type=F32)
            hg = h2[:, :f]
            hu = h2[:, f:]
            r = lax.broadcasted_iota(I32, hg.shape, 0)
            mine = (r >= lo_s) & (r < hi_s)
            hmid = jnp.where(mine, hg * _sigmoid(hg) * hu, 0.0).astype(BF16)
            y = jnp.dot(hmid, wd_b[...], preferred_element_type=F32)

            @pl.when(lo_s == 0)
            def _():
                acc[rows, :] = y

            @pl.when(lo_s > 0)
            def _():
                acc[rows, :] += y

            @pl.when((lo_s > 0) & (hi_s == FFN_SUB))
            def _():
                ys_ref[rows, :] = pack_rows(acc[rows, :])


def _ffn(xs, meta, wg, wu, wd, layer):
    n_rows, dp = xs.shape
    _, n_e, d, f = wg.shape
    tm = FFN_TM
    nv = n_rows // tm + N_EXPERTS - 1
    grid_spec = pltpu.PrefetchScalarGridSpec(
        num_scalar_prefetch=4,
        grid=(nv,),
        in_specs=[pl.BlockSpec((tm, dp), lambda v, ve, vt, lo, hi: (vt[v], 0)),
                  pl.BlockSpec((None, None, d, f), lambda v, ve, vt, lo, hi: (layer, ve[v], 0, 0)),
                  pl.BlockSpec((None, None, d, f), lambda v, ve, vt, lo, hi: (layer, ve[v], 0, 0)),
                  pl.BlockSpec((None, None, f, d), lambda v, ve, vt, lo, hi: (layer, ve[v], 0, 0))],
        out_specs=pl.BlockSpec((tm, dp), lambda v, ve, vt, lo, hi: (vt[v], 0)),
        scratch_shapes=[pltpu.VMEM((d, 2 * f), BF16), pltpu.VMEM((f, d), BF16), pltpu.VMEM((tm, d), F32)],
    )
    return pl.pallas_call(
        _ffn_kernel,
        grid_spec=grid_spec,
        out_shape=jax.ShapeDtypeStruct((n_rows, dp), I32),
        compiler_params=_params("arbitrary"),
        name="ffn",
    )(*meta, xs, wg, wu, wd)


def _combine_kernel(yk_ref, x1_ref, wtm_ref, p_ref, wsg_ref, wsu_ref, wsd_ref,
                    wpg_ref, wpp_ref, g_ref, b_ref, x2_ref, x2b_ref):
    tt = x1_ref.shape[0]
    xb = x1_ref[...].astype(BF16)
    hg = jnp.dot(xb, wsg_ref[...], preferred_element_type=F32)
    hu = jnp.dot(xb, wsu_ref[...], preferred_element_type=F32)
    shared = jnp.dot((hg * _sigmoid(hg) * hu).astype(BF16), wsd_ref[...], preferred_element_type=F32)
    pgate = _sigmoid(jnp.dot(xb, wpg_ref[...], preferred_element_type=F32))
    pproj = jnp.dot(p_ref[...].astype(BF16), wpp_ref[...], preferred_element_type=F32)
    rest = ALPHA * x1_ref[...] + shared + pgate * pproj

    wt = wtm_ref[...]
    r_lo = jnp.zeros((tt, yk_ref.shape[2]), F32)
    r_hi = jnp.zeros((tt, yk_ref.shape[2]), F32)
    for k in range(TOP_K):
        lo, hi = _unpack_pair(yk_ref[k])
        wk = wt[:, k:k + 1]
        r_lo = r_lo + wk * lo
        r_hi = r_hi + wk * hi
    routed = jnp.concatenate([r_lo, r_hi], axis=1)
    x2 = _layer_norm(rest + routed, g_ref[...], b_ref[...])
    x2_ref[...] = x2
    x2b_ref[...] = x2.astype(BF16)


def _combine(yk, x1, wtm, p, layer, wsg, wsu, wsd, wpg, wpp, g, b):
    m, d = x1.shape
    tt = min(COMBINE_TT, m)
    yk = yk.reshape(TOP_K, m, d // 2)

    def full(a):
        return pl.BlockSpec(a.shape, lambda i: (0,) * a.ndim)

    token_rows = pl.BlockSpec((tt, d), lambda i: (i, 0))
    return pl.pallas_call(
        _combine_kernel,
        grid=(m // tt,),
        in_specs=[pl.BlockSpec((TOP_K, tt, d // 2), lambda i: (0, i, 0)),
                  token_rows,
                  pl.BlockSpec((tt, LANES), lambda i: (i, 0)),
                  pl.BlockSpec((None, tt, p.shape[2]), lambda i: (layer, i, 0)),
                  full(wsg), full(wsu), full(wsd), full(wpg), full(wpp), full(g), full(b)],
        out_specs=[token_rows, token_rows],
        out_shape=[jax.ShapeDtypeStruct((m, d), F32), jax.ShapeDtypeStruct((m, d), BF16)],
        compiler_params=_params("parallel"),
        name="combine",
    )(yk, x1, wtm, p, wsg, wsu, wsd, wpg, wpp, g, b)


def kernel(x, p, w_in, conv_w, b_igate, b_fgate, mlstm_norm_g, w_branch_a, w_branch_b, w_out, ln1_g, ln1_b, w_router, router_bias, w_exp_gate, w_exp_up, w_exp_down, w_sh_gate, w_sh_up, w_sh_down, w_ple_gate, w_ple_proj, ln2_g, ln2_b):
    bsz, seq, d = x.shape
    m = bsz * seq
    depth = w_in.shape[0]
    hw = HEADS * HEAD_DIM
    if_lo = 3 * d + 4 * hw
    if_hi = if_lo + 2 * HEADS

    xf = x.reshape(m, d)
    xb = xf.astype(BF16)
    w_in_t = jnp.swapaxes(w_in, 1, 2)
    p = p.reshape(depth, m, -1)
    for i in range(depth):
        w_merge = w_in_t[i, if_hi:]
        w_if = jnp.pad(w_in_t[i, if_lo:if_hi], ((0, LANES - 2 * HEADS), (0, 0)))
        gate_bias = jnp.pad(jnp.concatenate([b_igate[i], b_fgate[i]]), (0, LANES - 2 * HEADS)).reshape(1, LANES)

        z = _matmul(xb, w_in_t, BF16, INPROJ_TM, INPROJ_TN, "inproj", layer=i, n=if_lo).reshape(bsz, seq, -1)
        zm = _matmul(xb, w_merge, BF16, INPROJ_TM, INPROJ_TN, "mergeproj").reshape(bsz, seq, -1)
        gates = _matmul(xb, w_if, F32, INPROJ_TM, LANES, "gateproj").reshape(bsz, seq, LANES)
        x1, x1p = _mixer(z, zm, gates, gate_bias, mlstm_norm_g[i].reshape(1, hw),
                         xf.reshape(bsz, seq, d), conv_w[i],
                         w_branch_a[i].astype(BF16), w_branch_b[i].astype(BF16), w_out[i].astype(BF16),
                         ln1_g[i].reshape(1, d), ln1_b[i].reshape(1, d))
        x1 = x1.reshape(m, d)
        x1p = x1p.reshape(m, d // 2)

        w_router_t = w_router[i].T.reshape(N_GROUPS, GROUP_SIZE, d).swapaxes(0, 1).reshape(N_EXPERTS, d)
        rb = router_bias[i].reshape(N_GROUPS, GROUP_SIZE).T.reshape(N_EXPERTS, 1)
        idx, rank, wtm, cnt = _router(x1, w_router_t, rb)
        counts = cnt[:, 0].astype(I32).reshape(GROUP_SIZE, N_GROUPS).T.reshape(N_EXPERTS)
        row_start = jnp.cumsum(counts) - counts
        expert_ids = jnp.arange(N_EXPERTS, dtype=I32)[:, None, None]
        dest = rank + jnp.sum(jnp.where(idx[None] == expert_ids, row_start[:, None, None], 0), axis=0)
        meta = _visit_metadata(counts, FFN_TM, m * TOP_K)
        xs = _dispatch(x1p, dest)
        ys = _ffn(xs, meta, w_exp_gate, w_exp_up, w_exp_down, i)
        yk = _gather_rows(ys, dest)
        xf, xb = _combine(yk, x1, wtm, p, i,
                          w_sh_gate[i].astype(BF16), w_sh_up[i].astype(BF16), w_sh_down[i].astype(BF16),
                          w_ple_gate[i].astype(BF16), w_ple_proj[i].astype(BF16),
                          ln2_g[i].reshape(1, d), ln2_b[i].reshape(1, d))
    return xf.reshape(bsz, seq, d)
```

```python
import math

import jax
import jax.numpy as jnp
from jax import lax
from jax.experimental import pallas as pl
from jax.experimental.pallas import tpu as pltpu
from jax.experimental.pallas import tpu_sc as plsc

F32 = jnp.float32
BF16 = jnp.bfloat16
U32 = jnp.uint32
I32 = jnp.int32

HEADS = 8
HEAD_DIM = 128
N_EXPERTS = 64
N_GROUPS = 8
GROUP_SIZE = N_EXPERTS // N_GROUPS
TOPK_GROUPS = 4
TOP_K = 8
ROUTED_SCALE = 2.5
DEPTH = 4
ALPHA = (2 * DEPTH) ** 0.25
LN_EPS = 1e-5
RMS_EPS = 1e-6
QK_SCALE = HEAD_DIM ** -0.5
LOG_QK_SCALE = math.log(QK_SCALE)

LANES = 128
VMEM_LIMIT = 56 * 1024 * 1024
NEG_INF = float("-inf")

MLSTM_CHUNK = 256
MLSTM_STEP_CHUNKS = 2
INPROJ_TM, INPROJ_TN = 2048, 1024
ROUTER_TT = 512
FFN_TM = 2048
FFN_SUB = 512
COMBINE_TT = 512

SC_CORES = 2
SC_SUBCORES = 16
SC_WORKERS = SC_CORES * SC_SUBCORES
SC_ROWS = 128


def _params(*sem):
    return pltpu.CompilerParams(dimension_semantics=sem, vmem_limit_bytes=VMEM_LIMIT)


def _sigmoid(x):
    return 1.0 / (1.0 + jnp.exp(-x))


def _layer_norm(r, g, b):
    mu = jnp.mean(r, axis=-1, keepdims=True)
    d = r - mu
    var = jnp.mean(d * d, axis=-1, keepdims=True)
    return d * lax.rsqrt(var + LN_EPS) * g + b


def _pack_pair(lo, hi):
    return lax.bitcast_convert_type(pltpu.pack_elementwise([lo, hi], packed_dtype=BF16), I32)


def _unpack_pair(w):
    w = lax.bitcast_convert_type(w, U32)
    lo = pltpu.unpack_elementwise(w, index=0, packed_dtype=BF16, unpacked_dtype=F32)
    hi = pltpu.unpack_elementwise(w, index=1, packed_dtype=BF16, unpacked_dtype=F32)
    return lo, hi


def _unpack_rows(w):
    lo, hi = _unpack_pair(w)
    return jnp.concatenate([lo.astype(BF16), hi.astype(BF16)], axis=1)


def _mm_kernel(x_ref, wt_ref, o_ref):
    o_ref[...] = lax.dot_general(x_ref[...], wt_ref[...].astype(BF16), (((1,), (1,)), ((), ())),
                                 preferred_element_type=F32).astype(o_ref.dtype)


def _matmul(x, wt, out_dtype, tm, tn, name, layer=None, n=None):
    m, k = x.shape
    n = wt.shape[-2] if n is None else n
    tm = min(tm, m)
    if layer is None:
        w_spec = pl.BlockSpec((tn, k), lambda i, j: (j, 0))
    else:
        w_spec = pl.BlockSpec((None, tn, k), lambda i, j: (layer, j, 0))
    return pl.pallas_call(
        _mm_kernel,
        grid=(m // tm, n // tn),
        in_specs=[pl.BlockSpec((tm, k), lambda i, j: (i, 0)), w_spec],
        out_specs=pl.BlockSpec((tm, tn), lambda i, j: (i, j)),
        out_shape=jax.ShapeDtypeStruct((m, n), out_dtype),
        compiler_params=_params("parallel", "parallel"),
        name=name,
    )(x, wt)


def _mlstm_rows(q_ref, k_ref, v_ref, o_ref, gates, ng_ref, c_scr, m_scr):
    L = MLSTM_CHUNK
    chunks = [pl.ds(j * L, L) for j in range(q_ref.shape[1] // L)]
    row = lax.broadcasted_iota(I32, (L, L), 0)
    col = lax.broadcasted_iota(I32, (L, L), 1)
    causal = col <= row
    rows = lax.broadcasted_iota(I32, (L, LANES), 0)

    def scan_rows(x, op, identity):
        step = 1
        while step < L:
            x = op(x, jnp.where(rows >= step, pltpu.roll(x, step, axis=0), identity))
            step *= 2
        return x

    gate = []
    m_prev = m_scr[...]
    for j in range(len(chunks)):
        g = gates[j * L:(j + 1) * L, :]
        ig = g
        fg = pltpu.roll(g, LANES - HEADS, axis=1)
        log_f = jnp.minimum(fg, 0.0) - jnp.log(1.0 + jnp.exp(-jnp.abs(fg)))
        b = scan_rows(log_f, jnp.add, 0.0)
        b_last = b[L - 1:L, :]
        a = b_last - b + ig
        m_loc = jnp.max(a, axis=0, keepdims=True)
        m_new = jnp.maximum(b_last + m_prev, m_loc)
        log_inter = b + m_prev
        r = ig - b
        m_out = jnp.maximum(log_inter, b + scan_rows(r, jnp.maximum, NEG_INF))
        gate.append(dict(
            w_loc=jnp.exp(a - m_loc),
            sp=jnp.exp(b_last + m_prev - m_new),
            sl=jnp.exp(m_loc - m_new),
            r_t=jnp.transpose(r),
            u=b - m_out + LOG_QK_SCALE,
            e_inter=jnp.exp(log_inter - m_out + LOG_QK_SCALE),
            e_floor=jnp.exp(-m_out)))
        m_prev = m_new
    m_scr[...] = m_prev

    ones_blk = jnp.ones((L, HEAD_DIM), BF16)
    lanes = [slice(h * HEAD_DIM, (h + 1) * HEAD_DIM) for h in range(HEADS)]
    pairs = [(j, h) for j in range(len(chunks)) for h in range(HEADS)]

    def col_of(name, j, h):
        return gate[j][name][:, h:h + 1]

    q = {(j, h): q_ref[0, chunks[j], lanes[h]] for j, h in pairs}
    k = {(j, h): k_ref[0, chunks[j], lanes[h]] for j, h in pairs}
    v_aug = {(j, h): jnp.concatenate([v_ref[0, chunks[j], lanes[h]], ones_blk], axis=1) for j, h in pairs}
    s1 = {p: lax.dot_general(q[p], k[p], (((1,), (1,)), ((), ())), preferred_element_type=F32) for p in pairs}
    kw = {(j, h): (k[j, h].astype(F32) * col_of("w_loc", j, h)).astype(BF16) for j, h in pairs}
    c_loc = {p: lax.dot_general(kw[p], v_aug[p], (((0,), (0,)), ((), ())), preferred_element_type=F32)
             for p in pairs}
    c_seen = {}
    for h in range(HEADS):
        c = c_scr[h]
        for j in range(len(chunks)):
            c_seen[j, h] = c
            c = col_of("sp", j, h) * c + col_of("sl", j, h) * c_loc[j, h]
        c_scr[h] = c
    inter = {p: jnp.dot(q[p], c_seen[p].astype(BF16), preferred_element_type=F32) for p in pairs}
    s = {(j, h): (s1[j, h] * jnp.where(causal, jnp.exp(col_of("u", j, h) + gate[j]["r_t"][h:h + 1, :]), 0.0)
                  ).astype(BF16) for j, h in pairs}
    intra = {p: jnp.dot(s[p], v_aug[p], preferred_element_type=F32) for p in pairs}
    tot = {(j, h): intra[j, h] + col_of("e_inter", j, h) * inter[j, h] for j, h in pairs}
    hh = {(j, h): tot[j, h][:, :HEAD_DIM] / jnp.maximum(jnp.abs(tot[j, h][:, HEAD_DIM:]), col_of("e_floor", j, h))
          for j, h in pairs}
    hh = {p: hh[p] * lax.rsqrt(jnp.mean(hh[p] * hh[p], axis=1, keepdims=True) + RMS_EPS) for p in pairs}
    y_b = {(j, h): (hh[j, h] * ng_ref[:, lanes[h]] * _sigmoid(o_ref[0, chunks[j], lanes[h]].astype(F32))).astype(BF16)
           for j, h in pairs}
    return [jnp.concatenate([y_b[j, h] for h in range(HEADS)], axis=1) for j in range(len(chunks))]


def _mixer_kernel(q_ref, k_ref, v_ref, o_ref, gates_ref, gb_ref, ng_ref,
                  cin_ref, cout_ref, cval_ref, mg0_ref, mg1_ref, x_ref, cw_ref,
                  wa_ref, wb_ref, wo_ref, g_ref, b_ref, x1_ref, x1p_ref, c_scr, m_scr, carry):
    @pl.when(pl.program_id(1) == 0)
    def _():
        c_scr[...] = jnp.zeros_like(c_scr)
        m_scr[...] = jnp.zeros_like(m_scr)
        carry[...] = jnp.zeros_like(carry)

    tp = MLSTM_CHUNK
    parts = [pl.ds(j * tp, tp) for j in range(x_ref.shape[1] // tp)]
    u = [cin_ref[0, pr, :].astype(F32) * cval_ref[0, pr, :].astype(F32) for pr in parts]
    prev = [carry[...]] + [uj[tp - 8:, :] for uj in u[:-1]]
    carry[...] = u[-1][tp - 8:, :]
    r8 = lax.broadcasted_iota(I32, (8, u[0].shape[1]), 0)

    def shifted(uj, pj, k):
        body = pltpu.roll(uj, k, axis=0)
        head = jnp.where(r8 < k, pltpu.roll(pj, k, axis=0), body[:8, :])
        return jnp.concatenate([head, body[8:, :]], axis=0)

    cw = cw_ref[...]
    conv = [cw[0:1, :] * shifted(uj, pj, 2) + cw[1:2, :] * shifted(uj, pj, 1) + cw[2:3, :] * uj
            for uj, pj in zip(u, prev)]
    y_a = [(cout_ref[0, pr, :].astype(F32) * cj).astype(BF16) for pr, cj in zip(parts, conv)]
    pa = [jnp.dot(yj, wa_ref[...], preferred_element_type=F32) for yj in y_a]
    gated_a = [_sigmoid(mg0_ref[0, pr, :].astype(F32)) * paj for pr, paj in zip(parts, pa)]
    y_b = _mlstm_rows(q_ref, k_ref, v_ref, o_ref, gates_ref[0] + gb_ref[...], ng_ref, c_scr, m_scr)
    pb = [jnp.dot(yj, wb_ref[...], preferred_element_type=F32) for yj in y_b]
    mixed = [(gaj + _sigmoid(mg1_ref[0, pr, :].astype(F32)) * pbj).astype(BF16)
             for pr, gaj, pbj in zip(parts, gated_a, pb)]
    hmix = [jnp.dot(mj, wo_ref[...], preferred_element_type=F32) for mj in mixed]
    for pr, hj in zip(parts, hmix):
        x1 = _layer_norm(ALPHA * x_ref[0, pr, :] + hj, g_ref[...], b_ref[...])
        x1_ref[0, pr, :] = x1
        half = x1.shape[1] // 2
        x1p_ref[0, pr, :] = _pack_pair(x1[:, :half], x1[:, half:])


def _mixer(z, zm, gates, gate_bias, norm_g, x, conv_w, wa, wb, wo, g, b):
    bsz, seq, d = x.shape
    ts = min(MLSTM_STEP_CHUNKS * MLSTM_CHUNK, seq)

    def zspec(cb):
        return pl.BlockSpec((1, ts, d), lambda i, j, cb=cb: (i, j, cb))

    def full(shape):
        return pl.BlockSpec(shape, lambda i, j: (0,) * len(shape))

    tile = pl.BlockSpec((1, ts, d), lambda i, j: (i, j, 0))
    ptile = pl.BlockSpec((1, ts, d // 2), lambda i, j: (i, j, 0))
    return pl.pallas_call(
        _mixer_kernel,
        grid=(bsz, seq // ts),
        in_specs=[zspec(3), zspec(4), zspec(5), zspec(6),
                  pl.BlockSpec((1, ts, LANES), lambda i, j: (i, j, 0)),
                  full(gate_bias.shape), full(norm_g.shape),
                  zspec(0), zspec(1), zspec(2), zspec(0), zspec(1), tile,
                  full(conv_w.shape), full(wa.shape), full(wb.shape), full(wo.shape),
                  full(g.shape), full(b.shape)],
        out_specs=[tile, ptile],
        out_shape=[jax.ShapeDtypeStruct((bsz, seq, d), F32),
                   jax.ShapeDtypeStruct((bsz, seq, d // 2), I32)],
        scratch_shapes=[pltpu.VMEM((HEADS, HEAD_DIM, 2 * HEAD_DIM), F32),
                        pltpu.VMEM((1, LANES), F32),
                        pltpu.VMEM((8, d), F32)],
        compiler_params=_params("parallel", "arbitrary"),
        name="mixer",
    )(z, z, z, z, gates, gate_bias, norm_g, z, z, z, zm, zm, x, conv_w, wa, wb, wo, g, b)


def _router_kernel(x_ref, wrt_ref, rb_ref, idx_ref, rank_ref, wtm_ref, cnt_ref, carry):
    t = x_ref.shape[0]

    @pl.when(pl.program_id(0) == 0)
    def _():
        carry[...] = jnp.zeros_like(carry)

    def split(a):
        hi = a.astype(BF16)
        return hi, (a - hi.astype(F32)).astype(BF16)

    def dot_t(a, b):
        return lax.dot_general(a, b, (((1,), (1,)), ((), ())), preferred_element_type=F32)

    w_hi, w_lo = split(wrt_ref[...])
    x_hi, x_lo = split(x_ref[...])
    logits = dot_t(w_hi, x_hi) + (dot_t(w_hi, x_lo) + dot_t(w_lo, x_hi))
    scores = _sigmoid(logits)
    shape3 = (GROUP_SIZE, N_GROUPS, t)
    sel = (scores + rb_ref[...]).reshape(shape3)
    scores = scores.reshape(shape3)
    mem = lax.broadcasted_iota(I32, shape3, 0)
    grp = lax.broadcasted_iota(I32, shape3, 1)
    eidx = grp * GROUP_SIZE + mem
    m1 = jnp.max(sel, axis=0, keepdims=True)
    first = jnp.min(jnp.where(sel == m1, mem, GROUP_SIZE), axis=0, keepdims=True)
    m2 = jnp.max(jnp.where(mem == first, NEG_INF, sel), axis=0, keepdims=True)
    rem = m1 + m2
    gidx = lax.broadcasted_iota(I32, rem.shape, 1)
    gmask = jnp.zeros(rem.shape, F32)
    for _ in range(TOPK_GROUPS):
        mx = jnp.max(rem, axis=1, keepdims=True)
        pick = gidx == jnp.min(jnp.where(rem == mx, gidx, N_GROUPS), axis=1, keepdims=True)
        gmask = jnp.where(pick, 1.0, gmask)
        rem = jnp.where(pick, NEG_INF, rem)
    masked = jnp.where(jnp.broadcast_to(gmask, shape3) > 0.5, sel, NEG_INF)
    chosen = jnp.zeros(shape3, F32)
    picks = []
    for _ in range(TOP_K):
        mx = jnp.max(jnp.max(masked, axis=0, keepdims=True), axis=1, keepdims=True)
        cand = jnp.where(masked == mx, eidx, N_EXPERTS)
        fi = jnp.min(jnp.min(cand, axis=0, keepdims=True), axis=1, keepdims=True)
        pick = eidx == fi
        picks.append((fi, pick))
        chosen = jnp.where(pick, 1.0, chosen)
        masked = jnp.where(pick, NEG_INF, masked)
    w = chosen * scores
    denom = jnp.sum(jnp.sum(w, axis=0, keepdims=True), axis=1, keepdims=True)
    gate3 = w / denom * ROUTED_SCALE

    chosen2 = chosen.reshape(N_EXPERTS, t).astype(BF16)
    tok_r = lax.broadcasted_iota(I32, (t, t), 0)
    tok_c = lax.broadcasted_iota(I32, (t, t), 1)
    before = jnp.where(tok_r < tok_c, 1.0, 0.0).astype(BF16)
    prefix = (jnp.dot(chosen2, before, preferred_element_type=F32) + carry[...]).reshape(shape3)
    carry[...] += jnp.dot(chosen2, jnp.ones((t, t), BF16), preferred_element_type=F32)
    cnt_ref[...] = carry[:, :LANES]

    def pick_sum(pick, val):
        return jnp.sum(jnp.sum(jnp.where(pick, val, 0.0), axis=0, keepdims=True), axis=1, keepdims=True)

    sub8 = lax.broadcasted_iota(I32, (TOP_K, t), 0)
    idx8 = jnp.zeros((TOP_K, t), I32)
    rank8 = jnp.zeros((TOP_K, t), F32)
    w8 = jnp.zeros((TOP_K, t), F32)
    for k, (fi, pick) in enumerate(picks):
        idx8 = jnp.where(sub8 == k, jnp.broadcast_to(fi.reshape(1, t), (TOP_K, t)), idx8)
        rank8 = jnp.where(sub8 == k, jnp.broadcast_to(pick_sum(pick, prefix).reshape(1, t), (TOP_K, t)), rank8)
        w8 = jnp.where(sub8 == k, jnp.broadcast_to(pick_sum(pick, gate3).reshape(1, t), (TOP_K, t)), w8)
    idx_ref[...] = idx8
    rank_ref[...] = rank8.astype(I32)
    pad = jnp.zeros((LANES - TOP_K, t), F32)
    wtm_ref[...] = jnp.transpose(jnp.concatenate([w8, pad], axis=0))


def _router(x1, w_router_t, router_bias):
    m, d = x1.shape
    tt = min(ROUTER_TT, m)
    kt = pl.BlockSpec((TOP_K, tt), lambda i: (0, i))
    return pl.pallas_call(
        _router_kernel,
        grid=(m // tt,),
        in_specs=[pl.BlockSpec((tt, d), lambda i: (i, 0)),
                  pl.BlockSpec((N_EXPERTS, d), lambda i: (0, 0)),
                  pl.BlockSpec((N_EXPERTS, 1), lambda i: (0, 0))],
        out_specs=[kt, kt,
                   pl.BlockSpec((tt, LANES), lambda i: (i, 0)),
                   pl.BlockSpec((N_EXPERTS, LANES), lambda i: (0, 0))],
        out_shape=[jax.ShapeDtypeStruct((TOP_K, m), I32),
                   jax.ShapeDtypeStruct((TOP_K, m), I32),
                   jax.ShapeDtypeStruct((m, LANES), F32),
                   jax.ShapeDtypeStruct((N_EXPERTS, LANES), F32)],
        scratch_shapes=[pltpu.VMEM((N_EXPERTS, tt), F32)],
        compiler_params=_params("arbitrary"),
        name="router",
    )(x1, w_router_t, router_bias)


def _visit_metadata(counts, tm, n_rows):
    nt = n_rows // tm
    nv = nt + N_EXPERTS - 1
    ends = jnp.cumsum(counts)
    starts = ends - counts
    first_tile = starts // tm
    ntiles = jnp.where(counts > 0, (ends - 1) // tm - first_tile + 1, 0)
    vend = jnp.cumsum(ntiles)
    vstart = vend - ntiles
    v = jnp.arange(nv, dtype=I32)
    valid = v < vend[-1]
    ve = jnp.minimum(jnp.sum((v[:, None] >= vend[None, :]).astype(I32), axis=1), N_EXPERTS - 1)
    ve = jnp.where(valid, ve, ve[jnp.maximum(vend[-1] - 1, 0)])
    vt = jnp.where(valid, first_tile[ve] + v - vstart[ve], nt - 1)
    lo = jnp.where(valid, jnp.clip(starts[ve] - vt * tm, 0, tm), 0)
    hi = jnp.where(valid, jnp.clip(ends[ve] - vt * tm, 0, tm), 0)
    return ve.astype(I32), vt.astype(I32), lo.astype(I32), hi.astype(I32)


def _sc_mesh():
    return plsc.VectorSubcoreMesh(core_axis_name="c", subcore_axis_name="s")


def _sc_worker():
    return lax.axis_index("s") * SC_CORES + lax.axis_index("c")


def _dispatch(x1p, dest):
    m, dp = x1p.shape
    blocks = m // SC_ROWS
    per_worker = blocks // SC_WORKERS
    table = dest.reshape(TOP_K, blocks, SC_ROWS).transpose(1, 0, 2).reshape(blocks * TOP_K, SC_ROWS)

    def body(x_hbm, idx_hbm, xs_hbm, idx_v, rows_v, sem):
        first = _sc_worker() * per_worker

        @pl.loop(0, per_worker)
        def _(j):
            b = first + j
            pltpu.sync_copy(idx_hbm.at[pl.ds(pl.multiple_of(b * TOP_K, TOP_K), TOP_K)], idx_v)
            pltpu.sync_copy(x_hbm.at[pl.ds(pl.multiple_of(b * SC_ROWS, SC_ROWS), SC_ROWS)], rows_v)
            copies = [pltpu.async_copy(rows_v, xs_hbm.at[idx_v.at[k]], sem) for k in range(TOP_K)]
            for cp in copies:
                cp.wait()

    return pl.kernel(
        body,
        out_type=jax.ShapeDtypeStruct((m * TOP_K, dp), x1p.dtype),
        mesh=_sc_mesh(),
        scratch_types=[pltpu.VMEM((TOP_K, SC_ROWS), I32), pltpu.VMEM((SC_ROWS, dp), x1p.dtype),
                       pltpu.SemaphoreType.DMA],
        name="dispatch",
    )(x1p, table)


def _gather_rows(ys, dest):
    n_rows, dp = ys.shape
    rows = SC_ROWS // 2
    idx = dest.reshape(n_rows // rows, rows)
    per_worker = n_rows // rows // SC_WORKERS

    def body(ys_hbm, idx_hbm, out_hbm, idx_v, rows_v, sems):
        first = _sc_worker() * per_worker

        def gather(slot):
            return pltpu.make_async_copy(ys_hbm.at[idx_v.at[slot]], rows_v.at[slot], sems.at[slot])

        def start(slot, b):
            pltpu.sync_copy(idx_hbm.at[pl.ds(b, 1)], idx_v.at[pl.ds(slot, 1)])
            gather(slot).start()

        def finish(slot, b):
            gather(slot).wait()
            pltpu.sync_copy(rows_v.at[slot], out_hbm.at[pl.ds(pl.multiple_of(b * rows, rows), rows)])

        start(0, first)

        @pl.loop(0, per_worker, step=2)
        def _(j):
            b = first + j
            start(1, b + 1)
            finish(0, b)

            @pl.when(j + 2 < per_worker)
            def _():
                start(0, b + 2)

            finish(1, b + 1)

    return pl.kernel(
        body,
        out_type=jax.ShapeDtypeStruct((n_rows, dp), ys.dtype),
        mesh=_sc_mesh(),
        scratch_types=[pltpu.VMEM((2, rows), I32), pltpu.VMEM((2, rows, dp), ys.dtype),
                       pltpu.SemaphoreType.DMA((2,))],
        name="gather_rows",
    )(ys, idx)


def _ffn_kernel(ve_ref, vt_ref, lo_ref, hi_ref, xs_ref, wg_ref, wu_ref, wd_ref, ys_ref,
                wgu_b, wd_b, acc):
    v = pl.program_id(0)
    lo = lo_ref[v]
    hi = hi_ref[v]
    tm = xs_ref.shape[0]
    f = wg_ref.shape[1]

    @pl.when((v == 0) | (ve_ref[v] != ve_ref[jnp.maximum(v - 1, 0)]))
    def _():
        wgu_b[:, :f] = wg_ref[...].astype(BF16)
        wgu_b[:, f:] = wu_ref[...].astype(BF16)
        wd_b[...] = wd_ref[...].astype(BF16)

    def pack_rows(a):
        half = a.shape[1] // 2
        return _pack_pair(a[:, :half], a[:, half:])

    for sb in range(tm // FFN_SUB):
        s0 = sb * FFN_SUB
        rows = pl.ds(s0, FFN_SUB)
        lo_s = jnp.clip(lo - s0, 0, FFN_SUB)
        hi_s = jnp.clip(hi - s0, 0, FFN_SUB)

        whole = (lo_s == 0) & (hi_s == FFN_SUB)

        @pl.when(whole)
        def _():
            halves = [pl.ds(s0 + j * (FFN_SUB // 2), FFN_SUB // 2) for j in range(2)]
            x = [_unpack_rows(xs_ref[hr, :]) for hr in halves]
            h2 = [jnp.dot(xj, wgu_b[...], preferred_element_type=F32) for xj in x]
            act = [(hj[:, :f] * _sigmoid(hj[:, :f]) * hj[:, f:]).astype(BF16) for hj in h2]
            y = [jnp.dot(aj, wd_b[...], preferred_element_type=F32) for aj in act]
            for hr, yj in zip(halves, y):
                ys_ref[hr, :] = pack_rows(yj)

        @pl.when((hi_s > lo_s) & jnp.logical_not(whole))
        def _():
            x = _unpack_rows(xs_ref[rows, :])
            h2 = jnp.dot(x, wgu_b[...], preferred_element_type=F32)
            hg = h2[:, :f]
            hu = h2[:, f:]
            r = lax.broadcasted_iota(I32, hg.shape, 0)
            mine = (r >= lo_s) & (r < hi_s)
            hmid = jnp.where(mine, hg * _sigmoid(hg) * hu, 0.0).astype(BF16)
            y = jnp.dot(hmid, wd_b[...], preferred_element_type=F32)

            @pl.when(lo_s == 0)
            def _():
                acc[rows, :] = y

            @pl.when(lo_s > 0)
            def _():
                acc[rows, :] += y

            @pl.when((lo_s > 0) & (hi_s == FFN_SUB))
            def _():
                ys_ref[rows, :] = pack_rows(acc[rows, :])


def _ffn(xs, meta, wg, wu, wd, layer):
    n_rows, dp = xs.shape
    _, n_e, d, f = wg.shape
    tm = FFN_TM
    nv = n_rows // tm + N_EXPERTS - 1
    grid_spec = pltpu.PrefetchScalarGridSpec(
        num_scalar_prefetch=4,
        grid=(nv,),
        in_specs=[pl.BlockSpec((tm, dp), lambda v, ve, vt, lo, hi: (vt[v], 0)),
                  pl.BlockSpec((None, None, d, f), lambda v, ve, vt, lo, hi: (layer, ve[v], 0, 0)),
                  pl.BlockSpec((None, None, d, f), lambda v, ve, vt, lo, hi: (layer, ve[v], 0, 0)),
                  pl.BlockSpec((None, None, f, d), lambda v, ve, vt, lo, hi: (layer, ve[v], 0, 0))],
        out_specs=pl.BlockSpec((tm, dp), lambda v, ve, vt, lo, hi: (vt[v], 0)),
        scratch_shapes=[pltpu.VMEM((d, 2 * f), BF16), pltpu.VMEM((f, d), BF16), pltpu.VMEM((tm, d), F32)],
    )
    return pl.pallas_call(
        _ffn_kernel,
        grid_spec=grid_spec,
        out_shape=jax.ShapeDtypeStruct((n_rows, dp), I32),
        compiler_params=_params("arbitrary"),
        name="ffn",
    )(*meta, xs, wg, wu, wd)


def _combine_kernel(yk_ref, x1_ref, wtm_ref, p_ref, wsg_ref, wsu_ref, wsd_ref,
                    wpg_ref, wpp_ref, g_ref, b_ref, x2_ref, x2b_ref):
    tt = x1_ref.shape[0]
    xb = x1_ref[...].astype(BF16)
    hg = jnp.dot(xb, wsg_ref[...], preferred_element_type=F32)
    hu = jnp.dot(xb, wsu_ref[...], preferred_element_type=F32)
    shared = jnp.dot((hg * _sigmoid(hg) * hu).astype(BF16), wsd_ref[...], preferred_element_type=F32)
    pgate = _sigmoid(jnp.dot(xb, wpg_ref[...], preferred_element_type=F32))
    pproj = jnp.dot(p_ref[...].astype(BF16), wpp_ref[...], preferred_element_type=F32)
    rest = ALPHA * x1_ref[...] + shared + pgate * pproj

    wt = wtm_ref[...]
    r_lo = jnp.zeros((tt, yk_ref.shape[2]), F32)
    r_hi = jnp.zeros((tt, yk_ref.shape[2]), F32)
    for k in range(TOP_K):
        lo, hi = _unpack_pair(yk_ref[k])
        wk = wt[:, k:k + 1]
        r_lo = r_lo + wk * lo
        r_hi = r_hi + wk * hi
    routed = jnp.concatenate([r_lo, r_hi], axis=1)
    x2 = _layer_norm(rest + routed, g_ref[...], b_ref[...])
    x2_ref[...] = x2
    x2b_ref[...] = x2.astype(BF16)


def _combine(yk, x1, wtm, p, layer, wsg, wsu, wsd, wpg, wpp, g, b):
    m, d = x1.shape
    tt = min(COMBINE_TT, m)
    yk = yk.reshape(TOP_K, m, d // 2)

    def full(a):
        return pl.BlockSpec(a.shape, lambda i: (0,) * a.ndim)

    token_rows = pl.BlockSpec((tt, d), lambda i: (i, 0))
    return pl.pallas_call(
        _combine_kernel,
        grid=(m // tt,),
        in_specs=[pl.BlockSpec((TOP_K, tt, d // 2), lambda i: (0, i, 0)),
                  token_rows,
                  pl.BlockSpec((tt, LANES), lambda i: (i, 0)),
                  pl.BlockSpec((None, tt, p.shape[2]), lambda i: (layer, i, 0)),
                  full(wsg), full(wsu), full(wsd), full(wpg), full(wpp), full(g), full(b)],
        out_specs=[token_rows, token_rows],
        out_shape=[jax.ShapeDtypeStruct((m, d), F32), jax.ShapeDtypeStruct((m, d), BF16)],
        compiler_params=_params("parallel"),
        name="combine",
    )(yk, x1, wtm, p, wsg, wsu, wsd, wpg, wpp, g, b)


def kernel(x, p, w_in, conv_w, b_igate, b_fgate, mlstm_norm_g, w_branch_a, w_branch_b, w_out, ln1_g, ln1_b, w_router, router_bias, w_exp_gate, w_exp_up, w_exp_down, w_sh_gate, w_sh_up, w_sh_down, w_ple_gate, w_ple_proj, ln2_g, ln2_b):
    bsz, seq, d = x.shape
    m = bsz * seq
    depth = w_in.shape[0]
    hw = HEADS * HEAD_DIM
    if_lo = 3 * d + 4 * hw
    if_hi = if_lo + 2 * HEADS

    xf = x.reshape(m, d)
    xb = xf.astype(BF16)
    w_in_t = jnp.swapaxes(w_in, 1, 2)
    p = p.reshape(depth, m, -1)
    for i in range(depth):
        w_merge = w_in_t[i, if_hi:]
        w_if = jnp.pad(w_in_t[i, if_lo:if_hi], ((0, LANES - 2 * HEADS), (0, 0)))
        gate_bias = jnp.pad(jnp.concatenate([b_igate[i], b_fgate[i]]), (0, LANES - 2 * HEADS)).reshape(1, LANES)

        z = _matmul(xb, w_in_t, BF16, INPROJ_TM, INPROJ_TN, "inproj", layer=i, n=if_lo).reshape(bsz, seq, -1)
        zm = _matmul(xb, w_merge, BF16, INPROJ_TM, INPROJ_TN, "mergeproj").reshape(bsz, seq, -1)
        gates = _matmul(xb, w_if, F32, INPROJ_TM, LANES, "gateproj").reshape(bsz, seq, LANES)
        x1, x1p = _mixer(z, zm, gates, gate_bias, mlstm_norm_g[i].reshape(1, hw),
                         xf.reshape(bsz, seq, d), conv_w[i],
                         w_branch_a[i].astype(BF16), w_branch_b[i].astype(BF16), w_out[i].astype(BF16),
                         ln1_g[i].reshape(1, d), ln1_b[i].reshape(1, d))
        x1 = x1.reshape(m, d)
        x1p = x1p.reshape(m, d // 2)

        w_router_t = w_router[i].T.reshape(N_GROUPS, GROUP_SIZE, d).swapaxes(0, 1).reshape(N_EXPERTS, d)
        rb = router_bias[i].reshape(N_GROUPS, GROUP_SIZE).T.reshape(N_EXPERTS, 1)
        idx, rank, wtm, cnt = _router(x1, w_router_t, rb)
        counts = cnt[:, 0].astype(I32).reshape(GROUP_SIZE, N_GROUPS).T.reshape(N_EXPERTS)
        row_start = jnp.cumsum(counts) - counts
        expert_ids = jnp.arange(N_EXPERTS, dtype=I32)[:, None, None]
        dest = rank + jnp.sum(jnp.where(idx[None] == expert_ids, row_start[:, None, None], 0), axis=0)
        meta = _visit_metadata(counts, FFN_TM, m * TOP_K)
        xs = _dispatch(x1p, dest)
        ys = _ffn(xs, meta, w_exp_gate, w_exp_up, w_exp_down, i)
        yk = _gather_rows(ys, dest)
        xf, xb = _combine(yk, x1, wtm, p, i,
                          w_sh_gate[i].astype(BF16), w_sh_up[i].astype(BF16), w_sh_down[i].astype(BF16),
                          w_ple_gate[i].astype(BF16), w_ple_proj[i].astype(BF16),
                          ln2_g[i].reshape(1, d), ln2_b[i].reshape(1, d))
    return xf.reshape(bsz, seq, d)
```

```python
import math

import jax
import jax.numpy as jnp
from jax import lax
from jax.experimental import pallas as pl
from jax.experimental.pallas import tpu as pltpu
from jax.experimental.pallas import tpu_sc as plsc

F32 = jnp.float32
BF16 = jnp.bfloat16
U32 = jnp.uint32
I32 = jnp.int32

HEADS = 8
HEAD_DIM = 128
N_EXPERTS = 64
N_GROUPS = 8
GROUP_SIZE = N_EXPERTS // N_GROUPS
TOPK_GROUPS = 4
TOP_K = 8
ROUTED_SCALE = 2.5
DEPTH = 4
ALPHA = (2 * DEPTH) ** 0.25
LN_EPS = 1e-5
RMS_EPS = 1e-6
QK_SCALE = HEAD_DIM ** -0.5
LOG_QK_SCALE = math.log(QK_SCALE)

LANES = 128
VMEM_LIMIT = 56 * 1024 * 1024
NEG_INF = float("-inf")

MLSTM_CHUNK = 256
MLSTM_STEP_CHUNKS = 2
INPROJ_TM, INPROJ_TN = 2048, 1024
ROUTER_TT = 512
FFN_TM = 2048
FFN_SUB = 512
COMBINE_TT = 512

SC_CORES = 2
SC_SUBCORES = 16
SC_WORKERS = SC_CORES * SC_SUBCORES
SC_ROWS = 128


def _params(*sem):
    return pltpu.CompilerParams(dimension_semantics=sem, vmem_limit_bytes=VMEM_LIMIT)


def _sigmoid(x):
    return 1.0 / (1.0 + jnp.exp(-x))


def _layer_norm(r, g, b):
    mu = jnp.mean(r, axis=-1, keepdims=True)
    d = r - mu
    var = jnp.mean(d * d, axis=-1, keepdims=True)
    return d * lax.rsqrt(var + LN_EPS) * g + b


def _pack_pair(lo, hi):
    return lax.bitcast_convert_type(pltpu.pack_elementwise([lo, hi], packed_dtype=BF16), I32)


def _unpack_pair(w):
    w = lax.bitcast_convert_type(w, U32)
    lo = pltpu.unpack_elementwise(w, index=0, packed_dtype=BF16, unpacked_dtype=F32)
    hi = pltpu.unpack_elementwise(w, index=1, packed_dtype=BF16, unpacked_dtype=F32)
    return lo, hi


def _unpack_rows(w):
    lo, hi = _unpack_pair(w)
    return jnp.concatenate([lo.astype(BF16), hi.astype(BF16)], axis=1)


def _mm_kernel(x_ref, wt_ref, o_ref):
    o_ref[...] = lax.dot_general(x_ref[...], wt_ref[...].astype(BF16), (((1,), (1,)), ((), ())),
                                 preferred_element_type=F32).astype(o_ref.dtype)


def _matmul(x, wt, out_dtype, tm, tn, name, layer=None, n=None):
    m, k = x.shape
    n = wt.shape[-2] if n is None else n
    tm = min(tm, m)
    if layer is None:
        w_spec = pl.BlockSpec((tn, k), lambda i, j: (j, 0))
    else:
        w_spec = pl.BlockSpec((None, tn, k), lambda i, j: (layer, j, 0))
    return pl.pallas_call(
        _mm_kernel,
        grid=(m // tm, n // tn),
        in_specs=[pl.BlockSpec((tm, k), lambda i, j: (i, 0)), w_spec],
        out_specs=pl.BlockSpec((tm, tn), lambda i, j: (i, j)),
        out_shape=jax.ShapeDtypeStruct((m, n), out_dtype),
        compiler_params=_params("parallel", "parallel"),
        name=name,
    )(x, wt)


def _mlstm_rows(q_ref, k_ref, v_ref, o_ref, gates, ng_ref, c_scr, m_scr):
    L = MLSTM_CHUNK
    chunks = [pl.ds(j * L, L) for j in range(q_ref.shape[1] // L)]
    row = lax.broadcasted_iota(I32, (L, L), 0)
    col = lax.broadcasted_iota(I32, (L, L), 1)
    causal = col <= row
    rows = lax.broadcasted_iota(I32, (L, LANES), 0)

    def scan_rows(x, op, identity):
        step = 1
        while step < L:
            x = op(x, jnp.where(rows >= step, pltpu.roll(x, step, axis=0), identity))
            step *= 2
        return x

    gate = []
    m_prev = m_scr[...]
    for j in range(len(chunks)):
        g = gates[j * L:(j + 1) * L, :]
        ig = g
        fg = pltpu.roll(g, LANES - HEADS, axis=1)
        log_f = jnp.minimum(fg, 0.0) - jnp.log(1.0 + jnp.exp(-jnp.abs(fg)))
        b = scan_rows(log_f, jnp.add, 0.0)
        b_last = b[L - 1:L, :]
        a = b_last - b + ig
        m_loc = jnp.max(a, axis=0, keepdims=True)
        m_new = jnp.maximum(b_last + m_prev, m_loc)
        log_inter = b + m_prev
        r = ig - b
        m_out = jnp.maximum(log_inter, b + scan_rows(r, jnp.maximum, NEG_INF))
        gate.append(dict(
            w_loc=jnp.exp(a - m_loc),
            sp=jnp.exp(b_last + m_prev - m_new),
            sl=jnp.exp(m_loc - m_new),
            r_t=jnp.transpose(r),
            u=b - m_out + LOG_QK_SCALE,
            e_inter=jnp.exp(log_inter - m_out + LOG_QK_SCALE),
            e_floor=jnp.exp(-m_out)))
        m_prev = m_new
    m_scr[...] = m_prev

    ones_blk = jnp.ones((L, HEAD_DIM), BF16)
    lanes = [slice(h * HEAD_DIM, (h + 1) * HEAD_DIM) for h in range(HEADS)]
    pairs = [(j, h) for j in range(len(chunks)) for h in range(HEADS)]

    def col_of(name, j, h):
        return gate[j][name][:, h:h + 1]

    q = {(j, h): q_ref[0, chunks[j], lanes[h]] for j, h in pairs}
    k = {(j, h): k_ref[0, chunks[j], lanes[h]] for j, h in pairs}
    v_aug = {(j, h): jnp.concatenate([v_ref[0, chunks[j], lanes[h]], ones_blk], axis=1) for j, h in pairs}
    s1 = {p: lax.dot_general(q[p], k[p], (((1,), (1,)), ((), ())), preferred_element_type=F32) for p in pairs}
    kw = {(j, h): (k[j, h].astype(F32) * col_of("w_loc", j, h)).astype(BF16) for j, h in pairs}
    c_loc = {p: lax.dot_general(kw[p], v_aug[p], (((0,), (0,)), ((), ())), preferred_element_type=F32)
             for p in pairs}
    c_seen = {}
    for h in range(HEADS):
        c = c_scr[h]
        for j in range(len(chunks)):
            c_seen[j, h] = c
            c = col_of("sp", j, h) * c + col_of("sl", j, h) * c_loc[j, h]
        c_scr[h] = c
    inter = {p: jnp.dot(q[p], c_seen[p].astype(BF16), preferred_element_type=F32) for p in pairs}
    s = {(j, h): (s1[j, h] * jnp.where(causal, jnp.exp(col_of("u", j, h) + gate[j]["r_t"][h:h + 1, :]), 0.0)
                  ).astype(BF16) for j, h in pairs}
    intra = {p: jnp.dot(s[p], v_aug[p], preferred_element_type=F32) for p in pairs}
    tot = {(j, h): intra[j, h] + col_of("e_inter", j, h) * inter[j, h] for j, h in pairs}
    hh = {(j, h): tot[j, h][:, :HEAD_DIM] / jnp.maximum(jnp.abs(tot[j, h][:, HEAD_DIM:]), col_of("e_floor", j, h))
          for j, h in pairs}
    hh = {p: hh[p] * lax.rsqrt(jnp.mean(hh[p] * hh[p], axis=1, keepdims=True) + RMS_EPS) for p in pairs}
    y_b = {(j, h): (hh[j, h] * ng_ref[:, lanes[h]] * _sigmoid(o_ref[0, chunks[j], lanes[h]].astype(F32))).astype(BF16)
           for j, h in pairs}
    return [jnp.concatenate([y_b[j, h] for h in range(HEADS)], axis=1) for j in range(len(chunks))]


def _mixer_kernel(q_ref, k_ref, v_ref, o_ref, gates_ref, gb_ref, ng_ref,
                  cin_ref, cout_ref, cval_ref, mg0_ref, mg1_ref, x_ref, cw_ref,
                  wa_ref, wb_ref, wo_ref, g_ref, b_ref, x1_ref, x1p_ref, c_scr, m_scr, carry):
    @pl.when(pl.program_id(1) == 0)
    def _():
        c_scr[...] = jnp.zeros_like(c_scr)
        m_scr[...] = jnp.zeros_like(m_scr)
        carry[...] = jnp.zeros_like(carry)

    tp = MLSTM_CHUNK
    parts = [pl.ds(j * tp, tp) for j in range(x_ref.shape[1] // tp)]
    u = [cin_ref[0, pr, :].astype(F32) * cval_ref[0, pr, :].astype(F32) for pr in parts]
    prev = [carry[...]] + [uj[tp - 8:, :] for uj in u[:-1]]
    carry[...] = u[-1][tp - 8:, :]
    r8 = lax.broadcasted_iota(I32, (8, u[0].shape[1]), 0)

    def shifted(uj, pj, k):
        body = pltpu.roll(uj, k, axis=0)
        head = jnp.where(r8 < k, pltpu.roll(pj, k, axis=0), body[:8, :])
        return jnp.concatenate([head, body[8:, :]], axis=0)

    cw = cw_ref[...]
    conv = [cw[0:1, :] * shifted(uj, pj, 2) + cw[1:2, :] * shifted(uj, pj, 1) + cw[2:3, :] * uj
            for uj, pj in zip(u, prev)]
    y_a = [(cout_ref[0, pr, :].astype(F32) * cj).astype(BF16) for pr, cj in zip(parts, conv)]
    pa = [jnp.dot(yj, wa_ref[...], preferred_element_type=F32) for yj in y_a]
    gated_a = [_sigmoid(mg0_ref[0, pr, :].astype(F32)) * paj for pr, paj in zip(parts, pa)]
    y_b = _mlstm_rows(q_ref, k_ref, v_ref, o_ref, gates_ref[0] + gb_ref[...], ng_ref, c_scr, m_scr)
    pb = [jnp.dot(yj, wb_ref[...], preferred_element_type=F32) for yj in y_b]
    mixed = [(gaj + _sigmoid(mg1_ref[0, pr, :].astype(F32)) * pbj).astype(BF16)
             for pr, gaj, pbj in zip(parts, gated_a, pb)]
    hmix = [jnp.dot(mj, wo_ref[...], preferred_element_type=F32) for mj in mixed]
    for pr, hj in zip(parts, hmix):
        x1 = _layer_norm(ALPHA * x_ref[0, pr, :] + hj, g_ref[...], b_ref[...])
        x1_ref[0, pr, :] = x1
        half = x1.shape[1] // 2
        x1p_ref[0, pr, :] = _pack_pair(x1[:, :half], x1[:, half:])


def _mixer(z, zm, gates, gate_bias, norm_g, x, conv_w, wa, wb, wo, g, b):
    bsz, seq, d = x.shape
    ts = min(MLSTM_STEP_CHUNKS * MLSTM_CHUNK, seq)

    def zspec(cb):
        return pl.BlockSpec((1, ts, d), lambda i, j, cb=cb: (i, j, cb))

    def full(shape):
        return pl.BlockSpec(shape, lambda i, j: (0,) * len(shape))

    tile = pl.BlockSpec((1, ts, d), lambda i, j: (i, j, 0))
    ptile = pl.BlockSpec((1, ts, d // 2), lambda i, j: (i, j, 0))
    return pl.pallas_call(
        _mixer_kernel,
        grid=(bsz, seq // ts),
        in_specs=[zspec(3), zspec(4), zspec(5), zspec(6),
                  pl.BlockSpec((1, ts, LANES), lambda i, j: (i, j, 0)),
                  full(gate_bias.shape), full(norm_g.shape),
                  zspec(0), zspec(1), zspec(2), zspec(0), zspec(1), tile,
                  full(conv_w.shape), full(wa.shape), full(wb.shape), full(wo.shape),
                  full(g.shape), full(b.shape)],
        out_specs=[tile, ptile],
        out_shape=[jax.ShapeDtypeStruct((bsz, seq, d), F32),
                   jax.ShapeDtypeStruct((bsz, seq, d // 2), I32)],
        scratch_shapes=[pltpu.VMEM((HEADS, HEAD_DIM, 2 * HEAD_DIM), F32),
                        pltpu.VMEM((1, LANES), F32),
                        pltpu.VMEM((8, d), F32)],
        compiler_params=_params("parallel", "arbitrary"),
        name="mixer",
    )(z, z, z, z, gates, gate_bias, norm_g, z, z, z, zm, zm, x, conv_w, wa, wb, wo, g, b)


def _router_kernel(x_ref, wrt_ref, rb_ref, idx_ref, rank_ref, wtm_ref, cnt_ref, carry):
    t = x_ref.shape[0]

    @pl.when(pl.program_id(0) == 0)
    def _():
        carry[...] = jnp.zeros_like(carry)

    def split(a):
        hi = a.astype(BF16)
        return hi, (a - hi.astype(F32)).astype(BF16)

    def dot_t(a, b):
        return lax.dot_general(a, b, (((1,), (1,)), ((), ())), preferred_element_type=F32)

    w_hi, w_lo = split(wrt_ref[...])
    x_hi, x_lo = split(x_ref[...])
    logits = dot_t(w_hi, x_hi) + (dot_t(w_hi, x_lo) + dot_t(w_lo, x_hi))
    scores = _sigmoid(logits)
    shape3 = (GROUP_SIZE, N_GROUPS, t)
    sel = (scores + rb_ref[...]).reshape(shape3)
    scores = scores.reshape(shape3)
    mem = lax.broadcasted_iota(I32, shape3, 0)
    grp = lax.broadcasted_iota(I32, shape3, 1)
    eidx = grp * GROUP_SIZE + mem
    m1 = jnp.max(sel, axis=0, keepdims=True)
    first = jnp.min(jnp.where(sel == m1, mem, GROUP_SIZE), axis=0, keepdims=True)
    m2 = jnp.max(jnp.where(mem == first, NEG_INF, sel), axis=0, keepdims=True)
    rem = m1 + m2
    gidx = lax.broadcasted_iota(I32, rem.shape, 1)
    gmask = jnp.zeros(rem.shape, F32)
    for _ in range(TOPK_GROUPS):
        mx = jnp.max(rem, axis=1, keepdims=True)
        pick = gidx == jnp.min(jnp.where(rem == mx, gidx, N_GROUPS), axis=1, keepdims=True)
        gmask = jnp.where(pick, 1.0, gmask)
        rem = jnp.where(pick, NEG_INF, rem)
    masked = jnp.where(jnp.broadcast_to(gmask, shape3) > 0.5, sel, NEG_INF)
    chosen = jnp.zeros(shape3, F32)
    picks = []
    for _ in range(TOP_K):
        mx = jnp.max(jnp.max(masked, axis=0, keepdims=True), axis=1, keepdims=True)
        cand = jnp.where(masked == mx, eidx, N_EXPERTS)
        fi = jnp.min(jnp.min(cand, axis=0, keepdims=True), axis=1, keepdims=True)
        pick = eidx == fi
        picks.append((fi, pick))
        chosen = jnp.where(pick, 1.0, chosen)
        masked = jnp.where(pick, NEG_INF, masked)
    w = chosen * scores
    denom = jnp.sum(jnp.sum(w, axis=0, keepdims=True), axis=1, keepdims=True)
    gate3 = w / denom * ROUTED_SCALE

    chosen2 = chosen.reshape(N_EXPERTS, t).astype(BF16)
    tok_r = lax.broadcasted_iota(I32, (t, t), 0)
    tok_c = lax.broadcasted_iota(I32, (t, t), 1)
    before = jnp.where(tok_r < tok_c, 1.0, 0.0).astype(BF16)
    prefix = (jnp.dot(chosen2, before, preferred_element_type=F32) + carry[...]).reshape(shape3)
    carry[...] += jnp.dot(chosen2, jnp.ones((t, t), BF16), preferred_element_type=F32)
    cnt_ref[...] = carry[:, :LANES]

    def pick_sum(pick, val):
        return jnp.sum(jnp.sum(jnp.where(pick, val, 0.0), axis=0, keepdims=True), axis=1, keepdims=True)

    sub8 = lax.broadcasted_iota(I32, (TOP_K, t), 0)
    idx8 = jnp.zeros((TOP_K, t), I32)
    rank8 = jnp.zeros((TOP_K, t), F32)
    w8 = jnp.zeros((TOP_K, t), F32)
    for k, (fi, pick) in enumerate(picks):
        idx8 = jnp.where(sub8 == k, jnp.broadcast_to(fi.reshape(1, t), (TOP_K, t)), idx8)
        rank8 = jnp.where(sub8 == k, jnp.broadcast_to(pick_sum(pick, prefix).reshape(1, t), (TOP_K, t)), rank8)
        w8 = jnp.where(sub8 == k, jnp.broadcast_to(pick_sum(pick, gate3).reshape(1, t), (TOP_K, t)), w8)
    idx_ref[...] = idx8
    rank_ref[...] = rank8.astype(I32)
    pad = jnp.zeros((LANES - TOP_K, t), F32)
    wtm_ref[...] = jnp.transpose(jnp.concatenate([w8, pad], axis=0))


def _router(x1, w_router_t, router_bias):
    m, d = x1.shape
    tt = min(ROUTER_TT, m)
    kt = pl.BlockSpec((TOP_K, tt), lambda i: (0, i))
    return pl.pallas_call(
        _router_kernel,
        grid=(m // tt,),
        in_specs=[pl.BlockSpec((tt, d), lambda i: (i, 0)),
                  pl.BlockSpec((N_EXPERTS, d), lambda i: (0, 0)),
                  pl.BlockSpec((N_EXPERTS, 1), lambda i: (0, 0))],
        out_specs=[kt, kt,
                   pl.BlockSpec((tt, LANES), lambda i: (i, 0)),
                   pl.BlockSpec((N_EXPERTS, LANES), lambda i: (0, 0))],
        out_shape=[jax.ShapeDtypeStruct((TOP_K, m), I32),
                   jax.ShapeDtypeStruct((TOP_K, m), I32),
                   jax.ShapeDtypeStruct((m, LANES), F32),
                   jax.ShapeDtypeStruct((N_EXPERTS, LANES), F32)],
        scratch_shapes=[pltpu.VMEM((N_EXPERTS, tt), F32)],
        compiler_params=_params("arbitrary"),
        name="router",
    )(x1, w_router_t, router_bias)


def _visit_metadata(counts, tm, n_rows):
    nt = n_rows // tm
    nv = nt + N_EXPERTS - 1
    ends = jnp.cumsum(counts)
    starts = ends - counts
    first_tile = starts // tm
    ntiles = jnp.where(counts > 0, (ends - 1) // tm - first_tile + 1, 0)
    vend = jnp.cumsum(ntiles)
    vstart = vend - ntiles
    v = jnp.arange(nv, dtype=I32)
    valid = v < vend[-1]
    ve = jnp.minimum(jnp.sum((v[:, None] >= vend[None, :]).astype(I32), axis=1), N_EXPERTS - 1)
    ve = jnp.where(valid, ve, ve[jnp.maximum(vend[-1] - 1, 0)])
    vt = jnp.where(valid, first_tile[ve] + v - vstart[ve], nt - 1)
    lo = jnp.where(valid, jnp.clip(starts[ve] - vt * tm, 0, tm), 0)
    hi = jnp.where(valid, jnp.clip(ends[ve] - vt * tm, 0, tm), 0)
    return ve.astype(I32), vt.astype(I32), lo.astype(I32), hi.astype(I32)


def _sc_mesh():
    return plsc.VectorSubcoreMesh(core_axis_name="c", subcore_axis_name="s")


def _sc_worker():
    return lax.axis_index("s") * SC_CORES + lax.axis_index("c")


def _dispatch(x1p, dest):
    m, dp = x1p.shape
    blocks = m // SC_ROWS
    per_worker = blocks // SC_WORKERS
    table = dest.reshape(TOP_K, blocks, SC_ROWS).transpose(1, 0, 2).reshape(blocks * TOP_K, SC_ROWS)

    def body(x_hbm, idx_hbm, xs_hbm, idx_v, rows_v, sem):
        first = _sc_worker() * per_worker

        @pl.loop(0, per_worker)
        def _(j):
            b = first + j
            pltpu.sync_copy(idx_hbm.at[pl.ds(pl.multiple_of(b * TOP_K, TOP_K), TOP_K)], idx_v)
            pltpu.sync_copy(x_hbm.at[pl.ds(pl.multiple_of(b * SC_ROWS, SC_ROWS), SC_ROWS)], rows_v)
            copies = [pltpu.async_copy(rows_v, xs_hbm.at[idx_v.at[k]], sem) for k in range(TOP_K)]
            for cp in copies:
                cp.wait()

    return pl.kernel(
        body,
        out_type=jax.ShapeDtypeStruct((m * TOP_K, dp), x1p.dtype),
        mesh=_sc_mesh(),
        scratch_types=[pltpu.VMEM((TOP_K, SC_ROWS), I32), pltpu.VMEM((SC_ROWS, dp), x1p.dtype),
                       pltpu.SemaphoreType.DMA],
        name="dispatch",
    )(x1p, table)


def _gather_rows(ys, dest):
    n_rows, dp = ys.shape
    rows = SC_ROWS // 2
    idx = dest.reshape(n_rows // rows, rows)
    per_worker = n_rows // rows // SC_WORKERS

    def body(ys_hbm, idx_hbm, out_hbm, idx_v, rows_v, sems):
        first = _sc_worker() * per_worker

        def gather(slot):
            return pltpu.make_async_copy(ys_hbm.at[idx_v.at[slot]], rows_v.at[slot], sems.at[slot])

        def start(slot, b):
            pltpu.sync_copy(idx_hbm.at[pl.ds(b, 1)], idx_v.at[pl.ds(slot, 1)])
            gather(slot).start()

        def finish(slot, b):
            gather(slot).wait()
            pltpu.sync_copy(rows_v.at[slot], out_hbm.at[pl.ds(pl.multiple_of(b * rows, rows), rows)])

        start(0, first)

        @pl.loop(0, per_worker, step=2)
        def _(j):
            b = first + j
            start(1, b + 1)
            finish(0, b)

            @pl.when(j + 2 < per_worker)
            def _():
                start(0, b + 2)

            finish(1, b + 1)

    return pl.kernel(
        body,
        out_type=jax.ShapeDtypeStruct((n_rows, dp), ys.dtype),
        mesh=_sc_mesh(),
        scratch_types=[pltpu.VMEM((2, rows), I32), pltpu.VMEM((2, rows, dp), ys.dtype),
                       pltpu.SemaphoreType.DMA((2,))],
        name="gather_rows",
    )(ys, idx)


def _ffn_kernel(ve_ref, vt_ref, lo_ref, hi_ref, xs_ref, wg_ref, wu_ref, wd_ref, ys_ref,
                wgu_b, wd_b):
    v = pl.program_id(0)
    lo = lo_ref[v]
    hi = hi_ref[v]
    tm = xs_ref.shape[0]
    f = wg_ref.shape[1]

    @pl.when((v == 0) | (ve_ref[v] != ve_ref[jnp.maximum(v - 1, 0)]))
    def _():
        wgu_b[:, :f] = wg_ref[...].astype(BF16)
        wgu_b[:, f:] = wu_ref[...].astype(BF16)
        wd_b[...] = wd_ref[...].astype(BF16)

    def pack_rows(a):
        half = a.shape[1] // 2
        return _pack_pair(a[:, :half], a[:, half:])

    for sb in range(tm // FFN_SUB):
        s0 = sb * FFN_SUB
        lo_s = jnp.clip(lo - s0, 0, FFN_SUB)
        hi_s = jnp.clip(hi - s0, 0, FFN_SUB)

        half = FFN_SUB // 2
        halves = [pl.ds(s0 + j * half, half) for j in range(2)]

        def expert_rows():
            x = [_unpack_rows(xs_ref[hr, :]) for hr in halves]
            h2 = [jnp.dot(xj, wgu_b[...], preferred_element_type=F32) for xj in x]
            act = [(hj[:, :f] * _sigmoid(hj[:, :f]) * hj[:, f:]).astype(BF16) for hj in h2]
            y = [jnp.dot(aj, wd_b[...], preferred_element_type=F32) for aj in act]
            return [pack_rows(yj) for yj in y]

        @pl.when((hi_s > lo_s) & (lo_s == 0))
        def _():
            for hr, yj in zip(halves, expert_rows()):
                ys_ref[hr, :] = yj

        @pl.when((hi_s > lo_s) & (lo_s > 0))
        def _():
            for j, (hr, yj) in enumerate(zip(halves, expert_rows())):
                r = lax.broadcasted_iota(I32, yj.shape, 0) + j * half
                mine = (r >= lo_s) & (r < hi_s)
                ys_ref[hr, :] = jnp.where(mine, yj, ys_ref[hr, :])


def _ffn(xs, meta, wg, wu, wd, layer):
    n_rows, dp = xs.shape
    _, n_e, d, f = wg.shape
    tm = FFN_TM
    nv = n_rows // tm + N_EXPERTS - 1
    grid_spec = pltpu.PrefetchScalarGridSpec(
        num_scalar_prefetch=4,
        grid=(nv,),
        in_specs=[pl.BlockSpec((tm, dp), lambda v, ve, vt, lo, hi: (vt[v], 0)),
                  pl.BlockSpec((None, None, d, f), lambda v, ve, vt, lo, hi: (layer, ve[v], 0, 0)),
                  pl.BlockSpec((None, None, d, f), lambda v, ve, vt, lo, hi: (layer, ve[v], 0, 0)),
                  pl.BlockSpec((None, None, f, d), lambda v, ve, vt, lo, hi: (layer, ve[v], 0, 0))],
        out_specs=pl.BlockSpec((tm, dp), lambda v, ve, vt, lo, hi: (vt[v], 0)),
        scratch_shapes=[pltpu.VMEM((d, 2 * f), BF16), pltpu.VMEM((f, d), BF16)],
    )
    return pl.pallas_call(
        _ffn_kernel,
        grid_spec=grid_spec,
        out_shape=jax.ShapeDtypeStruct((n_rows, dp), I32),
        compiler_params=_params("arbitrary"),
        name="ffn",
    )(*meta, xs, wg, wu, wd)


def _combine_kernel(yk_ref, x1_ref, wtm_ref, p_ref, wsg_ref, wsu_ref, wsd_ref,
                    wpg_ref, wpp_ref, g_ref, b_ref, x2_ref, x2b_ref):
    tt = x1_ref.shape[0]
    xb = x1_ref[...].astype(BF16)
    hg = jnp.dot(xb, wsg_ref[...], preferred_element_type=F32)
    hu = jnp.dot(xb, wsu_ref[...], preferred_element_type=F32)
    shared = jnp.dot((hg * _sigmoid(hg) * hu).astype(BF16), wsd_ref[...], preferred_element_type=F32)
    pgate = _sigmoid(jnp.dot(xb, wpg_ref[...], preferred_element_type=F32))
    pproj = jnp.dot(p_ref[...].astype(BF16), wpp_ref[...], preferred_element_type=F32)
    rest = ALPHA * x1_ref[...] + shared + pgate * pproj

    wt = wtm_ref[...]
    r_lo = jnp.zeros((tt, yk_ref.shape[2]), F32)
    r_hi = jnp.zeros((tt, yk_ref.shape[2]), F32)
    for k in range(TOP_K):
        lo, hi = _unpack_pair(yk_ref[k])
        wk = wt[:, k:k + 1]
        r_lo = r_lo + wk * lo
        r_hi = r_hi + wk * hi
    routed = jnp.concatenate([r_lo, r_hi], axis=1)
    x2 = _layer_norm(rest + routed, g_ref[...], b_ref[...])
    x2_ref[...] = x2
    x2b_ref[...] = x2.astype(BF16)


def _combine(yk, x1, wtm, p, layer, wsg, wsu, wsd, wpg, wpp, g, b):
    m, d = x1.shape
    tt = min(COMBINE_TT, m)
    yk = yk.reshape(TOP_K, m, d // 2)

    def full(a):
        return pl.BlockSpec(a.shape, lambda i: (0,) * a.ndim)

    token_rows = pl.BlockSpec((tt, d), lambda i: (i, 0))
    return pl.pallas_call(
        _combine_kernel,
        grid=(m // tt,),
        in_specs=[pl.BlockSpec((TOP_K, tt, d // 2), lambda i: (0, i, 0)),
                  token_rows,
                  pl.BlockSpec((tt, LANES), lambda i: (i, 0)),
                  pl.BlockSpec((None, tt, p.shape[2]), lambda i: (layer, i, 0)),
                  full(wsg), full(wsu), full(wsd), full(wpg), full(wpp), full(g), full(b)],
        out_specs=[token_rows, token_rows],
        out_shape=[jax.ShapeDtypeStruct((m, d), F32), jax.ShapeDtypeStruct((m, d), BF16)],
        compiler_params=_params("parallel"),
        name="combine",
    )(yk, x1, wtm, p, wsg, wsu, wsd, wpg, wpp, g, b)


def kernel(x, p, w_in, conv_w, b_igate, b_fgate, mlstm_norm_g, w_branch_a, w_branch_b, w_out, ln1_g, ln1_b, w_router, router_bias, w_exp_gate, w_exp_up, w_exp_down, w_sh_gate, w_sh_up, w_sh_down, w_ple_gate, w_ple_proj, ln2_g, ln2_b):
    bsz, seq, d = x.shape
    m = bsz * seq
    depth = w_in.shape[0]
    hw = HEADS * HEAD_DIM
    if_lo = 3 * d + 4 * hw
    if_hi = if_lo + 2 * HEADS

    xf = x.reshape(m, d)
    xb = xf.astype(BF16)
    w_in_t = jnp.swapaxes(w_in, 1, 2)
    p = p.reshape(depth, m, -1)
    for i in range(depth):
        w_merge = w_in_t[i, if_hi:]
        w_if = jnp.pad(w_in_t[i, if_lo:if_hi], ((0, LANES - 2 * HEADS), (0, 0)))
        gate_bias = jnp.pad(jnp.concatenate([b_igate[i], b_fgate[i]]), (0, LANES - 2 * HEADS)).reshape(1, LANES)

        z = _matmul(xb, w_in_t, BF16, INPROJ_TM, INPROJ_TN, "inproj", layer=i, n=if_lo).reshape(bsz, seq, -1)
        zm = _matmul(xb, w_merge, BF16, INPROJ_TM, INPROJ_TN, "mergeproj").reshape(bsz, seq, -1)
        gates = _matmul(xb, w_if, F32, INPROJ_TM, LANES, "gateproj").reshape(bsz, seq, LANES)
        x1, x1p = _mixer(z, zm, gates, gate_bias, mlstm_norm_g[i].reshape(1, hw),
                         xf.reshape(bsz, seq, d), conv_w[i],
                         w_branch_a[i].astype(BF16), w_branch_b[i].astype(BF16), w_out[i].astype(BF16),
                         ln1_g[i].reshape(1, d), ln1_b[i].reshape(1, d))
        x1 = x1.reshape(m, d)
        x1p = x1p.reshape(m, d // 2)

        w_router_t = w_router[i].T.reshape(N_GROUPS, GROUP_SIZE, d).swapaxes(0, 1).reshape(N_EXPERTS, d)
        rb = router_bias[i].reshape(N_GROUPS, GROUP_SIZE).T.reshape(N_EXPERTS, 1)
        idx, rank, wtm, cnt = _router(x1, w_router_t, rb)
        counts = cnt[:, 0].astype(I32).reshape(GROUP_SIZE, N_GROUPS).T.reshape(N_EXPERTS)
        row_start = jnp.cumsum(counts) - counts
        expert_ids = jnp.arange(N_EXPERTS, dtype=I32)[:, None, None]
        dest = rank + jnp.sum(jnp.where(idx[None] == expert_ids, row_start[:, None, None], 0), axis=0)
        meta = _visit_metadata(counts, FFN_TM, m * TOP_K)
        xs = _dispatch(x1p, dest)
        ys = _ffn(xs, meta, w_exp_gate, w_exp_up, w_exp_down, i)
        yk = _gather_rows(ys, dest)
        xf, xb = _combine(yk, x1, wtm, p, i,
                          w_sh_gate[i].astype(BF16), w_sh_up[i].astype(BF16), w_sh_down[i].astype(BF16),
                          w_ple_gate[i].astype(BF16), w_ple_proj[i].astype(BF16),
                          ln2_g[i].reshape(1, d), ln2_b[i].reshape(1, d))
    return xf.reshape(bsz, seq, d)
```

```python
import math

import jax
import jax.numpy as jnp
from jax import lax
from jax.experimental import pallas as pl
from jax.experimental.pallas import tpu as pltpu
from jax.experimental.pallas import tpu_sc as plsc

F32 = jnp.float32
BF16 = jnp.bfloat16
U32 = jnp.uint32
I32 = jnp.int32

HEADS = 8
HEAD_DIM = 128
N_EXPERTS = 64
N_GROUPS = 8
GROUP_SIZE = N_EXPERTS // N_GROUPS
TOPK_GROUPS = 4
TOP_K = 8
ROUTED_SCALE = 2.5
DEPTH = 4
ALPHA = (2 * DEPTH) ** 0.25
LN_EPS = 1e-5
RMS_EPS = 1e-6
QK_SCALE = HEAD_DIM ** -0.5
LOG_QK_SCALE = math.log(QK_SCALE)

LANES = 128
VMEM_LIMIT = 56 * 1024 * 1024
NEG_INF = float("-inf")

MLSTM_CHUNK = 256
MLSTM_STEP_CHUNKS = 2
INPROJ_TM, INPROJ_TN = 2048, 1024
ROUTER_TT = 512
FFN_TM = 4096
FFN_SUB = 512
COMBINE_TT = 512

SC_CORES = 2
SC_SUBCORES = 16
SC_WORKERS = SC_CORES * SC_SUBCORES
SC_ROWS = 128


def _params(*sem):
    return pltpu.CompilerParams(dimension_semantics=sem, vmem_limit_bytes=VMEM_LIMIT)


def _sigmoid(x):
    return 1.0 / (1.0 + jnp.exp(-x))


def _layer_norm(r, g, b):
    mu = jnp.mean(r, axis=-1, keepdims=True)
    d = r - mu
    var = jnp.mean(d * d, axis=-1, keepdims=True)
    return d * lax.rsqrt(var + LN_EPS) * g + b


def _pack_pair(lo, hi):
    return lax.bitcast_convert_type(pltpu.pack_elementwise([lo, hi], packed_dtype=BF16), I32)


def _unpack_pair(w):
    w = lax.bitcast_convert_type(w, U32)
    lo = pltpu.unpack_elementwise(w, index=0, packed_dtype=BF16, unpacked_dtype=F32)
    hi = pltpu.unpack_elementwise(w, index=1, packed_dtype=BF16, unpacked_dtype=F32)
    return lo, hi


def _unpack_rows(w):
    lo, hi = _unpack_pair(w)
    return jnp.concatenate([lo.astype(BF16), hi.astype(BF16)], axis=1)


def _mm_kernel(x_ref, wt_ref, o_ref):
    o_ref[...] = lax.dot_general(x_ref[...], wt_ref[...].astype(BF16), (((1,), (1,)), ((), ())),
                                 preferred_element_type=F32).astype(o_ref.dtype)


def _matmul(x, wt, out_dtype, tm, tn, name, layer=None, n=None):
    m, k = x.shape
    n = wt.shape[-2] if n is None else n
    tm = min(tm, m)
    if layer is None:
        w_spec = pl.BlockSpec((tn, k), lambda i, j: (j, 0))
    else:
        w_spec = pl.BlockSpec((None, tn, k), lambda i, j: (layer, j, 0))
    return pl.pallas_call(
        _mm_kernel,
        grid=(m // tm, n // tn),
        in_specs=[pl.BlockSpec((tm, k), lambda i, j: (i, 0)), w_spec],
        out_specs=pl.BlockSpec((tm, tn), lambda i, j: (i, j)),
        out_shape=jax.ShapeDtypeStruct((m, n), out_dtype),
        compiler_params=_params("parallel", "parallel"),
        name=name,
    )(x, wt)


def _mlstm_rows(q_ref, k_ref, v_ref, o_ref, gates, ng_ref, c_scr, m_scr):
    L = MLSTM_CHUNK
    chunks = [pl.ds(j * L, L) for j in range(q_ref.shape[1] // L)]
    row = lax.broadcasted_iota(I32, (L, L), 0)
    col = lax.broadcasted_iota(I32, (L, L), 1)
    causal = col <= row
    rows = lax.broadcasted_iota(I32, (L, LANES), 0)

    def scan_rows(x, op, identity):
        step = 1
        while step < L:
            x = op(x, jnp.where(rows >= step, pltpu.roll(x, step, axis=0), identity))
            step *= 2
        return x

    gate = []
    m_prev = m_scr[...]
    for j in range(len(chunks)):
        g = gates[j * L:(j + 1) * L, :]
        ig = g
        fg = pltpu.roll(g, LANES - HEADS, axis=1)
        log_f = jnp.minimum(fg, 0.0) - jnp.log(1.0 + jnp.exp(-jnp.abs(fg)))
        b = scan_rows(log_f, jnp.add, 0.0)
        b_last = b[L - 1:L, :]
        a = b_last - b + ig
        m_loc = jnp.max(a, axis=0, keepdims=True)
        m_new = jnp.maximum(b_last + m_prev, m_loc)
        log_inter = b + m_prev
        r = ig - b
        m_out = jnp.maximum(log_inter, b + scan_rows(r, jnp.maximum, NEG_INF))
        gate.append(dict(
            w_loc=jnp.exp(a - m_loc),
            sp=jnp.exp(b_last + m_prev - m_new),
            sl=jnp.exp(m_loc - m_new),
            r_t=jnp.transpose(r),
            u=b - m_out + LOG_QK_SCALE,
            e_inter=jnp.exp(log_inter - m_out + LOG_QK_SCALE),
            e_floor=jnp.exp(-m_out)))
        m_prev = m_new
    m_scr[...] = m_prev

    ones_blk = jnp.ones((L, HEAD_DIM), BF16)
    lanes = [slice(h * HEAD_DIM, (h + 1) * HEAD_DIM) for h in range(HEADS)]
    pairs = [(j, h) for j in range(len(chunks)) for h in range(HEADS)]

    def col_of(name, j, h):
        return gate[j][name][:, h:h + 1]

    q = {(j, h): q_ref[0, chunks[j], lanes[h]] for j, h in pairs}
    k = {(j, h): k_ref[0, chunks[j], lanes[h]] for j, h in pairs}
    v_aug = {(j, h): jnp.concatenate([v_ref[0, chunks[j], lanes[h]], ones_blk], axis=1) for j, h in pairs}
    s1 = {p: lax.dot_general(q[p], k[p], (((1,), (1,)), ((), ())), preferred_element_type=F32) for p in pairs}
    kw = {(j, h): (k[j, h].astype(F32) * col_of("w_loc", j, h)).astype(BF16) for j, h in pairs}
    c_loc = {p: lax.dot_general(kw[p], v_aug[p], (((0,), (0,)), ((), ())), preferred_element_type=F32)
             for p in pairs}
    c_seen = {}
    for h in range(HEADS):
        c = c_scr[h]
        for j in range(len(chunks)):
            c_seen[j, h] = c
            c = col_of("sp", j, h) * c + col_of("sl", j, h) * c_loc[j, h]
        c_scr[h] = c
    inter = {p: jnp.dot(q[p], c_seen[p].astype(BF16), preferred_element_type=F32) for p in pairs}
    s = {(j, h): (s1[j, h] * jnp.where(causal, jnp.exp(col_of("u", j, h) + gate[j]["r_t"][h:h + 1, :]), 0.0)
                  ).astype(BF16) for j, h in pairs}
    intra = {p: jnp.dot(s[p], v_aug[p], preferred_element_type=F32) for p in pairs}
    tot = {(j, h): intra[j, h] + col_of("e_inter", j, h) * inter[j, h] for j, h in pairs}
    hh = {(j, h): tot[j, h][:, :HEAD_DIM] / jnp.maximum(jnp.abs(tot[j, h][:, HEAD_DIM:]), col_of("e_floor", j, h))
          for j, h in pairs}
    hh = {p: hh[p] * lax.rsqrt(jnp.mean(hh[p] * hh[p], axis=1, keepdims=True) + RMS_EPS) for p in pairs}
    y_b = {(j, h): (hh[j, h] * ng_ref[:, lanes[h]] * _sigmoid(o_ref[0, chunks[j], lanes[h]].astype(F32))).astype(BF16)
           for j, h in pairs}
    return [jnp.concatenate([y_b[j, h] for h in range(HEADS)], axis=1) for j in range(len(chunks))]


def _mixer_kernel(q_ref, k_ref, v_ref, o_ref, gates_ref, gb_ref, ng_ref,
                  cin_ref, cout_ref, cval_ref, mg0_ref, mg1_ref, x_ref, cw_ref,
                  wa_ref, wb_ref, wo_ref, g_ref, b_ref, x1_ref, x1p_ref, c_scr, m_scr, carry):
    @pl.when(pl.program_id(1) == 0)
    def _():
        c_scr[...] = jnp.zeros_like(c_scr)
        m_scr[...] = jnp.zeros_like(m_scr)
        carry[...] = jnp.zeros_like(carry)

    tp = MLSTM_CHUNK
    parts = [pl.ds(j * tp, tp) for j in range(x_ref.shape[1] // tp)]
    u = [cin_ref[0, pr, :].astype(F32) * cval_ref[0, pr, :].astype(F32) for pr in parts]
    prev = [carry[...]] + [uj[tp - 8:, :] for uj in u[:-1]]
    carry[...] = u[-1][tp - 8:, :]
    r8 = lax.broadcasted_iota(I32, (8, u[0].shape[1]), 0)

    def shifted(uj, pj, k):
        body = pltpu.roll(uj, k, axis=0)
        head = jnp.where(r8 < k, pltpu.roll(pj, k, axis=0), body[:8, :])
        return jnp.concatenate([head, body[8:, :]], axis=0)

    cw = cw_ref[...]
    conv = [cw[0:1, :] * shifted(uj, pj, 2) + cw[1:2, :] * shifted(uj, pj, 1) + cw[2:3, :] * uj
            for uj, pj in zip(u, prev)]
    y_a = [(cout_ref[0, pr, :].astype(F32) * cj).astype(BF16) for pr, cj in zip(parts, conv)]
    pa = [jnp.dot(yj, wa_ref[...], preferred_element_type=F32) for yj in y_a]
    gated_a = [_sigmoid(mg0_ref[0, pr, :].astype(F32)) * paj for pr, paj in zip(parts, pa)]
    y_b = _mlstm_rows(q_ref, k_ref, v_ref, o_ref, gates_ref[0] + gb_ref[...], ng_ref, c_scr, m_scr)
    pb = [jnp.dot(yj, wb_ref[...], preferred_element_type=F32) for yj in y_b]
    mixed = [(gaj + _sigmoid(mg1_ref[0, pr, :].astype(F32)) * pbj).astype(BF16)
             for pr, gaj, pbj in zip(parts, gated_a, pb)]
    hmix = [jnp.dot(mj, wo_ref[...], preferred_element_type=F32) for mj in mixed]
    for pr, hj in zip(parts, hmix):
        x1 = _layer_norm(ALPHA * x_ref[0, pr, :] + hj, g_ref[...], b_ref[...])
        x1_ref[0, pr, :] = x1
        half = x1.shape[1] // 2
        x1p_ref[0, pr, :] = _pack_pair(x1[:, :half], x1[:, half:])


def _mixer(z, zm, gates, gate_bias, norm_g, x, conv_w, wa, wb, wo, g, b):
    bsz, seq, d = x.shape
    ts = min(MLSTM_STEP_CHUNKS * MLSTM_CHUNK, seq)

    def zspec(cb):
        return pl.BlockSpec((1, ts, d), lambda i, j, cb=cb: (i, j, cb))

    def full(shape):
        return pl.BlockSpec(shape, lambda i, j: (0,) * len(shape))

    tile = pl.BlockSpec((1, ts, d), lambda i, j: (i, j, 0))
    ptile = pl.BlockSpec((1, ts, d // 2), lambda i, j: (i, j, 0))
    return pl.pallas_call(
        _mixer_kernel,
        grid=(bsz, seq // ts),
        in_specs=[zspec(3), zspec(4), zspec(5), zspec(6),
                  pl.BlockSpec((1, ts, LANES), lambda i, j: (i, j, 0)),
                  full(gate_bias.shape), full(norm_g.shape),
                  zspec(0), zspec(1), zspec(2), zspec(0), zspec(1), tile,
                  full(conv_w.shape), full(wa.shape), full(wb.shape), full(wo.shape),
                  full(g.shape), full(b.shape)],
        out_specs=[tile, ptile],
        out_shape=[jax.ShapeDtypeStruct((bsz, seq, d), F32),
                   jax.ShapeDtypeStruct((bsz, seq, d // 2), I32)],
        scratch_shapes=[pltpu.VMEM((HEADS, HEAD_DIM, 2 * HEAD_DIM), F32),
                        pltpu.VMEM((1, LANES), F32),
                        pltpu.VMEM((8, d), F32)],
        compiler_params=_params("parallel", "arbitrary"),
        name="mixer",
    )(z, z, z, z, gates, gate_bias, norm_g, z, z, z, zm, zm, x, conv_w, wa, wb, wo, g, b)


def _router_kernel(x_ref, wrt_ref, rb_ref, idx_ref, rank_ref, wtm_ref, cnt_ref, carry):
    t = x_ref.shape[0]

    @pl.when(pl.program_id(0) == 0)
    def _():
        carry[...] = jnp.zeros_like(carry)

    def split(a):
        hi = a.astype(BF16)
        return hi, (a - hi.astype(F32)).astype(BF16)

    def dot_t(a, b):
        return lax.dot_general(a, b, (((1,), (1,)), ((), ())), preferred_element_type=F32)

    w_hi, w_lo = split(wrt_ref[...])
    x_hi, x_lo = split(x_ref[...])
    logits = dot_t(w_hi, x_hi) + (dot_t(w_hi, x_lo) + dot_t(w_lo, x_hi))
    scores = _sigmoid(logits)
    shape3 = (GROUP_SIZE, N_GROUPS, t)
    sel = (scores + rb_ref[...]).reshape(shape3)
    scores = scores.reshape(shape3)
    mem = lax.broadcasted_iota(I32, shape3, 0)
    grp = lax.broadcasted_iota(I32, shape3, 1)
    eidx = grp * GROUP_SIZE + mem
    m1 = jnp.max(sel, axis=0, keepdims=True)
    first = jnp.min(jnp.where(sel == m1, mem, GROUP_SIZE), axis=0, keepdims=True)
    m2 = jnp.max(jnp.where(mem == first, NEG_INF, sel), axis=0, keepdims=True)
    rem = m1 + m2
    gidx = lax.broadcasted_iota(I32, rem.shape, 1)
    gmask = jnp.zeros(rem.shape, F32)
    for _ in range(TOPK_GROUPS):
        mx = jnp.max(rem, axis=1, keepdims=True)
        pick = gidx == jnp.min(jnp.where(rem == mx, gidx, N_GROUPS), axis=1, keepdims=True)
        gmask = jnp.where(pick, 1.0, gmask)
        rem = jnp.where(pick, NEG_INF, rem)
    masked = jnp.where(jnp.broadcast_to(gmask, shape3) > 0.5, sel, NEG_INF)
    chosen = jnp.zeros(shape3, F32)
    picks = []
    for _ in range(TOP_K):
        mx = jnp.max(jnp.max(masked, axis=0, keepdims=True), axis=1, keepdims=True)
        cand = jnp.where(masked == mx, eidx, N_EXPERTS)
        fi = jnp.min(jnp.min(cand, axis=0, keepdims=True), axis=1, keepdims=True)
        pick = eidx == fi
        picks.append((fi, pick))
        chosen = jnp.where(pick, 1.0, chosen)
        masked = jnp.where(pick, NEG_INF, masked)
    w = chosen * scores
    denom = jnp.sum(jnp.sum(w, axis=0, keepdims=True), axis=1, keepdims=True)
    gate3 = w / denom * ROUTED_SCALE

    chosen2 = chosen.reshape(N_EXPERTS, t).astype(BF16)
    tok_r = lax.broadcasted_iota(I32, (t, t), 0)
    tok_c = lax.broadcasted_iota(I32, (t, t), 1)
    before = jnp.where(tok_r < tok_c, 1.0, 0.0).astype(BF16)
    prefix = (jnp.dot(chosen2, before, preferred_element_type=F32) + carry[...]).reshape(shape3)
    carry[...] += jnp.dot(chosen2, jnp.ones((t, t), BF16), preferred_element_type=F32)
    cnt_ref[...] = carry[:, :LANES]

    def pick_sum(pick, val):
        return jnp.sum(jnp.sum(jnp.where(pick, val, 0.0), axis=0, keepdims=True), axis=1, keepdims=True)

    sub8 = lax.broadcasted_iota(I32, (TOP_K, t), 0)
    idx8 = jnp.zeros((TOP_K, t), I32)
    rank8 = jnp.zeros((TOP_K, t), F32)
    w8 = jnp.zeros((TOP_K, t), F32)
    for k, (fi, pick) in enumerate(picks):
        idx8 = jnp.where(sub8 == k, jnp.broadcast_to(fi.reshape(1, t), (TOP_K, t)), idx8)
        rank8 = jnp.where(sub8 == k, jnp.broadcast_to(pick_sum(pick, prefix).reshape(1, t), (TOP_K, t)), rank8)
        w8 = jnp.where(sub8 == k, jnp.broadcast_to(pick_sum(pick, gate3).reshape(1, t), (TOP_K, t)), w8)
    idx_ref[...] = idx8
    rank_ref[...] = rank8.astype(I32)
    pad = jnp.zeros((LANES - TOP_K, t), F32)
    wtm_ref[...] = jnp.transpose(jnp.concatenate([w8, pad], axis=0))


def _router(x1, w_router_t, router_bias):
    m, d = x1.shape
    tt = min(ROUTER_TT, m)
    kt = pl.BlockSpec((TOP_K, tt), lambda i: (0, i))
    return pl.pallas_call(
        _router_kernel,
        grid=(m // tt,),
        in_specs=[pl.BlockSpec((tt, d), lambda i: (i, 0)),
                  pl.BlockSpec((N_EXPERTS, d), lambda i: (0, 0)),
                  pl.BlockSpec((N_EXPERTS, 1), lambda i: (0, 0))],
        out_specs=[kt, kt,
                   pl.BlockSpec((tt, LANES), lambda i: (i, 0)),
                   pl.BlockSpec((N_EXPERTS, LANES), lambda i: (0, 0))],
        out_shape=[jax.ShapeDtypeStruct((TOP_K, m), I32),
                   jax.ShapeDtypeStruct((TOP_K, m), I32),
                   jax.ShapeDtypeStruct((m, LANES), F32),
                   jax.ShapeDtypeStruct((N_EXPERTS, LANES), F32)],
        scratch_shapes=[pltpu.VMEM((N_EXPERTS, tt), F32)],
        compiler_params=_params("arbitrary"),
        name="router",
    )(x1, w_router_t, router_bias)


def _visit_metadata(counts, tm, n_rows):
    nt = n_rows // tm
    nv = nt + N_EXPERTS - 1
    ends = jnp.cumsum(counts)
    starts = ends - counts
    first_tile = starts // tm
    ntiles = jnp.where(counts > 0, (ends - 1) // tm - first_tile + 1, 0)
    vend = jnp.cumsum(ntiles)
    vstart = vend - ntiles
    v = jnp.arange(nv, dtype=I32)
    valid = v < vend[-1]
    ve = jnp.minimum(jnp.sum((v[:, None] >= vend[None, :]).astype(I32), axis=1), N_EXPERTS - 1)
    ve = jnp.where(valid, ve, ve[jnp.maximum(vend[-1] - 1, 0)])
    vt = jnp.where(valid, first_tile[ve] + v - vstart[ve], nt - 1)
    lo = jnp.where(valid, jnp.clip(starts[ve] - vt * tm, 0, tm), 0)
    hi = jnp.where(valid, jnp.clip(ends[ve] - vt * tm, 0, tm), 0)
    return ve.astype(I32), vt.astype(I32), lo.astype(I32), hi.astype(I32)


def _sc_mesh():
    return plsc.VectorSubcoreMesh(core_axis_name="c", subcore_axis_name="s")


def _sc_worker():
    return lax.axis_index("s") * SC_CORES + lax.axis_index("c")


def _dispatch(x1p, dest):
    m, dp = x1p.shape
    blocks = m // SC_ROWS
    per_worker = blocks // SC_WORKERS
    table = dest.reshape(TOP_K, blocks, SC_ROWS).transpose(1, 0, 2).reshape(blocks * TOP_K, SC_ROWS)

    def body(x_hbm, idx_hbm, xs_hbm, idx_v, rows_v, sem):
        first = _sc_worker() * per_worker

        @pl.loop(0, per_worker)
        def _(j):
            b = first + j
            pltpu.sync_copy(idx_hbm.at[pl.ds(pl.multiple_of(b * TOP_K, TOP_K), TOP_K)], idx_v)
            pltpu.sync_copy(x_hbm.at[pl.ds(pl.multiple_of(b * SC_ROWS, SC_ROWS), SC_ROWS)], rows_v)
            copies = [pltpu.async_copy(rows_v, xs_hbm.at[idx_v.at[k]], sem) for k in range(TOP_K)]
            for cp in copies:
                cp.wait()

    return pl.kernel(
        body,
        out_type=jax.ShapeDtypeStruct((m * TOP_K, dp), x1p.dtype),
        mesh=_sc_mesh(),
        scratch_types=[pltpu.VMEM((TOP_K, SC_ROWS), I32), pltpu.VMEM((SC_ROWS, dp), x1p.dtype),
                       pltpu.SemaphoreType.DMA],
        name="dispatch",
    )(x1p, table)


def _gather_rows(ys, dest):
    n_rows, dp = ys.shape
    rows = SC_ROWS // 2
    idx = dest.reshape(n_rows // rows, rows)
    per_worker = n_rows // rows // SC_WORKERS

    def body(ys_hbm, idx_hbm, out_hbm, idx_v, rows_v, sems):
        first = _sc_worker() * per_worker

        def gather(slot):
            return pltpu.make_async_copy(ys_hbm.at[idx_v.at[slot]], rows_v.at[slot], sems.at[slot])

        def start(slot, b):
            pltpu.sync_copy(idx_hbm.at[pl.ds(b, 1)], idx_v.at[pl.ds(slot, 1)])
            gather(slot).start()

        def finish(slot, b):
            gather(slot).wait()
            pltpu.sync_copy(rows_v.at[slot], out_hbm.at[pl.ds(pl.multiple_of(b * rows, rows), rows)])

        start(0, first)

        @pl.loop(0, per_worker, step=2)
        def _(j):
            b = first + j
            start(1, b + 1)
            finish(0, b)

            @pl.when(j + 2 < per_worker)
            def _():
                start(0, b + 2)

            finish(1, b + 1)

    return pl.kernel(
        body,
        out_type=jax.ShapeDtypeStruct((n_rows, dp), ys.dtype),
        mesh=_sc_mesh(),
        scratch_types=[pltpu.VMEM((2, rows), I32), pltpu.VMEM((2, rows, dp), ys.dtype),
                       pltpu.SemaphoreType.DMA((2,))],
        name="gather_rows",
    )(ys, idx)


def _ffn_kernel(ve_ref, vt_ref, lo_ref, hi_ref, xs_ref, wg_ref, wu_ref, wd_ref, ys_ref,
                wgu_b, wd_b):
    v = pl.program_id(0)
    lo = lo_ref[v]
    hi = hi_ref[v]
    tm = xs_ref.shape[0]
    f = wg_ref.shape[1]

    @pl.when((v == 0) | (ve_ref[v] != ve_ref[jnp.maximum(v - 1, 0)]))
    def _():
        wgu_b[:, :f] = wg_ref[...].astype(BF16)
        wgu_b[:, f:] = wu_ref[...].astype(BF16)
        wd_b[...] = wd_ref[...].astype(BF16)

    def pack_rows(a):
        half = a.shape[1] // 2
        return _pack_pair(a[:, :half], a[:, half:])

    for sb in range(tm // FFN_SUB):
        s0 = sb * FFN_SUB
        lo_s = jnp.clip(lo - s0, 0, FFN_SUB)
        hi_s = jnp.clip(hi - s0, 0, FFN_SUB)

        half = FFN_SUB // 2
        halves = [pl.ds(s0 + j * half, half) for j in range(2)]

        def expert_rows():
            x = [_unpack_rows(xs_ref[hr, :]) for hr in halves]
            h2 = [jnp.dot(xj, wgu_b[...], preferred_element_type=F32) for xj in x]
            act = [(hj[:, :f] * _sigmoid(hj[:, :f]) * hj[:, f:]).astype(BF16) for hj in h2]
            y = [jnp.dot(aj, wd_b[...], preferred_element_type=F32) for aj in act]
            return [pack_rows(yj) for yj in y]

        @pl.when((hi_s > lo_s) & (lo_s == 0))
        def _():
            for hr, yj in zip(halves, expert_rows()):
                ys_ref[hr, :] = yj

        @pl.when((hi_s > lo_s) & (lo_s > 0))
        def _():
            for j, (hr, yj) in enumerate(zip(halves, expert_rows())):
                r = lax.broadcasted_iota(I32, yj.shape, 0) + j * half
                mine = (r >= lo_s) & (r < hi_s)
                ys_ref[hr, :] = jnp.where(mine, yj, ys_ref[hr, :])


def _ffn(xs, meta, wg, wu, wd, layer):
    n_rows, dp = xs.shape
    _, n_e, d, f = wg.shape
    tm = FFN_TM
    nv = n_rows // tm + N_EXPERTS - 1
    grid_spec = pltpu.PrefetchScalarGridSpec(
        num_scalar_prefetch=4,
        grid=(nv,),
        in_specs=[pl.BlockSpec((tm, dp), lambda v, ve, vt, lo, hi: (vt[v], 0)),
                  pl.BlockSpec((None, None, d, f), lambda v, ve, vt, lo, hi: (layer, ve[v], 0, 0)),
                  pl.BlockSpec((None, None, d, f), lambda v, ve, vt, lo, hi: (layer, ve[v], 0, 0)),
                  pl.BlockSpec((None, None, f, d), lambda v, ve, vt, lo, hi: (layer, ve[v], 0, 0))],
        out_specs=pl.BlockSpec((tm, dp), lambda v, ve, vt, lo, hi: (vt[v], 0)),
        scratch_shapes=[pltpu.VMEM((d, 2 * f), BF16), pltpu.VMEM((f, d), BF16)],
    )
    return pl.pallas_call(
        _ffn_kernel,
        grid_spec=grid_spec,
        out_shape=jax.ShapeDtypeStruct((n_rows, dp), I32),
        compiler_params=_params("arbitrary"),
        name="ffn",
    )(*meta, xs, wg, wu, wd)


def _combine_kernel(yk_ref, x1_ref, wtm_ref, p_ref, wsg_ref, wsu_ref, wsd_ref,
                    wpg_ref, wpp_ref, g_ref, b_ref, x2_ref, x2b_ref):
    tt = x1_ref.shape[0]
    xb = x1_ref[...].astype(BF16)
    hg = jnp.dot(xb, wsg_ref[...], preferred_element_type=F32)
    hu = jnp.dot(xb, wsu_ref[...], preferred_element_type=F32)
    shared = jnp.dot((hg * _sigmoid(hg) * hu).astype(BF16), wsd_ref[...], preferred_element_type=F32)
    pgate = _sigmoid(jnp.dot(xb, wpg_ref[...], preferred_element_type=F32))
    pproj = jnp.dot(p_ref[...].astype(BF16), wpp_ref[...], preferred_element_type=F32)
    rest = ALPHA * x1_ref[...] + shared + pgate * pproj

    wt = wtm_ref[...]
    r_lo = jnp.zeros((tt, yk_ref.shape[2]), F32)
    r_hi = jnp.zeros((tt, yk_ref.shape[2]), F32)
    for k in range(TOP_K):
        lo, hi = _unpack_pair(yk_ref[k])
        wk = wt[:, k:k + 1]
        r_lo = r_lo + wk * lo
        r_hi = r_hi + wk * hi
    routed = jnp.concatenate([r_lo, r_hi], axis=1)
    x2 = _layer_norm(rest + routed, g_ref[...], b_ref[...])
    x2_ref[...] = x2
    x2b_ref[...] = x2.astype(BF16)


def _combine(yk, x1, wtm, p, layer, wsg, wsu, wsd, wpg, wpp, g, b):
    m, d = x1.shape
    tt = min(COMBINE_TT, m)
    yk = yk.reshape(TOP_K, m, d // 2)

    def full(a):
        return pl.BlockSpec(a.shape, lambda i: (0,) * a.ndim)

    token_rows = pl.BlockSpec((tt, d), lambda i: (i, 0))
    return pl.pallas_call(
        _combine_kernel,
        grid=(m // tt,),
        in_specs=[pl.BlockSpec((TOP_K, tt, d // 2), lambda i: (0, i, 0)),
                  token_rows,
                  pl.BlockSpec((tt, LANES), lambda i: (i, 0)),
                  pl.BlockSpec((None, tt, p.shape[2]), lambda i: (layer, i, 0)),
                  full(wsg), full(wsu), full(wsd), full(wpg), full(wpp), full(g), full(b)],
        out_specs=[token_rows, token_rows],
        out_shape=[jax.ShapeDtypeStruct((m, d), F32), jax.ShapeDtypeStruct((m, d), BF16)],
        compiler_params=_params("parallel"),
        name="combine",
    )(yk, x1, wtm, p, wsg, wsu, wsd, wpg, wpp, g, b)


def kernel(x, p, w_in, conv_w, b_igate, b_fgate, mlstm_norm_g, w_branch_a, w_branch_b, w_out, ln1_g, ln1_b, w_router, router_bias, w_exp_gate, w_exp_up, w_exp_down, w_sh_gate, w_sh_up, w_sh_down, w_ple_gate, w_ple_proj, ln2_g, ln2_b):
    bsz, seq, d = x.shape
    m = bsz * seq
    depth = w_in.shape[0]
    hw = HEADS * HEAD_DIM
    if_lo = 3 * d + 4 * hw
    if_hi = if_lo + 2 * HEADS

    xf = x.reshape(m, d)
    xb = xf.astype(BF16)
    w_in_t = jnp.swapaxes(w_in, 1, 2)
    p = p.reshape(depth, m, -1)
    for i in range(depth):
        w_merge = w_in_t[i, if_hi:]
        w_if = jnp.pad(w_in_t[i, if_lo:if_hi], ((0, LANES - 2 * HEADS), (0, 0)))
        gate_bias = jnp.pad(jnp.concatenate([b_igate[i], b_fgate[i]]), (0, LANES - 2 * HEADS)).reshape(1, LANES)

        z = _matmul(xb, w_in_t, BF16, INPROJ_TM, INPROJ_TN, "inproj", layer=i, n=if_lo).reshape(bsz, seq, -1)
        zm = _matmul(xb, w_merge, BF16, INPROJ_TM, INPROJ_TN, "mergeproj").reshape(bsz, seq, -1)
        gates = _matmul(xb, w_if, F32, INPROJ_TM, LANES, "gateproj").reshape(bsz, seq, LANES)
        x1, x1p = _mixer(z, zm, gates, gate_bias, mlstm_norm_g[i].reshape(1, hw),
                         xf.reshape(bsz, seq, d), conv_w[i],
                         w_branch_a[i].astype(BF16), w_branch_b[i].astype(BF16), w_out[i].astype(BF16),
                         ln1_g[i].reshape(1, d), ln1_b[i].reshape(1, d))
        x1 = x1.reshape(m, d)
        x1p = x1p.reshape(m, d // 2)

        w_router_t = w_router[i].T.reshape(N_GROUPS, GROUP_SIZE, d).swapaxes(0, 1).reshape(N_EXPERTS, d)
        rb = router_bias[i].reshape(N_GROUPS, GROUP_SIZE).T.reshape(N_EXPERTS, 1)
        idx, rank, wtm, cnt = _router(x1, w_router_t, rb)
        counts = cnt[:, 0].astype(I32).reshape(GROUP_SIZE, N_GROUPS).T.reshape(N_EXPERTS)
        row_start = jnp.cumsum(counts) - counts
        expert_ids = jnp.arange(N_EXPERTS, dtype=I32)[:, None, None]
        dest = rank + jnp.sum(jnp.where(idx[None] == expert_ids, row_start[:, None, None], 0), axis=0)
        meta = _visit_metadata(counts, FFN_TM, m * TOP_K)
        xs = _dispatch(x1p, dest)
        ys = _ffn(xs, meta, w_exp_gate, w_exp_up, w_exp_down, i)
        yk = _gather_rows(ys, dest)
        xf, xb = _combine(yk, x1, wtm, p, i,
                          w_sh_gate[i].astype(BF16), w_sh_up[i].astype(BF16), w_sh_down[i].astype(BF16),
                          w_ple_gate[i].astype(BF16), w_ple_proj[i].astype(BF16),
                          ln2_g[i].reshape(1, d), ln2_b[i].reshape(1, d))
    return xf.reshape(bsz, seq, d)
```

```python
import functools
import math

import jax
import jax.numpy as jnp
from jax import lax
from jax.experimental import pallas as pl
from jax.experimental.pallas import tpu as pltpu
from jax.experimental.pallas import tpu_sc as plsc

F32 = jnp.float32
BF16 = jnp.bfloat16
U32 = jnp.uint32
I32 = jnp.int32

HEADS = 8
HEAD_DIM = 128
N_EXPERTS = 64
N_GROUPS = 8
GROUP_SIZE = N_EXPERTS // N_GROUPS
TOPK_GROUPS = 4
TOP_K = 8
ROUTED_SCALE = 2.5
DEPTH = 4
ALPHA = (2 * DEPTH) ** 0.25
LN_EPS = 1e-5
RMS_EPS = 1e-6
QK_SCALE = HEAD_DIM ** -0.5
LOG_QK_SCALE = math.log(QK_SCALE)

LANES = 128
VMEM_LIMIT = 56 * 1024 * 1024
NEG_INF = float("-inf")

MLSTM_CHUNK = 256
MLSTM_STEP_CHUNKS = 2
INPROJ_TM, INPROJ_TN = 2048, 1024
ROUTER_TT = 512
FFN_TM = 2048
FFN_SUB = 512
COMBINE_TT = 512

SC_CORES = 2
SC_SUBCORES = 16
SC_WORKERS = SC_CORES * SC_SUBCORES
SC_ROWS = 128


def _params(*sem):
    return pltpu.CompilerParams(dimension_semantics=sem, vmem_limit_bytes=VMEM_LIMIT)


def _sigmoid(x):
    return 1.0 / (1.0 + jnp.exp(-x))


def _layer_norm(r, g, b):
    mu = jnp.mean(r, axis=-1, keepdims=True)
    d = r - mu
    var = jnp.mean(d * d, axis=-1, keepdims=True)
    return d * lax.rsqrt(var + LN_EPS) * g + b


def _pack_pair(lo, hi):
    return lax.bitcast_convert_type(pltpu.pack_elementwise([lo, hi], packed_dtype=BF16), I32)


def _unpack_pair(w):
    w = lax.bitcast_convert_type(w, U32)
    lo = pltpu.unpack_elementwise(w, index=0, packed_dtype=BF16, unpacked_dtype=F32)
    hi = pltpu.unpack_elementwise(w, index=1, packed_dtype=BF16, unpacked_dtype=F32)
    return lo, hi


def _unpack_rows(w):
    lo, hi = _unpack_pair(w)
    return jnp.concatenate([lo.astype(BF16), hi.astype(BF16)], axis=1)


def _inproj_kernel(x_ref, wmain_ref, wmerge_ref, wif_ref, z_ref, zm_ref, g_ref, *, n_main, n_merge):
    j = pl.program_id(1)

    def project(w_ref):
        return lax.dot_general(x_ref[...], w_ref[...].astype(BF16), (((1,), (1,)), ((), ())),
                               preferred_element_type=F32)

    @pl.when(j < n_main)
    def _():
        z_ref[...] = project(wmain_ref).astype(z_ref.dtype)

    @pl.when((j >= n_main) & (j < n_main + n_merge))
    def _():
        zm_ref[...] = project(wmerge_ref).astype(zm_ref.dtype)

    @pl.when(j == n_main + n_merge)
    def _():
        g_ref[...] = project(wif_ref)


def _inproj(x, w_in_t, layer, n, w_merge, w_if):
    m, k = x.shape
    tm, tn = min(INPROJ_TM, m), INPROJ_TN
    n_main, n_merge = n // tn, w_merge.shape[0] // tn
    last_main, last_merge = n_main - 1, n_merge - 1
    return pl.pallas_call(
        functools.partial(_inproj_kernel, n_main=n_main, n_merge=n_merge),
        grid=(m // tm, n_main + n_merge + 1),
        in_specs=[pl.BlockSpec((tm, k), lambda i, j: (i, 0)),
                  pl.BlockSpec((None, tn, k), lambda i, j: (layer, jnp.minimum(j, last_main), 0)),
                  pl.BlockSpec((tn, k), lambda i, j: (jnp.clip(j - n_main, 0, last_merge), 0)),
                  pl.BlockSpec((LANES, k), lambda i, j: (0, 0))],
        out_specs=[pl.BlockSpec((tm, tn), lambda i, j: (i, jnp.minimum(j, last_main))),
                   pl.BlockSpec((tm, tn), lambda i, j: (i, jnp.clip(j - n_main, 0, last_merge))),
                   pl.BlockSpec((tm, LANES), lambda i, j: (i, 0))],
        out_shape=[jax.ShapeDtypeStruct((m, n), BF16),
                   jax.ShapeDtypeStruct((m, w_merge.shape[0]), BF16),
                   jax.ShapeDtypeStruct((m, LANES), F32)],
        compiler_params=_params("parallel", "arbitrary"),
        name="inproj",
    )(x, w_in_t, w_merge, w_if)


def _mlstm_rows(q_ref, k_ref, v_ref, o_ref, gates, ng_ref, c_scr, m_scr):
    L = MLSTM_CHUNK
    chunks = [pl.ds(j * L, L) for j in range(q_ref.shape[1] // L)]
    row = lax.broadcasted_iota(I32, (L, L), 0)
    col = lax.broadcasted_iota(I32, (L, L), 1)
    causal = col <= row
    rows = lax.broadcasted_iota(I32, (L, LANES), 0)

    def scan_rows(x, op, identity):
        step = 1
        while step < L:
            x = op(x, jnp.where(rows >= step, pltpu.roll(x, step, axis=0), identity))
            step *= 2
        return x

    gate = []
    m_prev = m_scr[...]
    for j in range(len(chunks)):
        g = gates[j * L:(j + 1) * L, :]
        ig = g
        fg = pltpu.roll(g, LANES - HEADS, axis=1)
        log_f = jnp.minimum(fg, 0.0) - jnp.log(1.0 + jnp.exp(-jnp.abs(fg)))
        b = scan_rows(log_f, jnp.add, 0.0)
        b_last = b[L - 1:L, :]
        a = b_last - b + ig
        m_loc = jnp.max(a, axis=0, keepdims=True)
        m_new = jnp.maximum(b_last + m_prev, m_loc)
        log_inter = b + m_prev
        r = ig - b
        m_out = jnp.maximum(log_inter, b + scan_rows(r, jnp.maximum, NEG_INF))
        gate.append(dict(
            w_loc=jnp.exp(a - m_loc),
            sp=jnp.exp(b_last + m_prev - m_new),
            sl=jnp.exp(m_loc - m_new),
            r_t=jnp.transpose(r),
            u=b - m_out + LOG_QK_SCALE,
            e_inter=jnp.exp(log_inter - m_out + LOG_QK_SCALE),
            e_floor=jnp.exp(-m_out)))
        m_prev = m_new
    m_scr[...] = m_prev

    ones_blk = jnp.ones((L, HEAD_DIM), BF16)
    lanes = [slice(h * HEAD_DIM, (h + 1) * HEAD_DIM) for h in range(HEADS)]
    pairs = [(j, h) for j in range(len(chunks)) for h in range(HEADS)]

    def col_of(name, j, h):
        return gate[j][name][:, h:h + 1]

    q = {(j, h): q_ref[0, chunks[j], lanes[h]] for j, h in pairs}
    k = {(j, h): k_ref[0, chunks[j], lanes[h]] for j, h in pairs}
    v_aug = {(j, h): jnp.concatenate([v_ref[0, chunks[j], lanes[h]], ones_blk], axis=1) for j, h in pairs}
    s1 = {p: lax.dot_general(q[p], k[p], (((1,), (1,)), ((), ())), preferred_element_type=F32) for p in pairs}
    kw = {(j, h): (k[j, h].astype(F32) * col_of("w_loc", j, h)).astype(BF16) for j, h in pairs}
    c_loc = {p: lax.dot_general(kw[p], v_aug[p], (((0,), (0,)), ((), ())), preferred_element_type=F32)
             for p in pairs}
    c_seen = {}
    for h in range(HEADS):
        c = c_scr[h]
        for j in range(len(chunks)):
            c_seen[j, h] = c
            c = col_of("sp", j, h) * c + col_of("sl", j, h) * c_loc[j, h]
        c_scr[h] = c
    inter = {p: jnp.dot(q[p], c_seen[p].astype(BF16), preferred_element_type=F32) for p in pairs}
    s = {(j, h): (s1[j, h] * jnp.where(causal, jnp.exp(col_of("u", j, h) + gate[j]["r_t"][h:h + 1, :]), 0.0)
                  ).astype(BF16) for j, h in pairs}
    intra = {p: jnp.dot(s[p], v_aug[p], preferred_element_type=F32) for p in pairs}
    tot = {(j, h): intra[j, h] + col_of("e_inter", j, h) * inter[j, h] for j, h in pairs}
    hh = {(j, h): tot[j, h][:, :HEAD_DIM] / jnp.maximum(jnp.abs(tot[j, h][:, HEAD_DIM:]), col_of("e_floor", j, h))
          for j, h in pairs}
    hh = {p: hh[p] * lax.rsqrt(jnp.mean(hh[p] * hh[p], axis=1, keepdims=True) + RMS_EPS) for p in pairs}
    y_b = {(j, h): (hh[j, h] * ng_ref[:, lanes[h]] * _sigmoid(o_ref[0, chunks[j], lanes[h]].astype(F32))).astype(BF16)
           for j, h in pairs}
    return [jnp.concatenate([y_b[j, h] for h in range(HEADS)], axis=1) for j in range(len(chunks))]


def _mixer_kernel(q_ref, k_ref, v_ref, o_ref, gates_ref, gb_ref, ng_ref,
                  cin_ref, cout_ref, cval_ref, mg0_ref, mg1_ref, x_ref, cw_ref,
                  wa_ref, wb_ref, wo_ref, g_ref, b_ref, x1_ref, x1p_ref, c_scr, m_scr, carry):
    @pl.when(pl.program_id(1) == 0)
    def _():
        c_scr[...] = jnp.zeros_like(c_scr)
        m_scr[...] = jnp.zeros_like(m_scr)
        carry[...] = jnp.zeros_like(carry)

    tp = MLSTM_CHUNK
    parts = [pl.ds(j * tp, tp) for j in range(x_ref.shape[1] // tp)]
    u = [cin_ref[0, pr, :].astype(F32) * cval_ref[0, pr, :].astype(F32) for pr in parts]
    prev = [carry[...]] + [uj[tp - 8:, :] for uj in u[:-1]]
    carry[...] = u[-1][tp - 8:, :]
    r8 = lax.broadcasted_iota(I32, (8, u[0].shape[1]), 0)

    def shifted(uj, pj, k):
        body = pltpu.roll(uj, k, axis=0)
        head = jnp.where(r8 < k, pltpu.roll(pj, k, axis=0), body[:8, :])
        return jnp.concatenate([head, body[8:, :]], axis=0)

    cw = cw_ref[...]
    conv = [cw[0:1, :] * shifted(uj, pj, 2) + cw[1:2, :] * shifted(uj, pj, 1) + cw[2:3, :] * uj
            for uj, pj in zip(u, prev)]
    y_a = [(cout_ref[0, pr, :].astype(F32) * cj).astype(BF16) for pr, cj in zip(parts, conv)]
    pa = [jnp.dot(yj, wa_ref[...], preferred_element_type=F32) for yj in y_a]
    gated_a = [_sigmoid(mg0_ref[0, pr, :].astype(F32)) * paj for pr, paj in zip(parts, pa)]
    y_b = _mlstm_rows(q_ref, k_ref, v_ref, o_ref, gates_ref[0] + gb_ref[...], ng_ref, c_scr, m_scr)
    pb = [jnp.dot(yj, wb_ref[...], preferred_element_type=F32) for yj in y_b]
    mixed = [(gaj + _sigmoid(mg1_ref[0, pr, :].astype(F32)) * pbj).astype(BF16)
             for pr, gaj, pbj in zip(parts, gated_a, pb)]
    hmix = [jnp.dot(mj, wo_ref[...], preferred_element_type=F32) for mj in mixed]
    for pr, hj in zip(parts, hmix):
        x1 = _layer_norm(ALPHA * x_ref[0, pr, :] + hj, g_ref[...], b_ref[...])
        x1_ref[0, pr, :] = x1
        half = x1.shape[1] // 2
        x1p_ref[0, pr, :] = _pack_pair(x1[:, :half], x1[:, half:])


def _mixer(z, zm, gates, gate_bias, norm_g, x, conv_w, wa, wb, wo, g, b):
    bsz, seq, d = x.shape
    ts = min(MLSTM_STEP_CHUNKS * MLSTM_CHUNK, seq)

    def zspec(cb):
        return pl.BlockSpec((1, ts, d), lambda i, j, cb=cb: (i, j, cb))

    def full(shape):
        return pl.BlockSpec(shape, lambda i, j: (0,) * len(shape))

    tile = pl.BlockSpec((1, ts, d), lambda i, j: (i, j, 0))
    ptile = pl.BlockSpec((1, ts, d // 2), lambda i, j: (i, j, 0))
    return pl.pallas_call(
        _mixer_kernel,
        grid=(bsz, seq // ts),
        in_specs=[zspec(3), zspec(4), zspec(5), zspec(6),
                  pl.BlockSpec((1, ts, LANES), lambda i, j: (i, j, 0)),
                  full(gate_bias.shape), full(norm_g.shape),
                  zspec(0), zspec(1), zspec(2), zspec(0), zspec(1), tile,
                  full(conv_w.shape), full(wa.shape), full(wb.shape), full(wo.shape),
                  full(g.shape), full(b.shape)],
        out_specs=[tile, ptile],
        out_shape=[jax.ShapeDtypeStruct((bsz, seq, d), F32),
                   jax.ShapeDtypeStruct((bsz, seq, d // 2), I32)],
        scratch_shapes=[pltpu.VMEM((HEADS, HEAD_DIM, 2 * HEAD_DIM), F32),
                        pltpu.VMEM((1, LANES), F32),
                        pltpu.VMEM((8, d), F32)],
        compiler_params=_params("parallel", "arbitrary"),
        name="mixer",
    )(z, z, z, z, gates, gate_bias, norm_g, z, z, z, zm, zm, x, conv_w, wa, wb, wo, g, b)


def _router_kernel(x_ref, wrt_ref, rb_ref, idx_ref, rank_ref, wtm_ref, cnt_ref, carry):
    t = x_ref.shape[0]

    @pl.when(pl.program_id(0) == 0)
    def _():
        carry[...] = jnp.zeros_like(carry)

    def split(a):
        hi = a.astype(BF16)
        return hi, (a - hi.astype(F32)).astype(BF16)

    def dot_t(a, b):
        return lax.dot_general(a, b, (((1,), (1,)), ((), ())), preferred_element_type=F32)

    w_hi, w_lo = split(wrt_ref[...])
    x_hi, x_lo = split(x_ref[...])
    logits = dot_t(w_hi, x_hi) + (dot_t(w_hi, x_lo) + dot_t(w_lo, x_hi))
    scores = _sigmoid(logits)
    shape3 = (GROUP_SIZE, N_GROUPS, t)
    sel = (scores + rb_ref[...]).reshape(shape3)
    scores = scores.reshape(shape3)
    mem = lax.broadcasted_iota(I32, shape3, 0)
    grp = lax.broadcasted_iota(I32, shape3, 1)
    eidx = grp * GROUP_SIZE + mem
    m1 = jnp.max(sel, axis=0, keepdims=True)
    first = jnp.min(jnp.where(sel == m1, mem, GROUP_SIZE), axis=0, keepdims=True)
    m2 = jnp.max(jnp.where(mem == first, NEG_INF, sel), axis=0, keepdims=True)
    rem = m1 + m2
    gidx = lax.broadcasted_iota(I32, rem.shape, 1)
    gmask = jnp.zeros(rem.shape, F32)
    for _ in range(TOPK_GROUPS):
        mx = jnp.max(rem, axis=1, keepdims=True)
        pick = gidx == jnp.min(jnp.where(rem == mx, gidx, N_GROUPS), axis=1, keepdims=True)
        gmask = jnp.where(pick, 1.0, gmask)
        rem = jnp.where(pick, NEG_INF, rem)
    masked = jnp.where(jnp.broadcast_to(gmask, shape3) > 0.5, sel, NEG_INF)
    chosen = jnp.zeros(shape3, F32)
    picks = []
    for _ in range(TOP_K):
        mx = jnp.max(jnp.max(masked, axis=0, keepdims=True), axis=1, keepdims=True)
        cand = jnp.where(masked == mx, eidx, N_EXPERTS)
        fi = jnp.min(jnp.min(cand, axis=0, keepdims=True), axis=1, keepdims=True)
        pick = eidx == fi
        picks.append((fi, pick))
        chosen = jnp.where(pick, 1.0, chosen)
        masked = jnp.where(pick, NEG_INF, masked)
    w = chosen * scores
    denom = jnp.sum(jnp.sum(w, axis=0, keepdims=True), axis=1, keepdims=True)
    gate3 = w / denom * ROUTED_SCALE

    chosen2 = chosen.reshape(N_EXPERTS, t).astype(BF16)
    tok_r = lax.broadcasted_iota(I32, (t, t), 0)
    tok_c = lax.broadcasted_iota(I32, (t, t), 1)
    before = jnp.where(tok_r < tok_c, 1.0, 0.0).astype(BF16)
    prefix = (jnp.dot(chosen2, before, preferred_element_type=F32) + carry[...]).reshape(shape3)
    carry[...] += jnp.dot(chosen2, jnp.ones((t, t), BF16), preferred_element_type=F32)
    cnt_ref[...] = carry[:, :LANES]

    def pick_sum(pick, val):
        return jnp.sum(jnp.sum(jnp.where(pick, val, 0.0), axis=0, keepdims=True), axis=1, keepdims=True)

    sub8 = lax.broadcasted_iota(I32, (TOP_K, t), 0)
    idx8 = jnp.zeros((TOP_K, t), I32)
    rank8 = jnp.zeros((TOP_K, t), F32)
    w8 = jnp.zeros((TOP_K, t), F32)
    for k, (fi, pick) in enumerate(picks):
        idx8 = jnp.where(sub8 == k, jnp.broadcast_to(fi.reshape(1, t), (TOP_K, t)), idx8)
        rank8 = jnp.where(sub8 == k, jnp.broadcast_to(pick_sum(pick, prefix).reshape(1, t), (TOP_K, t)), rank8)
        w8 = jnp.where(sub8 == k, jnp.broadcast_to(pick_sum(pick, gate3).reshape(1, t), (TOP_K, t)), w8)
    idx_ref[...] = idx8
    rank_ref[...] = rank8.astype(I32)
    pad = jnp.zeros((LANES - TOP_K, t), F32)
    wtm_ref[...] = jnp.transpose(jnp.concatenate([w8, pad], axis=0))


def _router(x1, w_router_t, router_bias):
    m, d = x1.shape
    tt = min(ROUTER_TT, m)
    kt = pl.BlockSpec((TOP_K, tt), lambda i: (0, i))
    return pl.pallas_call(
        _router_kernel,
        grid=(m // tt,),
        in_specs=[pl.BlockSpec((tt, d), lambda i: (i, 0)),
                  pl.BlockSpec((N_EXPERTS, d), lambda i: (0, 0)),
                  pl.BlockSpec((N_EXPERTS, 1), lambda i: (0, 0))],
        out_specs=[kt, kt,
                   pl.BlockSpec((tt, LANES), lambda i: (i, 0)),
                   pl.BlockSpec((N_EXPERTS, LANES), lambda i: (0, 0))],
        out_shape=[jax.ShapeDtypeStruct((TOP_K, m), I32),
                   jax.ShapeDtypeStruct((TOP_K, m), I32),
                   jax.ShapeDtypeStruct((m, LANES), F32),
                   jax.ShapeDtypeStruct((N_EXPERTS, LANES), F32)],
        scratch_shapes=[pltpu.VMEM((N_EXPERTS, tt), F32)],
        compiler_params=_params("arbitrary"),
        name="router",
    )(x1, w_router_t, router_bias)


def _visit_metadata(counts, tm, n_rows):
    nt = n_rows // tm
    nv = nt + N_EXPERTS - 1
    ends = jnp.cumsum(counts)
    starts = ends - counts
    first_tile = starts // tm
    ntiles = jnp.where(counts > 0, (ends - 1) // tm - first_tile + 1, 0)
    vend = jnp.cumsum(ntiles)
    vstart = vend - ntiles
    v = jnp.arange(nv, dtype=I32)
    valid = v < vend[-1]
    ve = jnp.minimum(jnp.sum((v[:, None] >= vend[None, :]).astype(I32), axis=1), N_EXPERTS - 1)
    ve = jnp.where(valid, ve, ve[jnp.maximum(vend[-1] - 1, 0)])
    vt = jnp.where(valid, first_tile[ve] + v - vstart[ve], nt - 1)
    lo = jnp.where(valid, jnp.clip(starts[ve] - vt * tm, 0, tm), 0)
    hi = jnp.where(valid, jnp.clip(ends[ve] - vt * tm, 0, tm), 0)
    return ve.astype(I32), vt.astype(I32), lo.astype(I32), hi.astype(I32)


def _sc_mesh():
    return plsc.VectorSubcoreMesh(core_axis_name="c", subcore_axis_name="s")


def _sc_worker():
    return lax.axis_index("s") * SC_CORES + lax.axis_index("c")


def _dispatch(x1p, dest):
    m, dp = x1p.shape
    blocks = m // SC_ROWS
    per_worker = blocks // SC_WORKERS
    table = dest.reshape(TOP_K, blocks, SC_ROWS).transpose(1, 0, 2).reshape(blocks * TOP_K, SC_ROWS)

    def body(x_hbm, idx_hbm, xs_hbm, idx_v, rows_v, sem):
        first = _sc_worker() * per_worker

        @pl.loop(0, per_worker)
        def _(j):
            b = first + j
            pltpu.sync_copy(idx_hbm.at[pl.ds(pl.multiple_of(b * TOP_K, TOP_K), TOP_K)], idx_v)
            pltpu.sync_copy(x_hbm.at[pl.ds(pl.multiple_of(b * SC_ROWS, SC_ROWS), SC_ROWS)], rows_v)
            copies = [pltpu.async_copy(rows_v, xs_hbm.at[idx_v.at[k]], sem) for k in range(TOP_K)]
            for cp in copies:
                cp.wait()

    return pl.kernel(
        body,
        out_type=jax.ShapeDtypeStruct((m * TOP_K, dp), x1p.dtype),
        mesh=_sc_mesh(),
        scratch_types=[pltpu.VMEM((TOP_K, SC_ROWS), I32), pltpu.VMEM((SC_ROWS, dp), x1p.dtype),
                       pltpu.SemaphoreType.DMA],
        name="dispatch",
    )(x1p, table)


def _gather_rows(ys, dest):
    n_rows, dp = ys.shape
    rows = SC_ROWS // 2
    idx = dest.reshape(n_rows // rows, rows)
    per_worker = n_rows // rows // SC_WORKERS

    def body(ys_hbm, idx_hbm, out_hbm, idx_v, rows_v, sems):
        first = _sc_worker() * per_worker

        def gather(slot):
            return pltpu.make_async_copy(ys_hbm.at[idx_v.at[slot]], rows_v.at[slot], sems.at[slot])

        def start(slot, b):
            pltpu.sync_copy(idx_hbm.at[pl.ds(b, 1)], idx_v.at[pl.ds(slot, 1)])
            gather(slot).start()

        def finish(slot, b):
            gather(slot).wait()
            pltpu.sync_copy(rows_v.at[slot], out_hbm.at[pl.ds(pl.multiple_of(b * rows, rows), rows)])

        start(0, first)

        @pl.loop(0, per_worker, step=2)
        def _(j):
            b = first + j
            start(1, b + 1)
            finish(0, b)

            @pl.when(j + 2 < per_worker)
            def _():
                start(0, b + 2)

            finish(1, b + 1)

    return pl.kernel(
        body,
        out_type=jax.ShapeDtypeStruct((n_rows, dp), ys.dtype),
        mesh=_sc_mesh(),
        scratch_types=[pltpu.VMEM((2, rows), I32), pltpu.VMEM((2, rows, dp), ys.dtype),
                       pltpu.SemaphoreType.DMA((2,))],
        name="gather_rows",
    )(ys, idx)


def _ffn_kernel(ve_ref, vt_ref, lo_ref, hi_ref, xs_ref, wg_ref, wu_ref, wd_ref, ys_ref,
                wgu_b, wd_b):
    v = pl.program_id(0)
    lo = lo_ref[v]
    hi = hi_ref[v]
    tm = xs_ref.shape[0]
    f = wg_ref.shape[1]

    @pl.when((v == 0) | (ve_ref[v] != ve_ref[jnp.maximum(v - 1, 0)]))
    def _():
        wgu_b[:, :f] = wg_ref[...].astype(BF16)
        wgu_b[:, f:] = wu_ref[...].astype(BF16)
        wd_b[...] = wd_ref[...].astype(BF16)

    def pack_rows(a):
        half = a.shape[1] // 2
        return _pack_pair(a[:, :half], a[:, half:])

    for sb in range(tm // FFN_SUB):
        s0 = sb * FFN_SUB
        lo_s = jnp.clip(lo - s0, 0, FFN_SUB)
        hi_s = jnp.clip(hi - s0, 0, FFN_SUB)

        half = FFN_SUB // 2
        halves = [pl.ds(s0 + j * half, half) for j in range(2)]

        def expert_rows():
            x = [_unpack_rows(xs_ref[hr, :]) for hr in halves]
            h2 = [jnp.dot(xj, wgu_b[...], preferred_element_type=F32) for xj in x]
            act = [(hj[:, :f] * _sigmoid(hj[:, :f]) * hj[:, f:]).astype(BF16) for hj in h2]
            y = [jnp.dot(aj, wd_b[...], preferred_element_type=F32) for aj in act]
            return [pack_rows(yj) for yj in y]

        @pl.when((hi_s > lo_s) & (lo_s == 0))
        def _():
            for hr, yj in zip(halves, expert_rows()):
                ys_ref[hr, :] = yj

        @pl.when((hi_s > lo_s) & (lo_s > 0))
        def _():
            for j, (hr, yj) in enumerate(zip(halves, expert_rows())):
                r = lax.broadcasted_iota(I32, yj.shape, 0) + j * half
                mine = (r >= lo_s) & (r < hi_s)
                ys_ref[hr, :] = jnp.where(mine, yj, ys_ref[hr, :])


def _ffn(xs, meta, wg, wu, wd, layer):
    n_rows, dp = xs.shape
    _, n_e, d, f = wg.shape
    tm = FFN_TM
    nv = n_rows // tm + N_EXPERTS - 1
    grid_spec = pltpu.PrefetchScalarGridSpec(
        num_scalar_prefetch=4,
        grid=(nv,),
        in_specs=[pl.BlockSpec((tm, dp), lambda v, ve, vt, lo, hi: (vt[v], 0)),
                  pl.BlockSpec((None, None, d, f), lambda v, ve, vt, lo, hi: (layer, ve[v], 0, 0)),
                  pl.BlockSpec((None, None, d, f), lambda v, ve, vt, lo, hi: (layer, ve[v], 0, 0)),
                  pl.BlockSpec((None, None, f, d), lambda v, ve, vt, lo, hi: (layer, ve[v], 0, 0))],
        out_specs=pl.BlockSpec((tm, dp), lambda v, ve, vt, lo, hi: (vt[v], 0)),
        scratch_shapes=[pltpu.VMEM((d, 2 * f), BF16), pltpu.VMEM((f, d), BF16)],
    )
    return pl.pallas_call(
        _ffn_kernel,
        grid_spec=grid_spec,
        out_shape=jax.ShapeDtypeStruct((n_rows, dp), I32),
        compiler_params=_params("arbitrary"),
        name="ffn",
    )(*meta, xs, wg, wu, wd)


def _combine_kernel(yk_ref, x1_ref, wtm_ref, p_ref, wsg_ref, wsu_ref, wsd_ref,
                    wpg_ref, wpp_ref, g_ref, b_ref, x2_ref, x2b_ref):
    tt = x1_ref.shape[0]
    xb = x1_ref[...].astype(BF16)
    hg = jnp.dot(xb, wsg_ref[...], preferred_element_type=F32)
    hu = jnp.dot(xb, wsu_ref[...], preferred_element_type=F32)
    shared = jnp.dot((hg * _sigmoid(hg) * hu).astype(BF16), wsd_ref[...], preferred_element_type=F32)
    pgate = _sigmoid(jnp.dot(xb, wpg_ref[...], preferred_element_type=F32))
    pproj = jnp.dot(p_ref[...].astype(BF16), wpp_ref[...], preferred_element_type=F32)
    rest = ALPHA * x1_ref[...] + shared + pgate * pproj

    wt = wtm_ref[...]
    r_lo = jnp.zeros((tt, yk_ref.shape[2]), F32)
    r_hi = jnp.zeros((tt, yk_ref.shape[2]), F32)
    for k in range(TOP_K):
        lo, hi = _unpack_pair(yk_ref[k])
        wk = wt[:, k:k + 1]
        r_lo = r_lo + wk * lo
        r_hi = r_hi + wk * hi
    routed = jnp.concatenate([r_lo, r_hi], axis=1)
    x2 = _layer_norm(rest + routed, g_ref[...], b_ref[...])
    x2_ref[...] = x2
    x2b_ref[...] = x2.astype(BF16)


def _combine(yk, x1, wtm, p, layer, wsg, wsu, wsd, wpg, wpp, g, b):
    m, d = x1.shape
    tt = min(COMBINE_TT, m)
    yk = yk.reshape(TOP_K, m, d // 2)

    def full(a):
        return pl.BlockSpec(a.shape, lambda i: (0,) * a.ndim)

    token_rows = pl.BlockSpec((tt, d), lambda i: (i, 0))
    return pl.pallas_call(
        _combine_kernel,
        grid=(m // tt,),
        in_specs=[pl.BlockSpec((TOP_K, tt, d // 2), lambda i: (0, i, 0)),
                  token_rows,
                  pl.BlockSpec((tt, LANES), lambda i: (i, 0)),
                  pl.BlockSpec((None, tt, p.shape[2]), lambda i: (layer, i, 0)),
                  full(wsg), full(wsu), full(wsd), full(wpg), full(wpp), full(g), full(b)],
        out_specs=[token_rows, token_rows],
        out_shape=[jax.ShapeDtypeStruct((m, d), F32), jax.ShapeDtypeStruct((m, d), BF16)],
        compiler_params=_params("parallel"),
        name="combine",
    )(yk, x1, wtm, p, wsg, wsu, wsd, wpg, wpp, g, b)


def kernel(x, p, w_in, conv_w, b_igate, b_fgate, mlstm_norm_g, w_branch_a, w_branch_b, w_out, ln1_g, ln1_b, w_router, router_bias, w_exp_gate, w_exp_up, w_exp_down, w_sh_gate, w_sh_up, w_sh_down, w_ple_gate, w_ple_proj, ln2_g, ln2_b):
    bsz, seq, d = x.shape
    m = bsz * seq
    depth = w_in.shape[0]
    hw = HEADS * HEAD_DIM
    if_lo = 3 * d + 4 * hw
    if_hi = if_lo + 2 * HEADS

    xf = x.reshape(m, d)
    xb = xf.astype(BF16)
    w_in_t = jnp.swapaxes(w_in, 1, 2)
    p = p.reshape(depth, m, -1)
    for i in range(depth):
        w_merge = w_in_t[i, if_hi:].astype(BF16)
        w_if = jnp.pad(w_in_t[i, if_lo:if_hi], ((0, LANES - 2 * HEADS), (0, 0))).astype(BF16)
        gate_bias = jnp.pad(jnp.concatenate([b_igate[i], b_fgate[i]]), (0, LANES - 2 * HEADS)).reshape(1, LANES)

        z, zm, gates = _inproj(xb, w_in_t, i, if_lo, w_merge, w_if)
        z, zm, gates = (a.reshape(bsz, seq, -1) for a in (z, zm, gates))
        x1, x1p = _mixer(z, zm, gates, gate_bias, mlstm_norm_g[i].reshape(1, hw),
                         xf.reshape(bsz, seq, d), conv_w[i],
                         w_branch_a[i].astype(BF16), w_branch_b[i].astype(BF16), w_out[i].astype(BF16),
                         ln1_g[i].reshape(1, d), ln1_b[i].reshape(1, d))
        x1 = x1.reshape(m, d)
        x1p = x1p.reshape(m, d // 2)

        w_router_t = w_router[i].T.reshape(N_GROUPS, GROUP_SIZE, d).swapaxes(0, 1).reshape(N_EXPERTS, d)
        rb = router_bias[i].reshape(N_GROUPS, GROUP_SIZE).T.reshape(N_EXPERTS, 1)
        idx, rank, wtm, cnt = _router(x1, w_router_t, rb)
        counts = cnt[:, 0].astype(I32).reshape(GROUP_SIZE, N_GROUPS).T.reshape(N_EXPERTS)
        row_start = jnp.cumsum(counts) - counts
        expert_ids = jnp.arange(N_EXPERTS, dtype=I32)[:, None, None]
        dest = rank + jnp.sum(jnp.where(idx[None] == expert_ids, row_start[:, None, None], 0), axis=0)
        meta = _visit_metadata(counts, FFN_TM, m * TOP_K)
        xs = _dispatch(x1p, dest)
        ys = _ffn(xs, meta, w_exp_gate, w_exp_up, w_exp_down, i)
        yk = _gather_rows(ys, dest)
        xf, xb = _combine(yk, x1, wtm, p, i,
                          w_sh_gate[i].astype(BF16), w_sh_up[i].astype(BF16), w_sh_down[i].astype(BF16),
                          w_ple_gate[i].astype(BF16), w_ple_proj[i].astype(BF16),
                          ln2_g[i].reshape(1, d), ln2_b[i].reshape(1, d))
    return xf.reshape(bsz, seq, d)
```

```python
import math

import jax
import jax.numpy as jnp
from jax import lax
from jax.experimental import pallas as pl
from jax.experimental.pallas import tpu as pltpu
from jax.experimental.pallas import tpu_sc as plsc

F32 = jnp.float32
BF16 = jnp.bfloat16
U32 = jnp.uint32
I32 = jnp.int32

HEADS = 8
HEAD_DIM = 128
N_EXPERTS = 64
N_GROUPS = 8
GROUP_SIZE = N_EXPERTS // N_GROUPS
TOPK_GROUPS = 4
TOP_K = 8
ROUTED_SCALE = 2.5
DEPTH = 4
ALPHA = (2 * DEPTH) ** 0.25
LN_EPS = 1e-5
RMS_EPS = 1e-6
QK_SCALE = HEAD_DIM ** -0.5
LOG_QK_SCALE = math.log(QK_SCALE)

LANES = 128
VMEM_LIMIT = 56 * 1024 * 1024
NEG_INF = float("-inf")

MLSTM_CHUNK = 256
MLSTM_STEP_CHUNKS = 2
INPROJ_TM, INPROJ_TN = 2048, 1024
ROUTER_TT = 512
FFN_TM = 2048
FFN_SUB = 512
COMBINE_TT = 512

SC_CORES = 2
SC_SUBCORES = 16
SC_WORKERS = SC_CORES * SC_SUBCORES
SC_ROWS = 128


def _params(*sem):
    return pltpu.CompilerParams(dimension_semantics=sem, vmem_limit_bytes=VMEM_LIMIT)


def _sigmoid(x):
    return 1.0 / (1.0 + jnp.exp(-x))


def _layer_norm(r, g, b):
    mu = jnp.mean(r, axis=-1, keepdims=True)
    d = r - mu
    var = jnp.mean(d * d, axis=-1, keepdims=True)
    return d * lax.rsqrt(var + LN_EPS) * g + b


def _pack_pair(lo, hi):
    return lax.bitcast_convert_type(pltpu.pack_elementwise([lo, hi], packed_dtype=BF16), I32)


def _unpack_pair(w):
    w = lax.bitcast_convert_type(w, U32)
    lo = pltpu.unpack_elementwise(w, index=0, packed_dtype=BF16, unpacked_dtype=F32)
    hi = pltpu.unpack_elementwise(w, index=1, packed_dtype=BF16, unpacked_dtype=F32)
    return lo, hi


def _unpack_rows(w):
    lo, hi = _unpack_pair(w)
    return jnp.concatenate([lo.astype(BF16), hi.astype(BF16)], axis=1)


def _mm_kernel(x_ref, wt_ref, o_ref):
    wt = wt_ref[...].reshape(wt_ref.shape[-2:]).astype(BF16)
    o_ref[...] = lax.dot_general(x_ref[...], wt, (((1,), (1,)), ((), ())),
                                 preferred_element_type=F32).astype(o_ref.dtype)


def _matmul(x, wt, out_dtype, tm, tn, name, layer=None, n=None, row0=0):
    m, k = x.shape
    n = wt.shape[-2] if n is None else n
    tm = min(tm, m)
    if layer is None:
        w_spec = pl.BlockSpec((tn, k), lambda i, j: (j, 0))
    elif row0 % tn == 0:
        w_spec = pl.BlockSpec((None, tn, k), lambda i, j: (layer, row0 // tn + j, 0))
    else:
        w_spec = pl.BlockSpec((pl.Element(1), pl.Element(tn), pl.Element(k)),
                              lambda i, j: (layer, pl.multiple_of(row0 + j * tn, 16), 0))
    return pl.pallas_call(
        _mm_kernel,
        grid=(m // tm, n // tn),
        in_specs=[pl.BlockSpec((tm, k), lambda i, j: (i, 0)), w_spec],
        out_specs=pl.BlockSpec((tm, tn), lambda i, j: (i, j)),
        out_shape=jax.ShapeDtypeStruct((m, n), out_dtype),
        compiler_params=_params("parallel", "parallel"),
        name=name,
    )(x, wt)


def _mlstm_rows(q_ref, k_ref, v_ref, o_ref, gates, ng_ref, c_scr, m_scr):
    L = MLSTM_CHUNK
    chunks = [pl.ds(j * L, L) for j in range(q_ref.shape[1] // L)]
    row = lax.broadcasted_iota(I32, (L, L), 0)
    col = lax.broadcasted_iota(I32, (L, L), 1)
    causal = col <= row
    rows = lax.broadcasted_iota(I32, (L, LANES), 0)

    def scan_rows(x, op, identity):
        step = 1
        while step < L:
            x = op(x, jnp.where(rows >= step, pltpu.roll(x, step, axis=0), identity))
            step *= 2
        return x

    gate = []
    m_prev = m_scr[...]
    for j in range(len(chunks)):
        g = gates[j * L:(j + 1) * L, :]
        ig = g
        fg = pltpu.roll(g, LANES - HEADS, axis=1)
        log_f = jnp.minimum(fg, 0.0) - jnp.log(1.0 + jnp.exp(-jnp.abs(fg)))
        b = scan_rows(log_f, jnp.add, 0.0)
        b_last = b[L - 1:L, :]
        a = b_last - b + ig
        m_loc = jnp.max(a, axis=0, keepdims=True)
        m_new = jnp.maximum(b_last + m_prev, m_loc)
        log_inter = b + m_prev
        r = ig - b
        m_out = jnp.maximum(log_inter, b + scan_rows(r, jnp.maximum, NEG_INF))
        gate.append(dict(
            w_loc=jnp.exp(a - m_loc),
            sp=jnp.exp(b_last + m_prev - m_new),
            sl=jnp.exp(m_loc - m_new),
            r_t=jnp.transpose(r),
            u=b - m_out + LOG_QK_SCALE,
            e_inter=jnp.exp(log_inter - m_out + LOG_QK_SCALE),
            e_floor=jnp.exp(-m_out)))
        m_prev = m_new
    m_scr[...] = m_prev

    ones_blk = jnp.ones((L, HEAD_DIM), BF16)
    lanes = [slice(h * HEAD_DIM, (h + 1) * HEAD_DIM) for h in range(HEADS)]
    pairs = [(j, h) for j in range(len(chunks)) for h in range(HEADS)]

    def col_of(name, j, h):
        return gate[j][name][:, h:h + 1]

    q = {(j, h): q_ref[0, chunks[j], lanes[h]] for j, h in pairs}
    k = {(j, h): k_ref[0, chunks[j], lanes[h]] for j, h in pairs}
    v_aug = {(j, h): jnp.concatenate([v_ref[0, chunks[j], lanes[h]], ones_blk], axis=1) for j, h in pairs}
    s1 = {p: lax.dot_general(q[p], k[p], (((1,), (1,)), ((), ())), preferred_element_type=F32) for p in pairs}
    kw = {(j, h): (k[j, h].astype(F32) * col_of("w_loc", j, h)).astype(BF16) for j, h in pairs}
    c_loc = {p: lax.dot_general(kw[p], v_aug[p], (((0,), (0,)), ((), ())), preferred_element_type=F32)
             for p in pairs}
    c_seen = {}
    for h in range(HEADS):
        c = c_scr[h]
        for j in range(len(chunks)):
            c_seen[j, h] = c
            c = col_of("sp", j, h) * c + col_of("sl", j, h) * c_loc[j, h]
        c_scr[h] = c
    inter = {p: jnp.dot(q[p], c_seen[p].astype(BF16), preferred_element_type=F32) for p in pairs}
    s = {(j, h): (s1[j, h] * jnp.where(causal, jnp.exp(col_of("u", j, h) + gate[j]["r_t"][h:h + 1, :]), 0.0)
                  ).astype(BF16) for j, h in pairs}
    intra = {p: jnp.dot(s[p], v_aug[p], preferred_element_type=F32) for p in pairs}
    tot = {(j, h): intra[j, h] + col_of("e_inter", j, h) * inter[j, h] for j, h in pairs}
    hh = {(j, h): tot[j, h][:, :HEAD_DIM] / jnp.maximum(jnp.abs(tot[j, h][:, HEAD_DIM:]), col_of("e_floor", j, h))
          for j, h in pairs}
    hh = {p: hh[p] * lax.rsqrt(jnp.mean(hh[p] * hh[p], axis=1, keepdims=True) + RMS_EPS) for p in pairs}
    y_b = {(j, h): (hh[j, h] * ng_ref[:, lanes[h]] * _sigmoid(o_ref[0, chunks[j], lanes[h]].astype(F32))).astype(BF16)
           for j, h in pairs}
    return [jnp.concatenate([y_b[j, h] for h in range(HEADS)], axis=1) for j in range(len(chunks))]


def _mixer_kernel(q_ref, k_ref, v_ref, o_ref, gates_ref, gb_ref, ng_ref,
                  cin_ref, cout_ref, cval_ref, mg0_ref, mg1_ref, x_ref, cw_ref,
                  wa_ref, wb_ref, wo_ref, g_ref, b_ref, x1_ref, x1p_ref, c_scr, m_scr, carry):
    @pl.when(pl.program_id(1) == 0)
    def _():
        c_scr[...] = jnp.zeros_like(c_scr)
        m_scr[...] = jnp.zeros_like(m_scr)
        carry[...] = jnp.zeros_like(carry)

    tp = MLSTM_CHUNK
    parts = [pl.ds(j * tp, tp) for j in range(x_ref.shape[1] // tp)]
    u = [cin_ref[0, pr, :].astype(F32) * cval_ref[0, pr, :].astype(F32) for pr in parts]
    prev = [carry[...]] + [uj[tp - 8:, :] for uj in u[:-1]]
    carry[...] = u[-1][tp - 8:, :]
    r8 = lax.broadcasted_iota(I32, (8, u[0].shape[1]), 0)

    def shifted(uj, pj, k):
        body = pltpu.roll(uj, k, axis=0)
        head = jnp.where(r8 < k, pltpu.roll(pj, k, axis=0), body[:8, :])
        return jnp.concatenate([head, body[8:, :]], axis=0)

    cw = cw_ref[...]
    conv = [cw[0:1, :] * shifted(uj, pj, 2) + cw[1:2, :] * shifted(uj, pj, 1) + cw[2:3, :] * uj
            for uj, pj in zip(u, prev)]
    y_a = [(cout_ref[0, pr, :].astype(F32) * cj).astype(BF16) for pr, cj in zip(parts, conv)]
    pa = [jnp.dot(yj, wa_ref[...], preferred_element_type=F32) for yj in y_a]
    gated_a = [_sigmoid(mg0_ref[0, pr, :].astype(F32)) * paj for pr, paj in zip(parts, pa)]
    y_b = _mlstm_rows(q_ref, k_ref, v_ref, o_ref, gates_ref[0] + gb_ref[...], ng_ref, c_scr, m_scr)
    pb = [jnp.dot(yj, wb_ref[...], preferred_element_type=F32) for yj in y_b]
    mixed = [(gaj + _sigmoid(mg1_ref[0, pr, :].astype(F32)) * pbj).astype(BF16)
             for pr, gaj, pbj in zip(parts, gated_a, pb)]
    hmix = [jnp.dot(mj, wo_ref[...], preferred_element_type=F32) for mj in mixed]
    for pr, hj in zip(parts, hmix):
        x1 = _layer_norm(ALPHA * x_ref[0, pr, :] + hj, g_ref[...], b_ref[...])
        x1_ref[0, pr, :] = x1
        half = x1.shape[1] // 2
        x1p_ref[0, pr, :] = _pack_pair(x1[:, :half], x1[:, half:])


def _mixer(z, zm, gates, gate_bias, norm_g, x, conv_w, wa, wb, wo, g, b):
    bsz, seq, d = x.shape
    ts = min(MLSTM_STEP_CHUNKS * MLSTM_CHUNK, seq)

    def zspec(cb):
        return pl.BlockSpec((1, ts, d), lambda i, j, cb=cb: (i, j, cb))

    def full(shape):
        return pl.BlockSpec(shape, lambda i, j: (0,) * len(shape))

    tile = pl.BlockSpec((1, ts, d), lambda i, j: (i, j, 0))
    ptile = pl.BlockSpec((1, ts, d // 2), lambda i, j: (i, j, 0))
    return pl.pallas_call(
        _mixer_kernel,
        grid=(bsz, seq // ts),
        in_specs=[zspec(3), zspec(4), zspec(5), zspec(6),
                  pl.BlockSpec((1, ts, LANES), lambda i, j: (i, j, 0)),
                  full(gate_bias.shape), full(norm_g.shape),
                  zspec(0), zspec(1), zspec(2), zspec(0), zspec(1), tile,
                  full(conv_w.shape), full(wa.shape), full(wb.shape), full(wo.shape),
                  full(g.shape), full(b.shape)],
        out_specs=[tile, ptile],
        out_shape=[jax.ShapeDtypeStruct((bsz, seq, d), F32),
                   jax.ShapeDtypeStruct((bsz, seq, d // 2), I32)],
        scratch_shapes=[pltpu.VMEM((HEADS, HEAD_DIM, 2 * HEAD_DIM), F32),
                        pltpu.VMEM((1, LANES), F32),
                        pltpu.VMEM((8, d), F32)],
        compiler_params=_params("parallel", "arbitrary"),
        name="mixer",
    )(z, z, z, z, gates, gate_bias, norm_g, z, z, z, zm, zm, x, conv_w, wa, wb, wo, g, b)


def _router_kernel(x_ref, wrt_ref, rb_ref, idx_ref, rank_ref, wtm_ref, cnt_ref, carry):
    t = x_ref.shape[0]

    @pl.when(pl.program_id(0) == 0)
    def _():
        carry[...] = jnp.zeros_like(carry)

    def split(a):
        hi = a.astype(BF16)
        return hi, (a - hi.astype(F32)).astype(BF16)

    def dot_t(a, b):
        return lax.dot_general(a, b, (((1,), (1,)), ((), ())), preferred_element_type=F32)

    w_hi, w_lo = split(wrt_ref[...])
    x_hi, x_lo = split(x_ref[...])
    logits = dot_t(w_hi, x_hi) + (dot_t(w_hi, x_lo) + dot_t(w_lo, x_hi))
    scores = _sigmoid(logits)
    shape3 = (GROUP_SIZE, N_GROUPS, t)
    sel = (scores + rb_ref[...]).reshape(shape3)
    scores = scores.reshape(shape3)
    mem = lax.broadcasted_iota(I32, shape3, 0)
    grp = lax.broadcasted_iota(I32, shape3, 1)
    eidx = grp * GROUP_SIZE + mem
    m1 = jnp.max(sel, axis=0, keepdims=True)
    first = jnp.min(jnp.where(sel == m1, mem, GROUP_SIZE), axis=0, keepdims=True)
    m2 = jnp.max(jnp.where(mem == first, NEG_INF, sel), axis=0, keepdims=True)
    rem = m1 + m2
    gidx = lax.broadcasted_iota(I32, rem.shape, 1)
    gmask = jnp.zeros(rem.shape, F32)
    for _ in range(TOPK_GROUPS):
        mx = jnp.max(rem, axis=1, keepdims=True)
        pick = gidx == jnp.min(jnp.where(rem == mx, gidx, N_GROUPS), axis=1, keepdims=True)
        gmask = jnp.where(pick, 1.0, gmask)
        rem = jnp.where(pick, NEG_INF, rem)
    masked = jnp.where(jnp.broadcast_to(gmask, shape3) > 0.5, sel, NEG_INF)
    chosen = jnp.zeros(shape3, F32)
    picks = []
    for _ in range(TOP_K):
        mx = jnp.max(jnp.max(masked, axis=0, keepdims=True), axis=1, keepdims=True)
        cand = jnp.where(masked == mx, eidx, N_EXPERTS)
        fi = jnp.min(jnp.min(cand, axis=0, keepdims=True), axis=1, keepdims=True)
        pick = eidx == fi
        picks.append((fi, pick))
        chosen = jnp.where(pick, 1.0, chosen)
        masked = jnp.where(pick, NEG_INF, masked)
    w = chosen * scores
    denom = jnp.sum(jnp.sum(w, axis=0, keepdims=True), axis=1, keepdims=True)
    gate3 = w / denom * ROUTED_SCALE

    chosen2 = chosen.reshape(N_EXPERTS, t).astype(BF16)
    tok_r = lax.broadcasted_iota(I32, (t, t), 0)
    tok_c = lax.broadcasted_iota(I32, (t, t), 1)
    before = jnp.where(tok_r < tok_c, 1.0, 0.0).astype(BF16)
    prefix = (jnp.dot(chosen2, before, preferred_element_type=F32) + carry[...]).reshape(shape3)
    carry[...] += jnp.dot(chosen2, jnp.ones((t, t), BF16), preferred_element_type=F32)
    cnt_ref[...] = carry[:, :LANES]

    def pick_sum(pick, val):
        return jnp.sum(jnp.sum(jnp.where(pick, val, 0.0), axis=0, keepdims=True), axis=1, keepdims=True)

    sub8 = lax.broadcasted_iota(I32, (TOP_K, t), 0)
    idx8 = jnp.zeros((TOP_K, t), I32)
    rank8 = jnp.zeros((TOP_K, t), F32)
    w8 = jnp.zeros((TOP_K, t), F32)
    for k, (fi, pick) in enumerate(picks):
        idx8 = jnp.where(sub8 == k, jnp.broadcast_to(fi.reshape(1, t), (TOP_K, t)), idx8)
        rank8 = jnp.where(sub8 == k, jnp.broadcast_to(pick_sum(pick, prefix).reshape(1, t), (TOP_K, t)), rank8)
        w8 = jnp.where(sub8 == k, jnp.broadcast_to(pick_sum(pick, gate3).reshape(1, t), (TOP_K, t)), w8)
    idx_ref[...] = idx8
    rank_ref[...] = rank8.astype(I32)
    pad = jnp.zeros((LANES - TOP_K, t), F32)
    wtm_ref[...] = jnp.transpose(jnp.concatenate([w8, pad], axis=0))


def _router(x1, w_router_t, router_bias):
    m, d = x1.shape
    tt = min(ROUTER_TT, m)
    kt = pl.BlockSpec((TOP_K, tt), lambda i: (0, i))
    return pl.pallas_call(
        _router_kernel,
        grid=(m // tt,),
        in_specs=[pl.BlockSpec((tt, d), lambda i: (i, 0)),
                  pl.BlockSpec((N_EXPERTS, d), lambda i: (0, 0)),
                  pl.BlockSpec((N_EXPERTS, 1), lambda i: (0, 0))],
        out_specs=[kt, kt,
                   pl.BlockSpec((tt, LANES), lambda i: (i, 0)),
                   pl.BlockSpec((N_EXPERTS, LANES), lambda i: (0, 0))],
        out_shape=[jax.ShapeDtypeStruct((TOP_K, m), I32),
                   jax.ShapeDtypeStruct((TOP_K, m), I32),
                   jax.ShapeDtypeStruct((m, LANES), F32),
                   jax.ShapeDtypeStruct((N_EXPERTS, LANES), F32)],
        scratch_shapes=[pltpu.VMEM((N_EXPERTS, tt), F32)],
        compiler_params=_params("arbitrary"),
        name="router",
    )(x1, w_router_t, router_bias)


def _visit_metadata(counts, tm, n_rows):
    nt = n_rows // tm
    nv = nt + N_EXPERTS - 1
    ends = jnp.cumsum(counts)
    starts = ends - counts
    first_tile = starts // tm
    ntiles = jnp.where(counts > 0, (ends - 1) // tm - first_tile + 1, 0)
    vend = jnp.cumsum(ntiles)
    vstart = vend - ntiles
    v = jnp.arange(nv, dtype=I32)
    valid = v < vend[-1]
    ve = jnp.minimum(jnp.sum((v[:, None] >= vend[None, :]).astype(I32), axis=1), N_EXPERTS - 1)
    ve = jnp.where(valid, ve, ve[jnp.maximum(vend[-1] - 1, 0)])
    vt = jnp.where(valid, first_tile[ve] + v - vstart[ve], nt - 1)
    lo = jnp.where(valid, jnp.clip(starts[ve] - vt * tm, 0, tm), 0)
    hi = jnp.where(valid, jnp.clip(ends[ve] - vt * tm, 0, tm), 0)
    return ve.astype(I32), vt.astype(I32), lo.astype(I32), hi.astype(I32)


def _sc_mesh():
    return plsc.VectorSubcoreMesh(core_axis_name="c", subcore_axis_name="s")


def _sc_worker():
    return lax.axis_index("s") * SC_CORES + lax.axis_index("c")


def _dispatch(x1p, dest):
    m, dp = x1p.shape
    blocks = m // SC_ROWS
    per_worker = blocks // SC_WORKERS
    table = dest.reshape(TOP_K, blocks, SC_ROWS).transpose(1, 0, 2).reshape(blocks * TOP_K, SC_ROWS)

    def body(x_hbm, idx_hbm, xs_hbm, idx_v, rows_v, sem):
        first = _sc_worker() * per_worker

        @pl.loop(0, per_worker)
        def _(j):
            b = first + j
            pltpu.sync_copy(idx_hbm.at[pl.ds(pl.multiple_of(b * TOP_K, TOP_K), TOP_K)], idx_v)
            pltpu.sync_copy(x_hbm.at[pl.ds(pl.multiple_of(b * SC_ROWS, SC_ROWS), SC_ROWS)], rows_v)
            copies = [pltpu.async_copy(rows_v, xs_hbm.at[idx_v.at[k]], sem) for k in range(TOP_K)]
            for cp in copies:
                cp.wait()

    return pl.kernel(
        body,
        out_type=jax.ShapeDtypeStruct((m * TOP_K, dp), x1p.dtype),
        mesh=_sc_mesh(),
        scratch_types=[pltpu.VMEM((TOP_K, SC_ROWS), I32), pltpu.VMEM((SC_ROWS, dp), x1p.dtype),
                       pltpu.SemaphoreType.DMA],
        name="dispatch",
    )(x1p, table)


def _gather_rows(ys, dest):
    n_rows, dp = ys.shape
    rows = SC_ROWS // 2
    idx = dest.reshape(n_rows // rows, rows)
    per_worker = n_rows // rows // SC_WORKERS

    def body(ys_hbm, idx_hbm, out_hbm, idx_v, rows_v, sems):
        first = _sc_worker() * per_worker

        def gather(slot):
            return pltpu.make_async_copy(ys_hbm.at[idx_v.at[slot]], rows_v.at[slot], sems.at[slot])

        def start(slot, b):
            pltpu.sync_copy(idx_hbm.at[pl.ds(b, 1)], idx_v.at[pl.ds(slot, 1)])
            gather(slot).start()

        def finish(slot, b):
            gather(slot).wait()
            pltpu.sync_copy(rows_v.at[slot], out_hbm.at[pl.ds(pl.multiple_of(b * rows, rows), rows)])

        start(0, first)

        @pl.loop(0, per_worker, step=2)
        def _(j):
            b = first + j
            start(1, b + 1)
            finish(0, b)

            @pl.when(j + 2 < per_worker)
            def _():
                start(0, b + 2)

            finish(1, b + 1)

    return pl.kernel(
        body,
        out_type=jax.ShapeDtypeStruct((n_rows, dp), ys.dtype),
        mesh=_sc_mesh(),
        scratch_types=[pltpu.VMEM((2, rows), I32), pltpu.VMEM((2, rows, dp), ys.dtype),
                       pltpu.SemaphoreType.DMA((2,))],
        name="gather_rows",
    )(ys, idx)


def _ffn_kernel(ve_ref, vt_ref, lo_ref, hi_ref, xs_ref, wg_ref, wu_ref, wd_ref, ys_ref,
                wgu_b, wd_b):
    v = pl.program_id(0)
    lo = lo_ref[v]
    hi = hi_ref[v]
    tm = xs_ref.shape[0]
    f = wg_ref.shape[1]

    @pl.when((v == 0) | (ve_ref[v] != ve_ref[jnp.maximum(v - 1, 0)]))
    def _():
        wgu_b[:, :f] = wg_ref[...].astype(BF16)
        wgu_b[:, f:] = wu_ref[...].astype(BF16)
        wd_b[...] = wd_ref[...].astype(BF16)

    def pack_rows(a):
        half = a.shape[1] // 2
        return _pack_pair(a[:, :half], a[:, half:])

    for sb in range(tm // FFN_SUB):
        s0 = sb * FFN_SUB
        lo_s = jnp.clip(lo - s0, 0, FFN_SUB)
        hi_s = jnp.clip(hi - s0, 0, FFN_SUB)

        half = FFN_SUB // 2
        halves = [pl.ds(s0 + j * half, half) for j in range(2)]

        def expert_rows():
            x = [_unpack_rows(xs_ref[hr, :]) for hr in halves]
            h2 = [jnp.dot(xj, wgu_b[...], preferred_element_type=F32) for xj in x]
            act = [(hj[:, :f] * _sigmoid(hj[:, :f]) * hj[:, f:]).astype(BF16) for hj in h2]
            y = [jnp.dot(aj, wd_b[...], preferred_element_type=F32) for aj in act]
            return [pack_rows(yj) for yj in y]

        @pl.when((hi_s > lo_s) & (lo_s == 0))
        def _():
            for hr, yj in zip(halves, expert_rows()):
                ys_ref[hr, :] = yj

        @pl.when((hi_s > lo_s) & (lo_s > 0))
        def _():
            for j, (hr, yj) in enumerate(zip(halves, expert_rows())):
                r = lax.broadcasted_iota(I32, yj.shape, 0) + j * half
                mine = (r >= lo_s) & (r < hi_s)
                ys_ref[hr, :] = jnp.where(mine, yj, ys_ref[hr, :])


def _ffn(xs, meta, wg, wu, wd, layer):
    n_rows, dp = xs.shape
    _, n_e, d, f = wg.shape
    tm = FFN_TM
    nv = n_rows // tm + N_EXPERTS - 1
    grid_spec = pltpu.PrefetchScalarGridSpec(
        num_scalar_prefetch=4,
        grid=(nv,),
        in_specs=[pl.BlockSpec((tm, dp), lambda v, ve, vt, lo, hi: (vt[v], 0)),
                  pl.BlockSpec((None, None, d, f), lambda v, ve, vt, lo, hi: (layer, ve[v], 0, 0)),
                  pl.BlockSpec((None, None, d, f), lambda v, ve, vt, lo, hi: (layer, ve[v], 0, 0)),
                  pl.BlockSpec((None, None, f, d), lambda v, ve, vt, lo, hi: (layer, ve[v], 0, 0))],
        out_specs=pl.BlockSpec((tm, dp), lambda v, ve, vt, lo, hi: (vt[v], 0)),
        scratch_shapes=[pltpu.VMEM((d, 2 * f), BF16), pltpu.VMEM((f, d), BF16)],
    )
    return pl.pallas_call(
        _ffn_kernel,
        grid_spec=grid_spec,
        out_shape=jax.ShapeDtypeStruct((n_rows, dp), I32),
        compiler_params=_params("arbitrary"),
        name="ffn",
    )(*meta, xs, wg, wu, wd)


def _combine_kernel(yk_ref, x1_ref, wtm_ref, p_ref, wsg_ref, wsu_ref, wsd_ref,
                    wpg_ref, wpp_ref, g_ref, b_ref, x2_ref, x2b_ref):
    tt = x1_ref.shape[0]
    xb = x1_ref[...].astype(BF16)
    hg = jnp.dot(xb, wsg_ref[...], preferred_element_type=F32)
    hu = jnp.dot(xb, wsu_ref[...], preferred_element_type=F32)
    shared = jnp.dot((hg * _sigmoid(hg) * hu).astype(BF16), wsd_ref[...], preferred_element_type=F32)
    pgate = _sigmoid(jnp.dot(xb, wpg_ref[...], preferred_element_type=F32))
    pproj = jnp.dot(p_ref[...].astype(BF16), wpp_ref[...], preferred_element_type=F32)
    rest = ALPHA * x1_ref[...] + shared + pgate * pproj

    wt = wtm_ref[...]
    r_lo = jnp.zeros((tt, yk_ref.shape[2]), F32)
    r_hi = jnp.zeros((tt, yk_ref.shape[2]), F32)
    for k in range(TOP_K):
        lo, hi = _unpack_pair(yk_ref[k])
        wk = wt[:, k:k + 1]
        r_lo = r_lo + wk * lo
        r_hi = r_hi + wk * hi
    routed = jnp.concatenate([r_lo, r_hi], axis=1)
    x2 = _layer_norm(rest + routed, g_ref[...], b_ref[...])
    x2_ref[...] = x2
    x2b_ref[...] = x2.astype(BF16)


def _combine(yk, x1, wtm, p, layer, wsg, wsu, wsd, wpg, wpp, g, b):
    m, d = x1.shape
    tt = min(COMBINE_TT, m)
    yk = yk.reshape(TOP_K, m, d // 2)

    def full(a):
        return pl.BlockSpec(a.shape, lambda i: (0,) * a.ndim)

    token_rows = pl.BlockSpec((tt, d), lambda i: (i, 0))
    return pl.pallas_call(
        _combine_kernel,
        grid=(m // tt,),
        in_specs=[pl.BlockSpec((TOP_K, tt, d // 2), lambda i: (0, i, 0)),
                  token_rows,
                  pl.BlockSpec((tt, LANES), lambda i: (i, 0)),
                  pl.BlockSpec((None, tt, p.shape[2]), lambda i: (layer, i, 0)),
                  full(wsg), full(wsu), full(wsd), full(wpg), full(wpp), full(g), full(b)],
        out_specs=[token_rows, token_rows],
        out_shape=[jax.ShapeDtypeStruct((m, d), F32), jax.ShapeDtypeStruct((m, d), BF16)],
        compiler_params=_params("parallel"),
        name="combine",
    )(yk, x1, wtm, p, wsg, wsu, wsd, wpg, wpp, g, b)


def kernel(x, p, w_in, conv_w, b_igate, b_fgate, mlstm_norm_g, w_branch_a, w_branch_b, w_out, ln1_g, ln1_b, w_router, router_bias, w_exp_gate, w_exp_up, w_exp_down, w_sh_gate, w_sh_up, w_sh_down, w_ple_gate, w_ple_proj, ln2_g, ln2_b):
    bsz, seq, d = x.shape
    m = bsz * seq
    depth = w_in.shape[0]
    hw = HEADS * HEAD_DIM
    if_lo = 3 * d + 4 * hw
    if_hi = if_lo + 2 * HEADS

    xf = x.reshape(m, d)
    xb = xf.astype(BF16)
    w_in_t = jnp.swapaxes(w_in, 1, 2)
    p = p.reshape(depth, m, -1)
    for i in range(depth):
        w_if = jnp.pad(w_in_t[i, if_lo:if_hi], ((0, LANES - 2 * HEADS), (0, 0)))
        gate_bias = jnp.pad(jnp.concatenate([b_igate[i], b_fgate[i]]), (0, LANES - 2 * HEADS)).reshape(1, LANES)

        z = _matmul(xb, w_in_t, BF16, INPROJ_TM, INPROJ_TN, "inproj", layer=i, n=if_lo).reshape(bsz, seq, -1)
        zm = _matmul(xb, w_in_t, BF16, INPROJ_TM, INPROJ_TN, "mergeproj", layer=i, n=2 * d,
                     row0=if_hi).reshape(bsz, seq, -1)
        gates = _matmul(xb, w_if, F32, INPROJ_TM, LANES, "gateproj").reshape(bsz, seq, LANES)
        x1, x1p = _mixer(z, zm, gates, gate_bias, mlstm_norm_g[i].reshape(1, hw),
                         xf.reshape(bsz, seq, d), conv_w[i],
                         w_branch_a[i].astype(BF16), w_branch_b[i].astype(BF16), w_out[i].astype(BF16),
                         ln1_g[i].reshape(1, d), ln1_b[i].reshape(1, d))
        x1 = x1.reshape(m, d)
        x1p = x1p.reshape(m, d // 2)

        w_router_t = w_router[i].T.reshape(N_GROUPS, GROUP_SIZE, d).swapaxes(0, 1).reshape(N_EXPERTS, d)
        rb = router_bias[i].reshape(N_GROUPS, GROUP_SIZE).T.reshape(N_EXPERTS, 1)
        idx, rank, wtm, cnt = _router(x1, w_router_t, rb)
        counts = cnt[:, 0].astype(I32).reshape(GROUP_SIZE, N_GROUPS).T.reshape(N_EXPERTS)
        row_start = jnp.cumsum(counts) - counts
        expert_ids = jnp.arange(N_EXPERTS, dtype=I32)[:, None, None]
        dest = rank + jnp.sum(jnp.where(idx[None] == expert_ids, row_start[:, None, None], 0), axis=0)
        meta = _visit_metadata(counts, FFN_TM, m * TOP_K)
        xs = _dispatch(x1p, dest)
        ys = _ffn(xs, meta, w_exp_gate, w_exp_up, w_exp_down, i)
        yk = _gather_rows(ys, dest)
        xf, xb = _combine(yk, x1, wtm, p, i,
                          w_sh_gate[i].astype(BF16), w_sh_up[i].astype(BF16), w_sh_down[i].astype(BF16),
                          w_ple_gate[i].astype(BF16), w_ple_proj[i].astype(BF16),
                          ln2_g[i].reshape(1, d), ln2_b[i].reshape(1, d))
    return xf.reshape(bsz, seq, d)
```

```python
import math

import jax
import jax.numpy as jnp
from jax import lax
from jax.experimental import pallas as pl
from jax.experimental.pallas import tpu as pltpu
from jax.experimental.pallas import tpu_sc as plsc

F32 = jnp.float32
BF16 = jnp.bfloat16
U32 = jnp.uint32
I32 = jnp.int32

HEADS = 8
HEAD_DIM = 128
N_EXPERTS = 64
N_GROUPS = 8
GROUP_SIZE = N_EXPERTS // N_GROUPS
TOPK_GROUPS = 4
TOP_K = 8
ROUTED_SCALE = 2.5
DEPTH = 4
ALPHA = (2 * DEPTH) ** 0.25
LN_EPS = 1e-5
RMS_EPS = 1e-6
QK_SCALE = HEAD_DIM ** -0.5
LOG_QK_SCALE = math.log(QK_SCALE)

LANES = 128
VMEM_LIMIT = 56 * 1024 * 1024
NEG_INF = float("-inf")

MLSTM_CHUNK = 256
MLSTM_STEP_CHUNKS = 2
INPROJ_TM, INPROJ_TN = 2048, 1024
ROUTER_TT = 512
FFN_TM = 2048
FFN_SUB = 512
COMBINE_TT = 512

SC_CORES = 2
SC_SUBCORES = 16
SC_WORKERS = SC_CORES * SC_SUBCORES
SC_ROWS = 128


def _params(*sem):
    return pltpu.CompilerParams(dimension_semantics=sem, vmem_limit_bytes=VMEM_LIMIT)


def _sigmoid(x):
    return 1.0 / (1.0 + jnp.exp(-x))


def _layer_norm(r, g, b):
    mu = jnp.mean(r, axis=-1, keepdims=True)
    d = r - mu
    var = jnp.mean(d * d, axis=-1, keepdims=True)
    return d * lax.rsqrt(var + LN_EPS) * g + b


def _pack_pair(lo, hi):
    return lax.bitcast_convert_type(pltpu.pack_elementwise([lo, hi], packed_dtype=BF16), I32)


def _unpack_pair(w):
    w = lax.bitcast_convert_type(w, U32)
    lo = pltpu.unpack_elementwise(w, index=0, packed_dtype=BF16, unpacked_dtype=F32)
    hi = pltpu.unpack_elementwise(w, index=1, packed_dtype=BF16, unpacked_dtype=F32)
    return lo, hi


def _unpack_rows(w):
    lo, hi = _unpack_pair(w)
    return jnp.concatenate([lo.astype(BF16), hi.astype(BF16)], axis=1)


def _mm_kernel(x_ref, wt_ref, o_ref):
    wt = wt_ref[...].reshape(wt_ref.shape[-2:]).astype(BF16)
    o_ref[...] = lax.dot_general(x_ref[...], wt, (((1,), (1,)), ((), ())),
                                 preferred_element_type=F32).astype(o_ref.dtype)


def _matmul(x, wt, out_dtype, tm, tn, name, layer=None, n=None, row0=0):
    m, k = x.shape
    n = wt.shape[-2] if n is None else n
    tm = min(tm, m)
    if layer is None:
        w_spec = pl.BlockSpec((tn, k), lambda i, j: (j, 0))
    elif row0 % tn == 0:
        w_spec = pl.BlockSpec((None, tn, k), lambda i, j: (layer, row0 // tn + j, 0))
    else:
        w_spec = pl.BlockSpec((pl.Element(1), pl.Element(tn), pl.Element(k)),
                              lambda i, j: (layer, pl.multiple_of(row0 + j * tn, 16), 0))
    return pl.pallas_call(
        _mm_kernel,
        grid=(m // tm, n // tn),
        in_specs=[pl.BlockSpec((tm, k), lambda i, j: (i, 0)), w_spec],
        out_specs=pl.BlockSpec((tm, tn), lambda i, j: (i, j)),
        out_shape=jax.ShapeDtypeStruct((m, n), out_dtype),
        compiler_params=_params("parallel", "parallel"),
        name=name,
    )(x, wt)


def _mlstm_rows(q_ref, k_ref, v_ref, o_ref, gates, ng_ref, c_scr, m_scr):
    L = MLSTM_CHUNK
    chunks = [pl.ds(j * L, L) for j in range(q_ref.shape[1] // L)]
    row = lax.broadcasted_iota(I32, (L, L), 0)
    col = lax.broadcasted_iota(I32, (L, L), 1)
    causal = col <= row
    rows = lax.broadcasted_iota(I32, (L, LANES), 0)

    def scan_rows(x, op, identity):
        step = 1
        while step < L:
            x = op(x, jnp.where(rows >= step, pltpu.roll(x, step, axis=0), identity))
            step *= 2
        return x

    gate = []
    m_prev = m_scr[...]
    for j in range(len(chunks)):
        g = gates[j * L:(j + 1) * L, :]
        ig = g
        fg = pltpu.roll(g, LANES - HEADS, axis=1)
        log_f = jnp.minimum(fg, 0.0) - jnp.log(1.0 + jnp.exp(-jnp.abs(fg)))
        b = scan_rows(log_f, jnp.add, 0.0)
        b_last = b[L - 1:L, :]
        a = b_last - b + ig
        m_loc = jnp.max(a, axis=0, keepdims=True)
        m_new = jnp.maximum(b_last + m_prev, m_loc)
        log_inter = b + m_prev
        r = ig - b
        m_out = jnp.maximum(log_inter, b + scan_rows(r, jnp.maximum, NEG_INF))
        gate.append(dict(
            w_loc=jnp.exp(a - m_loc),
            sp=jnp.exp(b_last + m_prev - m_new),
            sl=jnp.exp(m_loc - m_new),
            r_t=jnp.transpose(r),
            u=b - m_out + LOG_QK_SCALE,
            e_inter=jnp.exp(log_inter - m_out + LOG_QK_SCALE),
            e_floor=jnp.exp(-m_out)))
        m_prev = m_new
    m_scr[...] = m_prev

    ones_blk = jnp.ones((L, HEAD_DIM), BF16)
    lanes = [slice(h * HEAD_DIM, (h + 1) * HEAD_DIM) for h in range(HEADS)]
    pairs = [(j, h) for j in range(len(chunks)) for h in range(HEADS)]

    def col_of(name, j, h):
        return gate[j][name][:, h:h + 1]

    q = {(j, h): q_ref[0, chunks[j], lanes[h]] for j, h in pairs}
    k = {(j, h): k_ref[0, chunks[j], lanes[h]] for j, h in pairs}
    v_aug = {(j, h): jnp.concatenate([v_ref[0, chunks[j], lanes[h]], ones_blk], axis=1) for j, h in pairs}
    s1 = {p: lax.dot_general(q[p], k[p], (((1,), (1,)), ((), ())), preferred_element_type=F32) for p in pairs}
    kw = {(j, h): (k[j, h].astype(F32) * col_of("w_loc", j, h)).astype(BF16) for j, h in pairs}
    c_loc = {p: lax.dot_general(kw[p], v_aug[p], (((0,), (0,)), ((), ())), preferred_element_type=F32)
             for p in pairs}
    c_seen = {}
    for h in range(HEADS):
        c = c_scr[h]
        for j in range(len(chunks)):
            c_seen[j, h] = c
            c = col_of("sp", j, h) * c + col_of("sl", j, h) * c_loc[j, h]
        c_scr[h] = c
    inter = {p: jnp.dot(q[p], c_seen[p].astype(BF16), preferred_element_type=F32) for p in pairs}
    s = {(j, h): (s1[j, h] * jnp.where(causal, jnp.exp(col_of("u", j, h) + gate[j]["r_t"][h:h + 1, :]), 0.0)
                  ).astype(BF16) for j, h in pairs}
    intra = {p: jnp.dot(s[p], v_aug[p], preferred_element_type=F32) for p in pairs}
    tot = {(j, h): intra[j, h] + col_of("e_inter", j, h) * inter[j, h] for j, h in pairs}
    hh = {(j, h): tot[j, h][:, :HEAD_DIM] / jnp.maximum(jnp.abs(tot[j, h][:, HEAD_DIM:]), col_of("e_floor", j, h))
          for j, h in pairs}
    hh = {p: hh[p] * lax.rsqrt(jnp.mean(hh[p] * hh[p], axis=1, keepdims=True) + RMS_EPS) for p in pairs}
    y_b = {(j, h): (hh[j, h] * ng_ref[:, lanes[h]] * _sigmoid(o_ref[0, chunks[j], lanes[h]].astype(F32))).astype(BF16)
           for j, h in pairs}
    return [jnp.concatenate([y_b[j, h] for h in range(HEADS)], axis=1) for j in range(len(chunks))]


def _mixer_kernel(q_ref, k_ref, v_ref, o_ref, gates_ref, gb_ref, ng_ref,
                  cin_ref, cout_ref, cval_ref, mg0_ref, mg1_ref, x_ref, cw_ref,
                  wa_ref, wb_ref, wo_ref, g_ref, b_ref, x1_ref, x1p_ref, c_scr, m_scr, carry):
    @pl.when(pl.program_id(1) == 0)
    def _():
        c_scr[...] = jnp.zeros_like(c_scr)
        m_scr[...] = jnp.zeros_like(m_scr)
        carry[...] = jnp.zeros_like(carry)

    tp = MLSTM_CHUNK
    parts = [pl.ds(j * tp, tp) for j in range(x_ref.shape[1] // tp)]
    u = [cin_ref[0, pr, :].astype(F32) * cval_ref[0, pr, :].astype(F32) for pr in parts]
    prev = [carry[...]] + [uj[tp - 8:, :] for uj in u[:-1]]
    carry[...] = u[-1][tp - 8:, :]
    r8 = lax.broadcasted_iota(I32, (8, u[0].shape[1]), 0)

    def shifted(uj, pj, k):
        body = pltpu.roll(uj, k, axis=0)
        head = jnp.where(r8 < k, pltpu.roll(pj, k, axis=0), body[:8, :])
        return jnp.concatenate([head, body[8:, :]], axis=0)

    cw = cw_ref[...]
    conv = [cw[0:1, :] * shifted(uj, pj, 2) + cw[1:2, :] * shifted(uj, pj, 1) + cw[2:3, :] * uj
            for uj, pj in zip(u, prev)]
    y_a = [(cout_ref[0, pr, :].astype(F32) * cj).astype(BF16) for pr, cj in zip(parts, conv)]
    pa = [jnp.dot(yj, wa_ref[...], preferred_element_type=F32) for yj in y_a]
    gated_a = [_sigmoid(mg0_ref[0, pr, :].astype(F32)) * paj for pr, paj in zip(parts, pa)]
    y_b = _mlstm_rows(q_ref, k_ref, v_ref, o_ref, gates_ref[0] + gb_ref[...], ng_ref, c_scr, m_scr)
    pb = [jnp.dot(yj, wb_ref[...], preferred_element_type=F32) for yj in y_b]
    mixed = [(gaj + _sigmoid(mg1_ref[0, pr, :].astype(F32)) * pbj).astype(BF16)
             for pr, gaj, pbj in zip(parts, gated_a, pb)]
    hmix = [jnp.dot(mj, wo_ref[...], preferred_element_type=F32) for mj in mixed]
    for pr, hj in zip(parts, hmix):
        x1 = _layer_norm(ALPHA * x_ref[0, pr, :] + hj, g_ref[...], b_ref[...])
        x1_ref[0, pr, :] = x1
        half = x1.shape[1] // 2
        x1p_ref[0, pr, :] = _pack_pair(x1[:, :half], x1[:, half:])


def _mixer(z, zm, gates, gate_bias, norm_g, x, conv_w, wa, wb, wo, g, b):
    bsz, seq, d = x.shape
    ts = min(MLSTM_STEP_CHUNKS * MLSTM_CHUNK, seq)

    def zspec(cb):
        return pl.BlockSpec((1, ts, d), lambda i, j, cb=cb: (i, j, cb))

    def full(shape):
        return pl.BlockSpec(shape, lambda i, j: (0,) * len(shape))

    tile = pl.BlockSpec((1, ts, d), lambda i, j: (i, j, 0))
    ptile = pl.BlockSpec((1, ts, d // 2), lambda i, j: (i, j, 0))
    return pl.pallas_call(
        _mixer_kernel,
        grid=(bsz, seq // ts),
        in_specs=[zspec(3), zspec(4), zspec(5), zspec(6),
                  pl.BlockSpec((1, ts, LANES), lambda i, j: (i, j, 0)),
                  full(gate_bias.shape), full(norm_g.shape),
                  zspec(0), zspec(1), zspec(2), zspec(0), zspec(1), tile,
                  full(conv_w.shape), full(wa.shape), full(wb.shape), full(wo.shape),
                  full(g.shape), full(b.shape)],
        out_specs=[tile, ptile],
        out_shape=[jax.ShapeDtypeStruct((bsz, seq, d), F32),
                   jax.ShapeDtypeStruct((bsz, seq, d // 2), I32)],
        scratch_shapes=[pltpu.VMEM((HEADS, HEAD_DIM, 2 * HEAD_DIM), F32),
                        pltpu.VMEM((1, LANES), F32),
                        pltpu.VMEM((8, d), F32)],
        compiler_params=_params("parallel", "arbitrary"),
        name="mixer",
    )(z, z, z, z, gates, gate_bias, norm_g, z, z, z, zm, zm, x, conv_w, wa, wb, wo, g, b)


def _router_kernel(x_ref, wrt_ref, rb_ref, idx_ref, rank_ref, wtm_ref, cnt_ref, carry, before):
    t = x_ref.shape[0]

    @pl.when(pl.program_id(0) == 0)
    def _():
        carry[...] = jnp.zeros_like(carry)
        tok_r = lax.broadcasted_iota(I32, (t, t), 0)
        tok_c = lax.broadcasted_iota(I32, (t, t), 1)
        before[...] = jnp.where(tok_r < tok_c, 1.0, 0.0).astype(BF16)

    def split(a):
        hi = a.astype(BF16)
        return hi, (a - hi.astype(F32)).astype(BF16)

    def dot_t(a, b):
        return lax.dot_general(a, b, (((1,), (1,)), ((), ())), preferred_element_type=F32)

    w_hi, w_lo = split(wrt_ref[...])
    x_hi, x_lo = split(x_ref[...])
    logits = dot_t(w_hi, x_hi) + (dot_t(w_hi, x_lo) + dot_t(w_lo, x_hi))
    scores = _sigmoid(logits)
    shape3 = (GROUP_SIZE, N_GROUPS, t)
    sel = (scores + rb_ref[...]).reshape(shape3)
    scores = scores.reshape(shape3)
    mem = lax.broadcasted_iota(I32, shape3, 0)
    grp = lax.broadcasted_iota(I32, shape3, 1)
    eidx = grp * GROUP_SIZE + mem
    m1 = jnp.max(sel, axis=0, keepdims=True)
    first = jnp.min(jnp.where(sel == m1, mem, GROUP_SIZE), axis=0, keepdims=True)
    m2 = jnp.max(jnp.where(mem == first, NEG_INF, sel), axis=0, keepdims=True)
    rem = m1 + m2
    gidx = lax.broadcasted_iota(I32, rem.shape, 1)
    gmask = jnp.zeros(rem.shape, F32)
    for _ in range(TOPK_GROUPS):
        mx = jnp.max(rem, axis=1, keepdims=True)
        pick = gidx == jnp.min(jnp.where(rem == mx, gidx, N_GROUPS), axis=1, keepdims=True)
        gmask = jnp.where(pick, 1.0, gmask)
        rem = jnp.where(pick, NEG_INF, rem)
    masked = jnp.where(jnp.broadcast_to(gmask, shape3) > 0.5, sel, NEG_INF)
    chosen = jnp.zeros(shape3, F32)
    picks = []
    for _ in range(TOP_K):
        mx = jnp.max(jnp.max(masked, axis=0, keepdims=True), axis=1, keepdims=True)
        cand = jnp.where(masked == mx, eidx, N_EXPERTS)
        fi = jnp.min(jnp.min(cand, axis=0, keepdims=True), axis=1, keepdims=True)
        pick = eidx == fi
        picks.append((fi, pick))
        chosen = jnp.where(pick, 1.0, chosen)
        masked = jnp.where(pick, NEG_INF, masked)
    w = chosen * scores
    denom = jnp.sum(jnp.sum(w, axis=0, keepdims=True), axis=1, keepdims=True)
    gate3 = w / denom * ROUTED_SCALE

    chosen2 = chosen.reshape(N_EXPERTS, t).astype(BF16)
    prefix = (jnp.dot(chosen2, before[...], preferred_element_type=F32) + carry[...]).reshape(shape3)
    carry[...] += jnp.dot(chosen2, jnp.ones((t, t), BF16), preferred_element_type=F32)
    cnt_ref[...] = carry[:, :LANES]

    def pick_sum(pick, val):
        return jnp.sum(jnp.sum(jnp.where(pick, val, 0.0), axis=0, keepdims=True), axis=1, keepdims=True)

    sub8 = lax.broadcasted_iota(I32, (TOP_K, t), 0)
    idx8 = jnp.zeros((TOP_K, t), I32)
    rank8 = jnp.zeros((TOP_K, t), F32)
    w8 = jnp.zeros((TOP_K, t), F32)
    for k, (fi, pick) in enumerate(picks):
        idx8 = jnp.where(sub8 == k, jnp.broadcast_to(fi.reshape(1, t), (TOP_K, t)), idx8)
        rank8 = jnp.where(sub8 == k, jnp.broadcast_to(pick_sum(pick, prefix).reshape(1, t), (TOP_K, t)), rank8)
        w8 = jnp.where(sub8 == k, jnp.broadcast_to(pick_sum(pick, gate3).reshape(1, t), (TOP_K, t)), w8)
    idx_ref[...] = idx8
    rank_ref[...] = rank8.astype(I32)
    pad = jnp.zeros((LANES - TOP_K, t), F32)
    wtm_ref[...] = jnp.transpose(jnp.concatenate([w8, pad], axis=0))


def _router(x1, w_router_t, router_bias):
    m, d = x1.shape
    tt = min(ROUTER_TT, m)
    kt = pl.BlockSpec((TOP_K, tt), lambda i: (0, i))
    return pl.pallas_call(
        _router_kernel,
        grid=(m // tt,),
        in_specs=[pl.BlockSpec((tt, d), lambda i: (i, 0)),
                  pl.BlockSpec((N_EXPERTS, d), lambda i: (0, 0)),
                  pl.BlockSpec((N_EXPERTS, 1), lambda i: (0, 0))],
        out_specs=[kt, kt,
                   pl.BlockSpec((tt, LANES), lambda i: (i, 0)),
                   pl.BlockSpec((N_EXPERTS, LANES), lambda i: (0, 0))],
        out_shape=[jax.ShapeDtypeStruct((TOP_K, m), I32),
                   jax.ShapeDtypeStruct((TOP_K, m), I32),
                   jax.ShapeDtypeStruct((m, LANES), F32),
                   jax.ShapeDtypeStruct((N_EXPERTS, LANES), F32)],
        scratch_shapes=[pltpu.VMEM((N_EXPERTS, tt), F32), pltpu.VMEM((tt, tt), BF16)],
        compiler_params=_params("arbitrary"),
        name="router",
    )(x1, w_router_t, router_bias)


def _visit_metadata(counts, tm, n_rows):
    nt = n_rows // tm
    nv = nt + N_EXPERTS - 1
    ends = jnp.cumsum(counts)
    starts = ends - counts
    first_tile = starts // tm
    ntiles = jnp.where(counts > 0, (ends - 1) // tm - first_tile + 1, 0)
    vend = jnp.cumsum(ntiles)
    vstart = vend - ntiles
    v = jnp.arange(nv, dtype=I32)
    valid = v < vend[-1]
    ve = jnp.minimum(jnp.sum((v[:, None] >= vend[None, :]).astype(I32), axis=1), N_EXPERTS - 1)
    ve = jnp.where(valid, ve, ve[jnp.maximum(vend[-1] - 1, 0)])
    vt = jnp.where(valid, first_tile[ve] + v - vstart[ve], nt - 1)
    lo = jnp.where(valid, jnp.clip(starts[ve] - vt * tm, 0, tm), 0)
    hi = jnp.where(valid, jnp.clip(ends[ve] - vt * tm, 0, tm), 0)
    return ve.astype(I32), vt.astype(I32), lo.astype(I32), hi.astype(I32)


def _sc_mesh():
    return plsc.VectorSubcoreMesh(core_axis_name="c", subcore_axis_name="s")


def _sc_worker():
    return lax.axis_index("s") * SC_CORES + lax.axis_index("c")


def _dispatch(x1p, dest):
    m, dp = x1p.shape
    blocks = m // SC_ROWS
    per_worker = blocks // SC_WORKERS
    table = dest.reshape(TOP_K, blocks, SC_ROWS).transpose(1, 0, 2).reshape(blocks * TOP_K, SC_ROWS)

    def body(x_hbm, idx_hbm, xs_hbm, idx_v, rows_v, sem):
        first = _sc_worker() * per_worker

        @pl.loop(0, per_worker)
        def _(j):
            b = first + j
            pltpu.sync_copy(idx_hbm.at[pl.ds(pl.multiple_of(b * TOP_K, TOP_K), TOP_K)], idx_v)
            pltpu.sync_copy(x_hbm.at[pl.ds(pl.multiple_of(b * SC_ROWS, SC_ROWS), SC_ROWS)], rows_v)
            copies = [pltpu.async_copy(rows_v, xs_hbm.at[idx_v.at[k]], sem) for k in range(TOP_K)]
            for cp in copies:
                cp.wait()

    return pl.kernel(
        body,
        out_type=jax.ShapeDtypeStruct((m * TOP_K, dp), x1p.dtype),
        mesh=_sc_mesh(),
        scratch_types=[pltpu.VMEM((TOP_K, SC_ROWS), I32), pltpu.VMEM((SC_ROWS, dp), x1p.dtype),
                       pltpu.SemaphoreType.DMA],
        name="dispatch",
    )(x1p, table)


def _gather_rows(ys, dest):
    n_rows, dp = ys.shape
    rows = SC_ROWS // 2
    idx = dest.reshape(n_rows // rows, rows)
    per_worker = n_rows // rows // SC_WORKERS

    def body(ys_hbm, idx_hbm, out_hbm, idx_v, rows_v, sems):
        first = _sc_worker() * per_worker

        def gather(slot):
            return pltpu.make_async_copy(ys_hbm.at[idx_v.at[slot]], rows_v.at[slot], sems.at[slot])

        def start(slot, b):
            pltpu.sync_copy(idx_hbm.at[pl.ds(b, 1)], idx_v.at[pl.ds(slot, 1)])
            gather(slot).start()

        def finish(slot, b):
            gather(slot).wait()
            pltpu.sync_copy(rows_v.at[slot], out_hbm.at[pl.ds(pl.multiple_of(b * rows, rows), rows)])

        start(0, first)

        @pl.loop(0, per_worker, step=2)
        def _(j):
            b = first + j
            start(1, b + 1)
            finish(0, b)

            @pl.when(j + 2 < per_worker)
            def _():
                start(0, b + 2)

            finish(1, b + 1)

    return pl.kernel(
        body,
        out_type=jax.ShapeDtypeStruct((n_rows, dp), ys.dtype),
        mesh=_sc_mesh(),
        scratch_types=[pltpu.VMEM((2, rows), I32), pltpu.VMEM((2, rows, dp), ys.dtype),
                       pltpu.SemaphoreType.DMA((2,))],
        name="gather_rows",
    )(ys, idx)


def _ffn_kernel(ve_ref, vt_ref, lo_ref, hi_ref, xs_ref, wg_ref, wu_ref, wd_ref, ys_ref,
                wgu_b, wd_b):
    v = pl.program_id(0)
    lo = lo_ref[v]
    hi = hi_ref[v]
    tm = xs_ref.shape[0]
    f = wg_ref.shape[1]

    @pl.when((v == 0) | (ve_ref[v] != ve_ref[jnp.maximum(v - 1, 0)]))
    def _():
        wgu_b[:, :f] = wg_ref[...].astype(BF16)
        wgu_b[:, f:] = wu_ref[...].astype(BF16)
        wd_b[...] = wd_ref[...].astype(BF16)

    def pack_rows(a):
        half = a.shape[1] // 2
        return _pack_pair(a[:, :half], a[:, half:])

    for sb in range(tm // FFN_SUB):
        s0 = sb * FFN_SUB
        lo_s = jnp.clip(lo - s0, 0, FFN_SUB)
        hi_s = jnp.clip(hi - s0, 0, FFN_SUB)

        half = FFN_SUB // 2
        halves = [pl.ds(s0 + j * half, half) for j in range(2)]

        def expert_rows():
            x = [_unpack_rows(xs_ref[hr, :]) for hr in halves]
            h2 = [jnp.dot(xj, wgu_b[...], preferred_element_type=F32) for xj in x]
            act = [(hj[:, :f] * _sigmoid(hj[:, :f]) * hj[:, f:]).astype(BF16) for hj in h2]
            y = [jnp.dot(aj, wd_b[...], preferred_element_type=F32) for aj in act]
            return [pack_rows(yj) for yj in y]

        @pl.when((hi_s > lo_s) & (lo_s == 0))
        def _():
            for hr, yj in zip(halves, expert_rows()):
                ys_ref[hr, :] = yj

        @pl.when((hi_s > lo_s) & (lo_s > 0))
        def _():
            for j, (hr, yj) in enumerate(zip(halves, expert_rows())):
                r = lax.broadcasted_iota(I32, yj.shape, 0) + j * half
                mine = (r >= lo_s) & (r < hi_s)
                ys_ref[hr, :] = jnp.where(mine, yj, ys_ref[hr, :])


def _ffn(xs, meta, wg, wu, wd, layer):
    n_rows, dp = xs.shape
    _, n_e, d, f = wg.shape
    tm = FFN_TM
    nv = n_rows // tm + N_EXPERTS - 1
    grid_spec = pltpu.PrefetchScalarGridSpec(
        num_scalar_prefetch=4,
        grid=(nv,),
        in_specs=[pl.BlockSpec((tm, dp), lambda v, ve, vt, lo, hi: (vt[v], 0)),
                  pl.BlockSpec((None, None, d, f), lambda v, ve, vt, lo, hi: (layer, ve[v], 0, 0)),
                  pl.BlockSpec((None, None, d, f), lambda v, ve, vt, lo, hi: (layer, ve[v], 0, 0)),
                  pl.BlockSpec((None, None, f, d), lambda v, ve, vt, lo, hi: (layer, ve[v], 0, 0))],
        out_specs=pl.BlockSpec((tm, dp), lambda v, ve, vt, lo, hi: (vt[v], 0)),
        scratch_shapes=[pltpu.VMEM((d, 2 * f), BF16), pltpu.VMEM((f, d), BF16)],
    )
    return pl.pallas_call(
        _ffn_kernel,
        grid_spec=grid_spec,
        out_shape=jax.ShapeDtypeStruct((n_rows, dp), I32),
        compiler_params=_params("arbitrary"),
        name="ffn",
    )(*meta, xs, wg, wu, wd)


def _combine_kernel(yk_ref, x1_ref, wtm_ref, p_ref, wsg_ref, wsu_ref, wsd_ref,
                    wpg_ref, wpp_ref, g_ref, b_ref, x2_ref, x2b_ref):
    tt = x1_ref.shape[0]
    xb = x1_ref[...].astype(BF16)
    hg = jnp.dot(xb, wsg_ref[...], preferred_element_type=F32)
    hu = jnp.dot(xb, wsu_ref[...], preferred_element_type=F32)
    shared = jnp.dot((hg * _sigmoid(hg) * hu).astype(BF16), wsd_ref[...], preferred_element_type=F32)
    pgate = _sigmoid(jnp.dot(xb, wpg_ref[...], preferred_element_type=F32))
    pproj = jnp.dot(p_ref[...].astype(BF16), wpp_ref[...], preferred_element_type=F32)
    rest = ALPHA * x1_ref[...] + shared + pgate * pproj

    wt = wtm_ref[...]
    r_lo = jnp.zeros((tt, yk_ref.shape[2]), F32)
    r_hi = jnp.zeros((tt, yk_ref.shape[2]), F32)
    for k in range(TOP_K):
        lo, hi = _unpack_pair(yk_ref[k])
        wk = wt[:, k:k + 1]
        r_lo = r_lo + wk * lo
        r_hi = r_hi + wk * hi
    routed = jnp.concatenate([r_lo, r_hi], axis=1)
    x2 = _layer_norm(rest + routed, g_ref[...], b_ref[...])
    x2_ref[...] = x2
    x2b_ref[...] = x2.astype(BF16)


def _combine(yk, x1, wtm, p, layer, wsg, wsu, wsd, wpg, wpp, g, b):
    m, d = x1.shape
    tt = min(COMBINE_TT, m)
    yk = yk.reshape(TOP_K, m, d // 2)

    def full(a):
        return pl.BlockSpec(a.shape, lambda i: (0,) * a.ndim)

    token_rows = pl.BlockSpec((tt, d), lambda i: (i, 0))
    return pl.pallas_call(
        _combine_kernel,
        grid=(m // tt,),
        in_specs=[pl.BlockSpec((TOP_K, tt, d // 2), lambda i: (0, i, 0)),
                  token_rows,
                  pl.BlockSpec((tt, LANES), lambda i: (i, 0)),
                  pl.BlockSpec((None, tt, p.shape[2]), lambda i: (layer, i, 0)),
                  full(wsg), full(wsu), full(wsd), full(wpg), full(wpp), full(g), full(b)],
        out_specs=[token_rows, token_rows],
        out_shape=[jax.ShapeDtypeStruct((m, d), F32), jax.ShapeDtypeStruct((m, d), BF16)],
        compiler_params=_params("parallel"),
        name="combine",
    )(yk, x1, wtm, p, wsg, wsu, wsd, wpg, wpp, g, b)


def kernel(x, p, w_in, conv_w, b_igate, b_fgate, mlstm_norm_g, w_branch_a, w_branch_b, w_out, ln1_g, ln1_b, w_router, router_bias, w_exp_gate, w_exp_up, w_exp_down, w_sh_gate, w_sh_up, w_sh_down, w_ple_gate, w_ple_proj, ln2_g, ln2_b):
    bsz, seq, d = x.shape
    m = bsz * seq
    depth = w_in.shape[0]
    hw = HEADS * HEAD_DIM
    if_lo = 3 * d + 4 * hw
    if_hi = if_lo + 2 * HEADS

    xf = x.reshape(m, d)
    xb = xf.astype(BF16)
    w_in_t = jnp.swapaxes(w_in, 1, 2)
    p = p.reshape(depth, m, -1)
    for i in range(depth):
        gate_bias = jnp.pad(jnp.concatenate([b_igate[i], b_fgate[i]]), (0, LANES - 2 * HEADS)).reshape(1, LANES)

        z = _matmul(xb, w_in_t, BF16, INPROJ_TM, INPROJ_TN, "inproj", layer=i, n=if_lo).reshape(bsz, seq, -1)
        zm = _matmul(xb, w_in_t, BF16, INPROJ_TM, INPROJ_TN, "mergeproj", layer=i, n=2 * d,
                     row0=if_hi).reshape(bsz, seq, -1)
        gates = _matmul(xb, w_in_t, F32, INPROJ_TM, LANES, "gateproj", layer=i, n=LANES,
                        row0=if_lo).reshape(bsz, seq, LANES)
        x1, x1p = _mixer(z, zm, gates, gate_bias, mlstm_norm_g[i].reshape(1, hw),
                         xf.reshape(bsz, seq, d), conv_w[i],
                         w_branch_a[i].astype(BF16), w_branch_b[i].astype(BF16), w_out[i].astype(BF16),
                         ln1_g[i].reshape(1, d), ln1_b[i].reshape(1, d))
        x1 = x1.reshape(m, d)
        x1p = x1p.reshape(m, d // 2)

        w_router_t = w_router[i].T.reshape(N_GROUPS, GROUP_SIZE, d).swapaxes(0, 1).reshape(N_EXPERTS, d)
        rb = router_bias[i].reshape(N_GROUPS, GROUP_SIZE).T.reshape(N_EXPERTS, 1)
        idx, rank, wtm, cnt = _router(x1, w_router_t, rb)
        counts = cnt[:, 0].astype(I32).reshape(GROUP_SIZE, N_GROUPS).T.reshape(N_EXPERTS)
        row_start = jnp.cumsum(counts) - counts
        expert_ids = jnp.arange(N_EXPERTS, dtype=I32)[:, None, None]
        dest = rank + jnp.sum(jnp.where(idx[None] == expert_ids, row_start[:, None, None], 0), axis=0)
        meta = _visit_metadata(counts, FFN_TM, m * TOP_K)
        xs = _dispatch(x1p, dest)
        ys = _ffn(xs, meta, w_exp_gate, w_exp_up, w_exp_down, i)
        yk = _gather_rows(ys, dest)
        xf, xb = _combine(yk, x1, wtm, p, i,
                          w_sh_gate[i].astype(BF16), w_sh_up[i].astype(BF16), w_sh_down[i].astype(BF16),
                          w_ple_gate[i].astype(BF16), w_ple_proj[i].astype(BF16),
                          ln2_g[i].reshape(1, d), ln2_b[i].reshape(1, d))
    return xf.reshape(bsz, seq, d)
```

```python
import math

import jax
import jax.numpy as jnp
from jax import lax
from jax.experimental import pallas as pl
from jax.experimental.pallas import tpu as pltpu
from jax.experimental.pallas import tpu_sc as plsc

F32 = jnp.float32
BF16 = jnp.bfloat16
U32 = jnp.uint32
I32 = jnp.int32

HEADS = 8
HEAD_DIM = 128
N_EXPERTS = 64
N_GROUPS = 8
GROUP_SIZE = N_EXPERTS // N_GROUPS
TOPK_GROUPS = 4
TOP_K = 8
ROUTED_SCALE = 2.5
DEPTH = 4
ALPHA = (2 * DEPTH) ** 0.25
LN_EPS = 1e-5
RMS_EPS = 1e-6
QK_SCALE = HEAD_DIM ** -0.5
LOG_QK_SCALE = math.log(QK_SCALE)

LANES = 128
VMEM_LIMIT = 56 * 1024 * 1024
NEG_INF = float("-inf")

MLSTM_CHUNK = 256
MLSTM_STEP_CHUNKS = 2
INPROJ_TM, INPROJ_TN = 2048, 1024
ROUTER_TT = 512
FFN_TM = 2048
FFN_SUB = 512
COMBINE_TT = 512

SC_CORES = 2
SC_SUBCORES = 16
SC_WORKERS = SC_CORES * SC_SUBCORES
SC_ROWS = 128


def _params(*sem):
    return pltpu.CompilerParams(dimension_semantics=sem, vmem_limit_bytes=VMEM_LIMIT)


def _sigmoid(x):
    return 1.0 / (1.0 + jnp.exp(-x))


def _layer_norm(r, g, b):
    mu = jnp.mean(r, axis=-1, keepdims=True)
    d = r - mu
    var = jnp.mean(d * d, axis=-1, keepdims=True)
    return d * lax.rsqrt(var + LN_EPS) * g + b


def _pack_pair(lo, hi):
    return lax.bitcast_convert_type(pltpu.pack_elementwise([lo, hi], packed_dtype=BF16), I32)


def _unpack_pair(w):
    w = lax.bitcast_convert_type(w, U32)
    lo = pltpu.unpack_elementwise(w, index=0, packed_dtype=BF16, unpacked_dtype=F32)
    hi = pltpu.unpack_elementwise(w, index=1, packed_dtype=BF16, unpacked_dtype=F32)
    return lo, hi


def _unpack_rows(w):
    lo, hi = _unpack_pair(w)
    return jnp.concatenate([lo.astype(BF16), hi.astype(BF16)], axis=1)


def _mm_kernel(x_ref, wt_ref, o_ref, wt_b):
    @pl.when(pl.program_id(1) == 0)
    def _():
        wt_b[...] = wt_ref[...].reshape(wt_b.shape).astype(BF16)

    o_ref[...] = lax.dot_general(x_ref[...], wt_b[...], (((1,), (1,)), ((), ())),
                                 preferred_element_type=F32).astype(o_ref.dtype)


def _matmul(x, wt, out_dtype, tm, tn, name, layer=None, n=None, row0=0):
    m, k = x.shape
    n = wt.shape[-2] if n is None else n
    tm = min(tm, m)
    if layer is None:
        w_spec = pl.BlockSpec((tn, k), lambda j, i: (j, 0))
    elif row0 % tn == 0:
        w_spec = pl.BlockSpec((None, tn, k), lambda j, i: (layer, row0 // tn + j, 0))
    else:
        w_spec = pl.BlockSpec((pl.Element(1), pl.Element(tn), pl.Element(k)),
                              lambda j, i: (layer, pl.multiple_of(row0 + j * tn, 16), 0))
    return pl.pallas_call(
        _mm_kernel,
        grid=(n // tn, m // tm),
        in_specs=[pl.BlockSpec((tm, k), lambda j, i: (i, 0)), w_spec],
        out_specs=pl.BlockSpec((tm, tn), lambda j, i: (i, j)),
        out_shape=jax.ShapeDtypeStruct((m, n), out_dtype),
        scratch_shapes=[pltpu.VMEM((tn, k), BF16)],
        compiler_params=_params("arbitrary", "arbitrary"),
        name=name,
    )(x, wt)


def _mlstm_rows(q_ref, k_ref, v_ref, o_ref, gates, ng_ref, c_scr, m_scr):
    L = MLSTM_CHUNK
    chunks = [pl.ds(j * L, L) for j in range(q_ref.shape[1] // L)]
    row = lax.broadcasted_iota(I32, (L, L), 0)
    col = lax.broadcasted_iota(I32, (L, L), 1)
    causal = col <= row
    rows = lax.broadcasted_iota(I32, (L, LANES), 0)

    def scan_rows(x, op, identity):
        step = 1
        while step < L:
            x = op(x, jnp.where(rows >= step, pltpu.roll(x, step, axis=0), identity))
            step *= 2
        return x

    gate = []
    m_prev = m_scr[...]
    for j in range(len(chunks)):
        g = gates[j * L:(j + 1) * L, :]
        ig = g
        fg = pltpu.roll(g, LANES - HEADS, axis=1)
        log_f = jnp.minimum(fg, 0.0) - jnp.log(1.0 + jnp.exp(-jnp.abs(fg)))
        b = scan_rows(log_f, jnp.add, 0.0)
        b_last = b[L - 1:L, :]
        a = b_last - b + ig
        m_loc = jnp.max(a, axis=0, keepdims=True)
        m_new = jnp.maximum(b_last + m_prev, m_loc)
        log_inter = b + m_prev
        r = ig - b
        m_out = jnp.maximum(log_inter, b + scan_rows(r, jnp.maximum, NEG_INF))
        gate.append(dict(
            w_loc=jnp.exp(a - m_loc),
            sp=jnp.exp(b_last + m_prev - m_new),
            sl=jnp.exp(m_loc - m_new),
            r_t=jnp.transpose(r),
            u=b - m_out + LOG_QK_SCALE,
            e_inter=jnp.exp(log_inter - m_out + LOG_QK_SCALE),
            e_floor=jnp.exp(-m_out)))
        m_prev = m_new
    m_scr[...] = m_prev

    ones_blk = jnp.ones((L, HEAD_DIM), BF16)
    lanes = [slice(h * HEAD_DIM, (h + 1) * HEAD_DIM) for h in range(HEADS)]
    pairs = [(j, h) for j in range(len(chunks)) for h in range(HEADS)]

    def col_of(name, j, h):
        return gate[j][name][:, h:h + 1]

    q = {(j, h): q_ref[0, chunks[j], lanes[h]] for j, h in pairs}
    k = {(j, h): k_ref[0, chunks[j], lanes[h]] for j, h in pairs}
    v_aug = {(j, h): jnp.concatenate([v_ref[0, chunks[j], lanes[h]], ones_blk], axis=1) for j, h in pairs}
    s1 = {p: lax.dot_general(q[p], k[p], (((1,), (1,)), ((), ())), preferred_element_type=F32) for p in pairs}
    kw = {(j, h): (k[j, h].astype(F32) * col_of("w_loc", j, h)).astype(BF16) for j, h in pairs}
    c_loc = {p: lax.dot_general(kw[p], v_aug[p], (((0,), (0,)), ((), ())), preferred_element_type=F32)
             for p in pairs}
    c_seen = {}
    for h in range(HEADS):
        c = c_scr[h]
        for j in range(len(chunks)):
            c_seen[j, h] = c
            c = col_of("sp", j, h) * c + col_of("sl", j, h) * c_loc[j, h]
        c_scr[h] = c
    inter = {p: jnp.dot(q[p], c_seen[p].astype(BF16), preferred_element_type=F32) for p in pairs}
    s = {(j, h): (s1[j, h] * jnp.where(causal, jnp.exp(col_of("u", j, h) + gate[j]["r_t"][h:h + 1, :]), 0.0)
                  ).astype(BF16) for j, h in pairs}
    intra = {p: jnp.dot(s[p], v_aug[p], preferred_element_type=F32) for p in pairs}
    tot = {(j, h): intra[j, h] + col_of("e_inter", j, h) * inter[j, h] for j, h in pairs}
    hh = {(j, h): tot[j, h][:, :HEAD_DIM] / jnp.maximum(jnp.abs(tot[j, h][:, HEAD_DIM:]), col_of("e_floor", j, h))
          for j, h in pairs}
    hh = {p: hh[p] * lax.rsqrt(jnp.mean(hh[p] * hh[p], axis=1, keepdims=True) + RMS_EPS) for p in pairs}
    y_b = {(j, h): (hh[j, h] * ng_ref[:, lanes[h]] * _sigmoid(o_ref[0, chunks[j], lanes[h]].astype(F32))).astype(BF16)
           for j, h in pairs}
    return [jnp.concatenate([y_b[j, h] for h in range(HEADS)], axis=1) for j in range(len(chunks))]


def _mixer_kernel(q_ref, k_ref, v_ref, o_ref, gates_ref, gb_ref, ng_ref,
                  cin_ref, cout_ref, cval_ref, mg0_ref, mg1_ref, x_ref, cw_ref,
                  wa_ref, wb_ref, wo_ref, g_ref, b_ref, x1_ref, x1p_ref, c_scr, m_scr, carry):
    @pl.when(pl.program_id(1) == 0)
    def _():
        c_scr[...] = jnp.zeros_like(c_scr)
        m_scr[...] = jnp.zeros_like(m_scr)
        carry[...] = jnp.zeros_like(carry)

    tp = MLSTM_CHUNK
    parts = [pl.ds(j * tp, tp) for j in range(x_ref.shape[1] // tp)]
    u = [cin_ref[0, pr, :].astype(F32) * cval_ref[0, pr, :].astype(F32) for pr in parts]
    prev = [carry[...]] + [uj[tp - 8:, :] for uj in u[:-1]]
    carry[...] = u[-1][tp - 8:, :]
    r8 = lax.broadcasted_iota(I32, (8, u[0].shape[1]), 0)

    def shifted(uj, pj, k):
        body = pltpu.roll(uj, k, axis=0)
        head = jnp.where(r8 < k, pltpu.roll(pj, k, axis=0), body[:8, :])
        return jnp.concatenate([head, body[8:, :]], axis=0)

    cw = cw_ref[...]
    conv = [cw[0:1, :] * shifted(uj, pj, 2) + cw[1:2, :] * shifted(uj, pj, 1) + cw[2:3, :] * uj
            for uj, pj in zip(u, prev)]
    y_a = [(cout_ref[0, pr, :].astype(F32) * cj).astype(BF16) for pr, cj in zip(parts, conv)]
    pa = [jnp.dot(yj, wa_ref[...], preferred_element_type=F32) for yj in y_a]
    gated_a = [_sigmoid(mg0_ref[0, pr, :].astype(F32)) * paj for pr, paj in zip(parts, pa)]
    y_b = _mlstm_rows(q_ref, k_ref, v_ref, o_ref, gates_ref[0] + gb_ref[...], ng_ref, c_scr, m_scr)
    pb = [jnp.dot(yj, wb_ref[...], preferred_element_type=F32) for yj in y_b]
    mixed = [(gaj + _sigmoid(mg1_ref[0, pr, :].astype(F32)) * pbj).astype(BF16)
             for pr, gaj, pbj in zip(parts, gated_a, pb)]
    hmix = [jnp.dot(mj, wo_ref[...], preferred_element_type=F32) for mj in mixed]
    for pr, hj in zip(parts, hmix):
        x1 = _layer_norm(ALPHA * x_ref[0, pr, :] + hj, g_ref[...], b_ref[...])
        x1_ref[0, pr, :] = x1
        half = x1.shape[1] // 2
        x1p_ref[0, pr, :] = _pack_pair(x1[:, :half], x1[:, half:])


def _mixer(z, zm, gates, gate_bias, norm_g, x, conv_w, wa, wb, wo, g, b):
    bsz, seq, d = x.shape
    ts = min(MLSTM_STEP_CHUNKS * MLSTM_CHUNK, seq)

    def zspec(cb):
        return pl.BlockSpec((1, ts, d), lambda i, j, cb=cb: (i, j, cb))

    def full(shape):
        return pl.BlockSpec(shape, lambda i, j: (0,) * len(shape))

    tile = pl.BlockSpec((1, ts, d), lambda i, j: (i, j, 0))
    ptile = pl.BlockSpec((1, ts, d // 2), lambda i, j: (i, j, 0))
    return pl.pallas_call(
        _mixer_kernel,
        grid=(bsz, seq // ts),
        in_specs=[zspec(3), zspec(4), zspec(5), zspec(6),
                  pl.BlockSpec((1, ts, LANES), lambda i, j: (i, j, 0)),
                  full(gate_bias.shape), full(norm_g.shape),
                  zspec(0), zspec(1), zspec(2), zspec(0), zspec(1), tile,
                  full(conv_w.shape), full(wa.shape), full(wb.shape), full(wo.shape),
                  full(g.shape), full(b.shape)],
        out_specs=[tile, ptile],
        out_shape=[jax.ShapeDtypeStruct((bsz, seq, d), F32),
                   jax.ShapeDtypeStruct((bsz, seq, d // 2), I32)],
        scratch_shapes=[pltpu.VMEM((HEADS, HEAD_DIM, 2 * HEAD_DIM), F32),
                        pltpu.VMEM((1, LANES), F32),
                        pltpu.VMEM((8, d), F32)],
        compiler_params=_params("parallel", "arbitrary"),
        name="mixer",
    )(z, z, z, z, gates, gate_bias, norm_g, z, z, z, zm, zm, x, conv_w, wa, wb, wo, g, b)


def _router_kernel(x_ref, wrt_ref, rb_ref, idx_ref, rank_ref, wtm_ref, cnt_ref, carry):
    t = x_ref.shape[0]

    @pl.when(pl.program_id(0) == 0)
    def _():
        carry[...] = jnp.zeros_like(carry)

    def split(a):
        hi = a.astype(BF16)
        return hi, (a - hi.astype(F32)).astype(BF16)

    def dot_t(a, b):
        return lax.dot_general(a, b, (((1,), (1,)), ((), ())), preferred_element_type=F32)

    w_hi, w_lo = split(wrt_ref[...])
    x_hi, x_lo = split(x_ref[...])
    logits = dot_t(w_hi, x_hi) + (dot_t(w_hi, x_lo) + dot_t(w_lo, x_hi))
    scores = _sigmoid(logits)
    shape3 = (GROUP_SIZE, N_GROUPS, t)
    sel = (scores + rb_ref[...]).reshape(shape3)
    scores = scores.reshape(shape3)
    mem = lax.broadcasted_iota(I32, shape3, 0)
    grp = lax.broadcasted_iota(I32, shape3, 1)
    eidx = grp * GROUP_SIZE + mem
    m1 = jnp.max(sel, axis=0, keepdims=True)
    first = jnp.min(jnp.where(sel == m1, mem, GROUP_SIZE), axis=0, keepdims=True)
    m2 = jnp.max(jnp.where(mem == first, NEG_INF, sel), axis=0, keepdims=True)
    rem = m1 + m2
    gidx = lax.broadcasted_iota(I32, rem.shape, 1)
    gmask = jnp.zeros(rem.shape, F32)
    for _ in range(TOPK_GROUPS):
        mx = jnp.max(rem, axis=1, keepdims=True)
        pick = gidx == jnp.min(jnp.where(rem == mx, gidx, N_GROUPS), axis=1, keepdims=True)
        gmask = jnp.where(pick, 1.0, gmask)
        rem = jnp.where(pick, NEG_INF, rem)
    masked = jnp.where(jnp.broadcast_to(gmask, shape3) > 0.5, sel, NEG_INF)
    chosen = jnp.zeros(shape3, F32)
    picks = []
    for _ in range(TOP_K):
        mx = jnp.max(jnp.max(masked, axis=0, keepdims=True), axis=1, keepdims=True)
        cand = jnp.where(masked == mx, eidx, N_EXPERTS)
        fi = jnp.min(jnp.min(cand, axis=0, keepdims=True), axis=1, keepdims=True)
        pick = eidx == fi
        picks.append((fi, pick))
        chosen = jnp.where(pick, 1.0, chosen)
        masked = jnp.where(pick, NEG_INF, masked)
    w = chosen * scores
    denom = jnp.sum(jnp.sum(w, axis=0, keepdims=True), axis=1, keepdims=True)
    gate3 = w / denom * ROUTED_SCALE

    chosen2 = chosen.reshape(N_EXPERTS, t).astype(BF16)
    tok_r = lax.broadcasted_iota(I32, (t, t), 0)
    tok_c = lax.broadcasted_iota(I32, (t, t), 1)
    before = jnp.where(tok_r < tok_c, 1.0, 0.0).astype(BF16)
    prefix = (jnp.dot(chosen2, before, preferred_element_type=F32) + carry[...]).reshape(shape3)
    carry[...] += jnp.dot(chosen2, jnp.ones((t, t), BF16), preferred_element_type=F32)
    cnt_ref[...] = carry[:, :LANES]

    def pick_sum(pick, val):
        return jnp.sum(jnp.sum(jnp.where(pick, val, 0.0), axis=0, keepdims=True), axis=1, keepdims=True)

    sub8 = lax.broadcasted_iota(I32, (TOP_K, t), 0)
    idx8 = jnp.zeros((TOP_K, t), I32)
    rank8 = jnp.zeros((TOP_K, t), F32)
    w8 = jnp.zeros((TOP_K, t), F32)
    for k, (fi, pick) in enumerate(picks):
        idx8 = jnp.where(sub8 == k, jnp.broadcast_to(fi.reshape(1, t), (TOP_K, t)), idx8)
        rank8 = jnp.where(sub8 == k, jnp.broadcast_to(pick_sum(pick, prefix).reshape(1, t), (TOP_K, t)), rank8)
        w8 = jnp.where(sub8 == k, jnp.broadcast_to(pick_sum(pick, gate3).reshape(1, t), (TOP_K, t)), w8)
    idx_ref[...] = idx8
    rank_ref[...] = rank8.astype(I32)
    pad = jnp.zeros((LANES - TOP_K, t), F32)
    wtm_ref[...] = jnp.transpose(jnp.concatenate([w8, pad], axis=0))


def _router(x1, w_router_t, router_bias):
    m, d = x1.shape
    tt = min(ROUTER_TT, m)
    kt = pl.BlockSpec((TOP_K, tt), lambda i: (0, i))
    return pl.pallas_call(
        _router_kernel,
        grid=(m // tt,),
        in_specs=[pl.BlockSpec((tt, d), lambda i: (i, 0)),
                  pl.BlockSpec((N_EXPERTS, d), lambda i: (0, 0)),
                  pl.BlockSpec((N_EXPERTS, 1), lambda i: (0, 0))],
        out_specs=[kt, kt,
                   pl.BlockSpec((tt, LANES), lambda i: (i, 0)),
                   pl.BlockSpec((N_EXPERTS, LANES), lambda i: (0, 0))],
        out_shape=[jax.ShapeDtypeStruct((TOP_K, m), I32),
                   jax.ShapeDtypeStruct((TOP_K, m), I32),
                   jax.ShapeDtypeStruct((m, LANES), F32),
                   jax.ShapeDtypeStruct((N_EXPERTS, LANES), F32)],
        scratch_shapes=[pltpu.VMEM((N_EXPERTS, tt), F32)],
        compiler_params=_params("arbitrary"),
        name="router",
    )(x1, w_router_t, router_bias)


def _visit_metadata(counts, tm, n_rows):
    nt = n_rows // tm
    nv = nt + N_EXPERTS - 1
    ends = jnp.cumsum(counts)
    starts = ends - counts
    first_tile = starts // tm
    ntiles = jnp.where(counts > 0, (ends - 1) // tm - first_tile + 1, 0)
    vend = jnp.cumsum(ntiles)
    vstart = vend - ntiles
    v = jnp.arange(nv, dtype=I32)
    valid = v < vend[-1]
    ve = jnp.minimum(jnp.sum((v[:, None] >= vend[None, :]).astype(I32), axis=1), N_EXPERTS - 1)
    ve = jnp.where(valid, ve, ve[jnp.maximum(vend[-1] - 1, 0)])
    vt = jnp.where(valid, first_tile[ve] + v - vstart[ve], nt - 1)
    lo = jnp.where(valid, jnp.clip(starts[ve] - vt * tm, 0, tm), 0)
    hi = jnp.where(valid, jnp.clip(ends[ve] - vt * tm, 0, tm), 0)
    return ve.astype(I32), vt.astype(I32), lo.astype(I32), hi.astype(I32)


def _sc_mesh():
    return plsc.VectorSubcoreMesh(core_axis_name="c", subcore_axis_name="s")


def _sc_worker():
    return lax.axis_index("s") * SC_CORES + lax.axis_index("c")


def _dispatch(x1p, dest):
    m, dp = x1p.shape
    blocks = m // SC_ROWS
    per_worker = blocks // SC_WORKERS
    table = dest.reshape(TOP_K, blocks, SC_ROWS).transpose(1, 0, 2).reshape(blocks * TOP_K, SC_ROWS)

    def body(x_hbm, idx_hbm, xs_hbm, idx_v, rows_v, sem):
        first = _sc_worker() * per_worker

        @pl.loop(0, per_worker)
        def _(j):
            b = first + j
            pltpu.sync_copy(idx_hbm.at[pl.ds(pl.multiple_of(b * TOP_K, TOP_K), TOP_K)], idx_v)
            pltpu.sync_copy(x_hbm.at[pl.ds(pl.multiple_of(b * SC_ROWS, SC_ROWS), SC_ROWS)], rows_v)
            copies = [pltpu.async_copy(rows_v, xs_hbm.at[idx_v.at[k]], sem) for k in range(TOP_K)]
            for cp in copies:
                cp.wait()

    return pl.kernel(
        body,
        out_type=jax.ShapeDtypeStruct((m * TOP_K, dp), x1p.dtype),
        mesh=_sc_mesh(),
        scratch_types=[pltpu.VMEM((TOP_K, SC_ROWS), I32), pltpu.VMEM((SC_ROWS, dp), x1p.dtype),
                       pltpu.SemaphoreType.DMA],
        name="dispatch",
    )(x1p, table)


def _gather_rows(ys, dest):
    n_rows, dp = ys.shape
    rows = SC_ROWS // 2
    idx = dest.reshape(n_rows // rows, rows)
    per_worker = n_rows // rows // SC_WORKERS

    def body(ys_hbm, idx_hbm, out_hbm, idx_v, rows_v, sems):
        first = _sc_worker() * per_worker

        def gather(slot):
            return pltpu.make_async_copy(ys_hbm.at[idx_v.at[slot]], rows_v.at[slot], sems.at[slot])

        def start(slot, b):
            pltpu.sync_copy(idx_hbm.at[pl.ds(b, 1)], idx_v.at[pl.ds(slot, 1)])
            gather(slot).start()

        def finish(slot, b):
            gather(slot).wait()
            pltpu.sync_copy(rows_v.at[slot], out_hbm.at[pl.ds(pl.multiple_of(b * rows, rows), rows)])

        start(0, first)

        @pl.loop(0, per_worker, step=2)
        def _(j):
            b = first + j
            start(1, b + 1)
            finish(0, b)

            @pl.when(j + 2 < per_worker)
            def _():
                start(0, b + 2)

            finish(1, b + 1)

    return pl.kernel(
        body,
        out_type=jax.ShapeDtypeStruct((n_rows, dp), ys.dtype),
        mesh=_sc_mesh(),
        scratch_types=[pltpu.VMEM((2, rows), I32), pltpu.VMEM((2, rows, dp), ys.dtype),
                       pltpu.SemaphoreType.DMA((2,))],
        name="gather_rows",
    )(ys, idx)


def _ffn_kernel(ve_ref, vt_ref, lo_ref, hi_ref, xs_ref, wg_ref, wu_ref, wd_ref, ys_ref,
                wgu_b, wd_b):
    v = pl.program_id(0)
    lo = lo_ref[v]
    hi = hi_ref[v]
    tm = xs_ref.shape[0]
    f = wg_ref.shape[1]

    @pl.when((v == 0) | (ve_ref[v] != ve_ref[jnp.maximum(v - 1, 0)]))
    def _():
        wgu_b[:, :f] = wg_ref[...].astype(BF16)
        wgu_b[:, f:] = wu_ref[...].astype(BF16)
        wd_b[...] = wd_ref[...].astype(BF16)

    def pack_rows(a):
        half = a.shape[1] // 2
        return _pack_pair(a[:, :half], a[:, half:])

    for sb in range(tm // FFN_SUB):
        s0 = sb * FFN_SUB
        lo_s = jnp.clip(lo - s0, 0, FFN_SUB)
        hi_s = jnp.clip(hi - s0, 0, FFN_SUB)

        half = FFN_SUB // 2
        halves = [pl.ds(s0 + j * half, half) for j in range(2)]

        def expert_rows():
            x = [_unpack_rows(xs_ref[hr, :]) for hr in halves]
            h2 = [jnp.dot(xj, wgu_b[...], preferred_element_type=F32) for xj in x]
            act = [(hj[:, :f] * _sigmoid(hj[:, :f]) * hj[:, f:]).astype(BF16) for hj in h2]
            y = [jnp.dot(aj, wd_b[...], preferred_element_type=F32) for aj in act]
            return [pack_rows(yj) for yj in y]

        @pl.when((hi_s > lo_s) & (lo_s == 0))
        def _():
            for hr, yj in zip(halves, expert_rows()):
                ys_ref[hr, :] = yj

        @pl.when((hi_s > lo_s) & (lo_s > 0))
        def _():
            for j, (hr, yj) in enumerate(zip(halves, expert_rows())):
                r = lax.broadcasted_iota(I32, yj.shape, 0) + j * half
                mine = (r >= lo_s) & (r < hi_s)
                ys_ref[hr, :] = jnp.where(mine, yj, ys_ref[hr, :])


def _ffn(xs, meta, wg, wu, wd, layer):
    n_rows, dp = xs.shape
    _, n_e, d, f = wg.shape
    tm = FFN_TM
    nv = n_rows // tm + N_EXPERTS - 1
    grid_spec = pltpu.PrefetchScalarGridSpec(
        num_scalar_prefetch=4,
        grid=(nv,),
        in_specs=[pl.BlockSpec((tm, dp), lambda v, ve, vt, lo, hi: (vt[v], 0)),
                  pl.BlockSpec((None, None, d, f), lambda v, ve, vt, lo, hi: (layer, ve[v], 0, 0)),
                  pl.BlockSpec((None, None, d, f), lambda v, ve, vt, lo, hi: (layer, ve[v], 0, 0)),
                  pl.BlockSpec((None, None, f, d), lambda v, ve, vt, lo, hi: (layer, ve[v], 0, 0))],
        out_specs=pl.BlockSpec((tm, dp), lambda v, ve, vt, lo, hi: (vt[v], 0)),
        scratch_shapes=[pltpu.VMEM((d, 2 * f), BF16), pltpu.VMEM((f, d), BF16)],
    )
    return pl.pallas_call(
        _ffn_kernel,
        grid_spec=grid_spec,
        out_shape=jax.ShapeDtypeStruct((n_rows, dp), I32),
        compiler_params=_params("arbitrary"),
        name="ffn",
    )(*meta, xs, wg, wu, wd)


def _combine_kernel(yk_ref, x1_ref, wtm_ref, p_ref, wsg_ref, wsu_ref, wsd_ref,
                    wpg_ref, wpp_ref, g_ref, b_ref, x2_ref, x2b_ref):
    tt = x1_ref.shape[0]
    xb = x1_ref[...].astype(BF16)
    hg = jnp.dot(xb, wsg_ref[...], preferred_element_type=F32)
    hu = jnp.dot(xb, wsu_ref[...], preferred_element_type=F32)
    shared = jnp.dot((hg * _sigmoid(hg) * hu).astype(BF16), wsd_ref[...], preferred_element_type=F32)
    pgate = _sigmoid(jnp.dot(xb, wpg_ref[...], preferred_element_type=F32))
    pproj = jnp.dot(p_ref[...].astype(BF16), wpp_ref[...], preferred_element_type=F32)
    rest = ALPHA * x1_ref[...] + shared + pgate * pproj

    wt = wtm_ref[...]
    r_lo = jnp.zeros((tt, yk_ref.shape[2]), F32)
    r_hi = jnp.zeros((tt, yk_ref.shape[2]), F32)
    for k in range(TOP_K):
        lo, hi = _unpack_pair(yk_ref[k])
        wk = wt[:, k:k + 1]
        r_lo = r_lo + wk * lo
        r_hi = r_hi + wk * hi
    routed = jnp.concatenate([r_lo, r_hi], axis=1)
    x2 = _layer_norm(rest + routed, g_ref[...], b_ref[...])
    x2_ref[...] = x2
    x2b_ref[...] = x2.astype(BF16)


def _combine(yk, x1, wtm, p, layer, wsg, wsu, wsd, wpg, wpp, g, b):
    m, d = x1.shape
    tt = min(COMBINE_TT, m)
    yk = yk.reshape(TOP_K, m, d // 2)

    def full(a):
        return pl.BlockSpec(a.shape, lambda i: (0,) * a.ndim)

    token_rows = pl.BlockSpec((tt, d), lambda i: (i, 0))
    return pl.pallas_call(
        _combine_kernel,
        grid=(m // tt,),
        in_specs=[pl.BlockSpec((TOP_K, tt, d // 2), lambda i: (0, i, 0)),
                  token_rows,
                  pl.BlockSpec((tt, LANES), lambda i: (i, 0)),
                  pl.BlockSpec((None, tt, p.shape[2]), lambda i: (layer, i, 0)),
                  full(wsg), full(wsu), full(wsd), full(wpg), full(wpp), full(g), full(b)],
        out_specs=[token_rows, token_rows],
        out_shape=[jax.ShapeDtypeStruct((m, d), F32), jax.ShapeDtypeStruct((m, d), BF16)],
        compiler_params=_params("parallel"),
        name="combine",
    )(yk, x1, wtm, p, wsg, wsu, wsd, wpg, wpp, g, b)


def kernel(x, p, w_in, conv_w, b_igate, b_fgate, mlstm_norm_g, w_branch_a, w_branch_b, w_out, ln1_g, ln1_b, w_router, router_bias, w_exp_gate, w_exp_up, w_exp_down, w_sh_gate, w_sh_up, w_sh_down, w_ple_gate, w_ple_proj, ln2_g, ln2_b):
    bsz, seq, d = x.shape
    m = bsz * seq
    depth = w_in.shape[0]
    hw = HEADS * HEAD_DIM
    if_lo = 3 * d + 4 * hw
    if_hi = if_lo + 2 * HEADS

    xf = x.reshape(m, d)
    xb = xf.astype(BF16)
    w_in_t = jnp.swapaxes(w_in, 1, 2)
    p = p.reshape(depth, m, -1)
    for i in range(depth):
        w_if = jnp.pad(w_in_t[i, if_lo:if_hi], ((0, LANES - 2 * HEADS), (0, 0)))
        gate_bias = jnp.pad(jnp.concatenate([b_igate[i], b_fgate[i]]), (0, LANES - 2 * HEADS)).reshape(1, LANES)

        z = _matmul(xb, w_in_t, BF16, INPROJ_TM, INPROJ_TN, "inproj", layer=i, n=if_lo).reshape(bsz, seq, -1)
        zm = _matmul(xb, w_in_t, BF16, INPROJ_TM, INPROJ_TN, "mergeproj", layer=i, n=2 * d,
                     row0=if_hi).reshape(bsz, seq, -1)
        gates = _matmul(xb, w_if, F32, INPROJ_TM, LANES, "gateproj").reshape(bsz, seq, LANES)
        x1, x1p = _mixer(z, zm, gates, gate_bias, mlstm_norm_g[i].reshape(1, hw),
                         xf.reshape(bsz, seq, d), conv_w[i],
                         w_branch_a[i].astype(BF16), w_branch_b[i].astype(BF16), w_out[i].astype(BF16),
                         ln1_g[i].reshape(1, d), ln1_b[i].reshape(1, d))
        x1 = x1.reshape(m, d)
        x1p = x1p.reshape(m, d // 2)

        w_router_t = w_router[i].T.reshape(N_GROUPS, GROUP_SIZE, d).swapaxes(0, 1).reshape(N_EXPERTS, d)
        rb = router_bias[i].reshape(N_GROUPS, GROUP_SIZE).T.reshape(N_EXPERTS, 1)
        idx, rank, wtm, cnt = _router(x1, w_router_t, rb)
        counts = cnt[:, 0].astype(I32).reshape(GROUP_SIZE, N_GROUPS).T.reshape(N_EXPERTS)
        row_start = jnp.cumsum(counts) - counts
        expert_ids = jnp.arange(N_EXPERTS, dtype=I32)[:, None, None]
        dest = rank + jnp.sum(jnp.where(idx[None] == expert_ids, row_start[:, None, None], 0), axis=0)
        meta = _visit_metadata(counts, FFN_TM, m * TOP_K)
        xs = _dispatch(x1p, dest)
        ys = _ffn(xs, meta, w_exp_gate, w_exp_up, w_exp_down, i)
        yk = _gather_rows(ys, dest)
        xf, xb = _combine(yk, x1, wtm, p, i,
                          w_sh_gate[i].astype(BF16), w_sh_up[i].astype(BF16), w_sh_down[i].astype(BF16),
                          w_ple_gate[i].astype(BF16), w_ple_proj[i].astype(BF16),
                          ln2_g[i].reshape(1, d), ln2_b[i].reshape(1, d))
    return xf.reshape(bsz, seq, d)
```
